```python
import math
import jax
import jax.numpy as jnp
from jax import lax
import numpy as np

D_MODEL = 2048
BATCH = 8
SEQ = 4096
DEPTH = 4

N_MIXERS = 4
Q_BLOCK = 128
EPS = 1e-6
NEG = -1e30

T5_BUCKETS = 32
T5_MAX_DIST = 128
T5_HEADS = 16

SB_HEADS = 16
SB_HEAD_DIM = D_MODEL // SB_HEADS

NSA_HEADS = 16
NSA_GROUPS = 4
NSA_HEAD_DIM = 128
NSA_CMP_BLOCK = 32
NSA_CMP_STRIDE = 16
NSA_SEL_BLOCK = 64
NSA_TOPN = 16
NSA_WINDOW = 512
NSA_Q_BLOCK = 32
NSA_IN_DIM = NSA_HEADS * NSA_HEAD_DIM + 6 * NSA_GROUPS * NSA_HEAD_DIM + 3 * NSA_HEADS
FORCED_SCORE = 1e9

DIFF_HEADS = 16
DIFF_HEAD_DIM = D_MODEL // (2 * DIFF_HEADS)

MLA_HEADS = 16
MLA_Q_LORA = 768
MLA_KV_LORA = 512
MLA_NOPE = 128
MLA_ROPE = 64
MLA_V = 128
MLA_IN_DIM = MLA_Q_LORA + MLA_KV_LORA + MLA_ROPE
ROPE_THETA = 10000.0

FFN_DIM = 5632
CONV_WIDTH = 3

kernel_name = "hybrid_sb_nsa_diff_mla_convffn_adaln"


def rmsnorm(x, g):
    xf = x.astype(jnp.float32)
    y = xf * lax.rsqrt(jnp.mean(xf * xf, axis=-1, keepdims=True) + EPS)
    return (y * g).astype(x.dtype)


def modulate(x, g, shift, scale):
    return rmsnorm(x, g) * (1.0 + scale[:, None, :]) + shift[:, None, :]


def masked_softmax(logits, mask):
    logits = jnp.where(mask, logits.astype(jnp.float32), NEG)
    m = jnp.max(logits, axis=-1, keepdims=True)
    p = jnp.where(mask, jnp.exp(logits - m), 0.0)
    return p / jnp.maximum(jnp.sum(p, axis=-1, keepdims=True), 1e-30)


def t5_bucket(dist):
    n = jnp.maximum(dist, 0)
    max_exact = T5_BUCKETS // 2
    nf = jnp.maximum(n, 1).astype(jnp.float32)
    large = max_exact + (jnp.log(nf / max_exact) / math.log(T5_MAX_DIST / max_exact)
                         * (T5_BUCKETS - max_exact)).astype(jnp.int32)
    return jnp.where(n < max_exact, n, jnp.minimum(large, T5_BUCKETS - 1))


def rel_bias(t5_bias, dist):
    return jnp.moveaxis(t5_bias[t5_bucket(dist)].astype(jnp.float32), -1, 0)


def rope(x, pos):
    half = x.shape[-1] // 2
    inv = jnp.power(ROPE_THETA, -jnp.arange(half, dtype=jnp.float32) / half)
    ang = pos.astype(jnp.float32)[:, None] * inv[None, :]
    cos = jnp.cos(ang)[None, :, None, :]
    sin = jnp.sin(ang)[None, :, None, :]
    x1, x2 = x[..., :half], x[..., half:]
    return jnp.concatenate([x1 * cos - x2 * sin, x1 * sin + x2 * cos], axis=-1).astype(x.dtype)


def blockwise(fn, n_q, q_block):
    out = lax.map(fn, jnp.arange(n_q // q_block) * q_block)
    nb = out.shape[0]
    out = jnp.moveaxis(out, 0, 1)
    return out.reshape(out.shape[0], nb * q_block, *out.shape[3:])


def stick_breaking_attention(h, w_in, w_out):
    B, S, _ = h.shape
    qkv = (h @ w_in).reshape(B, S, 3, SB_HEADS, SB_HEAD_DIM)
    q, k, v = qkv[:, :, 0], qkv[:, :, 1], qkv[:, :, 2]
    scale = SB_HEAD_DIM ** -0.5
    kpos = jnp.arange(S)

    def block(t0):
        qb = lax.dynamic_slice_in_dim(q, t0, Q_BLOCK, axis=1)
        z = jnp.einsum('bqhd,bkhd->bhqk', qb, k).astype(jnp.float32) * scale
        qpos = t0 + jnp.arange(Q_BLOCK)
        past = kpos[None, :] < qpos[:, None]
        log_keep = jnp.where(past, jax.nn.log_sigmoid(-z), 0.0)
        later = lax.cumsum(log_keep, axis=3, reverse=True) - log_keep
        a = jnp.where(past, jnp.exp(jax.nn.log_sigmoid(z) + later), 0.0)
        return jnp.einsum('bhqk,bkhd->bqhd', a.astype(v.dtype), v)

    o = blockwise(block, S, Q_BLOCK)
    return o.reshape(B, S, SB_HEADS * SB_HEAD_DIM) @ w_out


def nsa_attention(h, w_in, cmp_pe, cmp_w1, cmp_w2, w_out, t5_bias):
    B, S, _ = h.shape
    H, G, dh = NSA_HEADS, NSA_GROUPS, NSA_HEAD_DIM
    R = H // G
    lc, sc, ls, W, Qb = NSA_CMP_BLOCK, NSA_CMP_STRIDE, NSA_SEL_BLOCK, NSA_WINDOW, NSA_Q_BLOCK
    scale = dh ** -0.5
    proj = h @ w_in
    q = proj[..., :H * dh].reshape(B, S, G, R, dh)
    kvs = proj[..., H * dh:H * dh + 6 * G * dh].reshape(B, S, 6, G, dh)
    gates = jax.nn.sigmoid(proj[..., H * dh + 6 * G * dh:].astype(jnp.float32)).reshape(B, S, 3, G, R)
    k_cmp_raw, v_cmp_raw, k_sel, v_sel, k_win, v_win = (kvs[:, :, i] for i in range(6))

    n_cmp = (S - lc) // sc + 1
    cmp_start = jnp.arange(n_cmp) * sc
    cmp_end = cmp_start + lc - 1
    cmp_idx = cmp_start[:, None] + jnp.arange(lc)[None, :]

    def compress(raw, pe, w1, w2):
        blocks = raw[:, cmp_idx] + pe[:, None, :]
        flat = jnp.moveaxis(blocks, 2, 3).reshape(B, n_cmp, G, lc * dh)
        return jax.nn.silu(flat @ w1) @ w2

    k_cmp = compress(k_cmp_raw, cmp_pe[0], cmp_w1[0], cmp_w2[0])
    v_cmp = compress(v_cmp_raw, cmp_pe[1], cmp_w1[1], cmp_w2[1])

    n_sel = S // ls
    n_top = min(NSA_TOPN, n_sel)
    sel_start = jnp.arange(n_sel) * ls
    overlap = ((cmp_start[:, None] < sel_start[None, :] + ls)
               & (cmp_start[:, None] + lc > sel_start[None, :])).astype(jnp.float32)
    k_blk = jnp.moveaxis(k_sel.reshape(B, n_sel, ls, G, dh), 3, 1)
    v_blk = jnp.moveaxis(v_sel.reshape(B, n_sel, ls, G, dh), 3, 1)
    blk = jnp.arange(n_sel)

    k_pad = jnp.pad(k_win, ((0, 0), (W, 0), (0, 0), (0, 0)))
    v_pad = jnp.pad(v_win, ((0, 0), (W, 0), (0, 0), (0, 0)))

    tb = t5_bias.reshape(T5_BUCKETS, G, R)
    b_idx = jnp.arange(B)[:, None, None, None]
    g_idx = jnp.arange(G)[None, :, None, None]
    g_idx5 = jnp.arange(G)[None, :, None, None, None]

    def block(t0):
        qpos = t0 + jnp.arange(Qb)
        qb = lax.dynamic_slice_in_dim(q, t0, Qb, axis=1)
        dist_c = qpos[:, None] - cmp_end[None, :]
        s_c = (jnp.einsum('bqgrd,bcgd->bgrqc', qb, k_cmp).astype(jnp.float32) * scale
               + rel_bias(t5_bias, dist_c).reshape(G, R, Qb, n_cmp))
        p_c = masked_softmax(s_c, dist_c >= 0)
        o_c = jnp.einsum('bgrqc,bcgd->bqgrd', p_c.astype(v_cmp.dtype), v_cmp)
        imp = jnp.einsum('bgrqc,cn->bgqn', p_c, overlap)
        cur = qpos // ls
        causal_blk = sel_start[None, :] <= qpos[:, None]
        forced = (blk[None, :] == 0) | (blk[None, :] == cur[:, None]) | (blk[None, :] == cur[:, None] - 1)
        score = jnp.where(causal_blk, jnp.where(forced, FORCED_SCORE, imp), -1.0)
        _, idx = lax.top_k(score, n_top)
        kg = k_blk[b_idx, g_idx, idx]
        vg = v_blk[b_idx, g_idx, idx]
        dist_s = qpos[:, None, None] - (idx[..., None] * ls + jnp.arange(ls))
        bias_s = jnp.moveaxis(tb[t5_bucket(dist_s), g_idx5], -1, 2).astype(jnp.float32)
        s_s = jnp.einsum('bqgrd,bgqnkd->bgrqnk', qb, kg).astype(jnp.float32) * scale + bias_s
        p_s = masked_softmax(s_s.reshape(B, G, R, Qb, n_top * ls),
                             (dist_s >= 0).reshape(B, G, 1, Qb, n_top * ls))
        o_s = jnp.einsum('bgrqm,bgqmd->bqgrd', p_s.astype(vg.dtype),
                         vg.reshape(B, G, Qb, n_top * ls, dh))
        kw = lax.dynamic_slice_in_dim(k_pad, t0, W + Qb, axis=1)
        vw = lax.dynamic_slice_in_dim(v_pad, t0, W + Qb, axis=1)
        kp = t0 - W + jnp.arange(W + Qb)
        dist_w = qpos[:, None] - kp[None, :]
        mask_w = (dist_w >= 0) & (dist_w < W) & (kp[None, :] >= 0)
        s_w = (jnp.einsum('bqgrd,bkgd->bgrqk', qb, kw).astype(jnp.float32) * scale
               + rel_bias(t5_bias, dist_w).reshape(G, R, Qb, W + Qb))
        p_w = masked_softmax(s_w, mask_w)
        o_w = jnp.einsum('bgrqk,bkgd->bqgrd', p_w.astype(vw.dtype), vw)
        gb = lax.dynamic_slice_in_dim(gates, t0, Qb, axis=1)[..., None]
        return gb[:, :, 0] * o_c + gb[:, :, 1] * o_s + gb[:, :, 2] * o_w

    o = blockwise(block, S, Qb)
    return o.reshape(B, S, H * dh) @ w_out


def diff_attention(h, w_in, lam, head_g, w_out, t5_bias, lambda_init):
    B, S, _ = h.shape
    H, d = DIFF_HEADS, DIFF_HEAD_DIM
    qkv = (h @ w_in).reshape(B, S, 3, H, 2, d)
    q, k = qkv[:, :, 0], qkv[:, :, 1]
    v = qkv[:, :, 2].reshape(B, S, H, 2 * d)
    lam_f = lam.astype(jnp.float32)
    lmbda = jnp.exp(jnp.sum(lam_f[0] * lam_f[1])) - jnp.exp(jnp.sum(lam_f[2] * lam_f[3])) + lambda_init
    scale = d ** -0.5
    kpos = jnp.arange(S)

    def block(t0):
        qb = lax.dynamic_slice_in_dim(q, t0, Q_BLOCK, axis=1)
        qpos = t0 + jnp.arange(Q_BLOCK)
        dist = qpos[:, None] - kpos[None, :]
        s = (jnp.einsum('bqhmd,bkhmd->bmhqk', qb, k).astype(jnp.float32) * scale
             + rel_bias(t5_bias, dist))
        p = masked_softmax(s, dist >= 0)
        a = p[:, 0] - lmbda * p[:, 1]
        return jnp.einsum('bhqk,bkhe->bqhe', a.astype(v.dtype), v)

    o = blockwise(block, S, Q_BLOCK)
    o = rmsnorm(o, head_g) * (1.0 - lambda_init)
    return o.reshape(B, S, H * 2 * d) @ w_out


def mla_attention(h, w_in, q_g, w_qb, kv_g, w_kvb, w_out):
    B, S, _ = h.shape
    H = MLA_HEADS
    a = h @ w_in
    cq = a[..., :MLA_Q_LORA]
    ckv = a[..., MLA_Q_LORA:MLA_Q_LORA + MLA_KV_LORA]
    kr = a[..., MLA_Q_LORA + MLA_KV_LORA:]
    qf = (rmsnorm(cq, q_g) @ w_qb).reshape(B, S, H, MLA_NOPE + MLA_ROPE)
    kv = (rmsnorm(ckv, kv_g) @ w_kvb).reshape(B, S, H, MLA_NOPE + MLA_V)
    pos = jnp.arange(S)
    q_nope = qf[..., :MLA_NOPE]
    q_rope = rope(qf[..., MLA_NOPE:], pos)
    k_nope, v = kv[..., :MLA_NOPE], kv[..., MLA_NOPE:]
    k_rope = rope(kr[:, :, None, :], pos)[:, :, 0]
    scale = (MLA_NOPE + MLA_ROPE) ** -0.5

    def block(t0):
        qn = lax.dynamic_slice_in_dim(q_nope, t0, Q_BLOCK, axis=1)
        qr = lax.dynamic_slice_in_dim(q_rope, t0, Q_BLOCK, axis=1)
        s = (jnp.einsum('bqhd,bkhd->bhqk', qn, k_nope)
             + jnp.einsum('bqhr,bkr->bhqk', qr, k_rope)).astype(jnp.float32) * scale
        qpos = t0 + jnp.arange(Q_BLOCK)
        p = masked_softmax(s, pos[None, :] <= qpos[:, None])
        return jnp.einsum('bhqk,bkhd->bqhd', p.astype(v.dtype), v)

    o = blockwise(block, S, Q_BLOCK)
    return o.reshape(B, S, H * MLA_V) @ w_out


def conv_ffn(h, w_up, conv_w, conv_b, w_down):
    u = h @ w_up
    ch = u.shape[-1]
    u = lax.conv_general_dilated(u, conv_w.astype(u.dtype)[:, None, :], window_strides=(1,),
                                 padding=[(CONV_WIDTH - 1, 0)],
                                 dimension_numbers=('NWC', 'WIO', 'NWC'),
                                 feature_group_count=ch) + conv_b
    gate, up = jnp.split(u, 2, axis=-1)
    return (jax.nn.silu(gate) * up) @ w_down


def setup_inputs(seed: int = 0) -> dict:
    key = jax.random.key(seed)
    ks = iter(jax.random.split(key, 40))

    def nrm(shape, scale):
        return jax.random.normal(next(ks), shape, jnp.float32) * scale

    def gain(shape):
        return 1.0 + nrm(shape, 0.05)

    D = D_MODEL
    F2 = 2 * FFN_DIM
    n_a, n_b, n_c, n_d = (len(range(m, DEPTH, N_MIXERS)) for m in range(N_MIXERS))
    lcd = NSA_CMP_BLOCK * NSA_HEAD_DIM
    return {
        'x': nrm((BATCH, SEQ, D), 1.0),
        'c': nrm((BATCH, D), 1.0),
        't5_bias': nrm((T5_BUCKETS, T5_HEADS), 0.3),
        'ada_w': nrm((DEPTH, D, 6 * D), 0.5 * D ** -0.5),
        'ada_b': nrm((DEPTH, 6 * D), 0.01),
        'norm_g': gain((DEPTH, 2, D)),
        'final_g': gain((D,)),
        'ffn_w_up': nrm((DEPTH, D, F2), D ** -0.5),
        'ffn_conv_w': nrm((DEPTH, CONV_WIDTH, F2), CONV_WIDTH ** -0.5),
        'ffn_conv_b': nrm((DEPTH, F2), 0.02),
        'ffn_w_down': nrm((DEPTH, FFN_DIM, D), FFN_DIM ** -0.5),
        'sb_w_in': nrm((n_a, D, 3 * SB_HEADS * SB_HEAD_DIM), D ** -0.5),
        'sb_w_out': nrm((n_a, SB_HEADS * SB_HEAD_DIM, D), (SB_HEADS * SB_HEAD_DIM) ** -0.5),
        'nsa_w_in': nrm((n_b, D, NSA_IN_DIM), D ** -0.5),
        'nsa_cmp_pe': nrm((n_b, 2, NSA_CMP_BLOCK, NSA_HEAD_DIM), 0.5),
        'nsa_cmp_w1': nrm((n_b, 2, lcd, NSA_HEAD_DIM), lcd ** -0.5),
        'nsa_cmp_w2': nrm((n_b, 2, NSA_HEAD_DIM, NSA_HEAD_DIM), NSA_HEAD_DIM ** -0.5),
        'nsa_w_out': nrm((n_b, NSA_HEADS * NSA_HEAD_DIM, D), (NSA_HEADS * NSA_HEAD_DIM) ** -0.5),
        'diff_w_in': nrm((n_c, D, 3 * DIFF_HEADS * 2 * DIFF_HEAD_DIM), D ** -0.5),
        'diff_lambda': nrm((n_c, 4, DIFF_HEAD_DIM), 0.1),
        'diff_head_g': gain((n_c, 2 * DIFF_HEAD_DIM)),
        'diff_w_out': nrm((n_c, DIFF_HEADS * 2 * DIFF_HEAD_DIM, D), (DIFF_HEADS * 2 * DIFF_HEAD_DIM) ** -0.5),
        'mla_w_in': nrm((n_d, D, MLA_IN_DIM), D ** -0.5),
        'mla_q_g': gain((n_d, MLA_Q_LORA)),
        'mla_w_qb': nrm((n_d, MLA_Q_LORA, MLA_HEADS * (MLA_NOPE + MLA_ROPE)), MLA_Q_LORA ** -0.5),
        'mla_kv_g': gain((n_d, MLA_KV_LORA)),
        'mla_w_kvb': nrm((n_d, MLA_KV_LORA, MLA_HEADS * (MLA_NOPE + MLA_V)), MLA_KV_LORA ** -0.5),
        'mla_w_out': nrm((n_d, MLA_HEADS * MLA_V, D), (MLA_HEADS * MLA_V) ** -0.5),
    }


def reference(x, c, t5_bias, ada_w, ada_b, norm_g, final_g, ffn_w_up, ffn_conv_w, ffn_conv_b,
              ffn_w_down, sb_w_in, sb_w_out, nsa_w_in, nsa_cmp_pe, nsa_cmp_w1, nsa_cmp_w2,
              nsa_w_out, diff_w_in, diff_lambda, diff_head_g, diff_w_out, mla_w_in, mla_q_g,
              mla_w_qb, mla_kv_g, mla_w_kvb, mla_w_out):
    c_act = jax.nn.silu(c)
    for i in range(DEPTH):
        m, j = i % N_MIXERS, i // N_MIXERS
        mod = c_act @ ada_w[i] + ada_b[i]
        shift1, scale1, gate1, shift2, scale2, gate2 = jnp.split(mod, 6, axis=-1)
        h = modulate(x, norm_g[i, 0], shift1, scale1)
        if m == 0:
            y = stick_breaking_attention(h, sb_w_in[j], sb_w_out[j])
        elif m == 1:
            y = nsa_attention(h, nsa_w_in[j], nsa_cmp_pe[j], nsa_cmp_w1[j], nsa_cmp_w2[j],
                              nsa_w_out[j], t5_bias)
        elif m == 2:
            y = diff_attention(h, diff_w_in[j], diff_lambda[j], diff_head_g[j], diff_w_out[j],
                               t5_bias, 0.8 - 0.6 * math.exp(-0.3 * i))
        else:
            y = mla_attention(h, mla_w_in[j], mla_q_g[j], mla_w_qb[j], mla_kv_g[j],
                              mla_w_kvb[j], mla_w_out[j])
        x = x + gate1[:, None, :] * y
        h = modulate(x, norm_g[i, 1], shift2, scale2)
        x = x + gate2[:, None, :] * conv_ffn(h, ffn_w_up[i], ffn_conv_w[i], ffn_conv_b[i], ffn_w_down[i])
    return rmsnorm(x, final_g)
```

```python
import functools
import math

import numpy as np
import jax
import jax.numpy as jnp
from jax import lax
from jax.experimental import pallas as pl
from jax.experimental.pallas import tpu as pltpu

F32 = jnp.float32
BF16 = jnp.bfloat16
EPS = 1e-6
NEG = -1e30

LANE = 128
HALO = 16
VMEM_LIMIT = 56 * 2**20

T5_BUCKETS = 32
T5_MAX_DIST = 128
N_HEADS = 16
HEAD_DIM = 128
NSA_GROUPS = 4
NSA_REP = 4
NSA_CMP_BLOCK = 32
NSA_CMP_STRIDE = 16
NSA_SEL_BLOCK = 64
NSA_TOPN = 16
NSA_WINDOW = 512
FORCED_SCORE = 1e9
DIFF_DIM = 64
MLA_Q_LORA = 768
MLA_KV_LORA = 512
MLA_NOPE = 128
MLA_ROPE = 64
ROPE_THETA = 10000.0
CONV_WIDTH = 3
ATT_T = 128


def _params(*sem):
    return pltpu.CompilerParams(dimension_semantics=sem, vmem_limit_bytes=VMEM_LIMIT)


def _dot(a, b):
    return jnp.dot(a, b, preferred_element_type=F32)


def _dot_nt(a, b):
    return lax.dot_general(a, b, (((1,), (1,)), ((), ())), preferred_element_type=F32)


def _split_bf16(x):
    hi = x.astype(BF16)
    lo = (x - hi.astype(F32)).astype(BF16)
    return hi, lo


def _rms(x, g):
    ms = jnp.mean(x * x, axis=-1, keepdims=True)
    return x * lax.rsqrt(ms + EPS) * g


def _ada_kernel(c_ref, w_ref, b_ref, o_ref):
    c = c_ref[...]
    a = c * jax.nn.sigmoid(c)
    a_hi, a_lo = _split_bf16(a)
    w_hi, w_lo = _split_bf16(w_ref[0])
    o_ref[0] = _dot(a_hi, w_hi) + _dot(a_lo, w_hi) + _dot(a_hi, w_lo) + b_ref[0]


def _ada_mod(c, ada_w, ada_b, tn=1024):
    depth, d, n = ada_w.shape
    b = c.shape[0]
    return pl.pallas_call(
        _ada_kernel,
        grid=(depth, n // tn),
        in_specs=[pl.BlockSpec((b, d), lambda l, j: (0, 0)),
                  pl.BlockSpec((1, d, tn), lambda l, j: (l, 0, j)),
                  pl.BlockSpec((1, 1, tn), lambda l, j: (l, 0, j))],
        out_specs=pl.BlockSpec((1, b, tn), lambda l, j: (l, 0, j)),
        out_shape=jax.ShapeDtypeStruct((depth, b, n), F32),
        compiler_params=_params("parallel", "parallel"),
        name="ada_mod",
    )(c, ada_w, ada_b.reshape(depth, 1, n))


def _norm_mm_kernel(*refs, modulated, rope, act):
    it = iter(refs)
    x_ref, g_ref = next(it), next(it)
    sh_ref = sc_ref = wb_ref = cos_ref = sin_ref = None
    if modulated:
        sh_ref, sc_ref = next(it), next(it)
    w_ref = next(it)
    if rope:
        wb_ref, cos_ref, sin_ref = next(it), next(it), next(it)
    o_ref, h_ref = next(it), next(it)

    @pl.when(pl.program_id(1) == 0)
    def _():
        y = _rms(x_ref[...], g_ref[...])
        if modulated:
            y = y * (1.0 + sc_ref[0]) + sh_ref[0]
        h_ref[...] = y.astype(BF16)

    h = h_ref[...]
    acc = _dot(h, w_ref[...])
    if rope:
        acc = acc * cos_ref[...] + _dot(h, wb_ref[...]) * sin_ref[...]
    if act == "sigmoid":
        acc = jax.nn.sigmoid(acc)
    o_ref[...] = acc.astype(o_ref.dtype)


def _norm_mm(x, g, w, *, seq_len, shift=None, scale=None, wb=None, cos=None, sin=None, act=None,
             out_dtype=BF16, x_cols=None, x_col_block=0, tm=512, tn=512, name="norm_mm"):
    m = x.shape[0]
    k = x.shape[1] if x_cols is None else x_cols
    n = w.shape[1]
    tn = min(tn, n)
    modulated, rope = shift is not None, wb is not None
    tps = seq_len // tm
    in_specs = [pl.BlockSpec((tm, k), lambda i, j: (i, x_col_block)),
                pl.BlockSpec((1, k), lambda i, j: (0, 0))]
    args = [x, g.reshape(1, k)]
    if modulated:
        in_specs += [pl.BlockSpec((1, 1, k), lambda i, j: (i // tps, 0, 0))] * 2
        args += [shift, scale]
    in_specs.append(pl.BlockSpec((k, tn), lambda i, j: (0, j)))
    args.append(w)
    if rope:
        in_specs.append(pl.BlockSpec((k, tn), lambda i, j: (0, j)))
        in_specs += [pl.BlockSpec((tm, tn), lambda i, j: (i % tps, j))] * 2
        args += [wb, cos, sin]
    return pl.pallas_call(
        functools.partial(_norm_mm_kernel, modulated=modulated, rope=rope, act=act),
        grid=(m // tm, n // tn),
        in_specs=in_specs,
        out_specs=pl.BlockSpec((tm, tn), lambda i, j: (i, j)),
        out_shape=jax.ShapeDtypeStruct((m, n), out_dtype),
        scratch_shapes=[pltpu.VMEM((tm, k), BF16)],
        compiler_params=_params("parallel", "arbitrary"),
        name=name,
    )(*args)


def _mm_res_kernel(a_ref, w_ref, x_ref, gate_ref, o_ref):
    o_ref[...] = x_ref[...] + gate_ref[0] * _dot(a_ref[...], w_ref[...])


def _mm_residual(a, w, x, gate, *, seq_len, tm=1024, tn=512, name="mm_residual"):
    m, k = a.shape
    n = w.shape[1]
    tps = seq_len // tm
    return pl.pallas_call(
        _mm_res_kernel,
        grid=(m // tm, n // tn),
        in_specs=[pl.BlockSpec((tm, k), lambda i, j: (i, 0)),
                  pl.BlockSpec((k, tn), lambda i, j: (0, j)),
                  pl.BlockSpec((tm, tn), lambda i, j: (i, j)),
                  pl.BlockSpec((1, 1, tn), lambda i, j: (i // tps, 0, j))],
        out_specs=pl.BlockSpec((tm, tn), lambda i, j: (i, j)),
        out_shape=jax.ShapeDtypeStruct((m, n), F32),
        compiler_params=_params("parallel", "arbitrary"),
        name=name,
    )(a, w, x, gate)


def _ffn_kernel(x_ref, xp_ref, g_ref, sh_ref, sc_ref, wg_ref, wu_ref, cwg_ref, cwu_ref, cbg_ref, cbu_ref,
                wd_ref, gate_ref, fg_ref, o_ref, h_ref, acc_ref, *, tiles_per_seq, final_norm):
    i, j = pl.program_id(0), pl.program_id(1)

    @pl.when(j == 0)
    def _():
        def nm(x):
            return _rms(x, g_ref[...]) * (1.0 + sc_ref[0]) + sh_ref[0]
        h_ref[HALO:, :] = nm(x_ref[...]).astype(BF16)
        keep = jnp.where(i % tiles_per_seq == 0, 0.0, 1.0)
        h_ref[:HALO, :] = (nm(xp_ref[...]) * keep).astype(BF16)
        acc_ref[...] = jnp.zeros_like(acc_ref)

    h = h_ref[...]

    def branch(w_ref, cw_ref, cb_ref):
        u = _dot(h, w_ref[...])
        u1 = pltpu.roll(u, 1, 0)
        u2 = pltpu.roll(u, 2, 0)
        cw = cw_ref[...]
        y = cw[0:1] * u2 + cw[1:2] * u1 + cw[2:3] * u + cb_ref[...]
        return y[HALO:]

    gt = branch(wg_ref, cwg_ref, cbg_ref)
    up = branch(wu_ref, cwu_ref, cbu_ref)
    a = gt * jax.nn.sigmoid(gt) * up
    acc_ref[...] += _dot(a.astype(BF16), wd_ref[...])

    @pl.when(j == pl.num_programs(1) - 1)
    def _():
        y = x_ref[...] + gate_ref[0] * acc_ref[...]
        if final_norm:
            y = _rms(y, fg_ref[...])
        o_ref[...] = y


def _conv_ffn(x, g, shift, scale, w_up, conv_w, conv_b, w_down, gate, final_g, *, seq_len, final_norm,
              tm=512, tf=512):
    m, d = x.shape
    f = w_down.shape[0]
    nf = f // tf
    tps = seq_len // tm
    hb = tm // HALO
    row = lambda i, j: (i, 0)
    per_batch = lambda i, j: (i // tps, 0, 0)
    return pl.pallas_call(
        functools.partial(_ffn_kernel, tiles_per_seq=tps, final_norm=final_norm),
        grid=(m // tm, nf),
        in_specs=[pl.BlockSpec((tm, d), row),
                  pl.BlockSpec((HALO, d), lambda i, j: (jnp.maximum(i * hb - 1, 0), 0)),
                  pl.BlockSpec((1, d), lambda i, j: (0, 0)),
                  pl.BlockSpec((1, 1, d), per_batch),
                  pl.BlockSpec((1, 1, d), per_batch),
                  pl.BlockSpec((d, tf), lambda i, j: (0, j)),
                  pl.BlockSpec((d, tf), lambda i, j: (0, j + nf)),
                  pl.BlockSpec((CONV_WIDTH, tf), lambda i, j: (0, j)),
                  pl.BlockSpec((CONV_WIDTH, tf), lambda i, j: (0, j + nf)),
                  pl.BlockSpec((1, tf), lambda i, j: (0, j)),
                  pl.BlockSpec((1, tf), lambda i, j: (0, j + nf)),
                  pl.BlockSpec((tf, d), lambda i, j: (j, 0)),
                  pl.BlockSpec((1, 1, d), per_batch),
                  pl.BlockSpec((1, d), lambda i, j: (0, 0))],
        out_specs=pl.BlockSpec((tm, d), row),
        out_shape=jax.ShapeDtypeStruct((m, d), F32),
        scratch_shapes=[pltpu.VMEM((HALO + tm, d), BF16), pltpu.VMEM((tm, d), F32)],
        compiler_params=_params("parallel", "arbitrary"),
        name="conv_ffn",
    )(x, x, g.reshape(1, d), shift, scale, w_up, w_up, conv_w, conv_w, conv_b.reshape(1, -1),
      conv_b.reshape(1, -1), w_down, gate, final_g.reshape(1, d))


def _t5_bucket_np(dist):
    n = np.maximum(dist, 0)
    max_exact = T5_BUCKETS // 2
    nf = np.maximum(n, 1).astype(np.float64)
    val = np.log(nf / max_exact) / math.log(T5_MAX_DIST / max_exact) * (T5_BUCKETS - max_exact)
    large = max_exact + np.trunc(val + 1e-6).astype(np.int64)
    return np.where(n < max_exact, n, np.minimum(large, T5_BUCKETS - 1)).astype(np.int32)


def _t5_gather_kernel(t5_ref, bkt_ref, o_ref):
    h = pl.program_id(0)
    bk = bkt_ref[...]
    acc = jnp.zeros(bk.shape, F32)
    for b in range(T5_BUCKETS):
        acc = jnp.where(bk == b, t5_ref[b, h], acc)
    o_ref[0] = acc


def _t5_gather(t5_bias, bkt, tr):
    rows, cols = bkt.shape
    heads = t5_bias.shape[1]
    return pl.pallas_call(
        _t5_gather_kernel,
        grid=(heads, rows // tr),
        in_specs=[pl.BlockSpec(memory_space=pltpu.SMEM),
                  pl.BlockSpec((tr, cols), lambda h, r: (r, 0))],
        out_specs=pl.BlockSpec((1, tr, cols), lambda h, r: (h, r, 0)),
        out_shape=jax.ShapeDtypeStruct((heads, rows, cols), F32),
        compiler_params=_params("parallel", "parallel"),
        name="t5_gather",
    )(t5_bias, jnp.asarray(bkt))


def _toeplitz_bias(t5_bias, t):
    i = np.arange(t)[:, None]
    j = np.arange(t)[None, :]
    bkt = np.concatenate([_t5_bucket_np(i - j), _t5_bucket_np(t + i - j)], axis=0)
    assert int(_t5_bucket_np(np.array(t + 1))) == T5_BUCKETS - 1
    near = _t5_gather(t5_bias, bkt, tr=2 * t)
    far = _t5_gather(t5_bias, np.full((8, t), T5_BUCKETS - 1, np.int32), tr=8)
    return near, far


def _osm(s, v, m, l, acc):
    m_new = jnp.maximum(m, jnp.max(s, axis=-1, keepdims=True))
    p = jnp.exp(s - m_new)
    alpha = jnp.exp(m - m_new)
    l = alpha * l + jnp.sum(p, axis=-1, keepdims=True)
    acc = alpha * acc + _dot(p.astype(BF16), v)
    return m_new, l, acc


def _osm_init(rows, dv):
    return jnp.full((rows, 1), NEG, F32), jnp.zeros((rows, 1), F32), jnp.zeros((rows, dv), F32)


def _tile_iota(t):
    return lax.broadcasted_iota(jnp.int32, (t, t), 0), lax.broadcasted_iota(jnp.int32, (t, t), 1)


def _sb_kernel(q_ref, k_ref, v_ref, o_ref, *, t, scale):
    i = pl.program_id(2)
    q = q_ref[...]
    row, col = _tile_iota(t)
    past = col < row
    upper = jnp.where(row > col, 1.0, 0.0).astype(BF16)

    def tile(kb, carry, diag):
        c, acc = carry
        start = pl.multiple_of(kb * t, t)
        z = _dot_nt(q, k_ref[pl.ds(start, t), :]) * scale
        lk = -(jnp.maximum(z, 0.0) + jnp.log1p(jnp.exp(-jnp.abs(z))))
        if diag:
            lk = jnp.where(past, lk, 0.0)
        hi, lo = _split_bf16(lk)
        later = c + _dot(hi, upper) + _dot(lo, upper)
        a = jnp.exp(z + lk + later)
        if diag:
            a = jnp.where(past, a, 0.0)
        acc = acc + _dot(a.astype(BF16), v_ref[pl.ds(start, t), :])
        c = c + jnp.sum(lk, axis=-1, keepdims=True)
        return c, acc

    carry = tile(i, (jnp.zeros((t, 1), F32), jnp.zeros((t, HEAD_DIM), F32)), True)
    carry = lax.fori_loop(0, i, lambda n, cr: tile(i - 1 - n, cr, False), carry)
    o_ref[...] = carry[1].astype(o_ref.dtype)


def _sb_attention(qkv, *, batch, seq_len, t=ATT_T):
    h = N_HEADS
    nq = seq_len // t
    return pl.pallas_call(
        functools.partial(_sb_kernel, t=t, scale=HEAD_DIM ** -0.5),
        grid=(batch, h, nq),
        in_specs=[pl.BlockSpec((t, HEAD_DIM), lambda b, hh, i: (b * nq + i, hh)),
                  pl.BlockSpec((seq_len, HEAD_DIM), lambda b, hh, i: (b, h + hh)),
                  pl.BlockSpec((seq_len, HEAD_DIM), lambda b, hh, i: (b, 2 * h + hh))],
        out_specs=pl.BlockSpec((t, HEAD_DIM), lambda b, hh, i: (b * nq + i, hh)),
        out_shape=jax.ShapeDtypeStruct((batch * seq_len, h * HEAD_DIM), BF16),
        compiler_params=_params("parallel", "parallel", "arbitrary"),
        name="sb_attention",
    )(qkv, qkv, qkv)


def _diff_kernel(q_ref, k_ref, v_ref, bias_ref, far_ref, lam_ref, hg_ref, o_ref, *, t, scale, lambda_init):
    i = pl.program_id(2)
    q = q_ref[...]
    lane = lax.broadcasted_iota(jnp.int32, q.shape, 1)
    zero = jnp.zeros_like(q)
    q2 = jnp.concatenate([jnp.where(lane < DIFF_DIM, q, zero), jnp.where(lane >= DIFF_DIM, q, zero)], axis=0)
    row, col = _tile_iota(t)
    causal = jnp.concatenate([col <= row] * 2, axis=0)
    bias_d = jnp.concatenate([bias_ref[0, :t, :]] * 2, axis=0)
    bias_s = jnp.concatenate([bias_ref[0, t:, :]] * 2, axis=0)
    bias_f = far_ref[0, 0:1, :]

    def tile(kb, carry, bias, mask):
        start = pl.multiple_of(kb * t, t)
        s = _dot_nt(q2, k_ref[pl.ds(start, t), :]) * scale + bias
        if mask:
            s = jnp.where(causal, s, NEG)
        return _osm(s, v_ref[pl.ds(start, t), :], *carry)

    carry = _osm_init(2 * t, HEAD_DIM)
    carry = lax.fori_loop(0, jnp.maximum(i - 1, 0), lambda kb, cr: tile(kb, cr, bias_f, False), carry)
    carry = lax.cond(i > 0, lambda cr: tile(i - 1, cr, bias_s, False), lambda cr: cr, carry)
    m, l, acc = tile(i, carry, bias_d, True)

    lam = lam_ref[...]
    lmbda = (jnp.exp(jnp.sum(lam[0:1] * lam[1:2], axis=-1, keepdims=True))
             - jnp.exp(jnp.sum(lam[2:3] * lam[3:4], axis=-1, keepdims=True)) + lambda_init)
    o = acc / l
    o = o[:t] - lmbda * o[t:]
    o_ref[...] = (_rms(o, hg_ref[...]) * (1.0 - lambda_init)).astype(o_ref.dtype)


def _diff_attention(qkv, bias_near, bias_far, lam, head_g, *, batch, seq_len, lambda_init, t=ATT_T):
    h = N_HEADS
    nq = seq_len // t
    return pl.pallas_call(
        functools.partial(_diff_kernel, t=t, scale=DIFF_DIM ** -0.5, lambda_init=lambda_init),
        grid=(batch, h, nq),
        in_specs=[pl.BlockSpec((t, HEAD_DIM), lambda b, hh, i: (b * nq + i, hh)),
                  pl.BlockSpec((seq_len, HEAD_DIM), lambda b, hh, i: (b, h + hh)),
                  pl.BlockSpec((seq_len, HEAD_DIM), lambda b, hh, i: (b, 2 * h + hh)),
                  pl.BlockSpec((1, 2 * t, t), lambda b, hh, i: (hh, 0, 0)),
                  pl.BlockSpec((1, 8, t), lambda b, hh, i: (hh, 0, 0)),
                  pl.BlockSpec((4, DIFF_DIM), lambda b, hh, i: (0, 0)),
                  pl.BlockSpec((1, HEAD_DIM), lambda b, hh, i: (0, 0))],
        out_specs=pl.BlockSpec((t, HEAD_DIM), lambda b, hh, i: (b * nq + i, hh)),
        out_shape=jax.ShapeDtypeStruct((batch * seq_len, h * HEAD_DIM), BF16),
        compiler_params=_params("parallel", "parallel", "arbitrary"),
        name="diff_attention",
    )(qkv, qkv, qkv, bias_near, bias_far, lam, head_g.reshape(1, HEAD_DIM))


def _mla_kernel(qn_ref, qr_ref, kn_ref, kr_ref, v_ref, o_ref, *, t, scale):
    i = pl.program_id(2)
    q = jnp.concatenate([qn_ref[...], qr_ref[...]], axis=-1)
    row, col = _tile_iota(t)
    causal = col <= row

    def tile(kb, carry, mask):
        start = pl.multiple_of(kb * t, t)
        k = jnp.concatenate([kn_ref[pl.ds(start, t), :], kr_ref[pl.ds(start, t), :]], axis=-1)
        s = _dot_nt(q, k) * scale
        if mask:
            s = jnp.where(causal, s, NEG)
        return _osm(s, v_ref[pl.ds(start, t), :], *carry)

    carry = lax.fori_loop(0, i, lambda kb, cr: tile(kb, cr, False), _osm_init(t, HEAD_DIM))
    m, l, acc = tile(i, carry, True)
    o_ref[...] = (acc / l).astype(o_ref.dtype)


def _mla_attention(qn, qr, kv, kr, *, batch, seq_len, t=ATT_T):
    h = N_HEADS
    nq = seq_len // t
    qspec = pl.BlockSpec((t, HEAD_DIM), lambda b, hh, i: (b * nq + i, hh))
    return pl.pallas_call(
        functools.partial(_mla_kernel, t=t, scale=(MLA_NOPE + MLA_ROPE) ** -0.5),
        grid=(batch, h, nq),
        in_specs=[qspec, qspec,
                  pl.BlockSpec((seq_len, HEAD_DIM), lambda b, hh, i: (b, 2 * hh)),
                  pl.BlockSpec((seq_len, HEAD_DIM), lambda b, hh, i: (b, 0)),
                  pl.BlockSpec((seq_len, HEAD_DIM), lambda b, hh, i: (b, 2 * hh + 1))],
        out_specs=qspec,
        out_shape=jax.ShapeDtypeStruct((batch * seq_len, h * HEAD_DIM), BF16),
        compiler_params=_params("parallel", "parallel", "arbitrary"),
        name="mla_attention",
    )(qn, qr, kv, kr, kv)


def _compress_kernel(raw_ref, pe_ref, w1_ref, w2_ref, o_ref, *, n_slots):
    half = NSA_CMP_BLOCK // 2
    p1 = jnp.zeros((n_slots, HEAD_DIM), F32)
    p2 = jnp.zeros((n_slots, HEAD_DIM), F32)
    for l in range(half):
        a = raw_ref[pl.ds(l, n_slots, stride=NSA_CMP_STRIDE), :]
        p1 = p1 + _dot((a + pe_ref[0, l:l + 1, :]).astype(BF16), w1_ref[0, l])
        p2 = p2 + _dot((a + pe_ref[0, half + l:half + l + 1, :]).astype(BF16), w1_ref[0, half + l])
    pre = p1 + pltpu.roll(p2, n_slots - 1, 0)
    hid = pre * jax.nn.sigmoid(pre)
    o_ref[0, 0, 0] = _dot(hid.astype(BF16), w2_ref[0]).astype(o_ref.dtype)


def _nsa_compress(raw, pe, w1, w2, *, batch, seq_len):
    g = NSA_GROUPS
    n_slots = seq_len // NSA_CMP_STRIDE
    return pl.pallas_call(
        functools.partial(_compress_kernel, n_slots=n_slots),
        grid=(batch, 2, g),
        in_specs=[pl.BlockSpec((seq_len, HEAD_DIM), lambda b, kv, gg: (b, kv * g + gg)),
                  pl.BlockSpec((1, NSA_CMP_BLOCK, HEAD_DIM), lambda b, kv, gg: (kv, 0, 0)),
                  pl.BlockSpec((1, NSA_CMP_BLOCK, HEAD_DIM, HEAD_DIM), lambda b, kv, gg: (kv, 0, 0, 0)),
                  pl.BlockSpec((1, HEAD_DIM, HEAD_DIM), lambda b, kv, gg: (kv, 0, 0))],
        out_specs=pl.BlockSpec((1, 1, 1, n_slots, HEAD_DIM), lambda b, kv, gg: (b, kv, gg, 0, 0)),
        out_shape=jax.ShapeDtypeStruct((batch, 2, g, n_slots, HEAD_DIM), BF16),
        compiler_params=_params("parallel", "parallel", "parallel"),
        name="nsa_compress",
    )(raw, pe, w1, w2)


def _nsa_cmp_kernel(q_ref, kc_ref, vc_ref, bias_ref, gates_ref, ovt_ref, oc_ref, sel_ref, *, t, n_slots, n_sel,
                    n_top, scale):
    i = pl.program_id(2)
    kc = kc_ref[0, 0, 0]
    vc = vc_ref[0, 0, 0]
    qpos = i * t + lax.broadcasted_iota(jnp.int32, (t, n_slots), 0)
    cmp_end = NSA_CMP_STRIDE * lax.broadcasted_iota(jnp.int32, (t, n_slots), 1) + (NSA_CMP_BLOCK - 1)
    valid = cmp_end <= qpos
    gates = gates_ref[...]
    psum = jnp.zeros((t, n_slots), F32)
    for r in range(NSA_REP):
        q = q_ref[:, r * HEAD_DIM:(r + 1) * HEAD_DIM]
        s = jnp.where(valid, _dot_nt(q, kc) * scale + bias_ref[r], NEG)
        m = jnp.max(s, axis=-1, keepdims=True)
        p = jnp.where(valid, jnp.exp(s - m), 0.0)
        p = p / jnp.maximum(jnp.sum(p, axis=-1, keepdims=True), 1e-30)
        psum = psum + p
        oc_ref[:, r * HEAD_DIM:(r + 1) * HEAD_DIM] = gates[:, r:r + 1] * _dot(p.astype(BF16), vc)

    hi, lo = _split_bf16(psum)
    ovt = ovt_ref[...]
    imp = _dot_nt(ovt, hi) + _dot_nt(ovt, lo)
    blk = lax.broadcasted_iota(jnp.int32, (n_sel, t), 0)
    tpos = i * t + lax.broadcasted_iota(jnp.int32, (n_sel, t), 1)
    cur = tpos // NSA_SEL_BLOCK
    forced = (blk == 0) | (blk == cur) | (blk == cur - 1)
    score = jnp.where(blk * NSA_SEL_BLOCK <= tpos, jnp.where(forced, FORCED_SCORE, imp), -1.0)
    rank = jnp.zeros((n_sel, t), F32)
    for mm in range(n_sel):
        sm = score[mm:mm + 1, :]
        ahead = (sm > score) | ((sm == score) & (blk > mm))
        rank = rank + jnp.where(ahead, 1.0, 0.0)
    sel_t = jnp.where(rank < n_top, 1.0, 0.0).astype(BF16)
    row, col = _tile_iota(t)
    eye = jnp.where(row == col, 1.0, 0.0).astype(BF16)
    sel_ref[0, 0] = _dot_nt(eye, sel_t).astype(sel_ref.dtype)


def _nsa_cmp_attention(q_all, kvc, bias_c, gates, *, batch, seq_len, t=ATT_T):
    g = NSA_GROUPS
    nq = seq_len // t
    n_slots = seq_len // NSA_CMP_STRIDE
    n_sel = seq_len // NSA_SEL_BLOCK
    c0 = NSA_CMP_STRIDE * np.arange(n_slots)[:, None]
    s0 = NSA_SEL_BLOCK * np.arange(n_sel)[None, :]
    overlap = (c0 < s0 + NSA_SEL_BLOCK) & (c0 + NSA_CMP_BLOCK > s0)
    ovt = jnp.asarray(overlap.T.astype(np.float32), dtype=BF16)
    gw = NSA_REP * HEAD_DIM
    return pl.pallas_call(
        functools.partial(_nsa_cmp_kernel, t=t, n_slots=n_slots, n_sel=n_sel, n_top=min(NSA_TOPN, n_sel),
                          scale=HEAD_DIM ** -0.5),
        grid=(batch, g, nq),
        in_specs=[pl.BlockSpec((t, gw), lambda b, gg, i: (b * nq + i, gg)),
                  pl.BlockSpec((1, 1, 1, n_slots, HEAD_DIM), lambda b, gg, i: (b, 0, gg, 0, 0)),
                  pl.BlockSpec((1, 1, 1, n_slots, HEAD_DIM), lambda b, gg, i: (b, 1, gg, 0, 0)),
                  pl.BlockSpec((NSA_REP, t, n_slots), lambda b, gg, i: (gg, i, 0)),
                  pl.BlockSpec((t, LANE), lambda b, gg, i: (b * nq + i, gg)),
                  pl.BlockSpec((n_sel, n_slots), lambda b, gg, i: (0, 0))],
        out_specs=[pl.BlockSpec((t, gw), lambda b, gg, i: (b * nq + i, gg)),
                   pl.BlockSpec((1, 1, t, n_sel), lambda b, gg, i: (b, gg, i, 0))],
        out_shape=[jax.ShapeDtypeStruct((batch * seq_len, g * gw), F32),
                   jax.ShapeDtypeStruct((batch, g, seq_len, n_sel), BF16)],
        compiler_params=_params("parallel", "parallel", "arbitrary"),
        name="nsa_cmp_attention",
    )(q_all, kvc, kvc, bias_c, gates, ovt)


def _nsa_main_kernel(q_ref, ks_ref, vs_ref, kw_ref, vw_ref, sel_ref, bias_ref, far_ref, gates_ref, oc_ref,
                     o_ref, *, t, scale):
    i = pl.program_id(2)
    rep = NSA_REP
    stack = lambda x: jnp.concatenate([x] * rep, axis=0)
    q = jnp.concatenate([q_ref[:, r * HEAD_DIM:(r + 1) * HEAD_DIM] for r in range(rep)], axis=0)
    row, col = _tile_iota(t)
    causal = stack(col <= row)
    bias_d = jnp.concatenate([bias_ref[r, :t, :] for r in range(rep)], axis=0)
    bias_s = jnp.concatenate([bias_ref[r, t:, :] for r in range(rep)], axis=0)
    bias_f = jnp.concatenate([jnp.broadcast_to(far_ref[r, 0:1, :], (t, t)) for r in range(rep)], axis=0)

    def tile_bias(kb):
        return jnp.where(kb == i, bias_d, jnp.where(kb == i - 1, bias_s, bias_f))

    sel = sel_ref[0, 0]
    n_sel = sel.shape[1]
    blk_row = lax.broadcasted_iota(jnp.int32, (n_sel, t), 0)
    key_col = lax.broadcasted_iota(jnp.int32, (n_sel, t), 1)

    def sel_tile(kb, carry, diag):
        start = pl.multiple_of(kb * t, t)
        expand = jnp.where((start + key_col) // NSA_SEL_BLOCK == blk_row, 1.0, 0.0).astype(BF16)
        picked = stack(_dot(sel, expand)) > 0.5
        if diag:
            picked = picked & causal
        s = _dot_nt(q, ks_ref[pl.ds(start, t), :]) * scale + tile_bias(kb)
        s = jnp.where(picked, s, NEG)
        return _osm(s, vs_ref[pl.ds(start, t), :], *carry)

    carry = lax.fori_loop(0, i, lambda kb, cr: sel_tile(kb, cr, False), _osm_init(rep * t, HEAD_DIM))
    _, l_s, acc_s = sel_tile(i, carry, True)
    o_s = acc_s / l_s

    n_back = NSA_WINDOW // t
    edge = stack(col > row)

    def win_tile(kb, carry, mask):
        start = pl.multiple_of(kb * t, t)
        s = _dot_nt(q, kw_ref[pl.ds(start, t), :]) * scale + tile_bias(kb)
        if mask is not None:
            s = jnp.where(mask, s, NEG)
        return _osm(s, vw_ref[pl.ds(start, t), :], *carry)

    carry = win_tile(i, _osm_init(rep * t, HEAD_DIM), causal)
    lo = jnp.maximum(i - n_back + 1, 0)
    carry = lax.fori_loop(lo, i, lambda kb, cr: win_tile(kb, cr, None), carry)
    carry = lax.cond(i >= n_back, lambda cr: win_tile(i - n_back, cr, edge), lambda cr: cr, carry)
    _, l_w, acc_w = carry
    o_w = acc_w / l_w

    gates = gates_ref[...]
    for r in range(rep):
        rows = slice(r * t, (r + 1) * t)
        cols = slice(r * HEAD_DIM, (r + 1) * HEAD_DIM)
        o = oc_ref[:, cols] + gates[:, rep + r:rep + r + 1] * o_s[rows] + gates[:, 2 * rep + r:2 * rep + r + 1] * o_w[rows]
        o_ref[:, cols] = o.astype(o_ref.dtype)


def _nsa_main_attention(qkv, sel, bias_near, bias_far, gates, oc, *, batch, seq_len, t=ATT_T):
    g = NSA_GROUPS
    nq = seq_len // t
    n_sel = seq_len // NSA_SEL_BLOCK
    gw = NSA_REP * HEAD_DIM
    qb = N_HEADS
    kv = lambda which: pl.BlockSpec((seq_len, HEAD_DIM), lambda b, gg, i: (b, qb + which * g + gg))
    tile = pl.BlockSpec((t, gw), lambda b, gg, i: (b * nq + i, gg))
    return pl.pallas_call(
        functools.partial(_nsa_main_kernel, t=t, scale=HEAD_DIM ** -0.5),
        grid=(batch, g, nq),
        in_specs=[tile, kv(0), kv(1), kv(2), kv(3),
                  pl.BlockSpec((1, 1, t, n_sel), lambda b, gg, i: (b, gg, i, 0)),
                  pl.BlockSpec((NSA_REP, 2 * t, t), lambda b, gg, i: (gg, 0, 0)),
                  pl.BlockSpec((NSA_REP, 8, t), lambda b, gg, i: (gg, 0, 0)),
                  pl.BlockSpec((t, LANE), lambda b, gg, i: (b * nq + i, gg)),
                  tile],
        out_specs=tile,
        out_shape=jax.ShapeDtypeStruct((batch * seq_len, g * gw), BF16),
        compiler_params=_params("parallel", "parallel", "arbitrary"),
        name="nsa_main_attention",
    )(qkv, qkv, qkv, qkv, qkv, sel, bias_near, bias_far, gates, oc)


def _rope_tables(seq_len, width):
    half = MLA_ROPE // 2
    inv = np.power(ROPE_THETA, -np.arange(half, dtype=np.float32) / half).astype(np.float32)
    ang = np.arange(seq_len, dtype=np.float32)[:, None] * inv[None, :]
    pad = np.zeros((seq_len, LANE - MLA_ROPE), np.float32)
    cos = np.concatenate([np.cos(ang), np.cos(ang), pad], axis=1)
    sin = np.concatenate([np.sin(ang), np.sin(ang), pad], axis=1)
    reps = width // LANE
    return jnp.asarray(np.tile(cos, (1, reps))), jnp.asarray(np.tile(sin, (1, reps)))


def _rope_weights(w):
    k, n, _ = w.shape
    half = MLA_ROPE // 2
    pad = jnp.zeros((k, n, LANE - MLA_ROPE), w.dtype)
    wa = jnp.concatenate([w, pad], axis=-1)
    wb = jnp.concatenate([-w[..., half:], w[..., :half], pad], axis=-1)
    return wa.reshape(k, n * LANE).astype(BF16), wb.reshape(k, n * LANE).astype(BF16)


def kernel(x, c, t5_bias, ada_w, ada_b, norm_g, final_g, ffn_w_up, ffn_conv_w, ffn_conv_b, ffn_w_down, sb_w_in, sb_w_out, nsa_w_in, nsa_cmp_pe, nsa_cmp_w1, nsa_cmp_w2, nsa_w_out, diff_w_in, diff_lambda, diff_head_g, diff_w_out, mla_w_in, mla_q_g, mla_w_qb, mla_kv_g, mla_w_kvb, mla_w_out):
    batch, seq_len, d = x.shape
    depth = ada_w.shape[0]
    h, dh, g = N_HEADS, HEAD_DIM, NSA_GROUPS
    sizes = dict(batch=batch, seq_len=seq_len)

    mod = _ada_mod(c, ada_w, ada_b)
    bias_near, bias_far = _toeplitz_bias(t5_bias, ATT_T)

    xf = x.reshape(batch * seq_len, d)
    for i in range(depth):
        mixer, j = i % 4, i // 4
        sh1, sc1, gt1, sh2, sc2, gt2 = (mod[i, :, n * d:(n + 1) * d].reshape(batch, 1, d) for n in range(6))
        nm = functools.partial(_norm_mm, xf, norm_g[i, 0], seq_len=seq_len, shift=sh1, scale=sc1)
        if mixer == 0:
            qkv = nm(sb_w_in[j].astype(BF16), name="sb_in")
            o = _sb_attention(qkv, **sizes)
            w_out = sb_w_out[j]
        elif mixer == 1:
            w_in = nsa_w_in[j]
            n_q, n_kv = h * dh, g * dh
            w_att = jnp.concatenate([w_in[:, :n_q], w_in[:, n_q + 2 * n_kv:n_q + 6 * n_kv]], axis=1)
            w_cmp = w_in[:, n_q:n_q + 2 * n_kv]
            w_g = w_in[:, n_q + 6 * n_kv:].reshape(d, 3, g, NSA_REP).transpose(0, 2, 1, 3).reshape(d, g, 3 * NSA_REP)
            w_g = jnp.pad(w_g, ((0, 0), (0, 0), (0, LANE - 3 * NSA_REP))).reshape(d, g * LANE)
            qkv = nm(w_att.astype(BF16), name="nsa_in")
            raw = nm(w_cmp.astype(BF16), out_dtype=F32, name="nsa_in_cmp")
            gates = nm(w_g.astype(BF16), out_dtype=F32, act="sigmoid", name="nsa_in_gates")
            kvc = _nsa_compress(raw, nsa_cmp_pe[j], nsa_cmp_w1[j].reshape(2, NSA_CMP_BLOCK, dh, dh).astype(BF16),
                                nsa_cmp_w2[j].astype(BF16), **sizes)
            n_slots = seq_len // NSA_CMP_STRIDE
            dist_c = np.arange(seq_len)[:, None] - (NSA_CMP_STRIDE * np.arange(n_slots)[None, :] + NSA_CMP_BLOCK - 1)
            bias_c = _t5_gather(t5_bias, _t5_bucket_np(dist_c), tr=min(seq_len, 512))
            oc, sel = _nsa_cmp_attention(qkv, kvc, bias_c, gates, **sizes)
            o = _nsa_main_attention(qkv, sel, bias_near, bias_far, gates, oc, **sizes)
            w_out = nsa_w_out[j]
        elif mixer == 2:
            lambda_init = 0.8 - 0.6 * math.exp(-0.3 * i)
            qkv = nm(diff_w_in[j].astype(BF16), name="diff_in")
            o = _diff_attention(qkv, bias_near, bias_far, diff_lambda[j], diff_head_g[j],
                                lambda_init=lambda_init, **sizes)
            w_out = diff_w_out[j]
        else:
            w_in = mla_w_in[j]
            nq_l, nkv_l = MLA_Q_LORA, MLA_KV_LORA
            w_lat = jnp.concatenate([w_in[:, :nq_l], jnp.zeros((d, 2 * nkv_l - nq_l), w_in.dtype),
                                     w_in[:, nq_l:nq_l + nkv_l]], axis=1)
            lat = nm(w_lat.astype(BF16), out_dtype=F32, name="mla_in")
            cos1, sin1 = _rope_tables(seq_len, LANE)
            wa, wb = _rope_weights(w_in[:, nq_l + nkv_l:].reshape(d, 1, MLA_ROPE))
            kr = nm(wa, wb=wb, cos=cos1, sin=sin1, name="mla_in_rope")
            w_qb = mla_w_qb[j].reshape(nq_l, h, MLA_NOPE + MLA_ROPE)
            qn = _norm_mm(lat, mla_q_g[j], w_qb[:, :, :MLA_NOPE].reshape(nq_l, h * MLA_NOPE).astype(BF16),
                          seq_len=seq_len, x_cols=nq_l, x_col_block=0, name="mla_q_nope")
            cosh, sinh = _rope_tables(seq_len, h * LANE)
            wa, wb = _rope_weights(w_qb[:, :, MLA_NOPE:])
            qr = _norm_mm(lat, mla_q_g[j], wa, wb=wb, cos=cosh, sin=sinh, seq_len=seq_len, x_cols=nq_l,
                          x_col_block=0, name="mla_q_rope")
            kv = _norm_mm(lat, mla_kv_g[j], mla_w_kvb[j].astype(BF16), seq_len=seq_len, x_cols=nkv_l,
                          x_col_block=2, name="mla_kv")
            o = _mla_attention(qn, qr, kv, kr, **sizes)
            w_out = mla_w_out[j]
        xf = _mm_residual(o, w_out.astype(BF16), xf, gt1, seq_len=seq_len)
        xf = _conv_ffn(xf, norm_g[i, 1], sh2, sc2, ffn_w_up[i].astype(BF16), ffn_conv_w[i], ffn_conv_b[i],
                       ffn_w_down[i].astype(BF16), gt2, final_g, seq_len=seq_len, final_norm=(i == depth - 1))
    return xf.reshape(batch, seq_len, d)
```

```python
import functools
import math

import numpy as np
import jax
import jax.numpy as jnp
from jax import lax
from jax.experimental import pallas as pl
from jax.experimental.pallas import tpu as pltpu

F32 = jnp.float32
BF16 = jnp.bfloat16
EPS = 1e-6
NEG = -1e30

LANE = 128
HALO = 16
VMEM_LIMIT = 56 * 2**20

T5_BUCKETS = 32
T5_MAX_DIST = 128
N_HEADS = 16
HEAD_DIM = 128
NSA_GROUPS = 4
NSA_REP = 4
NSA_CMP_BLOCK = 32
NSA_CMP_STRIDE = 16
NSA_SEL_BLOCK = 64
NSA_TOPN = 16
NSA_WINDOW = 512
FORCED_SCORE = 1e9
DIFF_DIM = 64
MLA_Q_LORA = 768
MLA_KV_LORA = 512
MLA_NOPE = 128
MLA_ROPE = 64
ROPE_THETA = 10000.0
CONV_WIDTH = 3
ATT_T = 256
ATT_TQ = 512
ATT_TK = 256


def _params(*sem):
    return pltpu.CompilerParams(dimension_semantics=sem, vmem_limit_bytes=VMEM_LIMIT)


def _dot(a, b):
    return jnp.dot(a, b, preferred_element_type=F32)


def _dot_nt(a, b):
    return lax.dot_general(a, b, (((1,), (1,)), ((), ())), preferred_element_type=F32)


def _split_bf16(x):
    hi = x.astype(BF16)
    lo = (x - hi.astype(F32)).astype(BF16)
    return hi, lo


def _rms(x, g):
    ms = jnp.mean(x * x, axis=-1, keepdims=True)
    return x * lax.rsqrt(ms + EPS) * g


def _ada_kernel(c_ref, w_ref, b_ref, o_ref):
    c = c_ref[...]
    a = c * jax.nn.sigmoid(c)
    a_hi, a_lo = _split_bf16(a)
    w_hi, w_lo = _split_bf16(w_ref[0])
    o_ref[0] = _dot(a_hi, w_hi) + _dot(a_lo, w_hi) + _dot(a_hi, w_lo) + b_ref[0]


def _ada_mod(c, ada_w, ada_b, tn=1024):
    depth, d, n = ada_w.shape
    b = c.shape[0]
    return pl.pallas_call(
        _ada_kernel,
        grid=(depth, n // tn),
        in_specs=[pl.BlockSpec((b, d), lambda l, j: (0, 0)),
                  pl.BlockSpec((1, d, tn), lambda l, j: (l, 0, j)),
                  pl.BlockSpec((1, 1, tn), lambda l, j: (l, 0, j))],
        out_specs=pl.BlockSpec((1, b, tn), lambda l, j: (l, 0, j)),
        out_shape=jax.ShapeDtypeStruct((depth, b, n), F32),
        compiler_params=_params("parallel", "parallel"),
        name="ada_mod",
    )(c, ada_w, ada_b.reshape(depth, 1, n))


def _norm_mm_kernel(*refs, modulated, rope, act):
    it = iter(refs)
    x_ref, g_ref = next(it), next(it)
    sh_ref = sc_ref = wb_ref = cos_ref = sin_ref = None
    if modulated:
        sh_ref, sc_ref = next(it), next(it)
    w_ref = next(it)
    if rope:
        wb_ref, cos_ref, sin_ref = next(it), next(it), next(it)
    o_ref, h_ref = next(it), next(it)

    @pl.when(pl.program_id(1) == 0)
    def _():
        y = _rms(x_ref[...], g_ref[...])
        if modulated:
            y = y * (1.0 + sc_ref[0]) + sh_ref[0]
        h_ref[...] = y.astype(BF16)

    h = h_ref[...]
    acc = _dot(h, w_ref[...])
    if rope:
        acc = acc * cos_ref[...] + _dot(h, wb_ref[...]) * sin_ref[...]
    if act == "sigmoid":
        acc = jax.nn.sigmoid(acc)
    o_ref[...] = acc.astype(o_ref.dtype)


def _norm_mm(x, g, w, *, seq_len, shift=None, scale=None, wb=None, cos=None, sin=None, act=None,
             out_dtype=BF16, x_cols=None, x_col_block=0, tm=512, tn=512, name="norm_mm"):
    m = x.shape[0]
    k = x.shape[1] if x_cols is None else x_cols
    n = w.shape[1]
    tn = min(tn, n)
    modulated, rope = shift is not None, wb is not None
    tps = seq_len // tm
    in_specs = [pl.BlockSpec((tm, k), lambda i, j: (i, x_col_block)),
                pl.BlockSpec((1, k), lambda i, j: (0, 0))]
    args = [x, g.reshape(1, k)]
    if modulated:
        in_specs += [pl.BlockSpec((1, 1, k), lambda i, j: (i // tps, 0, 0))] * 2
        args += [shift, scale]
    in_specs.append(pl.BlockSpec((k, tn), lambda i, j: (0, j)))
    args.append(w)
    if rope:
        in_specs.append(pl.BlockSpec((k, tn), lambda i, j: (0, j)))
        in_specs += [pl.BlockSpec((tm, tn), lambda i, j: (i % tps, j))] * 2
        args += [wb, cos, sin]
    return pl.pallas_call(
        functools.partial(_norm_mm_kernel, modulated=modulated, rope=rope, act=act),
        grid=(m // tm, n // tn),
        in_specs=in_specs,
        out_specs=pl.BlockSpec((tm, tn), lambda i, j: (i, j)),
        out_shape=jax.ShapeDtypeStruct((m, n), out_dtype),
        scratch_shapes=[pltpu.VMEM((tm, k), BF16)],
        compiler_params=_params("parallel", "arbitrary"),
        name=name,
    )(*args)


def _mm_res_kernel(a_ref, w_ref, x_ref, gate_ref, o_ref):
    o_ref[...] = x_ref[...] + gate_ref[0] * _dot(a_ref[...], w_ref[...])


def _mm_residual(a, w, x, gate, *, seq_len, tm=1024, tn=512, name="mm_residual"):
    m, k = a.shape
    n = w.shape[1]
    tps = seq_len // tm
    return pl.pallas_call(
        _mm_res_kernel,
        grid=(m // tm, n // tn),
        in_specs=[pl.BlockSpec((tm, k), lambda i, j: (i, 0)),
                  pl.BlockSpec((k, tn), lambda i, j: (0, j)),
                  pl.BlockSpec((tm, tn), lambda i, j: (i, j)),
                  pl.BlockSpec((1, 1, tn), lambda i, j: (i // tps, 0, j))],
        out_specs=pl.BlockSpec((tm, tn), lambda i, j: (i, j)),
        out_shape=jax.ShapeDtypeStruct((m, n), F32),
        compiler_params=_params("parallel", "arbitrary"),
        name=name,
    )(a, w, x, gate)


def _ffn_kernel(x_ref, xp_ref, g_ref, sh_ref, sc_ref, wg_ref, wu_ref, cwg_ref, cwu_ref, cbg_ref, cbu_ref,
                wd_ref, gate_ref, fg_ref, o_ref, h_ref, acc_ref, *, tiles_per_seq, final_norm):
    i, j = pl.program_id(0), pl.program_id(1)

    @pl.when(j == 0)
    def _():
        def nm(x):
            return _rms(x, g_ref[...]) * (1.0 + sc_ref[0]) + sh_ref[0]
        h_ref[HALO:, :] = nm(x_ref[...]).astype(BF16)
        keep = jnp.where(i % tiles_per_seq == 0, 0.0, 1.0)
        h_ref[:HALO, :] = (nm(xp_ref[...]) * keep).astype(BF16)
        acc_ref[...] = jnp.zeros_like(acc_ref)

    h = h_ref[...]

    def branch(w_ref, cw_ref, cb_ref):
        u = _dot(h, w_ref[...])
        u1 = pltpu.roll(u, 1, 0)
        u2 = pltpu.roll(u, 2, 0)
        cw = cw_ref[...]
        y = cw[0:1] * u2 + cw[1:2] * u1 + cw[2:3] * u + cb_ref[...]
        return y[HALO:]

    gt = branch(wg_ref, cwg_ref, cbg_ref)
    up = branch(wu_ref, cwu_ref, cbu_ref)
    a = gt * jax.nn.sigmoid(gt) * up
    acc_ref[...] += _dot(a.astype(BF16), wd_ref[...])

    @pl.when(j == pl.num_programs(1) - 1)
    def _():
        y = x_ref[...] + gate_ref[0] * acc_ref[...]
        if final_norm:
            y = _rms(y, fg_ref[...])
        o_ref[...] = y


def _conv_ffn(x, g, shift, scale, w_up, conv_w, conv_b, w_down, gate, final_g, *, seq_len, final_norm,
              tm=512, tf=512):
    m, d = x.shape
    f = w_down.shape[0]
    nf = f // tf
    tps = seq_len // tm
    hb = tm // HALO
    row = lambda i, j: (i, 0)
    per_batch = lambda i, j: (i // tps, 0, 0)
    return pl.pallas_call(
        functools.partial(_ffn_kernel, tiles_per_seq=tps, final_norm=final_norm),
        grid=(m // tm, nf),
        in_specs=[pl.BlockSpec((tm, d), row),
                  pl.BlockSpec((HALO, d), lambda i, j: (jnp.maximum(i * hb - 1, 0), 0)),
                  pl.BlockSpec((1, d), lambda i, j: (0, 0)),
                  pl.BlockSpec((1, 1, d), per_batch),
                  pl.BlockSpec((1, 1, d), per_batch),
                  pl.BlockSpec((d, tf), lambda i, j: (0, j)),
                  pl.BlockSpec((d, tf), lambda i, j: (0, j + nf)),
                  pl.BlockSpec((CONV_WIDTH, tf), lambda i, j: (0, j)),
                  pl.BlockSpec((CONV_WIDTH, tf), lambda i, j: (0, j + nf)),
                  pl.BlockSpec((1, tf), lambda i, j: (0, j)),
                  pl.BlockSpec((1, tf), lambda i, j: (0, j + nf)),
                  pl.BlockSpec((tf, d), lambda i, j: (j, 0)),
                  pl.BlockSpec((1, 1, d), per_batch),
                  pl.BlockSpec((1, d), lambda i, j: (0, 0))],
        out_specs=pl.BlockSpec((tm, d), row),
        out_shape=jax.ShapeDtypeStruct((m, d), F32),
        scratch_shapes=[pltpu.VMEM((HALO + tm, d), BF16), pltpu.VMEM((tm, d), F32)],
        compiler_params=_params("parallel", "arbitrary"),
        name="conv_ffn",
    )(x, x, g.reshape(1, d), shift, scale, w_up, w_up, conv_w, conv_w, conv_b.reshape(1, -1),
      conv_b.reshape(1, -1), w_down, gate, final_g.reshape(1, d))


def _t5_bucket_np(dist):
    n = np.maximum(dist, 0)
    max_exact = T5_BUCKETS // 2
    nf = np.maximum(n, 1).astype(np.float64)
    val = np.log(nf / max_exact) / math.log(T5_MAX_DIST / max_exact) * (T5_BUCKETS - max_exact)
    large = max_exact + np.trunc(val + 1e-6).astype(np.int64)
    return np.where(n < max_exact, n, np.minimum(large, T5_BUCKETS - 1)).astype(np.int32)


def _t5_gather_kernel(t5_ref, bkt_ref, o_ref):
    h = pl.program_id(0)
    bk = bkt_ref[...]
    acc = jnp.zeros(bk.shape, F32)
    for b in range(T5_BUCKETS):
        acc = jnp.where(bk == b, t5_ref[b, h], acc)
    o_ref[0] = acc


def _t5_gather(t5_bias, bkt, tr):
    rows, cols = bkt.shape
    heads = t5_bias.shape[1]
    return pl.pallas_call(
        _t5_gather_kernel,
        grid=(heads, rows // tr),
        in_specs=[pl.BlockSpec(memory_space=pltpu.SMEM),
                  pl.BlockSpec((tr, cols), lambda h, r: (r, 0))],
        out_specs=pl.BlockSpec((1, tr, cols), lambda h, r: (h, r, 0)),
        out_shape=jax.ShapeDtypeStruct((heads, rows, cols), F32),
        compiler_params=_params("parallel", "parallel"),
        name="t5_gather",
    )(t5_bias, jnp.asarray(bkt))


def _toeplitz_bias(t5_bias, t):
    i = np.arange(t)[:, None]
    j = np.arange(t)[None, :]
    bkt = np.concatenate([_t5_bucket_np(i - j), _t5_bucket_np(t + i - j)], axis=0)
    assert int(_t5_bucket_np(np.array(t + 1))) == T5_BUCKETS - 1
    near = _t5_gather(t5_bias, bkt, tr=2 * t)
    far = _t5_gather(t5_bias, np.full((8, t), T5_BUCKETS - 1, np.int32), tr=8)
    return near, far


def _osm(s, v, m, l, acc):
    m_new = jnp.maximum(m, jnp.max(s, axis=-1, keepdims=True))
    p = jnp.exp(s - m_new)
    alpha = jnp.exp(m - m_new)
    l = alpha * l + jnp.sum(p, axis=-1, keepdims=True)
    acc = alpha * acc + _dot(p.astype(BF16), v)
    return m_new, l, acc


def _osm_init(rows, dv):
    return jnp.full((rows, 1), NEG, F32), jnp.zeros((rows, 1), F32), jnp.zeros((rows, dv), F32)


def _tile_iota(t):
    return lax.broadcasted_iota(jnp.int32, (t, t), 0), lax.broadcasted_iota(jnp.int32, (t, t), 1)


def _loop2(hi, body, carry):
    carry = lax.fori_loop(0, hi // 2, lambda p, cr: body(2 * p + 1, body(2 * p, cr)), carry)
    return lax.cond(hi % 2 == 1, lambda cr: body(hi - 1, cr), lambda cr: cr, carry)


def _rect_iota(i, tq, start, tk):
    rowg = i * tq + lax.broadcasted_iota(jnp.int32, (tq, tk), 0)
    colg = start + lax.broadcasted_iota(jnp.int32, (tq, tk), 1)
    return rowg, colg


def _sb_kernel(q_ref, k_ref, v_ref, o_ref, *, tq, tk, scale):
    i = pl.program_id(2)
    q = q_ref[...]
    row, col = _tile_iota(tk)
    upper = jnp.where(row > col, 1.0, 0.0).astype(BF16)

    def tile(kb, carry, diag):
        c, acc = carry
        start = pl.multiple_of(kb * tk, tk)
        z = _dot_nt(q, k_ref[pl.ds(start, tk), :]) * scale
        lk = -(jnp.maximum(z, 0.0) + jnp.log1p(jnp.exp(-jnp.abs(z))))
        if diag:
            rowg, colg = _rect_iota(i, tq, start, tk)
            past = colg < rowg
            lk = jnp.where(past, lk, 0.0)
        hi, lo = _split_bf16(lk)
        later = c + _dot(hi, upper) + _dot(lo, upper)
        a = jnp.exp(z + lk + later)
        if diag:
            a = jnp.where(past, a, 0.0)
        acc = acc + _dot(a.astype(BF16), v_ref[pl.ds(start, tk), :])
        c = c + jnp.sum(lk, axis=-1, keepdims=True)
        return c, acc

    ratio = tq // tk
    carry = (jnp.zeros((tq, 1), F32), jnp.zeros((tq, HEAD_DIM), F32))
    for dd in range(ratio):
        carry = tile((i + 1) * ratio - 1 - dd, carry, True)

    def full_tiles(n, cr):
        for dd in range(ratio):
            cr = tile((i - n) * ratio - 1 - dd, cr, False)
        return cr

    carry = lax.fori_loop(0, i, full_tiles, carry)
    o_ref[...] = carry[1].astype(o_ref.dtype)


def _sb_attention(qkv, *, batch, seq_len, tq=ATT_TQ, tk=ATT_TK):
    h = N_HEADS
    nq = seq_len // tq
    return pl.pallas_call(
        functools.partial(_sb_kernel, tq=tq, tk=tk, scale=HEAD_DIM ** -0.5),
        grid=(batch, h, nq),
        in_specs=[pl.BlockSpec((tq, HEAD_DIM), lambda b, hh, i: (b * nq + i, hh)),
                  pl.BlockSpec((seq_len, HEAD_DIM), lambda b, hh, i: (b, h + hh)),
                  pl.BlockSpec((seq_len, HEAD_DIM), lambda b, hh, i: (b, 2 * h + hh))],
        out_specs=pl.BlockSpec((tq, HEAD_DIM), lambda b, hh, i: (b * nq + i, hh)),
        out_shape=jax.ShapeDtypeStruct((batch * seq_len, h * HEAD_DIM), BF16),
        compiler_params=_params("parallel", "parallel", "arbitrary"),
        name="sb_attention",
    )(qkv, qkv, qkv)


def _diff_kernel(q_ref, k_ref, v_ref, bias_ref, far_ref, lam_ref, hg_ref, o_ref, *, t, scale, lambda_init):
    i = pl.program_id(2)
    q = q_ref[...]
    lane = lax.broadcasted_iota(jnp.int32, q.shape, 1)
    zero = jnp.zeros_like(q)
    q2 = jnp.concatenate([jnp.where(lane < DIFF_DIM, q, zero), jnp.where(lane >= DIFF_DIM, q, zero)], axis=0)
    row, col = _tile_iota(t)
    causal = jnp.concatenate([col <= row] * 2, axis=0)
    bias_d = jnp.concatenate([bias_ref[0, :t, :]] * 2, axis=0)
    bias_s = jnp.concatenate([bias_ref[0, t:, :]] * 2, axis=0)
    bias_f = far_ref[0, 0:1, :]

    def tile(kb, carry, bias, mask):
        start = pl.multiple_of(kb * t, t)
        s = _dot_nt(q2, k_ref[pl.ds(start, t), :]) * scale + bias
        if mask:
            s = jnp.where(causal, s, NEG)
        return _osm(s, v_ref[pl.ds(start, t), :], *carry)

    carry = _osm_init(2 * t, HEAD_DIM)
    carry = _loop2(jnp.maximum(i - 1, 0), lambda kb, cr: tile(kb, cr, bias_f, False), carry)
    carry = lax.cond(i > 0, lambda cr: tile(i - 1, cr, bias_s, False), lambda cr: cr, carry)
    m, l, acc = tile(i, carry, bias_d, True)

    lam = lam_ref[...]
    lmbda = (jnp.exp(jnp.sum(lam[0:1] * lam[1:2], axis=-1, keepdims=True))
             - jnp.exp(jnp.sum(lam[2:3] * lam[3:4], axis=-1, keepdims=True)) + lambda_init)
    o = acc / l
    o = o[:t] - lmbda * o[t:]
    o_ref[...] = (_rms(o, hg_ref[...]) * (1.0 - lambda_init)).astype(o_ref.dtype)


def _diff_attention(qkv, bias_near, bias_far, lam, head_g, *, batch, seq_len, lambda_init, t=ATT_T):
    h = N_HEADS
    nq = seq_len // t
    return pl.pallas_call(
        functools.partial(_diff_kernel, t=t, scale=DIFF_DIM ** -0.5, lambda_init=lambda_init),
        grid=(batch, h, nq),
        in_specs=[pl.BlockSpec((t, HEAD_DIM), lambda b, hh, i: (b * nq + i, hh)),
                  pl.BlockSpec((seq_len, HEAD_DIM), lambda b, hh, i: (b, h + hh)),
                  pl.BlockSpec((seq_len, HEAD_DIM), lambda b, hh, i: (b, 2 * h + hh)),
                  pl.BlockSpec((1, 2 * t, t), lambda b, hh, i: (hh, 0, 0)),
                  pl.BlockSpec((1, 8, t), lambda b, hh, i: (hh, 0, 0)),
                  pl.BlockSpec((4, DIFF_DIM), lambda b, hh, i: (0, 0)),
                  pl.BlockSpec((1, HEAD_DIM), lambda b, hh, i: (0, 0))],
        out_specs=pl.BlockSpec((t, HEAD_DIM), lambda b, hh, i: (b * nq + i, hh)),
        out_shape=jax.ShapeDtypeStruct((batch * seq_len, h * HEAD_DIM), BF16),
        compiler_params=_params("parallel", "parallel", "arbitrary"),
        name="diff_attention",
    )(qkv, qkv, qkv, bias_near, bias_far, lam, head_g.reshape(1, HEAD_DIM))


def _mla_kernel(qn_ref, qr_ref, kn_ref, kr_ref, v_ref, o_ref, *, tq, tk, scale):
    i = pl.program_id(2)
    q = jnp.concatenate([qn_ref[...], qr_ref[...]], axis=-1)

    def tile(kb, carry, mask):
        start = pl.multiple_of(kb * tk, tk)
        k = jnp.concatenate([kn_ref[pl.ds(start, tk), :], kr_ref[pl.ds(start, tk), :]], axis=-1)
        s = _dot_nt(q, k) * scale
        if mask:
            rowg, colg = _rect_iota(i, tq, start, tk)
            s = jnp.where(colg <= rowg, s, NEG)
        return _osm(s, v_ref[pl.ds(start, tk), :], *carry)

    ratio = tq // tk
    def full_tiles(n, cr):
        for dd in range(ratio):
            cr = tile(n * ratio + dd, cr, False)
        return cr

    carry = lax.fori_loop(0, i, full_tiles, _osm_init(tq, HEAD_DIM))
    for dd in range(ratio):
        carry = tile(i * ratio + dd, carry, True)
    m, l, acc = carry
    o_ref[...] = (acc / l).astype(o_ref.dtype)


def _mla_attention(qn, qr, kv, kr, *, batch, seq_len, tq=ATT_TQ, tk=ATT_TK):
    h = N_HEADS
    nq = seq_len // tq
    qspec = pl.BlockSpec((tq, HEAD_DIM), lambda b, hh, i: (b * nq + i, hh))
    return pl.pallas_call(
        functools.partial(_mla_kernel, tq=tq, tk=tk, scale=(MLA_NOPE + MLA_ROPE) ** -0.5),
        grid=(batch, h, nq),
        in_specs=[qspec, qspec,
                  pl.BlockSpec((seq_len, HEAD_DIM), lambda b, hh, i: (b, 2 * hh)),
                  pl.BlockSpec((seq_len, HEAD_DIM), lambda b, hh, i: (b, 0)),
                  pl.BlockSpec((seq_len, HEAD_DIM), lambda b, hh, i: (b, 2 * hh + 1))],
        out_specs=qspec,
        out_shape=jax.ShapeDtypeStruct((batch * seq_len, h * HEAD_DIM), BF16),
        compiler_params=_params("parallel", "parallel", "arbitrary"),
        name="mla_attention",
    )(qn, qr, kv, kr, kv)


def _compress_kernel(raw_ref, pe_ref, w1_ref, w2_ref, o_ref, *, n_slots):
    half = NSA_CMP_BLOCK // 2
    p1 = jnp.zeros((n_slots, HEAD_DIM), F32)
    p2 = jnp.zeros((n_slots, HEAD_DIM), F32)
    for l in range(half):
        a = raw_ref[pl.ds(l, n_slots, stride=NSA_CMP_STRIDE), :]
        p1 = p1 + _dot((a + pe_ref[0, l:l + 1, :]).astype(BF16), w1_ref[0, l])
        p2 = p2 + _dot((a + pe_ref[0, half + l:half + l + 1, :]).astype(BF16), w1_ref[0, half + l])
    pre = p1 + pltpu.roll(p2, n_slots - 1, 0)
    hid = pre * jax.nn.sigmoid(pre)
    o_ref[0, 0, 0] = _dot(hid.astype(BF16), w2_ref[0]).astype(o_ref.dtype)


def _nsa_compress(raw, pe, w1, w2, *, batch, seq_len):
    g = NSA_GROUPS
    n_slots = seq_len // NSA_CMP_STRIDE
    return pl.pallas_call(
        functools.partial(_compress_kernel, n_slots=n_slots),
        grid=(batch, 2, g),
        in_specs=[pl.BlockSpec((seq_len, HEAD_DIM), lambda b, kv, gg: (b, kv * g + gg)),
                  pl.BlockSpec((1, NSA_CMP_BLOCK, HEAD_DIM), lambda b, kv, gg: (kv, 0, 0)),
                  pl.BlockSpec((1, NSA_CMP_BLOCK, HEAD_DIM, HEAD_DIM), lambda b, kv, gg: (kv, 0, 0, 0)),
                  pl.BlockSpec((1, HEAD_DIM, HEAD_DIM), lambda b, kv, gg: (kv, 0, 0))],
        out_specs=pl.BlockSpec((1, 1, 1, n_slots, HEAD_DIM), lambda b, kv, gg: (b, kv, gg, 0, 0)),
        out_shape=jax.ShapeDtypeStruct((batch, 2, g, n_slots, HEAD_DIM), BF16),
        compiler_params=_params("parallel", "parallel", "parallel"),
        name="nsa_compress",
    )(raw, pe, w1, w2)


def _nsa_cmp_kernel(q_ref, kc_ref, vc_ref, bias_ref, gates_ref, ovt_ref, oc_ref, sel_ref, *, t, n_slots, n_sel,
                    n_top, scale):
    i = pl.program_id(2)
    kc = kc_ref[0, 0, 0]
    vc = vc_ref[0, 0, 0]
    qpos = i * t + lax.broadcasted_iota(jnp.int32, (t, n_slots), 0)
    cmp_end = NSA_CMP_STRIDE * lax.broadcasted_iota(jnp.int32, (t, n_slots), 1) + (NSA_CMP_BLOCK - 1)
    valid = cmp_end <= qpos
    gates = gates_ref[...]
    psum = jnp.zeros((t, n_slots), F32)
    for r in range(NSA_REP):
        q = q_ref[:, r * HEAD_DIM:(r + 1) * HEAD_DIM]
        s = jnp.where(valid, _dot_nt(q, kc) * scale + bias_ref[r], NEG)
        m = jnp.max(s, axis=-1, keepdims=True)
        p = jnp.where(valid, jnp.exp(s - m), 0.0)
        p = p / jnp.maximum(jnp.sum(p, axis=-1, keepdims=True), 1e-30)
        psum = psum + p
        oc_ref[:, r * HEAD_DIM:(r + 1) * HEAD_DIM] = gates[:, r:r + 1] * _dot(p.astype(BF16), vc)

    hi, lo = _split_bf16(psum)
    ovt = ovt_ref[...]
    imp = _dot_nt(ovt, hi) + _dot_nt(ovt, lo)
    blk = lax.broadcasted_iota(jnp.int32, (n_sel, t), 0)
    tpos = i * t + lax.broadcasted_iota(jnp.int32, (n_sel, t), 1)
    cur = tpos // NSA_SEL_BLOCK
    forced = (blk == 0) | (blk == cur) | (blk == cur - 1)
    score = jnp.where(blk * NSA_SEL_BLOCK <= tpos, jnp.where(forced, FORCED_SCORE, imp), -1.0)
    rank = jnp.zeros((n_sel, t), F32)
    for mm in range(n_sel):
        sm = score[mm:mm + 1, :]
        ahead = (sm > score) | ((sm == score) & (blk > mm))
        rank = rank + jnp.where(ahead, 1.0, 0.0)
    sel_t = jnp.where(rank < n_top, 1.0, 0.0).astype(BF16)
    row, col = _tile_iota(t)
    eye = jnp.where(row == col, 1.0, 0.0).astype(BF16)
    sel_ref[0, 0] = _dot_nt(eye, sel_t).astype(sel_ref.dtype)


def _nsa_cmp_attention(q_all, kvc, bias_c, gates, *, batch, seq_len, t=ATT_T):
    g = NSA_GROUPS
    nq = seq_len // t
    n_slots = seq_len // NSA_CMP_STRIDE
    n_sel = seq_len // NSA_SEL_BLOCK
    c0 = NSA_CMP_STRIDE * np.arange(n_slots)[:, None]
    s0 = NSA_SEL_BLOCK * np.arange(n_sel)[None, :]
    overlap = (c0 < s0 + NSA_SEL_BLOCK) & (c0 + NSA_CMP_BLOCK > s0)
    ovt = jnp.asarray(overlap.T.astype(np.float32), dtype=BF16)
    gw = NSA_REP * HEAD_DIM
    return pl.pallas_call(
        functools.partial(_nsa_cmp_kernel, t=t, n_slots=n_slots, n_sel=n_sel, n_top=min(NSA_TOPN, n_sel),
                          scale=HEAD_DIM ** -0.5),
        grid=(batch, g, nq),
        in_specs=[pl.BlockSpec((t, gw), lambda b, gg, i: (b * nq + i, gg)),
                  pl.BlockSpec((1, 1, 1, n_slots, HEAD_DIM), lambda b, gg, i: (b, 0, gg, 0, 0)),
                  pl.BlockSpec((1, 1, 1, n_slots, HEAD_DIM), lambda b, gg, i: (b, 1, gg, 0, 0)),
                  pl.BlockSpec((NSA_REP, t, n_slots), lambda b, gg, i: (gg, i, 0)),
                  pl.BlockSpec((t, LANE), lambda b, gg, i: (b * nq + i, gg)),
                  pl.BlockSpec((n_sel, n_slots), lambda b, gg, i: (0, 0))],
        out_specs=[pl.BlockSpec((t, gw), lambda b, gg, i: (b * nq + i, gg)),
                   pl.BlockSpec((1, 1, t, n_sel), lambda b, gg, i: (b, gg, i, 0))],
        out_shape=[jax.ShapeDtypeStruct((batch * seq_len, g * gw), F32),
                   jax.ShapeDtypeStruct((batch, g, seq_len, n_sel), BF16)],
        compiler_params=_params("parallel", "parallel", "arbitrary"),
        name="nsa_cmp_attention",
    )(q_all, kvc, kvc, bias_c, gates, ovt)


def _nsa_main_kernel(q_ref, ks_ref, vs_ref, kw_ref, vw_ref, sel_ref, bias_ref, far_ref, gates_ref, oc_ref,
                     o_ref, *, t, scale):
    i = pl.program_id(2)
    rep = NSA_REP
    stack = lambda x: jnp.concatenate([x] * rep, axis=0)
    q = jnp.concatenate([q_ref[:, r * HEAD_DIM:(r + 1) * HEAD_DIM] for r in range(rep)], axis=0)
    row, col = _tile_iota(t)
    causal = stack(col <= row)
    bias_d = lambda: jnp.concatenate([bias_ref[r, :t, :] for r in range(rep)], axis=0)
    bias_s = lambda: jnp.concatenate([bias_ref[r, t:, :] for r in range(rep)], axis=0)
    bias_f = jnp.concatenate([jnp.broadcast_to(far_ref[r, 0:1, :], (t, t)) for r in range(rep)], axis=0)

    sel = sel_ref[0, 0]
    n_sel = sel.shape[1]
    blk_row = lax.broadcasted_iota(jnp.int32, (n_sel, t), 0)
    key_col = lax.broadcasted_iota(jnp.int32, (n_sel, t), 1)

    def sel_tile(kb, carry, bias, diag):
        start = pl.multiple_of(kb * t, t)
        expand = jnp.where((start + key_col) // NSA_SEL_BLOCK == blk_row, 1.0, 0.0).astype(BF16)
        picked = stack(_dot(sel, expand)) > 0.5
        if diag:
            picked = picked & causal
        s = _dot_nt(q, ks_ref[pl.ds(start, t), :]) * scale + bias
        s = jnp.where(picked, s, NEG)
        return _osm(s, vs_ref[pl.ds(start, t), :], *carry)

    carry = _osm_init(rep * t, HEAD_DIM)
    carry = _loop2(jnp.maximum(i - 1, 0), lambda kb, cr: sel_tile(kb, cr, bias_f, False), carry)
    carry = lax.cond(i > 0, lambda cr: sel_tile(i - 1, cr, bias_s(), False), lambda cr: cr, carry)
    _, l_s, acc_s = sel_tile(i, carry, bias_d(), True)
    o_s = acc_s / l_s

    n_back = NSA_WINDOW // t
    edge = stack(col > row)

    def win_tile(kb, carry, bias, mask):
        start = pl.multiple_of(kb * t, t)
        s = _dot_nt(q, kw_ref[pl.ds(start, t), :]) * scale + bias
        if mask is not None:
            s = jnp.where(mask, s, NEG)
        return _osm(s, vw_ref[pl.ds(start, t), :], *carry)

    carry = win_tile(i, _osm_init(rep * t, HEAD_DIM), bias_d(), causal)
    carry = lax.cond(i > 0, lambda cr: win_tile(i - 1, cr, bias_s(), None), lambda cr: cr, carry)
    carry = lax.fori_loop(jnp.maximum(i - n_back + 1, 0), jnp.maximum(i - 1, 0),
                          lambda kb, cr: win_tile(kb, cr, bias_f, None), carry)
    carry = lax.cond(i >= n_back, lambda cr: win_tile(i - n_back, cr, bias_f, edge), lambda cr: cr, carry)
    _, l_w, acc_w = carry
    o_w = acc_w / l_w

    gates = gates_ref[...]
    for r in range(rep):
        rows = slice(r * t, (r + 1) * t)
        cols = slice(r * HEAD_DIM, (r + 1) * HEAD_DIM)
        o = oc_ref[:, cols] + gates[:, rep + r:rep + r + 1] * o_s[rows] + gates[:, 2 * rep + r:2 * rep + r + 1] * o_w[rows]
        o_ref[:, cols] = o.astype(o_ref.dtype)


def _nsa_main_attention(qkv, sel, bias_near, bias_far, gates, oc, *, batch, seq_len, t=ATT_T):
    g = NSA_GROUPS
    nq = seq_len // t
    n_sel = seq_len // NSA_SEL_BLOCK
    gw = NSA_REP * HEAD_DIM
    qb = N_HEADS
    kv = lambda which: pl.BlockSpec((seq_len, HEAD_DIM), lambda b, gg, i: (b, qb + which * g + gg))
    tile = pl.BlockSpec((t, gw), lambda b, gg, i: (b * nq + i, gg))
    return pl.pallas_call(
        functools.partial(_nsa_main_kernel, t=t, scale=HEAD_DIM ** -0.5),
        grid=(batch, g, nq),
        in_specs=[tile, kv(0), kv(1), kv(2), kv(3),
                  pl.BlockSpec((1, 1, t, n_sel), lambda b, gg, i: (b, gg, i, 0)),
                  pl.BlockSpec((NSA_REP, 2 * t, t), lambda b, gg, i: (gg, 0, 0)),
                  pl.BlockSpec((NSA_REP, 8, t), lambda b, gg, i: (gg, 0, 0)),
                  pl.BlockSpec((t, LANE), lambda b, gg, i: (b * nq + i, gg)),
                  tile],
        out_specs=tile,
        out_shape=jax.ShapeDtypeStruct((batch * seq_len, g * gw), BF16),
        compiler_params=_params("parallel", "parallel", "arbitrary"),
        name="nsa_main_attention",
    )(qkv, qkv, qkv, qkv, qkv, sel, bias_near, bias_far, gates, oc)


def _rope_tables(seq_len, width):
    half = MLA_ROPE // 2
    inv = np.power(ROPE_THETA, -np.arange(half, dtype=np.float32) / half).astype(np.float32)
    ang = np.arange(seq_len, dtype=np.float32)[:, None] * inv[None, :]
    pad = np.zeros((seq_len, LANE - MLA_ROPE), np.float32)
    cos = np.concatenate([np.cos(ang), np.cos(ang), pad], axis=1)
    sin = np.concatenate([np.sin(ang), np.sin(ang), pad], axis=1)
    reps = width // LANE
    return jnp.asarray(np.tile(cos, (1, reps))), jnp.asarray(np.tile(sin, (1, reps)))


def _rope_weights(w):
    k, n, _ = w.shape
    half = MLA_ROPE // 2
    pad = jnp.zeros((k, n, LANE - MLA_ROPE), w.dtype)
    wa = jnp.concatenate([w, pad], axis=-1)
    wb = jnp.concatenate([-w[..., half:], w[..., :half], pad], axis=-1)
    return wa.reshape(k, n * LANE).astype(BF16), wb.reshape(k, n * LANE).astype(BF16)


def kernel(x, c, t5_bias, ada_w, ada_b, norm_g, final_g, ffn_w_up, ffn_conv_w, ffn_conv_b, ffn_w_down, sb_w_in, sb_w_out, nsa_w_in, nsa_cmp_pe, nsa_cmp_w1, nsa_cmp_w2, nsa_w_out, diff_w_in, diff_lambda, diff_head_g, diff_w_out, mla_w_in, mla_q_g, mla_w_qb, mla_kv_g, mla_w_kvb, mla_w_out):
    batch, seq_len, d = x.shape
    depth = ada_w.shape[0]
    h, dh, g = N_HEADS, HEAD_DIM, NSA_GROUPS
    sizes = dict(batch=batch, seq_len=seq_len)

    mod = _ada_mod(c, ada_w, ada_b)
    bias_near, bias_far = _toeplitz_bias(t5_bias, ATT_T)

    xf = x.reshape(batch * seq_len, d)
    for i in range(depth):
        mixer, j = i % 4, i // 4
        sh1, sc1, gt1, sh2, sc2, gt2 = (mod[i, :, n * d:(n + 1) * d].reshape(batch, 1, d) for n in range(6))
        nm = functools.partial(_norm_mm, xf, norm_g[i, 0], seq_len=seq_len, shift=sh1, scale=sc1)
        if mixer == 0:
            qkv = nm(sb_w_in[j].astype(BF16), name="sb_in")
            o = _sb_attention(qkv, **sizes)
            w_out = sb_w_out[j]
        elif mixer == 1:
            w_in = nsa_w_in[j]
            n_q, n_kv = h * dh, g * dh
            w_att = jnp.concatenate([w_in[:, :n_q], w_in[:, n_q + 2 * n_kv:n_q + 6 * n_kv]], axis=1)
            w_cmp = w_in[:, n_q:n_q + 2 * n_kv]
            w_g = w_in[:, n_q + 6 * n_kv:].reshape(d, 3, g, NSA_REP).transpose(0, 2, 1, 3).reshape(d, g, 3 * NSA_REP)
            w_g = jnp.pad(w_g, ((0, 0), (0, 0), (0, LANE - 3 * NSA_REP))).reshape(d, g * LANE)
            qkv = nm(w_att.astype(BF16), name="nsa_in")
            raw = nm(w_cmp.astype(BF16), out_dtype=F32, name="nsa_in_cmp")
            gates = nm(w_g.astype(BF16), out_dtype=F32, act="sigmoid", name="nsa_in_gates")
            kvc = _nsa_compress(raw, nsa_cmp_pe[j], nsa_cmp_w1[j].reshape(2, NSA_CMP_BLOCK, dh, dh).astype(BF16),
                                nsa_cmp_w2[j].astype(BF16), **sizes)
            n_slots = seq_len // NSA_CMP_STRIDE
            dist_c = np.arange(seq_len)[:, None] - (NSA_CMP_STRIDE * np.arange(n_slots)[None, :] + NSA_CMP_BLOCK - 1)
            bias_c = _t5_gather(t5_bias, _t5_bucket_np(dist_c), tr=min(seq_len, 512))
            oc, sel = _nsa_cmp_attention(qkv, kvc, bias_c, gates, **sizes)
            o = _nsa_main_attention(qkv, sel, bias_near, bias_far, gates, oc, **sizes)
            w_out = nsa_w_out[j]
        elif mixer == 2:
            lambda_init = 0.8 - 0.6 * math.exp(-0.3 * i)
            qkv = nm(diff_w_in[j].astype(BF16), name="diff_in")
            o = _diff_attention(qkv, bias_near, bias_far, diff_lambda[j], diff_head_g[j],
                                lambda_init=lambda_init, **sizes)
            w_out = diff_w_out[j]
        else:
            w_in = mla_w_in[j]
            nq_l, nkv_l = MLA_Q_LORA, MLA_KV_LORA
            w_lat = jnp.concatenate([w_in[:, :nq_l], jnp.zeros((d, 2 * nkv_l - nq_l), w_in.dtype),
                                     w_in[:, nq_l:nq_l + nkv_l]], axis=1)
            lat = nm(w_lat.astype(BF16), out_dtype=F32, name="mla_in")
            cos1, sin1 = _rope_tables(seq_len, LANE)
            wa, wb = _rope_weights(w_in[:, nq_l + nkv_l:].reshape(d, 1, MLA_ROPE))
            kr = nm(wa, wb=wb, cos=cos1, sin=sin1, name="mla_in_rope")
            w_qb = mla_w_qb[j].reshape(nq_l, h, MLA_NOPE + MLA_ROPE)
            qn = _norm_mm(lat, mla_q_g[j], w_qb[:, :, :MLA_NOPE].reshape(nq_l, h * MLA_NOPE).astype(BF16),
                          seq_len=seq_len, x_cols=nq_l, x_col_block=0, name="mla_q_nope")
            cosh, sinh = _rope_tables(seq_len, h * LANE)
            wa, wb = _rope_weights(w_qb[:, :, MLA_NOPE:])
            qr = _norm_mm(lat, mla_q_g[j], wa, wb=wb, cos=cosh, sin=sinh, seq_len=seq_len, x_cols=nq_l,
                          x_col_block=0, name="mla_q_rope")
            kv = _norm_mm(lat, mla_kv_g[j], mla_w_kvb[j].astype(BF16), seq_len=seq_len, x_cols=nkv_l,
                          x_col_block=2, name="mla_kv")
            o = _mla_attention(qn, qr, kv, kr, **sizes)
            w_out = mla_w_out[j]
        xf = _mm_residual(o, w_out.astype(BF16), xf, gt1, seq_len=seq_len)
        xf = _conv_ffn(xf, norm_g[i, 1], sh2, sc2, ffn_w_up[i].astype(BF16), ffn_conv_w[i], ffn_conv_b[i],
                       ffn_w_down[i].astype(BF16), gt2, final_g, seq_len=seq_len, final_norm=(i == depth - 1))
    return xf.reshape(batch, seq_len, d)
```

```python
import functools
import math

import numpy as np
import jax
import jax.numpy as jnp
from jax import lax
from jax.experimental import pallas as pl
from jax.experimental.pallas import tpu as pltpu

F32 = jnp.float32
BF16 = jnp.bfloat16
EPS = 1e-6
NEG = -1e30

LANE = 128
HALO = 16
VMEM_LIMIT = 56 * 2**20

T5_BUCKETS = 32
T5_MAX_DIST = 128
N_HEADS = 16
HEAD_DIM = 128
NSA_GROUPS = 4
NSA_REP = 4
NSA_CMP_BLOCK = 32
NSA_CMP_STRIDE = 16
NSA_SEL_BLOCK = 64
NSA_TOPN = 16
NSA_WINDOW = 512
FORCED_SCORE = 1e9
DIFF_DIM = 64
MLA_Q_LORA = 768
MLA_KV_LORA = 512
MLA_NOPE = 128
MLA_ROPE = 64
ROPE_THETA = 10000.0
CONV_WIDTH = 3
ATT_T = 256
ATT_TQ = 512
ATT_TK = 256
LOG2E = 1.4426950408889634
M_INIT = -1e30
MASKED = -2e30
T5_MASKED = T5_BUCKETS


def _params(*sem):
    return pltpu.CompilerParams(dimension_semantics=sem, vmem_limit_bytes=VMEM_LIMIT)


def _dot(a, b):
    return jnp.dot(a, b, preferred_element_type=F32)


def _dot_nt(a, b):
    return lax.dot_general(a, b, (((1,), (1,)), ((), ())), preferred_element_type=F32)


def _split_bf16(x):
    hi = x.astype(BF16)
    lo = (x - hi.astype(F32)).astype(BF16)
    return hi, lo


def _rms(x, g):
    ms = jnp.mean(x * x, axis=-1, keepdims=True)
    return x * lax.rsqrt(ms + EPS) * g


def _ada_kernel(c_ref, w_ref, b_ref, o_ref):
    c = c_ref[...]
    a = c * jax.nn.sigmoid(c)
    a_hi, a_lo = _split_bf16(a)
    w_hi, w_lo = _split_bf16(w_ref[0])
    o_ref[0] = _dot(a_hi, w_hi) + _dot(a_lo, w_hi) + _dot(a_hi, w_lo) + b_ref[0]


def _ada_mod(c, ada_w, ada_b, tn=1024):
    depth, d, n = ada_w.shape
    b = c.shape[0]
    return pl.pallas_call(
        _ada_kernel,
        grid=(depth, n // tn),
        in_specs=[pl.BlockSpec((b, d), lambda l, j: (0, 0)),
                  pl.BlockSpec((1, d, tn), lambda l, j: (l, 0, j)),
                  pl.BlockSpec((1, 1, tn), lambda l, j: (l, 0, j))],
        out_specs=pl.BlockSpec((1, b, tn), lambda l, j: (l, 0, j)),
        out_shape=jax.ShapeDtypeStruct((depth, b, n), F32),
        compiler_params=_params("parallel", "parallel"),
        name="ada_mod",
    )(c, ada_w, ada_b.reshape(depth, 1, n))


def _norm_mm_kernel(*refs, modulated, rope, act):
    it = iter(refs)
    x_ref, g_ref = next(it), next(it)
    sh_ref = sc_ref = wb_ref = cos_ref = sin_ref = None
    if modulated:
        sh_ref, sc_ref = next(it), next(it)
    w_ref = next(it)
    if rope:
        wb_ref, cos_ref, sin_ref = next(it), next(it), next(it)
    o_ref, h_ref = next(it), next(it)

    @pl.when(pl.program_id(1) == 0)
    def _():
        y = _rms(x_ref[...], g_ref[...])
        if modulated:
            y = y * (1.0 + sc_ref[0]) + sh_ref[0]
        h_ref[...] = y.astype(BF16)

    h = h_ref[...]
    acc = _dot(h, w_ref[...])
    if rope:
        acc = acc * cos_ref[...] + _dot(h, wb_ref[...]) * sin_ref[...]
    if act == "sigmoid":
        acc = jax.nn.sigmoid(acc)
    o_ref[...] = acc.astype(o_ref.dtype)


def _norm_mm(x, g, w, *, seq_len, shift=None, scale=None, wb=None, cos=None, sin=None, act=None,
             out_dtype=BF16, x_cols=None, x_col_block=0, tm=512, tn=512, name="norm_mm"):
    m = x.shape[0]
    k = x.shape[1] if x_cols is None else x_cols
    n = w.shape[1]
    tn = min(tn, n)
    modulated, rope = shift is not None, wb is not None
    tps = seq_len // tm
    in_specs = [pl.BlockSpec((tm, k), lambda i, j: (i, x_col_block)),
                pl.BlockSpec((1, k), lambda i, j: (0, 0))]
    args = [x, g.reshape(1, k)]
    if modulated:
        in_specs += [pl.BlockSpec((1, 1, k), lambda i, j: (i // tps, 0, 0))] * 2
        args += [shift, scale]
    in_specs.append(pl.BlockSpec((k, tn), lambda i, j: (0, j)))
    args.append(w)
    if rope:
        in_specs.append(pl.BlockSpec((k, tn), lambda i, j: (0, j)))
        in_specs += [pl.BlockSpec((tm, tn), lambda i, j: (i % tps, j))] * 2
        args += [wb, cos, sin]
    return pl.pallas_call(
        functools.partial(_norm_mm_kernel, modulated=modulated, rope=rope, act=act),
        grid=(m // tm, n // tn),
        in_specs=in_specs,
        out_specs=pl.BlockSpec((tm, tn), lambda i, j: (i, j)),
        out_shape=jax.ShapeDtypeStruct((m, n), out_dtype),
        scratch_shapes=[pltpu.VMEM((tm, k), BF16)],
        compiler_params=_params("parallel", "arbitrary"),
        name=name,
    )(*args)


def _mm_res_kernel(a_ref, w_ref, x_ref, gate_ref, o_ref):
    o_ref[...] = x_ref[...] + gate_ref[0] * _dot(a_ref[...], w_ref[...])


def _mm_residual(a, w, x, gate, *, seq_len, tm=1024, tn=512, name="mm_residual"):
    m, k = a.shape
    n = w.shape[1]
    tps = seq_len // tm
    return pl.pallas_call(
        _mm_res_kernel,
        grid=(m // tm, n // tn),
        in_specs=[pl.BlockSpec((tm, k), lambda i, j: (i, 0)),
                  pl.BlockSpec((k, tn), lambda i, j: (0, j)),
                  pl.BlockSpec((tm, tn), lambda i, j: (i, j)),
                  pl.BlockSpec((1, 1, tn), lambda i, j: (i // tps, 0, j))],
        out_specs=pl.BlockSpec((tm, tn), lambda i, j: (i, j)),
        out_shape=jax.ShapeDtypeStruct((m, n), F32),
        compiler_params=_params("parallel", "arbitrary"),
        name=name,
    )(a, w, x, gate)


def _ffn_kernel(x_ref, xp_ref, g_ref, sh_ref, sc_ref, wg_ref, wu_ref, cwg_ref, cwu_ref, cbg_ref, cbu_ref,
                wd_ref, gate_ref, fg_ref, o_ref, h_ref, acc_ref, *, tiles_per_seq, final_norm):
    i, j = pl.program_id(0), pl.program_id(1)

    @pl.when(j == 0)
    def _():
        def nm(x):
            return _rms(x, g_ref[...]) * (1.0 + sc_ref[0]) + sh_ref[0]
        h_ref[HALO:, :] = nm(x_ref[...]).astype(BF16)
        keep = jnp.where(i % tiles_per_seq == 0, 0.0, 1.0)
        h_ref[:HALO, :] = (nm(xp_ref[...]) * keep).astype(BF16)
        acc_ref[...] = jnp.zeros_like(acc_ref)

    h = h_ref[...]

    def branch(w_ref, cw_ref, cb_ref):
        u = _dot(h, w_ref[...])
        u1 = pltpu.roll(u, 1, 0)
        u2 = pltpu.roll(u, 2, 0)
        cw = cw_ref[...]
        y = cw[0:1] * u2 + cw[1:2] * u1 + cw[2:3] * u + cb_ref[...]
        return y[HALO:]

    gt = branch(wg_ref, cwg_ref, cbg_ref)
    up = branch(wu_ref, cwu_ref, cbu_ref)
    a = gt * jax.nn.sigmoid(gt) * up
    acc_ref[...] += _dot(a.astype(BF16), wd_ref[...])

    @pl.when(j == pl.num_programs(1) - 1)
    def _():
        y = x_ref[...] + gate_ref[0] * acc_ref[...]
        if final_norm:
            y = _rms(y, fg_ref[...])
        o_ref[...] = y


def _conv_ffn(x, g, shift, scale, w_up, conv_w, conv_b, w_down, gate, final_g, *, seq_len, final_norm,
              tm=512, tf=512):
    m, d = x.shape
    f = w_down.shape[0]
    nf = f // tf
    tps = seq_len // tm
    hb = tm // HALO
    row = lambda i, j: (i, 0)
    per_batch = lambda i, j: (i // tps, 0, 0)
    return pl.pallas_call(
        functools.partial(_ffn_kernel, tiles_per_seq=tps, final_norm=final_norm),
        grid=(m // tm, nf),
        in_specs=[pl.BlockSpec((tm, d), row),
                  pl.BlockSpec((HALO, d), lambda i, j: (jnp.maximum(i * hb - 1, 0), 0)),
                  pl.BlockSpec((1, d), lambda i, j: (0, 0)),
                  pl.BlockSpec((1, 1, d), per_batch),
                  pl.BlockSpec((1, 1, d), per_batch),
                  pl.BlockSpec((d, tf), lambda i, j: (0, j)),
                  pl.BlockSpec((d, tf), lambda i, j: (0, j + nf)),
                  pl.BlockSpec((CONV_WIDTH, tf), lambda i, j: (0, j)),
                  pl.BlockSpec((CONV_WIDTH, tf), lambda i, j: (0, j + nf)),
                  pl.BlockSpec((1, tf), lambda i, j: (0, j)),
                  pl.BlockSpec((1, tf), lambda i, j: (0, j + nf)),
                  pl.BlockSpec((tf, d), lambda i, j: (j, 0)),
                  pl.BlockSpec((1, 1, d), per_batch),
                  pl.BlockSpec((1, d), lambda i, j: (0, 0))],
        out_specs=pl.BlockSpec((tm, d), row),
        out_shape=jax.ShapeDtypeStruct((m, d), F32),
        scratch_shapes=[pltpu.VMEM((HALO + tm, d), BF16), pltpu.VMEM((tm, d), F32)],
        compiler_params=_params("parallel", "arbitrary"),
        name="conv_ffn",
    )(x, x, g.reshape(1, d), shift, scale, w_up, w_up, conv_w, conv_w, conv_b.reshape(1, -1),
      conv_b.reshape(1, -1), w_down, gate, final_g.reshape(1, d))


def _t5_bucket_np(dist):
    n = np.maximum(dist, 0)
    max_exact = T5_BUCKETS // 2
    nf = np.maximum(n, 1).astype(np.float64)
    val = np.log(nf / max_exact) / math.log(T5_MAX_DIST / max_exact) * (T5_BUCKETS - max_exact)
    large = max_exact + np.trunc(val + 1e-6).astype(np.int64)
    return np.where(n < max_exact, n, np.minimum(large, T5_BUCKETS - 1)).astype(np.int32)


def _t5_gather_kernel(t5_ref, bkt_ref, o_ref, *, mult):
    h = pl.program_id(0)
    bk = bkt_ref[...]
    acc = jnp.zeros(bk.shape, F32)
    for b in range(T5_BUCKETS):
        acc = jnp.where(bk == b, t5_ref[b, h], acc)
    o_ref[0] = jnp.where(bk == T5_MASKED, MASKED, acc * mult)


def _t5_gather(t5_bias, bkt, tr, mult=1.0):
    rows, cols = bkt.shape
    heads = t5_bias.shape[1]
    return pl.pallas_call(
        functools.partial(_t5_gather_kernel, mult=mult),
        grid=(heads, rows // tr),
        in_specs=[pl.BlockSpec(memory_space=pltpu.SMEM),
                  pl.BlockSpec((tr, cols), lambda h, r: (r, 0))],
        out_specs=pl.BlockSpec((1, tr, cols), lambda h, r: (h, r, 0)),
        out_shape=jax.ShapeDtypeStruct((heads, rows, cols), F32),
        compiler_params=_params("parallel", "parallel"),
        name="t5_gather",
    )(t5_bias, jnp.asarray(bkt))


TAB_DIAG, TAB_SUB, TAB_FAR, TAB_EDGE = 0, 1, 2, 3


def _attention_tables(t5_bias, t):
    i = np.arange(t)[:, None]
    j = np.arange(t)[None, :]
    assert int(_t5_bucket_np(np.array(t + 1))) == T5_BUCKETS - 1
    far = np.full((t, t), T5_BUCKETS - 1, np.int32)
    bkt = np.concatenate([np.where(j <= i, _t5_bucket_np(i - j), T5_MASKED), _t5_bucket_np(t + i - j), far,
                          np.where(j > i, far, T5_MASKED)], axis=0).astype(np.int32)
    return _t5_gather(t5_bias, bkt, tr=t, mult=LOG2E)


def _tile_iota(t):
    return lax.broadcasted_iota(jnp.int32, (t, t), 0), lax.broadcasted_iota(jnp.int32, (t, t), 1)


def _rect_iota(i, tq, start, tk):
    rowg = i * tq + lax.broadcasted_iota(jnp.int32, (tq, tk), 0)
    colg = start + lax.broadcasted_iota(jnp.int32, (tq, tk), 1)
    return rowg, colg


def _osm_reset(m_scr, l_scr, acc_scr):
    m_scr[...] = jnp.full(m_scr.shape, M_INIT, F32)
    l_scr[...] = jnp.zeros(l_scr.shape, F32)
    acc_scr[...] = jnp.zeros(acc_scr.shape, F32)


def _osm_update(t2, rows, slot, p_scr, m_scr, l_scr, acc_scr):
    m_prev = m_scr[rows, :]
    m_new = jnp.maximum(m_prev, jnp.max(t2, axis=-1, keepdims=True))
    p = jnp.exp2(t2 - pltpu.repeat(m_new, t2.shape[1] // LANE, axis=1))
    alpha = jnp.exp2(m_prev - m_new)
    l_scr[rows, :] = alpha * l_scr[rows, :] + jnp.sum(p, axis=-1, keepdims=True)
    m_scr[rows, :] = m_new
    p_scr[slot, rows, :] = p.astype(BF16)
    acc_scr[rows, :] = alpha * acc_scr[rows, :]


def _pipelined_tiles(n_tiles, scores, update):
    last = n_tiles - 1
    scores(0, 0)

    def pair(n, cr):
        scores(2 * n + 1, 1)
        update(2 * n, 0)
        scores(jnp.minimum(2 * n + 2, last), 0)
        update(2 * n + 1, 1)
        return cr

    lax.fori_loop(0, n_tiles // 2, pair, 0)

    @pl.when(n_tiles % 2 == 1)
    def _():
        update(last, 0)


def _att_scratch(rows, tk, slots=2):
    return [pltpu.VMEM((slots, rows, tk), F32), pltpu.VMEM((slots, rows, tk), BF16), pltpu.VMEM((rows, LANE), F32),
            pltpu.VMEM((rows, LANE), F32), pltpu.VMEM((rows, HEAD_DIM), F32)]


def _sb_kernel(q_ref, k_ref, v_ref, o_ref, s_scr, c_scr, acc_scr, *, tq, tk, scale):
    i = pl.program_id(2)
    q = q_ref[...]
    row, col = _tile_iota(tk)
    upper = jnp.where(row > col, 1.0, 0.0).astype(BF16)
    c_scr[...] = jnp.zeros(c_scr.shape, F32)
    acc_scr[...] = jnp.zeros(acc_scr.shape, F32)
    reps = tk // LANE

    def scores(kb, slot):
        start = pl.multiple_of(kb * tk, tk)
        s_scr[slot] = _dot_nt(q, k_ref[pl.ds(start, tk), :])

    def update(kb, slot, diag):
        start = pl.multiple_of(kb * tk, tk)
        nz = s_scr[slot] * (-scale)
        e = jnp.exp2(jnp.abs(nz) * (-LOG2E))
        lk = jnp.minimum(nz, 0.0) - jnp.log(1.0 + e)
        if diag:
            rowg, colg = _rect_iota(i, tq, start, tk)
            past = colg < rowg
            lk = jnp.where(past, lk, 0.0)
        hi, lo = _split_bf16(lk)
        c = c_scr[...]
        later = _dot(hi, upper) + _dot(lo, upper) + pltpu.repeat(c, reps, axis=1)
        a = jnp.exp(lk - nz + later)
        if diag:
            a = jnp.where(past, a, 0.0)
        acc_scr[...] += _dot(a.astype(BF16), v_ref[pl.ds(start, tk), :])
        c_scr[...] = c + jnp.sum(lk, axis=-1, keepdims=True)

    assert tq == 2 * tk
    scores(2 * i + 1, 1)
    scores(2 * i, 0)
    update(2 * i + 1, 1, True)
    scores(jnp.maximum(2 * i - 1, 0), 1)
    update(2 * i, 0, True)

    def pair(n, cr):
        kb = 2 * (i - n) - 1
        scores(kb - 1, 0)
        update(kb, 1, False)
        scores(jnp.maximum(kb - 2, 0), 1)
        update(kb - 1, 0, False)
        return cr

    lax.fori_loop(0, i, pair, 0)
    o_ref[...] = acc_scr[...].astype(o_ref.dtype)


def _sb_attention(qkv, *, batch, seq_len, tq=ATT_TQ, tk=ATT_TK):
    h = N_HEADS
    nq = seq_len // tq
    return pl.pallas_call(
        functools.partial(_sb_kernel, tq=tq, tk=tk, scale=HEAD_DIM ** -0.5),
        grid=(batch, h, nq),
        in_specs=[pl.BlockSpec((tq, HEAD_DIM), lambda b, hh, i: (b * nq + i, hh)),
                  pl.BlockSpec((seq_len, HEAD_DIM), lambda b, hh, i: (b, h + hh)),
                  pl.BlockSpec((seq_len, HEAD_DIM), lambda b, hh, i: (b, 2 * h + hh))],
        out_specs=pl.BlockSpec((tq, HEAD_DIM), lambda b, hh, i: (b * nq + i, hh)),
        out_shape=jax.ShapeDtypeStruct((batch * seq_len, h * HEAD_DIM), BF16),
        scratch_shapes=[pltpu.VMEM((2, tq, tk), F32), pltpu.VMEM((tq, LANE), F32), pltpu.VMEM((tq, HEAD_DIM), F32)],
        compiler_params=_params("parallel", "parallel", "arbitrary"),
        name="sb_attention",
    )(qkv, qkv, qkv)


def _table_offset(kb, i, t):
    return pl.multiple_of(jnp.where(kb == i, TAB_DIAG * t, jnp.where(kb == i - 1, TAB_SUB * t, TAB_FAR * t)), t)


def _diff_kernel(q_ref, k_ref, v_ref, tab_ref, lam_ref, hg_ref, o_ref, s_scr, p_scr, m_scr, l_scr, acc_scr, *, t,
                 scale, lambda_init):
    i = pl.program_id(2)
    q = q_ref[...]
    lane = lax.broadcasted_iota(jnp.int32, q.shape, 1)
    zero = jnp.zeros_like(q)
    q2 = jnp.concatenate([jnp.where(lane < DIFF_DIM, q, zero), jnp.where(lane >= DIFF_DIM, q, zero)], axis=0)
    _osm_reset(m_scr, l_scr, acc_scr)

    def scores(kb, slot):
        s_scr[slot] = _dot_nt(q2, k_ref[pl.ds(pl.multiple_of(kb * t, t), t), :])

    def update(kb, slot):
        off = _table_offset(kb, i, t)
        for half in range(2):
            rows = pl.ds(half * t, t)
            t2 = s_scr[slot, rows, :] * (scale * LOG2E) + tab_ref[0, pl.ds(off, t), :]
            _osm_update(t2, rows, slot, p_scr, m_scr, l_scr, acc_scr)
        acc_scr[...] += _dot(p_scr[slot], v_ref[pl.ds(pl.multiple_of(kb * t, t), t), :])

    _pipelined_tiles(i + 1, scores, update)

    lam = lam_ref[...]
    lmbda = (jnp.exp(jnp.sum(lam[0:1] * lam[1:2], axis=-1, keepdims=True))
             - jnp.exp(jnp.sum(lam[2:3] * lam[3:4], axis=-1, keepdims=True)) + lambda_init)
    o = acc_scr[...] / l_scr[...]
    o = o[:t] - lmbda * o[t:]
    o_ref[...] = (_rms(o, hg_ref[...]) * (1.0 - lambda_init)).astype(o_ref.dtype)


def _diff_attention(qkv, tables, lam, head_g, *, batch, seq_len, lambda_init, t=ATT_T):
    h = N_HEADS
    nq = seq_len // t
    return pl.pallas_call(
        functools.partial(_diff_kernel, t=t, scale=DIFF_DIM ** -0.5, lambda_init=lambda_init),
        grid=(batch, h, nq),
        in_specs=[pl.BlockSpec((t, HEAD_DIM), lambda b, hh, i: (b * nq + i, hh)),
                  pl.BlockSpec((seq_len, HEAD_DIM), lambda b, hh, i: (b, h + hh)),
                  pl.BlockSpec((seq_len, HEAD_DIM), lambda b, hh, i: (b, 2 * h + hh)),
                  pl.BlockSpec((1, 4 * t, t), lambda b, hh, i: (hh, 0, 0)),
                  pl.BlockSpec((4, DIFF_DIM), lambda b, hh, i: (0, 0)),
                  pl.BlockSpec((1, HEAD_DIM), lambda b, hh, i: (0, 0))],
        out_specs=pl.BlockSpec((t, HEAD_DIM), lambda b, hh, i: (b * nq + i, hh)),
        out_shape=jax.ShapeDtypeStruct((batch * seq_len, h * HEAD_DIM), BF16),
        scratch_shapes=_att_scratch(2 * t, t),
        compiler_params=_params("parallel", "parallel", "arbitrary"),
        name="diff_attention",
    )(qkv, qkv, qkv, tables, lam, head_g.reshape(1, HEAD_DIM))


def _mla_kernel(qn_ref, qr_ref, kn_ref, kr_ref, v_ref, o_ref, s_scr, p_scr, m_scr, l_scr, acc_scr, *, tq, tk, scale):
    i = pl.program_id(2)
    q = jnp.concatenate([qn_ref[...], qr_ref[...]], axis=-1)
    _osm_reset(m_scr, l_scr, acc_scr)

    def scores(kb, slot):
        start = pl.multiple_of(kb * tk, tk)
        k = jnp.concatenate([kn_ref[pl.ds(start, tk), :], kr_ref[pl.ds(start, tk), :]], axis=-1)
        s_scr[slot] = _dot_nt(q, k)

    def update(kb, slot, mask):
        start = pl.multiple_of(kb * tk, tk)
        t2 = s_scr[slot] * (scale * LOG2E)
        if mask:
            rowg, colg = _rect_iota(i, tq, start, tk)
            t2 = jnp.where(colg <= rowg, t2, MASKED)
        _osm_update(t2, pl.ds(0, tq), slot, p_scr, m_scr, l_scr, acc_scr)
        acc_scr[...] += _dot(p_scr[slot], v_ref[pl.ds(start, tk), :])

    assert tq == 2 * tk
    scores(0, 0)

    def pair(n, cr):
        scores(2 * n + 1, 1)
        update(2 * n, 0, False)
        scores(2 * n + 2, 0)
        update(2 * n + 1, 1, False)
        return cr

    lax.fori_loop(0, i, pair, 0)
    scores(2 * i + 1, 1)
    update(2 * i, 0, True)
    update(2 * i + 1, 1, True)
    o_ref[...] = (acc_scr[...] / l_scr[...]).astype(o_ref.dtype)


def _mla_attention(qn, qr, kv, kr, *, batch, seq_len, tq=ATT_TQ, tk=ATT_TK):
    h = N_HEADS
    nq = seq_len // tq
    qspec = pl.BlockSpec((tq, HEAD_DIM), lambda b, hh, i: (b * nq + i, hh))
    return pl.pallas_call(
        functools.partial(_mla_kernel, tq=tq, tk=tk, scale=(MLA_NOPE + MLA_ROPE) ** -0.5),
        grid=(batch, h, nq),
        in_specs=[qspec, qspec,
                  pl.BlockSpec((seq_len, HEAD_DIM), lambda b, hh, i: (b, 2 * hh)),
                  pl.BlockSpec((seq_len, HEAD_DIM), lambda b, hh, i: (b, 0)),
                  pl.BlockSpec((seq_len, HEAD_DIM), lambda b, hh, i: (b, 2 * hh + 1))],
        out_specs=qspec,
        out_shape=jax.ShapeDtypeStruct((batch * seq_len, h * HEAD_DIM), BF16),
        scratch_shapes=_att_scratch(tq, tk),
        compiler_params=_params("parallel", "parallel", "arbitrary"),
        name="mla_attention",
    )(qn, qr, kv, kr, kv)


def _compress_kernel(raw_ref, pe_ref, w1_ref, w2_ref, o_ref, *, n_slots):
    half = NSA_CMP_BLOCK // 2
    p1 = jnp.zeros((n_slots, HEAD_DIM), F32)
    p2 = jnp.zeros((n_slots, HEAD_DIM), F32)
    for l in range(half):
        a = raw_ref[pl.ds(l, n_slots, stride=NSA_CMP_STRIDE), :]
        p1 = p1 + _dot((a + pe_ref[0, l:l + 1, :]).astype(BF16), w1_ref[0, l])
        p2 = p2 + _dot((a + pe_ref[0, half + l:half + l + 1, :]).astype(BF16), w1_ref[0, half + l])
    pre = p1 + pltpu.roll(p2, n_slots - 1, 0)
    hid = pre * jax.nn.sigmoid(pre)
    o_ref[0, 0, 0] = _dot(hid.astype(BF16), w2_ref[0]).astype(o_ref.dtype)


def _nsa_compress(raw, pe, w1, w2, *, batch, seq_len):
    g = NSA_GROUPS
    n_slots = seq_len // NSA_CMP_STRIDE
    return pl.pallas_call(
        functools.partial(_compress_kernel, n_slots=n_slots),
        grid=(batch, 2, g),
        in_specs=[pl.BlockSpec((seq_len, HEAD_DIM), lambda b, kv, gg: (b, kv * g + gg)),
                  pl.BlockSpec((1, NSA_CMP_BLOCK, HEAD_DIM), lambda b, kv, gg: (kv, 0, 0)),
                  pl.BlockSpec((1, NSA_CMP_BLOCK, HEAD_DIM, HEAD_DIM), lambda b, kv, gg: (kv, 0, 0, 0)),
                  pl.BlockSpec((1, HEAD_DIM, HEAD_DIM), lambda b, kv, gg: (kv, 0, 0))],
        out_specs=pl.BlockSpec((1, 1, 1, n_slots, HEAD_DIM), lambda b, kv, gg: (b, kv, gg, 0, 0)),
        out_shape=jax.ShapeDtypeStruct((batch, 2, g, n_slots, HEAD_DIM), BF16),
        compiler_params=_params("parallel", "parallel", "parallel"),
        name="nsa_compress",
    )(raw, pe, w1, w2)


def _nsa_cmp_kernel(q_ref, kc_ref, vc_ref, bias_ref, gates_ref, ovt_ref, oc_ref, sel_ref, *, t, n_slots, n_sel,
                    n_top, scale):
    i = pl.program_id(2)
    kc = kc_ref[0, 0, 0]
    vc = vc_ref[0, 0, 0]
    qpos = i * t + lax.broadcasted_iota(jnp.int32, (t, n_slots), 0)
    cmp_end = NSA_CMP_STRIDE * lax.broadcasted_iota(jnp.int32, (t, n_slots), 1) + (NSA_CMP_BLOCK - 1)
    valid = cmp_end <= qpos
    gates = gates_ref[...]
    psum = jnp.zeros((t, n_slots), F32)
    for r in range(NSA_REP):
        q = q_ref[:, r * HEAD_DIM:(r + 1) * HEAD_DIM]
        s = jnp.where(valid, _dot_nt(q, kc) * scale + bias_ref[r], NEG)
        m = jnp.max(s, axis=-1, keepdims=True)
        p = jnp.where(valid, jnp.exp(s - m), 0.0)
        p = p / jnp.maximum(jnp.sum(p, axis=-1, keepdims=True), 1e-30)
        psum = psum + p
        oc_ref[:, r * HEAD_DIM:(r + 1) * HEAD_DIM] = gates[:, r:r + 1] * _dot(p.astype(BF16), vc)

    hi, lo = _split_bf16(psum)
    ovt = ovt_ref[...]
    imp = _dot_nt(ovt, hi) + _dot_nt(ovt, lo)
    blk = lax.broadcasted_iota(jnp.int32, (n_sel, t), 0)
    tpos = i * t + lax.broadcasted_iota(jnp.int32, (n_sel, t), 1)
    cur = tpos // NSA_SEL_BLOCK
    forced = (blk == 0) | (blk == cur) | (blk == cur - 1)
    score = jnp.where(blk * NSA_SEL_BLOCK <= tpos, jnp.where(forced, FORCED_SCORE, imp), -1.0)
    rank = jnp.zeros((n_sel, t), F32)
    for mm in range(n_sel):
        sm = score[mm:mm + 1, :]
        ahead = (sm > score) | ((sm == score) & (blk > mm))
        rank = rank + jnp.where(ahead, 1.0, 0.0)
    sel_t = jnp.where(rank < n_top, 1.0, 0.0).astype(BF16)
    row, col = _tile_iota(t)
    eye = jnp.where(row == col, 1.0, 0.0).astype(BF16)
    sel_ref[0, 0] = _dot_nt(eye, sel_t).astype(sel_ref.dtype)


def _nsa_cmp_attention(q_all, kvc, bias_c, gates, *, batch, seq_len, t=ATT_T):
    g = NSA_GROUPS
    nq = seq_len // t
    n_slots = seq_len // NSA_CMP_STRIDE
    n_sel = seq_len // NSA_SEL_BLOCK
    c0 = NSA_CMP_STRIDE * np.arange(n_slots)[:, None]
    s0 = NSA_SEL_BLOCK * np.arange(n_sel)[None, :]
    overlap = (c0 < s0 + NSA_SEL_BLOCK) & (c0 + NSA_CMP_BLOCK > s0)
    ovt = jnp.asarray(overlap.T.astype(np.float32), dtype=BF16)
    gw = NSA_REP * HEAD_DIM
    return pl.pallas_call(
        functools.partial(_nsa_cmp_kernel, t=t, n_slots=n_slots, n_sel=n_sel, n_top=min(NSA_TOPN, n_sel),
                          scale=HEAD_DIM ** -0.5),
        grid=(batch, g, nq),
        in_specs=[pl.BlockSpec((t, gw), lambda b, gg, i: (b * nq + i, gg)),
                  pl.BlockSpec((1, 1, 1, n_slots, HEAD_DIM), lambda b, gg, i: (b, 0, gg, 0, 0)),
                  pl.BlockSpec((1, 1, 1, n_slots, HEAD_DIM), lambda b, gg, i: (b, 1, gg, 0, 0)),
                  pl.BlockSpec((NSA_REP, t, n_slots), lambda b, gg, i: (gg, i, 0)),
                  pl.BlockSpec((t, LANE), lambda b, gg, i: (b * nq + i, gg)),
                  pl.BlockSpec((n_sel, n_slots), lambda b, gg, i: (0, 0))],
        out_specs=[pl.BlockSpec((t, gw), lambda b, gg, i: (b * nq + i, gg)),
                   pl.BlockSpec((1, 1, t, n_sel), lambda b, gg, i: (b, gg, i, 0))],
        out_shape=[jax.ShapeDtypeStruct((batch * seq_len, g * gw), F32),
                   jax.ShapeDtypeStruct((batch, g, seq_len, n_sel), BF16)],
        compiler_params=_params("parallel", "parallel", "arbitrary"),
        name="nsa_cmp_attention",
    )(q_all, kvc, kvc, bias_c, gates, ovt)


def _nsa_main_kernel(q_ref, ks_ref, vs_ref, kw_ref, vw_ref, sel_ref, tab_ref, gates_ref, oc_ref, o_ref,
                     s_scr, p_scr, m_scr, l_scr, acc_scr, os_scr, mk_scr, *, t, scale):
    i = pl.program_id(2)
    rep = NSA_REP
    q = jnp.concatenate([q_ref[:, r * HEAD_DIM:(r + 1) * HEAD_DIM] for r in range(rep)], axis=0)

    sel = sel_ref[0, 0]
    n_sel = sel.shape[1]
    blk_row = lax.broadcasted_iota(jnp.int32, (n_sel, t), 0)
    key_col = lax.broadcasted_iota(jnp.int32, (n_sel, t), 1)
    _osm_reset(m_scr, l_scr, acc_scr)

    def sel_scores(kb, slot):
        s_scr[slot] = _dot_nt(q, ks_ref[pl.ds(pl.multiple_of(kb * t, t), t), :])

    def sel_update(kb, slot):
        start = pl.multiple_of(kb * t, t)
        expand = jnp.where((start + key_col) // NSA_SEL_BLOCK == blk_row, 1.0, 0.0).astype(BF16)
        mk_scr[...] = _dot(sel, expand)
        off = _table_offset(kb, i, t)
        for r in range(rep):
            rows = pl.ds(r * t, t)
            t2 = s_scr[slot, rows, :] * (scale * LOG2E) + tab_ref[r, pl.ds(off, t), :]
            t2 = jnp.where(mk_scr[...] > 0.5, t2, MASKED)
            _osm_update(t2, rows, slot, p_scr, m_scr, l_scr, acc_scr)
        acc_scr[...] += _dot(p_scr[slot], vs_ref[pl.ds(start, t), :])

    _pipelined_tiles(i + 1, sel_scores, sel_update)
    os_scr[...] = acc_scr[...] / l_scr[...]

    n_back = NSA_WINDOW // t
    assert n_back == 2
    _osm_reset(m_scr, l_scr, acc_scr)

    def win_scores(kb, slot):
        s_scr[slot] = _dot_nt(q, kw_ref[pl.ds(pl.multiple_of(kb * t, t), t), :])

    def win_update(kb, slot, region):
        for r in range(rep):
            rows = pl.ds(r * t, t)
            t2 = s_scr[slot, rows, :] * (scale * LOG2E) + tab_ref[r, pl.ds(region * t, t), :]
            _osm_update(t2, rows, slot, p_scr, m_scr, l_scr, acc_scr)
        acc_scr[...] += _dot(p_scr[slot], vw_ref[pl.ds(pl.multiple_of(kb * t, t), t), :])

    win_scores(i, 0)

    @pl.when(i >= 1)
    def _():
        win_scores(i - 1, 1)

    @pl.when(i >= 2)
    def _():
        win_scores(i - 2, 2)
        win_update(i - 2, 2, TAB_EDGE)

    @pl.when(i >= 1)
    def _():
        win_update(i - 1, 1, TAB_SUB)

    win_update(i, 0, TAB_DIAG)
    o_w = acc_scr[...] / l_scr[...]
    o_s = os_scr[...]

    gates = gates_ref[...]
    for r in range(rep):
        rows = slice(r * t, (r + 1) * t)
        cols = slice(r * HEAD_DIM, (r + 1) * HEAD_DIM)
        o = oc_ref[:, cols] + gates[:, rep + r:rep + r + 1] * o_s[rows] + gates[:, 2 * rep + r:2 * rep + r + 1] * o_w[rows]
        o_ref[:, cols] = o.astype(o_ref.dtype)


def _nsa_main_attention(qkv, sel, tables, gates, oc, *, batch, seq_len, t=ATT_T):
    g = NSA_GROUPS
    nq = seq_len // t
    n_sel = seq_len // NSA_SEL_BLOCK
    gw = NSA_REP * HEAD_DIM
    qb = N_HEADS
    kv = lambda which: pl.BlockSpec((seq_len, HEAD_DIM), lambda b, gg, i: (b, qb + which * g + gg))
    tile = pl.BlockSpec((t, gw), lambda b, gg, i: (b * nq + i, gg))
    rows = NSA_REP * t
    return pl.pallas_call(
        functools.partial(_nsa_main_kernel, t=t, scale=HEAD_DIM ** -0.5),
        grid=(batch, g, nq),
        in_specs=[tile, kv(0), kv(1), kv(2), kv(3),
                  pl.BlockSpec((1, 1, t, n_sel), lambda b, gg, i: (b, gg, i, 0)),
                  pl.BlockSpec((NSA_REP, 4 * t, t), lambda b, gg, i: (gg, 0, 0)),
                  pl.BlockSpec((t, LANE), lambda b, gg, i: (b * nq + i, gg)),
                  tile],
        out_specs=tile,
        out_shape=jax.ShapeDtypeStruct((batch * seq_len, g * gw), BF16),
        scratch_shapes=_att_scratch(rows, t, slots=3) + [pltpu.VMEM((rows, HEAD_DIM), F32), pltpu.VMEM((t, t), F32)],
        compiler_params=_params("parallel", "parallel", "arbitrary"),
        name="nsa_main_attention",
    )(qkv, qkv, qkv, qkv, qkv, sel, tables, gates, oc)


def _rope_tables(seq_len, width):
    half = MLA_ROPE // 2
    inv = np.power(ROPE_THETA, -np.arange(half, dtype=np.float32) / half).astype(np.float32)
    ang = np.arange(seq_len, dtype=np.float32)[:, None] * inv[None, :]
    pad = np.zeros((seq_len, LANE - MLA_ROPE), np.float32)
    cos = np.concatenate([np.cos(ang), np.cos(ang), pad], axis=1)
    sin = np.concatenate([np.sin(ang), np.sin(ang), pad], axis=1)
    reps = width // LANE
    return jnp.asarray(np.tile(cos, (1, reps))), jnp.asarray(np.tile(sin, (1, reps)))


def _rope_weights(w):
    k, n, _ = w.shape
    half = MLA_ROPE // 2
    pad = jnp.zeros((k, n, LANE - MLA_ROPE), w.dtype)
    wa = jnp.concatenate([w, pad], axis=-1)
    wb = jnp.concatenate([-w[..., half:], w[..., :half], pad], axis=-1)
    return wa.reshape(k, n * LANE).astype(BF16), wb.reshape(k, n * LANE).astype(BF16)


def kernel(x, c, t5_bias, ada_w, ada_b, norm_g, final_g, ffn_w_up, ffn_conv_w, ffn_conv_b, ffn_w_down, sb_w_in, sb_w_out, nsa_w_in, nsa_cmp_pe, nsa_cmp_w1, nsa_cmp_w2, nsa_w_out, diff_w_in, diff_lambda, diff_head_g, diff_w_out, mla_w_in, mla_q_g, mla_w_qb, mla_kv_g, mla_w_kvb, mla_w_out):
    batch, seq_len, d = x.shape
    depth = ada_w.shape[0]
    h, dh, g = N_HEADS, HEAD_DIM, NSA_GROUPS
    sizes = dict(batch=batch, seq_len=seq_len)

    mod = _ada_mod(c, ada_w, ada_b)
    tables = _attention_tables(t5_bias, ATT_T)

    xf = x.reshape(batch * seq_len, d)
    for i in range(depth):
        mixer, j = i % 4, i // 4
        sh1, sc1, gt1, sh2, sc2, gt2 = (mod[i, :, n * d:(n + 1) * d].reshape(batch, 1, d) for n in range(6))
        nm = functools.partial(_norm_mm, xf, norm_g[i, 0], seq_len=seq_len, shift=sh1, scale=sc1)
        if mixer == 0:
            qkv = nm(sb_w_in[j].astype(BF16), name="sb_in")
            o = _sb_attention(qkv, **sizes)
            w_out = sb_w_out[j]
        elif mixer == 1:
            w_in = nsa_w_in[j]
            n_q, n_kv = h * dh, g * dh
            w_att = jnp.concatenate([w_in[:, :n_q], w_in[:, n_q + 2 * n_kv:n_q + 6 * n_kv]], axis=1)
            w_cmp = w_in[:, n_q:n_q + 2 * n_kv]
            w_g = w_in[:, n_q + 6 * n_kv:].reshape(d, 3, g, NSA_REP).transpose(0, 2, 1, 3).reshape(d, g, 3 * NSA_REP)
            w_g = jnp.pad(w_g, ((0, 0), (0, 0), (0, LANE - 3 * NSA_REP))).reshape(d, g * LANE)
            qkv = nm(w_att.astype(BF16), name="nsa_in")
            raw = nm(w_cmp.astype(BF16), out_dtype=F32, name="nsa_in_cmp")
            gates = nm(w_g.astype(BF16), out_dtype=F32, act="sigmoid", name="nsa_in_gates")
            kvc = _nsa_compress(raw, nsa_cmp_pe[j], nsa_cmp_w1[j].reshape(2, NSA_CMP_BLOCK, dh, dh).astype(BF16),
                                nsa_cmp_w2[j].astype(BF16), **sizes)
            n_slots = seq_len // NSA_CMP_STRIDE
            dist_c = np.arange(seq_len)[:, None] - (NSA_CMP_STRIDE * np.arange(n_slots)[None, :] + NSA_CMP_BLOCK - 1)
            bias_c = _t5_gather(t5_bias, _t5_bucket_np(dist_c), tr=min(seq_len, 512))
            oc, sel = _nsa_cmp_attention(qkv, kvc, bias_c, gates, **sizes)
            o = _nsa_main_attention(qkv, sel, tables, gates, oc, **sizes)
            w_out = nsa_w_out[j]
        elif mixer == 2:
            lambda_init = 0.8 - 0.6 * math.exp(-0.3 * i)
            qkv = nm(diff_w_in[j].astype(BF16), name="diff_in")
            o = _diff_attention(qkv, tables, diff_lambda[j], diff_head_g[j], lambda_init=lambda_init, **sizes)
            w_out = diff_w_out[j]
        else:
            w_in = mla_w_in[j]
            nq_l, nkv_l = MLA_Q_LORA, MLA_KV_LORA
            w_lat = jnp.concatenate([w_in[:, :nq_l], jnp.zeros((d, 2 * nkv_l - nq_l), w_in.dtype),
                                     w_in[:, nq_l:nq_l + nkv_l]], axis=1)
            lat = nm(w_lat.astype(BF16), out_dtype=F32, name="mla_in")
            cos1, sin1 = _rope_tables(seq_len, LANE)
            wa, wb = _rope_weights(w_in[:, nq_l + nkv_l:].reshape(d, 1, MLA_ROPE))
            kr = nm(wa, wb=wb, cos=cos1, sin=sin1, name="mla_in_rope")
            w_qb = mla_w_qb[j].reshape(nq_l, h, MLA_NOPE + MLA_ROPE)
            qn = _norm_mm(lat, mla_q_g[j], w_qb[:, :, :MLA_NOPE].reshape(nq_l, h * MLA_NOPE).astype(BF16),
                          seq_len=seq_len, x_cols=nq_l, x_col_block=0, name="mla_q_nope")
            cosh, sinh = _rope_tables(seq_len, h * LANE)
            wa, wb = _rope_weights(w_qb[:, :, MLA_NOPE:])
            qr = _norm_mm(lat, mla_q_g[j], wa, wb=wb, cos=cosh, sin=sinh, seq_len=seq_len, x_cols=nq_l,
                          x_col_block=0, name="mla_q_rope")
            kv = _norm_mm(lat, mla_kv_g[j], mla_w_kvb[j].astype(BF16), seq_len=seq_len, x_cols=nkv_l,
                          x_col_block=2, name="mla_kv")
            o = _mla_attention(qn, qr, kv, kr, **sizes)
            w_out = mla_w_out[j]
        xf = _mm_residual(o, w_out.astype(BF16), xf, gt1, seq_len=seq_len)
        xf = _conv_ffn(xf, norm_g[i, 1], sh2, sc2, ffn_w_up[i].astype(BF16), ffn_conv_w[i], ffn_conv_b[i],
                       ffn_w_down[i].astype(BF16), gt2, final_g, seq_len=seq_len, final_norm=(i == depth - 1))
    return xf.reshape(batch, seq_len, d)
```

```python
import functools
import math

import numpy as np
import jax
import jax.numpy as jnp
from jax import lax
from jax.experimental import pallas as pl
from jax.experimental.pallas import tpu as pltpu

F32 = jnp.float32
BF16 = jnp.bfloat16
EPS = 1e-6
NEG = -1e30

LANE = 128
HALO = 16
VMEM_LIMIT = 56 * 2**20

T5_BUCKETS = 32
T5_MAX_DIST = 128
N_HEADS = 16
HEAD_DIM = 128
NSA_GROUPS = 4
NSA_REP = 4
NSA_CMP_BLOCK = 32
NSA_CMP_STRIDE = 16
NSA_SEL_BLOCK = 64
NSA_TOPN = 16
NSA_WINDOW = 512
FORCED_SCORE = 1e9
DIFF_DIM = 64
MLA_Q_LORA = 768
MLA_KV_LORA = 512
MLA_NOPE = 128
MLA_ROPE = 64
ROPE_THETA = 10000.0
CONV_WIDTH = 3
ATT_T = 256
DIFF_T = 512
MLA_T = 512
ATT_TQ = 512
ATT_TK = 256
LOG2E = 1.4426950408889634
M_INIT = -1e30
MASKED = -2e30
T5_MASKED = T5_BUCKETS


def _params(*sem):
    return pltpu.CompilerParams(dimension_semantics=sem, vmem_limit_bytes=VMEM_LIMIT)


def _dot(a, b):
    return jnp.dot(a, b, preferred_element_type=F32)


def _dot_nt(a, b):
    return lax.dot_general(a, b, (((1,), (1,)), ((), ())), preferred_element_type=F32)


def _split_bf16(x):
    hi = x.astype(BF16)
    lo = (x - hi.astype(F32)).astype(BF16)
    return hi, lo


def _rms(x, g):
    ms = jnp.mean(x * x, axis=-1, keepdims=True)
    return x * lax.rsqrt(ms + EPS) * g


def _ada_kernel(c_ref, w_ref, b_ref, o_ref):
    c = c_ref[...]
    a = c * jax.nn.sigmoid(c)
    a_hi, a_lo = _split_bf16(a)
    w_hi, w_lo = _split_bf16(w_ref[0])
    o_ref[0] = _dot(a_hi, w_hi) + _dot(a_lo, w_hi) + _dot(a_hi, w_lo) + b_ref[0]


def _ada_mod(c, ada_w, ada_b, tn=1024):
    depth, d, n = ada_w.shape
    b = c.shape[0]
    return pl.pallas_call(
        _ada_kernel,
        grid=(depth, n // tn),
        in_specs=[pl.BlockSpec((b, d), lambda l, j: (0, 0)),
                  pl.BlockSpec((1, d, tn), lambda l, j: (l, 0, j)),
                  pl.BlockSpec((1, 1, tn), lambda l, j: (l, 0, j))],
        out_specs=pl.BlockSpec((1, b, tn), lambda l, j: (l, 0, j)),
        out_shape=jax.ShapeDtypeStruct((depth, b, n), F32),
        compiler_params=_params("parallel", "parallel"),
        name="ada_mod",
    )(c, ada_w, ada_b.reshape(depth, 1, n))


def _norm_mm_kernel(*refs, modulated, rope, act):
    it = iter(refs)
    x_ref, g_ref = next(it), next(it)
    sh_ref = sc_ref = wb_ref = cos_ref = sin_ref = None
    if modulated:
        sh_ref, sc_ref = next(it), next(it)
    w_ref = next(it)
    if rope:
        wb_ref, cos_ref, sin_ref = next(it), next(it), next(it)
    o_ref, h_ref = next(it), next(it)

    @pl.when(pl.program_id(1) == 0)
    def _():
        y = _rms(x_ref[...], g_ref[...])
        if modulated:
            y = y * (1.0 + sc_ref[0]) + sh_ref[0]
        h_ref[...] = y.astype(BF16)

    h = h_ref[...]
    acc = _dot(h, w_ref[...])
    if rope:
        acc = acc * cos_ref[...] + _dot(h, wb_ref[...]) * sin_ref[...]
    if act == "sigmoid":
        acc = jax.nn.sigmoid(acc)
    o_ref[...] = acc.astype(o_ref.dtype)


def _norm_mm(x, g, w, *, seq_len, shift=None, scale=None, wb=None, cos=None, sin=None, act=None,
             out_dtype=BF16, x_cols=None, x_col_block=0, tm=1024, tn=512, name="norm_mm"):
    m = x.shape[0]
    k = x.shape[1] if x_cols is None else x_cols
    n = w.shape[1]
    tn = min(tn, n)
    modulated, rope = shift is not None, wb is not None
    tps = seq_len // tm
    in_specs = [pl.BlockSpec((tm, k), lambda i, j: (i, x_col_block)),
                pl.BlockSpec((1, k), lambda i, j: (0, 0))]
    args = [x, g.reshape(1, k)]
    if modulated:
        in_specs += [pl.BlockSpec((1, 1, k), lambda i, j: (i // tps, 0, 0))] * 2
        args += [shift, scale]
    in_specs.append(pl.BlockSpec((k, tn), lambda i, j: (0, j)))
    args.append(w)
    if rope:
        in_specs.append(pl.BlockSpec((k, tn), lambda i, j: (0, j)))
        in_specs += [pl.BlockSpec((tm, tn), lambda i, j: (i % tps, j))] * 2
        args += [wb, cos, sin]
    return pl.pallas_call(
        functools.partial(_norm_mm_kernel, modulated=modulated, rope=rope, act=act),
        grid=(m // tm, n // tn),
        in_specs=in_specs,
        out_specs=pl.BlockSpec((tm, tn), lambda i, j: (i, j)),
        out_shape=jax.ShapeDtypeStruct((m, n), out_dtype),
        scratch_shapes=[pltpu.VMEM((tm, k), BF16)],
        compiler_params=_params("parallel", "arbitrary"),
        name=name,
    )(*args)


def _mm_res_kernel(a_ref, w_ref, x_ref, gate_ref, o_ref):
    o_ref[...] = x_ref[...] + gate_ref[0] * _dot(a_ref[...], w_ref[...])


def _mm_residual(a, w, x, gate, *, seq_len, tm=1024, tn=512, name="mm_residual"):
    m, k = a.shape
    n = w.shape[1]
    tps = seq_len // tm
    return pl.pallas_call(
        _mm_res_kernel,
        grid=(m // tm, n // tn),
        in_specs=[pl.BlockSpec((tm, k), lambda i, j: (i, 0)),
                  pl.BlockSpec((k, tn), lambda i, j: (0, j)),
                  pl.BlockSpec((tm, tn), lambda i, j: (i, j)),
                  pl.BlockSpec((1, 1, tn), lambda i, j: (i // tps, 0, j))],
        out_specs=pl.BlockSpec((tm, tn), lambda i, j: (i, j)),
        out_shape=jax.ShapeDtypeStruct((m, n), F32),
        compiler_params=_params("parallel", "arbitrary"),
        name=name,
    )(a, w, x, gate)


def _ffn_kernel(x_ref, xp_ref, g_ref, sh_ref, sc_ref, wg_ref, wu_ref, cwg_ref, cwu_ref, cbg_ref, cbu_ref,
                wd_ref, gate_ref, fg_ref, o_ref, h_ref, acc_ref, ug_scr, uu_scr, *, tiles_per_seq, final_norm):
    i, j = pl.program_id(0), pl.program_id(1)

    @pl.when(j == 0)
    def _():
        def nm(x):
            return _rms(x, g_ref[...]) * (1.0 + sc_ref[0]) + sh_ref[0]
        h_ref[HALO:, :] = nm(x_ref[...]).astype(BF16)
        keep = jnp.where(i % tiles_per_seq == 0, 0.0, 1.0)
        h_ref[:HALO, :] = (nm(xp_ref[...]) * keep).astype(BF16)
        acc_ref[...] = jnp.zeros_like(acc_ref)

    h = h_ref[...]

    def branch(w_ref, cw_ref, cb_ref, u_scr):
        u_scr[...] = _dot(h, w_ref[...])
        tm = u_scr.shape[0] - HALO
        cw = cw_ref[...]
        return (cw[0:1] * u_scr[pl.ds(HALO - 2, tm), :] + cw[1:2] * u_scr[pl.ds(HALO - 1, tm), :]
                + cw[2:3] * u_scr[pl.ds(HALO, tm), :] + cb_ref[...])

    gt = branch(wg_ref, cwg_ref, cbg_ref, ug_scr)
    up = branch(wu_ref, cwu_ref, cbu_ref, uu_scr)
    a = gt * jax.nn.sigmoid(gt) * up
    acc_ref[...] += _dot(a.astype(BF16), wd_ref[...])

    @pl.when(j == pl.num_programs(1) - 1)
    def _():
        y = x_ref[...] + gate_ref[0] * acc_ref[...]
        if final_norm:
            y = _rms(y, fg_ref[...])
        o_ref[...] = y


def _conv_ffn(x, g, shift, scale, w_up, conv_w, conv_b, w_down, gate, final_g, *, seq_len, final_norm,
              tm=512, tf=512):
    m, d = x.shape
    f = w_down.shape[0]
    nf = f // tf
    tps = seq_len // tm
    hb = tm // HALO
    row = lambda i, j: (i, 0)
    per_batch = lambda i, j: (i // tps, 0, 0)
    return pl.pallas_call(
        functools.partial(_ffn_kernel, tiles_per_seq=tps, final_norm=final_norm),
        grid=(m // tm, nf),
        in_specs=[pl.BlockSpec((tm, d), row),
                  pl.BlockSpec((HALO, d), lambda i, j: (jnp.maximum(i * hb - 1, 0), 0)),
                  pl.BlockSpec((1, d), lambda i, j: (0, 0)),
                  pl.BlockSpec((1, 1, d), per_batch),
                  pl.BlockSpec((1, 1, d), per_batch),
                  pl.BlockSpec((d, tf), lambda i, j: (0, j)),
                  pl.BlockSpec((d, tf), lambda i, j: (0, j + nf)),
                  pl.BlockSpec((CONV_WIDTH, tf), lambda i, j: (0, j)),
                  pl.BlockSpec((CONV_WIDTH, tf), lambda i, j: (0, j + nf)),
                  pl.BlockSpec((1, tf), lambda i, j: (0, j)),
                  pl.BlockSpec((1, tf), lambda i, j: (0, j + nf)),
                  pl.BlockSpec((tf, d), lambda i, j: (j, 0)),
                  pl.BlockSpec((1, 1, d), per_batch),
                  pl.BlockSpec((1, d), lambda i, j: (0, 0))],
        out_specs=pl.BlockSpec((tm, d), row),
        out_shape=jax.ShapeDtypeStruct((m, d), F32),
        scratch_shapes=[pltpu.VMEM((HALO + tm, d), BF16), pltpu.VMEM((tm, d), F32),
                        pltpu.VMEM((HALO + tm, tf), F32), pltpu.VMEM((HALO + tm, tf), F32)],
        compiler_params=_params("parallel", "arbitrary"),
        name="conv_ffn",
    )(x, x, g.reshape(1, d), shift, scale, w_up, w_up, conv_w, conv_w, conv_b.reshape(1, -1),
      conv_b.reshape(1, -1), w_down, gate, final_g.reshape(1, d))


def _t5_bucket_np(dist):
    n = np.maximum(dist, 0)
    max_exact = T5_BUCKETS // 2
    nf = np.maximum(n, 1).astype(np.float64)
    val = np.log(nf / max_exact) / math.log(T5_MAX_DIST / max_exact) * (T5_BUCKETS - max_exact)
    large = max_exact + np.trunc(val + 1e-6).astype(np.int64)
    return np.where(n < max_exact, n, np.minimum(large, T5_BUCKETS - 1)).astype(np.int32)


def _t5_gather_kernel(t5_ref, bkt_ref, o_ref, *, mult):
    h = pl.program_id(0)
    bk = bkt_ref[...]
    acc = jnp.zeros(bk.shape, F32)
    for b in range(T5_BUCKETS):
        acc = jnp.where(bk == b, t5_ref[b, h], acc)
    o_ref[0] = jnp.where(bk == T5_MASKED, MASKED, acc * mult)


def _t5_gather(t5_bias, bkt, tr, mult=1.0):
    rows, cols = bkt.shape
    heads = t5_bias.shape[1]
    return pl.pallas_call(
        functools.partial(_t5_gather_kernel, mult=mult),
        grid=(heads, rows // tr),
        in_specs=[pl.BlockSpec(memory_space=pltpu.SMEM),
                  pl.BlockSpec((tr, cols), lambda h, r: (r, 0))],
        out_specs=pl.BlockSpec((1, tr, cols), lambda h, r: (h, r, 0)),
        out_shape=jax.ShapeDtypeStruct((heads, rows, cols), F32),
        compiler_params=_params("parallel", "parallel"),
        name="t5_gather",
    )(t5_bias, jnp.asarray(bkt))


TAB_DIAG, TAB_SUB, TAB_FAR, TAB_EDGE = 0, 1, 2, 3


def _attention_tables(t5_bias, t):
    i = np.arange(t)[:, None]
    j = np.arange(t)[None, :]
    assert int(_t5_bucket_np(np.array(t + 1))) == T5_BUCKETS - 1
    far = np.full((t, t), T5_BUCKETS - 1, np.int32)
    bkt = np.concatenate([np.where(j <= i, _t5_bucket_np(i - j), T5_MASKED), _t5_bucket_np(t + i - j), far,
                          np.where(j > i, far, T5_MASKED)], axis=0).astype(np.int32)
    return _t5_gather(t5_bias, bkt, tr=t, mult=LOG2E)


def _tile_iota(t):
    return lax.broadcasted_iota(jnp.int32, (t, t), 0), lax.broadcasted_iota(jnp.int32, (t, t), 1)


def _rect_iota(i, tq, start, tk):
    rowg = i * tq + lax.broadcasted_iota(jnp.int32, (tq, tk), 0)
    colg = start + lax.broadcasted_iota(jnp.int32, (tq, tk), 1)
    return rowg, colg


def _osm_reset(m_scr, l_scr, acc_scr):
    m_scr[...] = jnp.full(m_scr.shape, M_INIT, F32)
    l_scr[...] = jnp.zeros(l_scr.shape, F32)
    acc_scr[...] = jnp.zeros(acc_scr.shape, F32)


def _osm_update(t2, rows, slot, p_scr, m_scr, l_scr, acc_scr):
    m_prev = m_scr[rows, :]
    m_new = jnp.maximum(m_prev, jnp.max(t2, axis=-1, keepdims=True))
    p = jnp.exp2(t2 - pltpu.repeat(m_new, t2.shape[1] // LANE, axis=1))
    alpha = jnp.exp2(m_prev - m_new)
    l_scr[rows, :] = alpha * l_scr[rows, :] + jnp.sum(p, axis=-1, keepdims=True)
    m_scr[rows, :] = m_new
    p_scr[slot, rows, :] = p.astype(BF16)
    acc_scr[rows, :] = alpha * acc_scr[rows, :]


def _pipelined_tiles(n_tiles, scores, update, update_last=None):
    update_last = update if update_last is None else update_last
    last = n_tiles - 1
    n_pairs = last // 2
    scores(0, 0)

    def pair(n, cr):
        scores(2 * n + 1, 1)
        update(2 * n, 0)
        scores(2 * n + 2, 0)
        update(2 * n + 1, 1)
        return cr

    lax.fori_loop(0, n_pairs, pair, 0)

    @pl.when(last % 2 == 0)
    def _():
        update_last(last, 0)

    @pl.when(last % 2 == 1)
    def _():
        scores(last, 1)
        update(last - 1, 0)
        update_last(last, 1)


def _att_scratch(rows, tk, slots=2):
    return [pltpu.VMEM((slots, rows, tk), F32), pltpu.VMEM((slots, rows, tk), BF16), pltpu.VMEM((rows, LANE), F32),
            pltpu.VMEM((rows, LANE), F32), pltpu.VMEM((rows, HEAD_DIM), F32)]


def _sb_kernel(q_ref, k_ref, v_ref, o_ref, s_scr, c_scr, acc_scr, *, tq, tk, scale):
    i = pl.program_id(2)
    q = q_ref[...]
    row, col = _tile_iota(tk)
    upper = jnp.where(row > col, 1.0, 0.0).astype(BF16)
    c_scr[...] = jnp.zeros(c_scr.shape, F32)
    acc_scr[...] = jnp.zeros(acc_scr.shape, F32)
    reps = tk // LANE

    def scores(kb, slot):
        start = pl.multiple_of(kb * tk, tk)
        s_scr[slot] = _dot_nt(q, k_ref[pl.ds(start, tk), :])

    def update(kb, slot, diag):
        start = pl.multiple_of(kb * tk, tk)
        nz = s_scr[slot] * (-scale)
        e = jnp.exp2(jnp.abs(nz) * (-LOG2E))
        lk = jnp.minimum(nz, 0.0) - jnp.log(1.0 + e)
        if diag:
            rowg, colg = _rect_iota(i, tq, start, tk)
            past = colg < rowg
            lk = jnp.where(past, lk, 0.0)
        hi, lo = _split_bf16(lk)
        c = c_scr[...]
        later = _dot(hi, upper) + _dot(lo, upper) + pltpu.repeat(c, reps, axis=1)
        a = jnp.exp(lk - nz + later)
        if diag:
            a = jnp.where(past, a, 0.0)
        acc_scr[...] += _dot(a.astype(BF16), v_ref[pl.ds(start, tk), :])
        c_scr[...] = c + jnp.sum(lk, axis=-1, keepdims=True)

    assert tq == 2 * tk
    scores(2 * i + 1, 1)
    scores(2 * i, 0)
    update(2 * i + 1, 1, True)
    scores(jnp.maximum(2 * i - 1, 0), 1)
    update(2 * i, 0, True)

    def pair(n, cr):
        kb = 2 * (i - n) - 1
        scores(kb - 1, 0)
        update(kb, 1, False)
        scores(jnp.maximum(kb - 2, 0), 1)
        update(kb - 1, 0, False)
        return cr

    lax.fori_loop(0, i, pair, 0)
    o_ref[...] = acc_scr[...].astype(o_ref.dtype)


def _sb_attention(qkv, *, batch, seq_len, tq=ATT_TQ, tk=ATT_TK):
    h = N_HEADS
    nq = seq_len // tq
    return pl.pallas_call(
        functools.partial(_sb_kernel, tq=tq, tk=tk, scale=HEAD_DIM ** -0.5),
        grid=(batch, h, nq),
        in_specs=[pl.BlockSpec((tq, HEAD_DIM), lambda b, hh, i: (b * nq + i, hh)),
                  pl.BlockSpec((seq_len, HEAD_DIM), lambda b, hh, i: (b, h + hh)),
                  pl.BlockSpec((seq_len, HEAD_DIM), lambda b, hh, i: (b, 2 * h + hh))],
        out_specs=pl.BlockSpec((tq, HEAD_DIM), lambda b, hh, i: (b * nq + i, hh)),
        out_shape=jax.ShapeDtypeStruct((batch * seq_len, h * HEAD_DIM), BF16),
        scratch_shapes=[pltpu.VMEM((2, tq, tk), F32), pltpu.VMEM((tq, LANE), F32), pltpu.VMEM((tq, HEAD_DIM), F32)],
        compiler_params=_params("parallel", "parallel", "arbitrary"),
        name="sb_attention",
    )(qkv, qkv, qkv)


def _table_offset(kb, i, t):
    return pl.multiple_of(jnp.where(kb == i, TAB_DIAG * t, jnp.where(kb == i - 1, TAB_SUB * t, TAB_FAR * t)), t)


def _diff_kernel(q_ref, k_ref, v_ref, tab_ref, lam_ref, hg_ref, o_ref, s_scr, p_scr, m_scr, l_scr, acc_scr, *, t,
                 scale, lambda_init):
    i = pl.program_id(2)
    q = q_ref[...]
    lane = lax.broadcasted_iota(jnp.int32, q.shape, 1)
    zero = jnp.zeros_like(q)
    q2 = jnp.concatenate([jnp.where(lane < DIFF_DIM, q, zero), jnp.where(lane >= DIFF_DIM, q, zero)], axis=0)
    _osm_reset(m_scr, l_scr, acc_scr)

    def scores(kb, slot):
        s_scr[slot] = _dot_nt(q2, k_ref[pl.ds(pl.multiple_of(kb * t, t), t), :])

    def update(kb, slot):
        off = _table_offset(kb, i, t)
        for half in range(2):
            rows = pl.ds(half * t, t)
            t2 = s_scr[slot, rows, :] * (scale * LOG2E) + tab_ref[0, pl.ds(off, t), :]
            _osm_update(t2, rows, slot, p_scr, m_scr, l_scr, acc_scr)
        acc_scr[...] += _dot(p_scr[slot], v_ref[pl.ds(pl.multiple_of(kb * t, t), t), :])

    _pipelined_tiles(i + 1, scores, update)

    lam = lam_ref[...]
    lmbda = (jnp.exp(jnp.sum(lam[0:1] * lam[1:2], axis=-1, keepdims=True))
             - jnp.exp(jnp.sum(lam[2:3] * lam[3:4], axis=-1, keepdims=True)) + lambda_init)
    o = acc_scr[...] / l_scr[...]
    o = o[:t] - lmbda * o[t:]
    o_ref[...] = (_rms(o, hg_ref[...]) * (1.0 - lambda_init)).astype(o_ref.dtype)


def _diff_attention(qkv, tables, lam, head_g, *, batch, seq_len, lambda_init, t=DIFF_T):
    h = N_HEADS
    nq = seq_len // t
    return pl.pallas_call(
        functools.partial(_diff_kernel, t=t, scale=DIFF_DIM ** -0.5, lambda_init=lambda_init),
        grid=(batch, h, nq),
        in_specs=[pl.BlockSpec((t, HEAD_DIM), lambda b, hh, i: (b * nq + i, hh)),
                  pl.BlockSpec((seq_len, HEAD_DIM), lambda b, hh, i: (b, h + hh)),
                  pl.BlockSpec((seq_len, HEAD_DIM), lambda b, hh, i: (b, 2 * h + hh)),
                  pl.BlockSpec((1, 4 * t, t), lambda b, hh, i: (hh, 0, 0)),
                  pl.BlockSpec((4, DIFF_DIM), lambda b, hh, i: (0, 0)),
                  pl.BlockSpec((1, HEAD_DIM), lambda b, hh, i: (0, 0))],
        out_specs=pl.BlockSpec((t, HEAD_DIM), lambda b, hh, i: (b * nq + i, hh)),
        out_shape=jax.ShapeDtypeStruct((batch * seq_len, h * HEAD_DIM), BF16),
        scratch_shapes=_att_scratch(2 * t, t),
        compiler_params=_params("parallel", "parallel", "arbitrary"),
        name="diff_attention",
    )(qkv, qkv, qkv, tables, lam, head_g.reshape(1, HEAD_DIM))


def _mla_kernel(qn_ref, qr_ref, kn_ref, kr_ref, v_ref, o_ref, s_scr, p_scr, m_scr, l_scr, acc_scr, *, tq, tk, scale):
    i = pl.program_id(2)
    q = jnp.concatenate([qn_ref[...], qr_ref[...]], axis=-1)
    _osm_reset(m_scr, l_scr, acc_scr)

    def scores(kb, slot):
        start = pl.multiple_of(kb * tk, tk)
        k = jnp.concatenate([kn_ref[pl.ds(start, tk), :], kr_ref[pl.ds(start, tk), :]], axis=-1)
        s_scr[slot] = _dot_nt(q, k)

    def update(kb, slot, mask):
        start = pl.multiple_of(kb * tk, tk)
        t2 = s_scr[slot] * (scale * LOG2E)
        if mask:
            rowg, colg = _rect_iota(i, tq, start, tk)
            t2 = jnp.where(colg <= rowg, t2, MASKED)
        _osm_update(t2, pl.ds(0, tq), slot, p_scr, m_scr, l_scr, acc_scr)
        acc_scr[...] += _dot(p_scr[slot], v_ref[pl.ds(start, tk), :])

    assert tq == tk
    _pipelined_tiles(i + 1, scores, lambda kb, slot: update(kb, slot, False), lambda kb, slot: update(kb, slot, True))
    o_ref[...] = (acc_scr[...] / l_scr[...]).astype(o_ref.dtype)


def _mla_attention(qn, qr, kv, kr, *, batch, seq_len, tq=MLA_T, tk=MLA_T):
    h = N_HEADS
    nq = seq_len // tq
    qspec = pl.BlockSpec((tq, HEAD_DIM), lambda b, hh, i: (b * nq + i, hh))
    return pl.pallas_call(
        functools.partial(_mla_kernel, tq=tq, tk=tk, scale=(MLA_NOPE + MLA_ROPE) ** -0.5),
        grid=(batch, h, nq),
        in_specs=[qspec, qspec,
                  pl.BlockSpec((seq_len, HEAD_DIM), lambda b, hh, i: (b, 2 * hh)),
                  pl.BlockSpec((seq_len, HEAD_DIM), lambda b, hh, i: (b, 0)),
                  pl.BlockSpec((seq_len, HEAD_DIM), lambda b, hh, i: (b, 2 * hh + 1))],
        out_specs=qspec,
        out_shape=jax.ShapeDtypeStruct((batch * seq_len, h * HEAD_DIM), BF16),
        scratch_shapes=_att_scratch(tq, tk),
        compiler_params=_params("parallel", "parallel", "arbitrary"),
        name="mla_attention",
    )(qn, qr, kv, kr, kv)


def _compress_kernel(raw_ref, pe_ref, w1_ref, w2_ref, o_ref, *, n_slots):
    half = NSA_CMP_BLOCK // 2
    p1 = jnp.zeros((n_slots, HEAD_DIM), F32)
    p2 = jnp.zeros((n_slots, HEAD_DIM), F32)
    for l in range(half):
        a = raw_ref[pl.ds(l, n_slots, stride=NSA_CMP_STRIDE), :]
        p1 = p1 + _dot((a + pe_ref[0, l:l + 1, :]).astype(BF16), w1_ref[0, l])
        p2 = p2 + _dot((a + pe_ref[0, half + l:half + l + 1, :]).astype(BF16), w1_ref[0, half + l])
    pre = p1 + pltpu.roll(p2, n_slots - 1, 0)
    hid = pre * jax.nn.sigmoid(pre)
    o_ref[0, 0, 0] = _dot(hid.astype(BF16), w2_ref[0]).astype(o_ref.dtype)


def _nsa_compress(raw, pe, w1, w2, *, batch, seq_len):
    g = NSA_GROUPS
    n_slots = seq_len // NSA_CMP_STRIDE
    return pl.pallas_call(
        functools.partial(_compress_kernel, n_slots=n_slots),
        grid=(batch, 2, g),
        in_specs=[pl.BlockSpec((seq_len, HEAD_DIM), lambda b, kv, gg: (b, kv * g + gg)),
                  pl.BlockSpec((1, NSA_CMP_BLOCK, HEAD_DIM), lambda b, kv, gg: (kv, 0, 0)),
                  pl.BlockSpec((1, NSA_CMP_BLOCK, HEAD_DIM, HEAD_DIM), lambda b, kv, gg: (kv, 0, 0, 0)),
                  pl.BlockSpec((1, HEAD_DIM, HEAD_DIM), lambda b, kv, gg: (kv, 0, 0))],
        out_specs=pl.BlockSpec((1, 1, 1, n_slots, HEAD_DIM), lambda b, kv, gg: (b, kv, gg, 0, 0)),
        out_shape=jax.ShapeDtypeStruct((batch, 2, g, n_slots, HEAD_DIM), BF16),
        compiler_params=_params("parallel", "parallel", "parallel"),
        name="nsa_compress",
    )(raw, pe, w1, w2)


def _nsa_cmp_kernel(q_ref, kc_ref, vc_ref, bias_ref, gates_ref, ovt_ref, oc_ref, sel_ref, *, t, n_slots, n_sel,
                    n_top, scale):
    i = pl.program_id(2)
    kc = kc_ref[0, 0, 0]
    vc = vc_ref[0, 0, 0]
    qpos = i * t + lax.broadcasted_iota(jnp.int32, (t, n_slots), 0)
    cmp_end = NSA_CMP_STRIDE * lax.broadcasted_iota(jnp.int32, (t, n_slots), 1) + (NSA_CMP_BLOCK - 1)
    valid = cmp_end <= qpos
    gates = gates_ref[...]
    psum = jnp.zeros((t, n_slots), F32)
    for r in range(NSA_REP):
        q = q_ref[:, r * HEAD_DIM:(r + 1) * HEAD_DIM]
        s = jnp.where(valid, _dot_nt(q, kc) * scale + bias_ref[r], NEG)
        m = jnp.max(s, axis=-1, keepdims=True)
        p = jnp.where(valid, jnp.exp(s - m), 0.0)
        p = p / jnp.maximum(jnp.sum(p, axis=-1, keepdims=True), 1e-30)
        psum = psum + p
        oc_ref[:, r * HEAD_DIM:(r + 1) * HEAD_DIM] = gates[:, r:r + 1] * _dot(p.astype(BF16), vc)

    hi, lo = _split_bf16(psum)
    ovt = ovt_ref[...]
    imp = _dot_nt(ovt, hi) + _dot_nt(ovt, lo)
    blk = lax.broadcasted_iota(jnp.int32, (n_sel, t), 0)
    tpos = i * t + lax.broadcasted_iota(jnp.int32, (n_sel, t), 1)
    cur = tpos // NSA_SEL_BLOCK
    forced = (blk == 0) | (blk == cur) | (blk == cur - 1)
    score = jnp.where(blk * NSA_SEL_BLOCK <= tpos, jnp.where(forced, FORCED_SCORE, imp), -1.0)
    rank = jnp.zeros((n_sel, t), F32)
    for mm in range(n_sel):
        sm = score[mm:mm + 1, :]
        ahead = (sm > score) | ((sm == score) & (blk > mm))
        rank = rank + jnp.where(ahead, 1.0, 0.0)
    sel_t = jnp.where(rank < n_top, 1.0, 0.0).astype(BF16)
    row, col = _tile_iota(t)
    eye = jnp.where(row == col, 1.0, 0.0).astype(BF16)
    sel_ref[0, 0] = _dot_nt(eye, sel_t).astype(sel_ref.dtype)


def _nsa_cmp_attention(q_all, kvc, bias_c, gates, *, batch, seq_len, t=ATT_T):
    g = NSA_GROUPS
    nq = seq_len // t
    n_slots = seq_len // NSA_CMP_STRIDE
    n_sel = seq_len // NSA_SEL_BLOCK
    c0 = NSA_CMP_STRIDE * np.arange(n_slots)[:, None]
    s0 = NSA_SEL_BLOCK * np.arange(n_sel)[None, :]
    overlap = (c0 < s0 + NSA_SEL_BLOCK) & (c0 + NSA_CMP_BLOCK > s0)
    ovt = jnp.asarray(overlap.T.astype(np.float32), dtype=BF16)
    gw = NSA_REP * HEAD_DIM
    return pl.pallas_call(
        functools.partial(_nsa_cmp_kernel, t=t, n_slots=n_slots, n_sel=n_sel, n_top=min(NSA_TOPN, n_sel),
                          scale=HEAD_DIM ** -0.5),
        grid=(batch, g, nq),
        in_specs=[pl.BlockSpec((t, gw), lambda b, gg, i: (b * nq + i, gg)),
                  pl.BlockSpec((1, 1, 1, n_slots, HEAD_DIM), lambda b, gg, i: (b, 0, gg, 0, 0)),
                  pl.BlockSpec((1, 1, 1, n_slots, HEAD_DIM), lambda b, gg, i: (b, 1, gg, 0, 0)),
                  pl.BlockSpec((NSA_REP, t, n_slots), lambda b, gg, i: (gg, i, 0)),
                  pl.BlockSpec((t, LANE), lambda b, gg, i: (b * nq + i, gg)),
                  pl.BlockSpec((n_sel, n_slots), lambda b, gg, i: (0, 0))],
        out_specs=[pl.BlockSpec((t, gw), lambda b, gg, i: (b * nq + i, gg)),
                   pl.BlockSpec((1, 1, t, n_sel), lambda b, gg, i: (b, gg, i, 0))],
        out_shape=[jax.ShapeDtypeStruct((batch * seq_len, g * gw), F32),
                   jax.ShapeDtypeStruct((batch, g, seq_len, n_sel), BF16)],
        compiler_params=_params("parallel", "parallel", "arbitrary"),
        name="nsa_cmp_attention",
    )(q_all, kvc, kvc, bias_c, gates, ovt)


def _nsa_main_kernel(q_ref, ks_ref, vs_ref, kw_ref, vw_ref, sel_ref, tab_ref, gates_ref, oc_ref, o_ref,
                     s_scr, p_scr, m_scr, l_scr, acc_scr, os_scr, mk_scr, *, t, scale):
    i = pl.program_id(2)
    rep = NSA_REP
    q = jnp.concatenate([q_ref[:, r * HEAD_DIM:(r + 1) * HEAD_DIM] for r in range(rep)], axis=0)

    sel = sel_ref[0, 0]
    n_sel = sel.shape[1]
    blk_row = lax.broadcasted_iota(jnp.int32, (n_sel, t), 0)
    key_col = lax.broadcasted_iota(jnp.int32, (n_sel, t), 1)
    _osm_reset(m_scr, l_scr, acc_scr)

    def sel_scores(kb, slot):
        s_scr[slot] = _dot_nt(q, ks_ref[pl.ds(pl.multiple_of(kb * t, t), t), :])

    def sel_update(kb, slot):
        start = pl.multiple_of(kb * t, t)
        expand = jnp.where((start + key_col) // NSA_SEL_BLOCK == blk_row, 1.0, 0.0).astype(BF16)
        mk_scr[...] = _dot(sel, expand)
        off = _table_offset(kb, i, t)
        for r in range(rep):
            rows = pl.ds(r * t, t)
            t2 = s_scr[slot, rows, :] * (scale * LOG2E) + tab_ref[r, pl.ds(off, t), :]
            t2 = jnp.where(mk_scr[...] > 0.5, t2, MASKED)
            _osm_update(t2, rows, slot, p_scr, m_scr, l_scr, acc_scr)
        acc_scr[...] += _dot(p_scr[slot], vs_ref[pl.ds(start, t), :])

    _pipelined_tiles(i + 1, sel_scores, sel_update)
    os_scr[...] = acc_scr[...] / l_scr[...]

    n_back = NSA_WINDOW // t
    assert n_back == 2
    _osm_reset(m_scr, l_scr, acc_scr)

    def win_scores(kb, slot):
        s_scr[slot] = _dot_nt(q, kw_ref[pl.ds(pl.multiple_of(kb * t, t), t), :])

    def win_update(kb, slot, region):
        for r in range(rep):
            rows = pl.ds(r * t, t)
            t2 = s_scr[slot, rows, :] * (scale * LOG2E) + tab_ref[r, pl.ds(region * t, t), :]
            _osm_update(t2, rows, slot, p_scr, m_scr, l_scr, acc_scr)
        acc_scr[...] += _dot(p_scr[slot], vw_ref[pl.ds(pl.multiple_of(kb * t, t), t), :])

    def window(n_tiles):
        regions = (TAB_DIAG, TAB_SUB, TAB_EDGE)
        for back in range(n_tiles):
            win_scores(i - back, back)
        for back in reversed(range(n_tiles)):
            win_update(i - back, back, regions[back])

    pl.when(i >= 2)(lambda: window(3))
    pl.when(i == 1)(lambda: window(2))
    pl.when(i == 0)(lambda: window(1))
    o_w = acc_scr[...] / l_scr[...]
    o_s = os_scr[...]

    gates = gates_ref[...]
    for r in range(rep):
        rows = slice(r * t, (r + 1) * t)
        cols = slice(r * HEAD_DIM, (r + 1) * HEAD_DIM)
        o = oc_ref[:, cols] + gates[:, rep + r:rep + r + 1] * o_s[rows] + gates[:, 2 * rep + r:2 * rep + r + 1] * o_w[rows]
        o_ref[:, cols] = o.astype(o_ref.dtype)


def _nsa_main_attention(qkv, sel, tables, gates, oc, *, batch, seq_len, t=ATT_T):
    g = NSA_GROUPS
    nq = seq_len // t
    n_sel = seq_len // NSA_SEL_BLOCK
    gw = NSA_REP * HEAD_DIM
    qb = N_HEADS
    kv = lambda which: pl.BlockSpec((seq_len, HEAD_DIM), lambda b, gg, i: (b, qb + which * g + gg))
    tile = pl.BlockSpec((t, gw), lambda b, gg, i: (b * nq + i, gg))
    rows = NSA_REP * t
    return pl.pallas_call(
        functools.partial(_nsa_main_kernel, t=t, scale=HEAD_DIM ** -0.5),
        grid=(batch, g, nq),
        in_specs=[tile, kv(0), kv(1), kv(2), kv(3),
                  pl.BlockSpec((1, 1, t, n_sel), lambda b, gg, i: (b, gg, i, 0)),
                  pl.BlockSpec((NSA_REP, 4 * t, t), lambda b, gg, i: (gg, 0, 0)),
                  pl.BlockSpec((t, LANE), lambda b, gg, i: (b * nq + i, gg)),
                  tile],
        out_specs=tile,
        out_shape=jax.ShapeDtypeStruct((batch * seq_len, g * gw), BF16),
        scratch_shapes=_att_scratch(rows, t, slots=3) + [pltpu.VMEM((rows, HEAD_DIM), F32), pltpu.VMEM((t, t), F32)],
        compiler_params=_params("parallel", "parallel", "arbitrary"),
        name="nsa_main_attention",
    )(qkv, qkv, qkv, qkv, qkv, sel, tables, gates, oc)


def _rope_tables(seq_len, width):
    half = MLA_ROPE // 2
    inv = np.power(ROPE_THETA, -np.arange(half, dtype=np.float32) / half).astype(np.float32)
    ang = np.arange(seq_len, dtype=np.float32)[:, None] * inv[None, :]
    pad = np.zeros((seq_len, LANE - MLA_ROPE), np.float32)
    cos = np.concatenate([np.cos(ang), np.cos(ang), pad], axis=1)
    sin = np.concatenate([np.sin(ang), np.sin(ang), pad], axis=1)
    reps = width // LANE
    return jnp.asarray(np.tile(cos, (1, reps))), jnp.asarray(np.tile(sin, (1, reps)))


def _rope_weights(w):
    k, n, _ = w.shape
    half = MLA_ROPE // 2
    pad = jnp.zeros((k, n, LANE - MLA_ROPE), w.dtype)
    wa = jnp.concatenate([w, pad], axis=-1)
    wb = jnp.concatenate([-w[..., half:], w[..., :half], pad], axis=-1)
    return wa.reshape(k, n * LANE).astype(BF16), wb.reshape(k, n * LANE).astype(BF16)


def kernel(x, c, t5_bias, ada_w, ada_b, norm_g, final_g, ffn_w_up, ffn_conv_w, ffn_conv_b, ffn_w_down, sb_w_in, sb_w_out, nsa_w_in, nsa_cmp_pe, nsa_cmp_w1, nsa_cmp_w2, nsa_w_out, diff_w_in, diff_lambda, diff_head_g, diff_w_out, mla_w_in, mla_q_g, mla_w_qb, mla_kv_g, mla_w_kvb, mla_w_out):
    batch, seq_len, d = x.shape
    depth = ada_w.shape[0]
    h, dh, g = N_HEADS, HEAD_DIM, NSA_GROUPS
    sizes = dict(batch=batch, seq_len=seq_len)

    mod = _ada_mod(c, ada_w, ada_b)
    tables = _attention_tables(t5_bias, ATT_T)

    xf = x.reshape(batch * seq_len, d)
    for i in range(depth):
        mixer, j = i % 4, i // 4
        sh1, sc1, gt1, sh2, sc2, gt2 = (mod[i, :, n * d:(n + 1) * d].reshape(batch, 1, d) for n in range(6))
        nm = functools.partial(_norm_mm, xf, norm_g[i, 0], seq_len=seq_len, shift=sh1, scale=sc1)
        if mixer == 0:
            qkv = nm(sb_w_in[j].astype(BF16), name="sb_in")
            o = _sb_attention(qkv, **sizes)
            w_out = sb_w_out[j]
        elif mixer == 1:
            w_in = nsa_w_in[j]
            n_q, n_kv = h * dh, g * dh
            w_att = jnp.concatenate([w_in[:, :n_q], w_in[:, n_q + 2 * n_kv:n_q + 6 * n_kv]], axis=1)
            w_cmp = w_in[:, n_q:n_q + 2 * n_kv]
            w_g = w_in[:, n_q + 6 * n_kv:].reshape(d, 3, g, NSA_REP).transpose(0, 2, 1, 3).reshape(d, g, 3 * NSA_REP)
            w_g = jnp.pad(w_g, ((0, 0), (0, 0), (0, LANE - 3 * NSA_REP))).reshape(d, g * LANE)
            qkv = nm(w_att.astype(BF16), name="nsa_in")
            raw = nm(w_cmp.astype(BF16), out_dtype=F32, name="nsa_in_cmp")
            gates = nm(w_g.astype(BF16), out_dtype=F32, act="sigmoid", name="nsa_in_gates")
            kvc = _nsa_compress(raw, nsa_cmp_pe[j], nsa_cmp_w1[j].reshape(2, NSA_CMP_BLOCK, dh, dh).astype(BF16),
                                nsa_cmp_w2[j].astype(BF16), **sizes)
            n_slots = seq_len // NSA_CMP_STRIDE
            dist_c = np.arange(seq_len)[:, None] - (NSA_CMP_STRIDE * np.arange(n_slots)[None, :] + NSA_CMP_BLOCK - 1)
            bias_c = _t5_gather(t5_bias, _t5_bucket_np(dist_c), tr=min(seq_len, 512))
            oc, sel = _nsa_cmp_attention(qkv, kvc, bias_c, gates, **sizes)
            o = _nsa_main_attention(qkv, sel, tables, gates, oc, **sizes)
            w_out = nsa_w_out[j]
        elif mixer == 2:
            lambda_init = 0.8 - 0.6 * math.exp(-0.3 * i)
            qkv = nm(diff_w_in[j].astype(BF16), name="diff_in")
            o = _diff_attention(qkv, _attention_tables(t5_bias, DIFF_T), diff_lambda[j], diff_head_g[j],
                                lambda_init=lambda_init, **sizes)
            w_out = diff_w_out[j]
        else:
            w_in = mla_w_in[j]
            nq_l, nkv_l = MLA_Q_LORA, MLA_KV_LORA
            w_lat = jnp.concatenate([w_in[:, :nq_l], jnp.zeros((d, 2 * nkv_l - nq_l), w_in.dtype),
                                     w_in[:, nq_l:nq_l + nkv_l]], axis=1)
            lat = nm(w_lat.astype(BF16), out_dtype=F32, name="mla_in")
            cos1, sin1 = _rope_tables(seq_len, LANE)
            wa, wb = _rope_weights(w_in[:, nq_l + nkv_l:].reshape(d, 1, MLA_ROPE))
            kr = nm(wa, wb=wb, cos=cos1, sin=sin1, name="mla_in_rope")
            w_qb = mla_w_qb[j].reshape(nq_l, h, MLA_NOPE + MLA_ROPE)
            qn = _norm_mm(lat, mla_q_g[j], w_qb[:, :, :MLA_NOPE].reshape(nq_l, h * MLA_NOPE).astype(BF16),
                          seq_len=seq_len, x_cols=nq_l, x_col_block=0, name="mla_q_nope")
            cosh, sinh = _rope_tables(seq_len, h * LANE)
            wa, wb = _rope_weights(w_qb[:, :, MLA_NOPE:])
            qr = _norm_mm(lat, mla_q_g[j], wa, wb=wb, cos=cosh, sin=sinh, seq_len=seq_len, x_cols=nq_l,
                          x_col_block=0, name="mla_q_rope")
            kv = _norm_mm(lat, mla_kv_g[j], mla_w_kvb[j].astype(BF16), seq_len=seq_len, x_cols=nkv_l,
                          x_col_block=2, name="mla_kv")
            o = _mla_attention(qn, qr, kv, kr, **sizes)
            w_out = mla_w_out[j]
        xf = _mm_residual(o, w_out.astype(BF16), xf, gt1, seq_len=seq_len)
        xf = _conv_ffn(xf, norm_g[i, 1], sh2, sc2, ffn_w_up[i].astype(BF16), ffn_conv_w[i], ffn_conv_b[i],
                       ffn_w_down[i].astype(BF16), gt2, final_g, seq_len=seq_len, final_norm=(i == depth - 1))
    return xf.reshape(batch, seq_len, d)
```

```python
import functools
import math

import numpy as np
import jax
import jax.numpy as jnp
from jax import lax
from jax.experimental import pallas as pl
from jax.experimental.pallas import tpu as pltpu

F32 = jnp.float32
BF16 = jnp.bfloat16
EPS = 1e-6
NEG = -1e30

LANE = 128
HALO = 16
VMEM_LIMIT = 56 * 2**20

T5_BUCKETS = 32
T5_MAX_DIST = 128
N_HEADS = 16
HEAD_DIM = 128
NSA_GROUPS = 4
NSA_REP = 4
NSA_CMP_BLOCK = 32
NSA_CMP_STRIDE = 16
NSA_SEL_BLOCK = 64
NSA_TOPN = 16
NSA_WINDOW = 512
FORCED_SCORE = 1e9
DIFF_DIM = 64
MLA_Q_LORA = 768
MLA_KV_LORA = 512
MLA_NOPE = 128
MLA_ROPE = 64
ROPE_THETA = 10000.0
CONV_WIDTH = 3
ATT_T = 256
DIFF_T = 512
MLA_T = 512
ATT_TQ = 512
ATT_TK = 256
LOG2E = 1.4426950408889634
M_INIT = -1e30
MASKED = -2e30
SB_DEAD = -120.0
T5_MASKED = T5_BUCKETS


def _params(*sem):
    return pltpu.CompilerParams(dimension_semantics=sem, vmem_limit_bytes=VMEM_LIMIT)


def _dot(a, b):
    return jnp.dot(a, b, preferred_element_type=F32)


def _dot_nt(a, b):
    return lax.dot_general(a, b, (((1,), (1,)), ((), ())), preferred_element_type=F32)


def _split_bf16(x):
    hi = x.astype(BF16)
    lo = (x - hi.astype(F32)).astype(BF16)
    return hi, lo


def _rms(x, g):
    ms = jnp.mean(x * x, axis=-1, keepdims=True)
    return x * lax.rsqrt(ms + EPS) * g


def _ada_kernel(c_ref, w_ref, b_ref, o_ref):
    c = c_ref[...]
    a = c * jax.nn.sigmoid(c)
    a_hi, a_lo = _split_bf16(a)
    w_hi, w_lo = _split_bf16(w_ref[0])
    o_ref[0] = _dot(a_hi, w_hi) + _dot(a_lo, w_hi) + _dot(a_hi, w_lo) + b_ref[0]


def _ada_mod(c, ada_w, ada_b, tn=1024):
    depth, d, n = ada_w.shape
    b = c.shape[0]
    return pl.pallas_call(
        _ada_kernel,
        grid=(depth, n // tn),
        in_specs=[pl.BlockSpec((b, d), lambda l, j: (0, 0)),
                  pl.BlockSpec((1, d, tn), lambda l, j: (l, 0, j)),
                  pl.BlockSpec((1, 1, tn), lambda l, j: (l, 0, j))],
        out_specs=pl.BlockSpec((1, b, tn), lambda l, j: (l, 0, j)),
        out_shape=jax.ShapeDtypeStruct((depth, b, n), F32),
        compiler_params=_params("parallel", "parallel"),
        name="ada_mod",
    )(c, ada_w, ada_b.reshape(depth, 1, n))


def _norm_mm_kernel(*refs, modulated, rope, act):
    it = iter(refs)
    x_ref, g_ref = next(it), next(it)
    sh_ref = sc_ref = wb_ref = cos_ref = sin_ref = None
    if modulated:
        sh_ref, sc_ref = next(it), next(it)
    w_ref = next(it)
    if rope:
        wb_ref, cos_ref, sin_ref = next(it), next(it), next(it)
    o_ref, h_ref = next(it), next(it)

    @pl.when(pl.program_id(1) == 0)
    def _():
        y = _rms(x_ref[...], g_ref[...])
        if modulated:
            y = y * (1.0 + sc_ref[0]) + sh_ref[0]
        h_ref[...] = y.astype(BF16)

    h = h_ref[...]
    acc = _dot(h, w_ref[...])
    if rope:
        acc = acc * cos_ref[...] + _dot(h, wb_ref[...]) * sin_ref[...]
    if act == "sigmoid":
        acc = jax.nn.sigmoid(acc)
    o_ref[...] = acc.astype(o_ref.dtype)


def _norm_mm(x, g, w, *, seq_len, shift=None, scale=None, wb=None, cos=None, sin=None, act=None,
             out_dtype=BF16, x_cols=None, x_col_block=0, tm=1024, tn=512, name="norm_mm"):
    m = x.shape[0]
    k = x.shape[1] if x_cols is None else x_cols
    n = w.shape[1]
    tn = min(tn, n)
    modulated, rope = shift is not None, wb is not None
    tps = seq_len // tm
    in_specs = [pl.BlockSpec((tm, k), lambda i, j: (i, x_col_block)),
                pl.BlockSpec((1, k), lambda i, j: (0, 0))]
    args = [x, g.reshape(1, k)]
    if modulated:
        in_specs += [pl.BlockSpec((1, 1, k), lambda i, j: (i // tps, 0, 0))] * 2
        args += [shift, scale]
    in_specs.append(pl.BlockSpec((k, tn), lambda i, j: (0, j)))
    args.append(w)
    if rope:
        in_specs.append(pl.BlockSpec((k, tn), lambda i, j: (0, j)))
        in_specs += [pl.BlockSpec((tm, tn), lambda i, j: (i % tps, j))] * 2
        args += [wb, cos, sin]
    return pl.pallas_call(
        functools.partial(_norm_mm_kernel, modulated=modulated, rope=rope, act=act),
        grid=(m // tm, n // tn),
        in_specs=in_specs,
        out_specs=pl.BlockSpec((tm, tn), lambda i, j: (i, j)),
        out_shape=jax.ShapeDtypeStruct((m, n), out_dtype),
        scratch_shapes=[pltpu.VMEM((tm, k), BF16)],
        compiler_params=_params("parallel", "arbitrary"),
        name=name,
    )(*args)


def _mm_res_kernel(a_ref, w_ref, x_ref, gate_ref, o_ref):
    o_ref[...] = x_ref[...] + gate_ref[0] * _dot(a_ref[...], w_ref[...])


def _mm_residual(a, w, x, gate, *, seq_len, tm=1024, tn=512, name="mm_residual"):
    m, k = a.shape
    n = w.shape[1]
    tps = seq_len // tm
    return pl.pallas_call(
        _mm_res_kernel,
        grid=(m // tm, n // tn),
        in_specs=[pl.BlockSpec((tm, k), lambda i, j: (i, 0)),
                  pl.BlockSpec((k, tn), lambda i, j: (0, j)),
                  pl.BlockSpec((tm, tn), lambda i, j: (i, j)),
                  pl.BlockSpec((1, 1, tn), lambda i, j: (i // tps, 0, j))],
        out_specs=pl.BlockSpec((tm, tn), lambda i, j: (i, j)),
        out_shape=jax.ShapeDtypeStruct((m, n), F32),
        compiler_params=_params("parallel", "arbitrary"),
        name=name,
    )(a, w, x, gate)


def _ffn_kernel(x_ref, xp_ref, g_ref, sh_ref, sc_ref, wg_ref, wu_ref, cwg_ref, cwu_ref, cbg_ref, cbu_ref,
                wd_ref, gate_ref, fg_ref, o_ref, h_ref, acc_ref, ug_scr, uu_scr, *, tiles_per_seq, final_norm):
    i, j = pl.program_id(0), pl.program_id(1)

    @pl.when(j == 0)
    def _():
        def nm(x):
            return _rms(x, g_ref[...]) * (1.0 + sc_ref[0]) + sh_ref[0]
        h_ref[HALO:, :] = nm(x_ref[...]).astype(BF16)
        keep = jnp.where(i % tiles_per_seq == 0, 0.0, 1.0)
        h_ref[:HALO, :] = (nm(xp_ref[...]) * keep).astype(BF16)
        acc_ref[...] = jnp.zeros_like(acc_ref)

    h = h_ref[...]

    def branch(w_ref, cw_ref, cb_ref, u_scr):
        u_scr[...] = _dot(h, w_ref[...])
        tm = u_scr.shape[0] - HALO
        cw = cw_ref[...]
        return (cw[0:1] * u_scr[pl.ds(HALO - 2, tm), :] + cw[1:2] * u_scr[pl.ds(HALO - 1, tm), :]
                + cw[2:3] * u_scr[pl.ds(HALO, tm), :] + cb_ref[...])

    gt = branch(wg_ref, cwg_ref, cbg_ref, ug_scr)
    up = branch(wu_ref, cwu_ref, cbu_ref, uu_scr)
    a = gt * jax.nn.sigmoid(gt) * up
    acc_ref[...] += _dot(a.astype(BF16), wd_ref[...])

    @pl.when(j == pl.num_programs(1) - 1)
    def _():
        y = x_ref[...] + gate_ref[0] * acc_ref[...]
        if final_norm:
            y = _rms(y, fg_ref[...])
        o_ref[...] = y


def _conv_ffn(x, g, shift, scale, w_up, conv_w, conv_b, w_down, gate, final_g, *, seq_len, final_norm,
              tm=512, tf=512):
    m, d = x.shape
    f = w_down.shape[0]
    nf = f // tf
    tps = seq_len // tm
    hb = tm // HALO
    row = lambda i, j: (i, 0)
    per_batch = lambda i, j: (i // tps, 0, 0)
    return pl.pallas_call(
        functools.partial(_ffn_kernel, tiles_per_seq=tps, final_norm=final_norm),
        grid=(m // tm, nf),
        in_specs=[pl.BlockSpec((tm, d), row),
                  pl.BlockSpec((HALO, d), lambda i, j: (jnp.maximum(i * hb - 1, 0), 0)),
                  pl.BlockSpec((1, d), lambda i, j: (0, 0)),
                  pl.BlockSpec((1, 1, d), per_batch),
                  pl.BlockSpec((1, 1, d), per_batch),
                  pl.BlockSpec((d, tf), lambda i, j: (0, j)),
                  pl.BlockSpec((d, tf), lambda i, j: (0, j + nf)),
                  pl.BlockSpec((CONV_WIDTH, tf), lambda i, j: (0, j)),
                  pl.BlockSpec((CONV_WIDTH, tf), lambda i, j: (0, j + nf)),
                  pl.BlockSpec((1, tf), lambda i, j: (0, j)),
                  pl.BlockSpec((1, tf), lambda i, j: (0, j + nf)),
                  pl.BlockSpec((tf, d), lambda i, j: (j, 0)),
                  pl.BlockSpec((1, 1, d), per_batch),
                  pl.BlockSpec((1, d), lambda i, j: (0, 0))],
        out_specs=pl.BlockSpec((tm, d), row),
        out_shape=jax.ShapeDtypeStruct((m, d), F32),
        scratch_shapes=[pltpu.VMEM((HALO + tm, d), BF16), pltpu.VMEM((tm, d), F32),
                        pltpu.VMEM((HALO + tm, tf), F32), pltpu.VMEM((HALO + tm, tf), F32)],
        compiler_params=_params("parallel", "arbitrary"),
        name="conv_ffn",
    )(x, x, g.reshape(1, d), shift, scale, w_up, w_up, conv_w, conv_w, conv_b.reshape(1, -1),
      conv_b.reshape(1, -1), w_down, gate, final_g.reshape(1, d))


def _t5_bucket_np(dist):
    n = np.maximum(dist, 0)
    max_exact = T5_BUCKETS // 2
    nf = np.maximum(n, 1).astype(np.float64)
    val = np.log(nf / max_exact) / math.log(T5_MAX_DIST / max_exact) * (T5_BUCKETS - max_exact)
    large = max_exact + np.trunc(val + 1e-6).astype(np.int64)
    return np.where(n < max_exact, n, np.minimum(large, T5_BUCKETS - 1)).astype(np.int32)


def _t5_gather_kernel(t5_ref, bkt_ref, o_ref, *, mult):
    h = pl.program_id(0)
    bk = bkt_ref[...]
    acc = jnp.zeros(bk.shape, F32)
    for b in range(T5_BUCKETS):
        acc = jnp.where(bk == b, t5_ref[b, h], acc)
    o_ref[0] = jnp.where(bk == T5_MASKED, MASKED, acc * mult)


def _t5_gather(t5_bias, bkt, tr, mult=1.0):
    rows, cols = bkt.shape
    heads = t5_bias.shape[1]
    return pl.pallas_call(
        functools.partial(_t5_gather_kernel, mult=mult),
        grid=(heads, rows // tr),
        in_specs=[pl.BlockSpec(memory_space=pltpu.SMEM),
                  pl.BlockSpec((tr, cols), lambda h, r: (r, 0))],
        out_specs=pl.BlockSpec((1, tr, cols), lambda h, r: (h, r, 0)),
        out_shape=jax.ShapeDtypeStruct((heads, rows, cols), F32),
        compiler_params=_params("parallel", "parallel"),
        name="t5_gather",
    )(t5_bias, jnp.asarray(bkt))


TAB_DIAG, TAB_SUB, TAB_FAR, TAB_EDGE = 0, 1, 2, 3


def _attention_tables(t5_bias, t):
    i = np.arange(t)[:, None]
    j = np.arange(t)[None, :]
    assert int(_t5_bucket_np(np.array(t + 1))) == T5_BUCKETS - 1
    far = np.full((t, t), T5_BUCKETS - 1, np.int32)
    bkt = np.concatenate([np.where(j <= i, _t5_bucket_np(i - j), T5_MASKED), _t5_bucket_np(t + i - j), far,
                          np.where(j > i, far, T5_MASKED)], axis=0).astype(np.int32)
    return _t5_gather(t5_bias, bkt, tr=t, mult=LOG2E)


def _tile_iota(t):
    return lax.broadcasted_iota(jnp.int32, (t, t), 0), lax.broadcasted_iota(jnp.int32, (t, t), 1)


def _rect_iota(i, tq, start, tk):
    rowg = i * tq + lax.broadcasted_iota(jnp.int32, (tq, tk), 0)
    colg = start + lax.broadcasted_iota(jnp.int32, (tq, tk), 1)
    return rowg, colg


def _osm_reset(m_scr, l_scr, acc_scr):
    m_scr[...] = jnp.full(m_scr.shape, M_INIT, F32)
    l_scr[...] = jnp.zeros(l_scr.shape, F32)
    acc_scr[...] = jnp.zeros(acc_scr.shape, F32)


def _osm_update(t2, rows, slot, p_scr, m_scr, l_scr, acc_scr):
    m_prev = m_scr[rows, :]
    m_new = jnp.maximum(m_prev, jnp.max(t2, axis=-1, keepdims=True))
    p = jnp.exp2(t2 - pltpu.repeat(m_new, t2.shape[1] // LANE, axis=1))
    alpha = jnp.exp2(m_prev - m_new)
    l_scr[rows, :] = alpha * l_scr[rows, :] + jnp.sum(p, axis=-1, keepdims=True)
    m_scr[rows, :] = m_new
    p_scr[slot, rows, :] = p.astype(BF16)
    acc_scr[rows, :] = alpha * acc_scr[rows, :]


def _pipelined_tiles(n_tiles, scores, update, update_last=None):
    update_last = update if update_last is None else update_last
    last = n_tiles - 1
    n_pairs = last // 2
    scores(0, 0)

    def pair(n, cr):
        scores(2 * n + 1, 1)
        update(2 * n, 0)
        scores(2 * n + 2, 0)
        update(2 * n + 1, 1)
        return cr

    lax.fori_loop(0, n_pairs, pair, 0)

    @pl.when(last % 2 == 0)
    def _():
        update_last(last, 0)

    @pl.when(last % 2 == 1)
    def _():
        scores(last, 1)
        update(last - 1, 0)
        update_last(last, 1)


def _att_scratch(rows, tk, slots=2):
    return [pltpu.VMEM((slots, rows, tk), F32), pltpu.VMEM((slots, rows, tk), BF16), pltpu.VMEM((rows, LANE), F32),
            pltpu.VMEM((rows, LANE), F32), pltpu.VMEM((rows, HEAD_DIM), F32)]


def _sb_kernel(q_ref, k_ref, v_ref, o_ref, s_scr, c_scr, acc_scr, *, tq, tk, scale):
    i = pl.program_id(2)
    q = q_ref[...]
    row, col = _tile_iota(tk)
    upper = jnp.where(row > col, 1.0, 0.0).astype(BF16)
    c_scr[...] = jnp.zeros(c_scr.shape, F32)
    acc_scr[...] = jnp.zeros(acc_scr.shape, F32)
    reps = tk // LANE

    def scores(kb, slot):
        start = pl.multiple_of(kb * tk, tk)
        s_scr[slot] = _dot_nt(q, k_ref[pl.ds(start, tk), :])

    def update(kb, slot, diag):
        start = pl.multiple_of(kb * tk, tk)
        nz = s_scr[slot] * (-scale)
        e = jnp.exp2(jnp.abs(nz) * (-LOG2E))
        lk = jnp.minimum(nz, 0.0) - jnp.log(1.0 + e)
        if diag:
            rowg, colg = _rect_iota(i, tq, start, tk)
            past = colg < rowg
            lk = jnp.where(past, lk, 0.0)
        hi, lo = _split_bf16(lk)
        c = c_scr[...]
        later = _dot(hi, upper) + _dot(lo, upper) + pltpu.repeat(c, reps, axis=1)
        a = jnp.exp(lk - nz + later)
        if diag:
            a = jnp.where(past, a, 0.0)
        acc_scr[...] += _dot(a.astype(BF16), v_ref[pl.ds(start, tk), :])
        c_scr[...] = c + jnp.sum(lk, axis=-1, keepdims=True)

    assert tq == 2 * tk
    scores(2 * i + 1, 1)
    scores(2 * i, 0)
    update(2 * i + 1, 1, True)
    scores(jnp.maximum(2 * i - 1, 0), 1)
    update(2 * i, 0, True)

    def more(carry):
        n, c_max = carry
        return (n < i) & (c_max > SB_DEAD)

    def pair(carry):
        n, _ = carry
        kb = 2 * (i - n) - 1
        scores(kb - 1, 0)
        update(kb, 1, False)
        scores(jnp.maximum(kb - 2, 0), 1)
        update(kb - 1, 0, False)
        return n + 1, jnp.max(c_scr[...])

    lax.while_loop(more, pair, (0, jnp.max(c_scr[...])))
    o_ref[...] = acc_scr[...].astype(o_ref.dtype)


def _sb_attention(qkv, *, batch, seq_len, tq=ATT_TQ, tk=ATT_TK):
    h = N_HEADS
    nq = seq_len // tq
    return pl.pallas_call(
        functools.partial(_sb_kernel, tq=tq, tk=tk, scale=HEAD_DIM ** -0.5),
        grid=(batch, h, nq),
        in_specs=[pl.BlockSpec((tq, HEAD_DIM), lambda b, hh, i: (b * nq + i, hh)),
                  pl.BlockSpec((seq_len, HEAD_DIM), lambda b, hh, i: (b, h + hh)),
                  pl.BlockSpec((seq_len, HEAD_DIM), lambda b, hh, i: (b, 2 * h + hh))],
        out_specs=pl.BlockSpec((tq, HEAD_DIM), lambda b, hh, i: (b * nq + i, hh)),
        out_shape=jax.ShapeDtypeStruct((batch * seq_len, h * HEAD_DIM), BF16),
        scratch_shapes=[pltpu.VMEM((2, tq, tk), F32), pltpu.VMEM((tq, LANE), F32), pltpu.VMEM((tq, HEAD_DIM), F32)],
        compiler_params=_params("parallel", "parallel", "arbitrary"),
        name="sb_attention",
    )(qkv, qkv, qkv)


def _table_offset(kb, i, t):
    return pl.multiple_of(jnp.where(kb == i, TAB_DIAG * t, jnp.where(kb == i - 1, TAB_SUB * t, TAB_FAR * t)), t)


def _diff_kernel(q_ref, k_ref, v_ref, tab_ref, lam_ref, hg_ref, o_ref, s_scr, p_scr, m_scr, l_scr, acc_scr, *, t,
                 scale, lambda_init):
    i = pl.program_id(2)
    q = q_ref[...]
    lane = lax.broadcasted_iota(jnp.int32, q.shape, 1)
    zero = jnp.zeros_like(q)
    q2 = jnp.concatenate([jnp.where(lane < DIFF_DIM, q, zero), jnp.where(lane >= DIFF_DIM, q, zero)], axis=0)
    _osm_reset(m_scr, l_scr, acc_scr)

    def scores(kb, slot):
        s_scr[slot] = _dot_nt(q2, k_ref[pl.ds(pl.multiple_of(kb * t, t), t), :])

    def update(kb, slot):
        off = _table_offset(kb, i, t)
        for half in range(2):
            rows = pl.ds(half * t, t)
            t2 = s_scr[slot, rows, :] * (scale * LOG2E) + tab_ref[0, pl.ds(off, t), :]
            _osm_update(t2, rows, slot, p_scr, m_scr, l_scr, acc_scr)
        acc_scr[...] += _dot(p_scr[slot], v_ref[pl.ds(pl.multiple_of(kb * t, t), t), :])

    _pipelined_tiles(i + 1, scores, update)

    lam = lam_ref[...]
    lmbda = (jnp.exp(jnp.sum(lam[0:1] * lam[1:2], axis=-1, keepdims=True))
             - jnp.exp(jnp.sum(lam[2:3] * lam[3:4], axis=-1, keepdims=True)) + lambda_init)
    o = acc_scr[...] / l_scr[...]
    o = o[:t] - lmbda * o[t:]
    o_ref[...] = (_rms(o, hg_ref[...]) * (1.0 - lambda_init)).astype(o_ref.dtype)


def _diff_attention(qkv, tables, lam, head_g, *, batch, seq_len, lambda_init, t=DIFF_T):
    h = N_HEADS
    nq = seq_len // t
    return pl.pallas_call(
        functools.partial(_diff_kernel, t=t, scale=DIFF_DIM ** -0.5, lambda_init=lambda_init),
        grid=(batch, h, nq),
        in_specs=[pl.BlockSpec((t, HEAD_DIM), lambda b, hh, i: (b * nq + i, hh)),
                  pl.BlockSpec((seq_len, HEAD_DIM), lambda b, hh, i: (b, h + hh)),
                  pl.BlockSpec((seq_len, HEAD_DIM), lambda b, hh, i: (b, 2 * h + hh)),
                  pl.BlockSpec((1, 4 * t, t), lambda b, hh, i: (hh, 0, 0)),
                  pl.BlockSpec((4, DIFF_DIM), lambda b, hh, i: (0, 0)),
                  pl.BlockSpec((1, HEAD_DIM), lambda b, hh, i: (0, 0))],
        out_specs=pl.BlockSpec((t, HEAD_DIM), lambda b, hh, i: (b * nq + i, hh)),
        out_shape=jax.ShapeDtypeStruct((batch * seq_len, h * HEAD_DIM), BF16),
        scratch_shapes=_att_scratch(2 * t, t),
        compiler_params=_params("parallel", "parallel", "arbitrary"),
        name="diff_attention",
    )(qkv, qkv, qkv, tables, lam, head_g.reshape(1, HEAD_DIM))


def _mla_kernel(qn_ref, qr_ref, kn_ref, kr_ref, v_ref, o_ref, s_scr, p_scr, m_scr, l_scr, acc_scr, *, tq, tk, scale):
    i = pl.program_id(2)
    q = jnp.concatenate([qn_ref[...], qr_ref[...]], axis=-1)
    _osm_reset(m_scr, l_scr, acc_scr)

    def scores(kb, slot):
        start = pl.multiple_of(kb * tk, tk)
        k = jnp.concatenate([kn_ref[pl.ds(start, tk), :], kr_ref[pl.ds(start, tk), :]], axis=-1)
        s_scr[slot] = _dot_nt(q, k)

    def update(kb, slot, mask):
        start = pl.multiple_of(kb * tk, tk)
        t2 = s_scr[slot] * (scale * LOG2E)
        if mask:
            rowg, colg = _rect_iota(i, tq, start, tk)
            t2 = jnp.where(colg <= rowg, t2, MASKED)
        _osm_update(t2, pl.ds(0, tq), slot, p_scr, m_scr, l_scr, acc_scr)
        acc_scr[...] += _dot(p_scr[slot], v_ref[pl.ds(start, tk), :])

    assert tq == tk
    _pipelined_tiles(i + 1, scores, lambda kb, slot: update(kb, slot, False), lambda kb, slot: update(kb, slot, True))
    o_ref[...] = (acc_scr[...] / l_scr[...]).astype(o_ref.dtype)


def _mla_attention(qn, qr, kv, kr, *, batch, seq_len, tq=MLA_T, tk=MLA_T):
    h = N_HEADS
    nq = seq_len // tq
    qspec = pl.BlockSpec((tq, HEAD_DIM), lambda b, hh, i: (b * nq + i, hh))
    return pl.pallas_call(
        functools.partial(_mla_kernel, tq=tq, tk=tk, scale=(MLA_NOPE + MLA_ROPE) ** -0.5),
        grid=(batch, h, nq),
        in_specs=[qspec, qspec,
                  pl.BlockSpec((seq_len, HEAD_DIM), lambda b, hh, i: (b, 2 * hh)),
                  pl.BlockSpec((seq_len, HEAD_DIM), lambda b, hh, i: (b, 0)),
                  pl.BlockSpec((seq_len, HEAD_DIM), lambda b, hh, i: (b, 2 * hh + 1))],
        out_specs=qspec,
        out_shape=jax.ShapeDtypeStruct((batch * seq_len, h * HEAD_DIM), BF16),
        scratch_shapes=_att_scratch(tq, tk),
        compiler_params=_params("parallel", "parallel", "arbitrary"),
        name="mla_attention",
    )(qn, qr, kv, kr, kv)


def _compress_kernel(raw_ref, pe_ref, w1_ref, w2_ref, o_ref, *, n_slots):
    half = NSA_CMP_BLOCK // 2
    p1 = jnp.zeros((n_slots, HEAD_DIM), F32)
    p2 = jnp.zeros((n_slots, HEAD_DIM), F32)
    for l in range(half):
        a = raw_ref[pl.ds(l, n_slots, stride=NSA_CMP_STRIDE), :]
        p1 = p1 + _dot((a + pe_ref[0, l:l + 1, :]).astype(BF16), w1_ref[0, l])
        p2 = p2 + _dot((a + pe_ref[0, half + l:half + l + 1, :]).astype(BF16), w1_ref[0, half + l])
    pre = p1 + pltpu.roll(p2, n_slots - 1, 0)
    hid = pre * jax.nn.sigmoid(pre)
    o_ref[0, 0, 0] = _dot(hid.astype(BF16), w2_ref[0]).astype(o_ref.dtype)


def _nsa_compress(raw, pe, w1, w2, *, batch, seq_len):
    g = NSA_GROUPS
    n_slots = seq_len // NSA_CMP_STRIDE
    return pl.pallas_call(
        functools.partial(_compress_kernel, n_slots=n_slots),
        grid=(batch, 2, g),
        in_specs=[pl.BlockSpec((seq_len, HEAD_DIM), lambda b, kv, gg: (b, kv * g + gg)),
                  pl.BlockSpec((1, NSA_CMP_BLOCK, HEAD_DIM), lambda b, kv, gg: (kv, 0, 0)),
                  pl.BlockSpec((1, NSA_CMP_BLOCK, HEAD_DIM, HEAD_DIM), lambda b, kv, gg: (kv, 0, 0, 0)),
                  pl.BlockSpec((1, HEAD_DIM, HEAD_DIM), lambda b, kv, gg: (kv, 0, 0))],
        out_specs=pl.BlockSpec((1, 1, 1, n_slots, HEAD_DIM), lambda b, kv, gg: (b, kv, gg, 0, 0)),
        out_shape=jax.ShapeDtypeStruct((batch, 2, g, n_slots, HEAD_DIM), BF16),
        compiler_params=_params("parallel", "parallel", "parallel"),
        name="nsa_compress",
    )(raw, pe, w1, w2)


def _nsa_cmp_kernel(q_ref, kc_ref, vc_ref, bias_ref, gates_ref, ovt_ref, oc_ref, sel_ref, *, t, n_slots, n_sel,
                    n_top, scale):
    i = pl.program_id(2)
    kc = kc_ref[0, 0, 0]
    vc = vc_ref[0, 0, 0]
    qpos = i * t + lax.broadcasted_iota(jnp.int32, (t, n_slots), 0)
    cmp_end = NSA_CMP_STRIDE * lax.broadcasted_iota(jnp.int32, (t, n_slots), 1) + (NSA_CMP_BLOCK - 1)
    valid = cmp_end <= qpos
    gates = gates_ref[...]
    psum = jnp.zeros((t, n_slots), F32)
    for r in range(NSA_REP):
        q = q_ref[:, r * HEAD_DIM:(r + 1) * HEAD_DIM]
        s = jnp.where(valid, _dot_nt(q, kc) * scale + bias_ref[r], NEG)
        m = jnp.max(s, axis=-1, keepdims=True)
        p = jnp.where(valid, jnp.exp(s - m), 0.0)
        p = p / jnp.maximum(jnp.sum(p, axis=-1, keepdims=True), 1e-30)
        psum = psum + p
        oc_ref[:, r * HEAD_DIM:(r + 1) * HEAD_DIM] = gates[:, r:r + 1] * _dot(p.astype(BF16), vc)

    hi, lo = _split_bf16(psum)
    ovt = ovt_ref[...]
    imp = _dot_nt(ovt, hi) + _dot_nt(ovt, lo)
    blk = lax.broadcasted_iota(jnp.int32, (n_sel, t), 0)
    tpos = i * t + lax.broadcasted_iota(jnp.int32, (n_sel, t), 1)
    cur = tpos // NSA_SEL_BLOCK
    forced = (blk == 0) | (blk == cur) | (blk == cur - 1)
    score = jnp.where(blk * NSA_SEL_BLOCK <= tpos, jnp.where(forced, FORCED_SCORE, imp), -1.0)
    rank = jnp.zeros((n_sel, t), F32)
    for mm in range(n_sel):
        sm = score[mm:mm + 1, :]
        ahead = (sm > score) | ((sm == score) & (blk > mm))
        rank = rank + jnp.where(ahead, 1.0, 0.0)
    sel_t = jnp.where(rank < n_top, 1.0, 0.0).astype(BF16)
    row, col = _tile_iota(t)
    eye = jnp.where(row == col, 1.0, 0.0).astype(BF16)
    sel_ref[0, 0] = _dot_nt(eye, sel_t).astype(sel_ref.dtype)


def _nsa_cmp_attention(q_all, kvc, bias_c, gates, *, batch, seq_len, t=ATT_T):
    g = NSA_GROUPS
    nq = seq_len // t
    n_slots = seq_len // NSA_CMP_STRIDE
    n_sel = seq_len // NSA_SEL_BLOCK
    c0 = NSA_CMP_STRIDE * np.arange(n_slots)[:, None]
    s0 = NSA_SEL_BLOCK * np.arange(n_sel)[None, :]
    overlap = (c0 < s0 + NSA_SEL_BLOCK) & (c0 + NSA_CMP_BLOCK > s0)
    ovt = jnp.asarray(overlap.T.astype(np.float32), dtype=BF16)
    gw = NSA_REP * HEAD_DIM
    return pl.pallas_call(
        functools.partial(_nsa_cmp_kernel, t=t, n_slots=n_slots, n_sel=n_sel, n_top=min(NSA_TOPN, n_sel),
                          scale=HEAD_DIM ** -0.5),
        grid=(batch, g, nq),
        in_specs=[pl.BlockSpec((t, gw), lambda b, gg, i: (b * nq + i, gg)),
                  pl.BlockSpec((1, 1, 1, n_slots, HEAD_DIM), lambda b, gg, i: (b, 0, gg, 0, 0)),
                  pl.BlockSpec((1, 1, 1, n_slots, HEAD_DIM), lambda b, gg, i: (b, 1, gg, 0, 0)),
                  pl.BlockSpec((NSA_REP, t, n_slots), lambda b, gg, i: (gg, i, 0)),
                  pl.BlockSpec((t, LANE), lambda b, gg, i: (b * nq + i, gg)),
                  pl.BlockSpec((n_sel, n_slots), lambda b, gg, i: (0, 0))],
        out_specs=[pl.BlockSpec((t, gw), lambda b, gg, i: (b * nq + i, gg)),
                   pl.BlockSpec((1, 1, t, n_sel), lambda b, gg, i: (b, gg, i, 0))],
        out_shape=[jax.ShapeDtypeStruct((batch * seq_len, g * gw), F32),
                   jax.ShapeDtypeStruct((batch, g, seq_len, n_sel), BF16)],
        compiler_params=_params("parallel", "parallel", "arbitrary"),
        name="nsa_cmp_attention",
    )(q_all, kvc, kvc, bias_c, gates, ovt)


def _nsa_main_kernel(q_ref, ks_ref, vs_ref, kw_ref, vw_ref, sel_ref, tab_ref, gates_ref, oc_ref, o_ref,
                     s_scr, p_scr, m_scr, l_scr, acc_scr, os_scr, mk_scr, *, t, scale):
    i = pl.program_id(2)
    rep = NSA_REP
    q = jnp.concatenate([q_ref[:, r * HEAD_DIM:(r + 1) * HEAD_DIM] for r in range(rep)], axis=0)

    sel = sel_ref[0, 0]
    n_sel = sel.shape[1]
    blk_row = lax.broadcasted_iota(jnp.int32, (n_sel, t), 0)
    key_col = lax.broadcasted_iota(jnp.int32, (n_sel, t), 1)
    _osm_reset(m_scr, l_scr, acc_scr)

    def sel_scores(kb, slot):
        s_scr[slot] = _dot_nt(q, ks_ref[pl.ds(pl.multiple_of(kb * t, t), t), :])

    def sel_update(kb, slot):
        start = pl.multiple_of(kb * t, t)
        expand = jnp.where((start + key_col) // NSA_SEL_BLOCK == blk_row, 1.0, 0.0).astype(BF16)
        mk_scr[...] = _dot(sel, expand)
        off = _table_offset(kb, i, t)
        for r in range(rep):
            rows = pl.ds(r * t, t)
            t2 = s_scr[slot, rows, :] * (scale * LOG2E) + tab_ref[r, pl.ds(off, t), :]
            t2 = jnp.where(mk_scr[...] > 0.5, t2, MASKED)
            _osm_update(t2, rows, slot, p_scr, m_scr, l_scr, acc_scr)
        acc_scr[...] += _dot(p_scr[slot], vs_ref[pl.ds(start, t), :])

    _pipelined_tiles(i + 1, sel_scores, sel_update)
    os_scr[...] = acc_scr[...] / l_scr[...]

    n_back = NSA_WINDOW // t
    assert n_back == 2
    _osm_reset(m_scr, l_scr, acc_scr)

    def win_scores(kb, slot):
        s_scr[slot] = _dot_nt(q, kw_ref[pl.ds(pl.multiple_of(kb * t, t), t), :])

    def win_update(kb, slot, region):
        for r in range(rep):
            rows = pl.ds(r * t, t)
            t2 = s_scr[slot, rows, :] * (scale * LOG2E) + tab_ref[r, pl.ds(region * t, t), :]
            _osm_update(t2, rows, slot, p_scr, m_scr, l_scr, acc_scr)
        acc_scr[...] += _dot(p_scr[slot], vw_ref[pl.ds(pl.multiple_of(kb * t, t), t), :])

    def window(n_tiles):
        regions = (TAB_DIAG, TAB_SUB, TAB_EDGE)
        for back in range(n_tiles):
            win_scores(i - back, back)
        for back in reversed(range(n_tiles)):
            win_update(i - back, back, regions[back])

    pl.when(i >= 2)(lambda: window(3))
    pl.when(i == 1)(lambda: window(2))
    pl.when(i == 0)(lambda: window(1))
    o_w = acc_scr[...] / l_scr[...]
    o_s = os_scr[...]

    gates = gates_ref[...]
    for r in range(rep):
        rows = slice(r * t, (r + 1) * t)
        cols = slice(r * HEAD_DIM, (r + 1) * HEAD_DIM)
        o = oc_ref[:, cols] + gates[:, rep + r:rep + r + 1] * o_s[rows] + gates[:, 2 * rep + r:2 * rep + r + 1] * o_w[rows]
        o_ref[:, cols] = o.astype(o_ref.dtype)


def _nsa_main_attention(qkv, sel, tables, gates, oc, *, batch, seq_len, t=ATT_T):
    g = NSA_GROUPS
    nq = seq_len // t
    n_sel = seq_len // NSA_SEL_BLOCK
    gw = NSA_REP * HEAD_DIM
    qb = N_HEADS
    kv = lambda which: pl.BlockSpec((seq_len, HEAD_DIM), lambda b, gg, i: (b, qb + which * g + gg))
    tile = pl.BlockSpec((t, gw), lambda b, gg, i: (b * nq + i, gg))
    rows = NSA_REP * t
    return pl.pallas_call(
        functools.partial(_nsa_main_kernel, t=t, scale=HEAD_DIM ** -0.5),
        grid=(batch, g, nq),
        in_specs=[tile, kv(0), kv(1), kv(2), kv(3),
                  pl.BlockSpec((1, 1, t, n_sel), lambda b, gg, i: (b, gg, i, 0)),
                  pl.BlockSpec((NSA_REP, 4 * t, t), lambda b, gg, i: (gg, 0, 0)),
                  pl.BlockSpec((t, LANE), lambda b, gg, i: (b * nq + i, gg)),
                  tile],
        out_specs=tile,
        out_shape=jax.ShapeDtypeStruct((batch * seq_len, g * gw), BF16),
        scratch_shapes=_att_scratch(rows, t, slots=3) + [pltpu.VMEM((rows, HEAD_DIM), F32), pltpu.VMEM((t, t), F32)],
        compiler_params=_params("parallel", "parallel", "arbitrary"),
        name="nsa_main_attention",
    )(qkv, qkv, qkv, qkv, qkv, sel, tables, gates, oc)


def _rope_tables(seq_len, width):
    half = MLA_ROPE // 2
    inv = np.power(ROPE_THETA, -np.arange(half, dtype=np.float32) / half).astype(np.float32)
    ang = np.arange(seq_len, dtype=np.float32)[:, None] * inv[None, :]
    pad = np.zeros((seq_len, LANE - MLA_ROPE), np.float32)
    cos = np.concatenate([np.cos(ang), np.cos(ang), pad], axis=1)
    sin = np.concatenate([np.sin(ang), np.sin(ang), pad], axis=1)
    reps = width // LANE
    return jnp.asarray(np.tile(cos, (1, reps))), jnp.asarray(np.tile(sin, (1, reps)))


def _rope_weights(w):
    k, n, _ = w.shape
    half = MLA_ROPE // 2
    pad = jnp.zeros((k, n, LANE - MLA_ROPE), w.dtype)
    wa = jnp.concatenate([w, pad], axis=-1)
    wb = jnp.concatenate([-w[..., half:], w[..., :half], pad], axis=-1)
    return wa.reshape(k, n * LANE).astype(BF16), wb.reshape(k, n * LANE).astype(BF16)


def kernel(x, c, t5_bias, ada_w, ada_b, norm_g, final_g, ffn_w_up, ffn_conv_w, ffn_conv_b, ffn_w_down, sb_w_in, sb_w_out, nsa_w_in, nsa_cmp_pe, nsa_cmp_w1, nsa_cmp_w2, nsa_w_out, diff_w_in, diff_lambda, diff_head_g, diff_w_out, mla_w_in, mla_q_g, mla_w_qb, mla_kv_g, mla_w_kvb, mla_w_out):
    batch, seq_len, d = x.shape
    depth = ada_w.shape[0]
    h, dh, g = N_HEADS, HEAD_DIM, NSA_GROUPS
    sizes = dict(batch=batch, seq_len=seq_len)

    mod = _ada_mod(c, ada_w, ada_b)
    tables = _attention_tables(t5_bias, ATT_T)

    xf = x.reshape(batch * seq_len, d)
    for i in range(depth):
        mixer, j = i % 4, i // 4
        sh1, sc1, gt1, sh2, sc2, gt2 = (mod[i, :, n * d:(n + 1) * d].reshape(batch, 1, d) for n in range(6))
        nm = functools.partial(_norm_mm, xf, norm_g[i, 0], seq_len=seq_len, shift=sh1, scale=sc1)
        if mixer == 0:
            qkv = nm(sb_w_in[j].astype(BF16), name="sb_in")
            o = _sb_attention(qkv, **sizes)
            w_out = sb_w_out[j]
        elif mixer == 1:
            w_in = nsa_w_in[j]
            n_q, n_kv = h * dh, g * dh
            w_att = jnp.concatenate([w_in[:, :n_q], w_in[:, n_q + 2 * n_kv:n_q + 6 * n_kv]], axis=1)
            w_cmp = w_in[:, n_q:n_q + 2 * n_kv]
            w_g = w_in[:, n_q + 6 * n_kv:].reshape(d, 3, g, NSA_REP).transpose(0, 2, 1, 3).reshape(d, g, 3 * NSA_REP)
            w_g = jnp.pad(w_g, ((0, 0), (0, 0), (0, LANE - 3 * NSA_REP))).reshape(d, g * LANE)
            qkv = nm(w_att.astype(BF16), name="nsa_in")
            raw = nm(w_cmp.astype(BF16), out_dtype=F32, name="nsa_in_cmp")
            gates = nm(w_g.astype(BF16), out_dtype=F32, act="sigmoid", name="nsa_in_gates")
            kvc = _nsa_compress(raw, nsa_cmp_pe[j], nsa_cmp_w1[j].reshape(2, NSA_CMP_BLOCK, dh, dh).astype(BF16),
                                nsa_cmp_w2[j].astype(BF16), **sizes)
            n_slots = seq_len // NSA_CMP_STRIDE
            dist_c = np.arange(seq_len)[:, None] - (NSA_CMP_STRIDE * np.arange(n_slots)[None, :] + NSA_CMP_BLOCK - 1)
            bias_c = _t5_gather(t5_bias, _t5_bucket_np(dist_c), tr=min(seq_len, 512))
            oc, sel = _nsa_cmp_attention(qkv, kvc, bias_c, gates, **sizes)
            o = _nsa_main_attention(qkv, sel, tables, gates, oc, **sizes)
            w_out = nsa_w_out[j]
        elif mixer == 2:
            lambda_init = 0.8 - 0.6 * math.exp(-0.3 * i)
            qkv = nm(diff_w_in[j].astype(BF16), name="diff_in")
            o = _diff_attention(qkv, _attention_tables(t5_bias, DIFF_T), diff_lambda[j], diff_head_g[j],
                                lambda_init=lambda_init, **sizes)
            w_out = diff_w_out[j]
        else:
            w_in = mla_w_in[j]
            nq_l, nkv_l = MLA_Q_LORA, MLA_KV_LORA
            w_lat = jnp.concatenate([w_in[:, :nq_l], jnp.zeros((d, 2 * nkv_l - nq_l), w_in.dtype),
                                     w_in[:, nq_l:nq_l + nkv_l]], axis=1)
            lat = nm(w_lat.astype(BF16), out_dtype=F32, name="mla_in")
            cos1, sin1 = _rope_tables(seq_len, LANE)
            wa, wb = _rope_weights(w_in[:, nq_l + nkv_l:].reshape(d, 1, MLA_ROPE))
            kr = nm(wa, wb=wb, cos=cos1, sin=sin1, name="mla_in_rope")
            w_qb = mla_w_qb[j].reshape(nq_l, h, MLA_NOPE + MLA_ROPE)
            qn = _norm_mm(lat, mla_q_g[j], w_qb[:, :, :MLA_NOPE].reshape(nq_l, h * MLA_NOPE).astype(BF16),
                          seq_len=seq_len, x_cols=nq_l, x_col_block=0, name="mla_q_nope")
            cosh, sinh = _rope_tables(seq_len, h * LANE)
            wa, wb = _rope_weights(w_qb[:, :, MLA_NOPE:])
            qr = _norm_mm(lat, mla_q_g[j], wa, wb=wb, cos=cosh, sin=sinh, seq_len=seq_len, x_cols=nq_l,
                          x_col_block=0, name="mla_q_rope")
            kv = _norm_mm(lat, mla_kv_g[j], mla_w_kvb[j].astype(BF16), seq_len=seq_len, x_cols=nkv_l,
                          x_col_block=2, name="mla_kv")
            o = _mla_attention(qn, qr, kv, kr, **sizes)
            w_out = mla_w_out[j]
        xf = _mm_residual(o, w_out.astype(BF16), xf, gt1, seq_len=seq_len)
        xf = _conv_ffn(xf, norm_g[i, 1], sh2, sc2, ffn_w_up[i].astype(BF16), ffn_conv_w[i], ffn_conv_b[i],
                       ffn_w_down[i].astype(BF16), gt2, final_g, seq_len=seq_len, final_norm=(i == depth - 1))
    return xf.reshape(batch, seq_len, d)
```

```python
import functools
import math

import numpy as np
import jax
import jax.numpy as jnp
from jax import lax
from jax.experimental import pallas as pl
from jax.experimental.pallas import tpu as pltpu

F32 = jnp.float32
BF16 = jnp.bfloat16
EPS = 1e-6
NEG = -1e30

LANE = 128
HALO = 16
VMEM_LIMIT = 56 * 2**20

T5_BUCKETS = 32
T5_MAX_DIST = 128
N_HEADS = 16
HEAD_DIM = 128
NSA_GROUPS = 4
NSA_REP = 4
NSA_CMP_BLOCK = 32
NSA_CMP_STRIDE = 16
NSA_SEL_BLOCK = 64
NSA_TOPN = 16
NSA_WINDOW = 512
FORCED_SCORE = 1e9
DIFF_DIM = 64
MLA_Q_LORA = 768
MLA_KV_LORA = 512
MLA_NOPE = 128
MLA_ROPE = 64
ROPE_THETA = 10000.0
CONV_WIDTH = 3
ATT_T = 256
DIFF_T = 512
MLA_T = 512
ATT_TQ = 512
ATT_TK = 256
LOG2E = 1.4426950408889634
SB_QSCALE = -(HEAD_DIM ** -0.5)
NSA_QSCALE = HEAD_DIM ** -0.5 * LOG2E
DIFF_QSCALE = DIFF_DIM ** -0.5 * LOG2E
MLA_QSCALE = (MLA_NOPE + MLA_ROPE) ** -0.5 * LOG2E
M_INIT = -1e30
MASKED = -2e30
SB_DEAD = -120.0
T5_MASKED = T5_BUCKETS


def _params(*sem):
    return pltpu.CompilerParams(dimension_semantics=sem, vmem_limit_bytes=VMEM_LIMIT)


def _dot(a, b):
    return jnp.dot(a, b, preferred_element_type=F32)


def _dot_nt(a, b):
    return lax.dot_general(a, b, (((1,), (1,)), ((), ())), preferred_element_type=F32)


def _split_bf16(x):
    hi = x.astype(BF16)
    lo = (x - hi.astype(F32)).astype(BF16)
    return hi, lo


def _rms(x, g):
    ms = jnp.mean(x * x, axis=-1, keepdims=True)
    return x * lax.rsqrt(ms + EPS) * g


def _ada_kernel(c_ref, w_ref, b_ref, o_ref):
    c = c_ref[...]
    a = c * jax.nn.sigmoid(c)
    a_hi, a_lo = _split_bf16(a)
    w_hi, w_lo = _split_bf16(w_ref[0])
    o_ref[0] = _dot(a_hi, w_hi) + _dot(a_lo, w_hi) + _dot(a_hi, w_lo) + b_ref[0]


def _ada_mod(c, ada_w, ada_b, tn=1024):
    depth, d, n = ada_w.shape
    b = c.shape[0]
    return pl.pallas_call(
        _ada_kernel,
        grid=(depth, n // tn),
        in_specs=[pl.BlockSpec((b, d), lambda l, j: (0, 0)),
                  pl.BlockSpec((1, d, tn), lambda l, j: (l, 0, j)),
                  pl.BlockSpec((1, 1, tn), lambda l, j: (l, 0, j))],
        out_specs=pl.BlockSpec((1, b, tn), lambda l, j: (l, 0, j)),
        out_shape=jax.ShapeDtypeStruct((depth, b, n), F32),
        compiler_params=_params("parallel", "parallel"),
        name="ada_mod",
    )(c, ada_w, ada_b.reshape(depth, 1, n))


def _norm_mm_kernel(*refs, modulated, rope, act):
    it = iter(refs)
    x_ref, g_ref = next(it), next(it)
    sh_ref = sc_ref = wb_ref = cos_ref = sin_ref = None
    if modulated:
        sh_ref, sc_ref = next(it), next(it)
    w_ref = next(it)
    if rope:
        wb_ref, cos_ref, sin_ref = next(it), next(it), next(it)
    o_ref, h_ref = next(it), next(it)

    @pl.when(pl.program_id(1) == 0)
    def _():
        y = _rms(x_ref[...], g_ref[...])
        if modulated:
            y = y * (1.0 + sc_ref[0]) + sh_ref[0]
        h_ref[...] = y.astype(BF16)

    h = h_ref[...]
    acc = _dot(h, w_ref[...])
    if rope:
        acc = acc * cos_ref[...] + _dot(h, wb_ref[...]) * sin_ref[...]
    if act == "sigmoid":
        acc = jax.nn.sigmoid(acc)
    o_ref[...] = acc.astype(o_ref.dtype)


def _norm_mm(x, g, w, *, seq_len, shift=None, scale=None, wb=None, cos=None, sin=None, act=None,
             out_dtype=BF16, x_cols=None, x_col_block=0, tm=1024, tn=512, name="norm_mm"):
    m = x.shape[0]
    k = x.shape[1] if x_cols is None else x_cols
    n = w.shape[1]
    tn = min(tn, n)
    modulated, rope = shift is not None, wb is not None
    tps = seq_len // tm
    in_specs = [pl.BlockSpec((tm, k), lambda i, j: (i, x_col_block)),
                pl.BlockSpec((1, k), lambda i, j: (0, 0))]
    args = [x, g.reshape(1, k)]
    if modulated:
        in_specs += [pl.BlockSpec((1, 1, k), lambda i, j: (i // tps, 0, 0))] * 2
        args += [shift, scale]
    in_specs.append(pl.BlockSpec((k, tn), lambda i, j: (0, j)))
    args.append(w)
    if rope:
        in_specs.append(pl.BlockSpec((k, tn), lambda i, j: (0, j)))
        in_specs += [pl.BlockSpec((tm, tn), lambda i, j: (i % tps, j))] * 2
        args += [wb, cos, sin]
    return pl.pallas_call(
        functools.partial(_norm_mm_kernel, modulated=modulated, rope=rope, act=act),
        grid=(m // tm, n // tn),
        in_specs=in_specs,
        out_specs=pl.BlockSpec((tm, tn), lambda i, j: (i, j)),
        out_shape=jax.ShapeDtypeStruct((m, n), out_dtype),
        scratch_shapes=[pltpu.VMEM((tm, k), BF16)],
        compiler_params=_params("parallel", "arbitrary"),
        name=name,
    )(*args)


def _mm_res_kernel(a_ref, w_ref, x_ref, gate_ref, o_ref):
    o_ref[...] = x_ref[...] + gate_ref[0] * _dot(a_ref[...], w_ref[...])


def _mm_residual(a, w, x, gate, *, seq_len, tm=1024, tn=512, name="mm_residual"):
    m, k = a.shape
    n = w.shape[1]
    tps = seq_len // tm
    return pl.pallas_call(
        _mm_res_kernel,
        grid=(m // tm, n // tn),
        in_specs=[pl.BlockSpec((tm, k), lambda i, j: (i, 0)),
                  pl.BlockSpec((k, tn), lambda i, j: (0, j)),
                  pl.BlockSpec((tm, tn), lambda i, j: (i, j)),
                  pl.BlockSpec((1, 1, tn), lambda i, j: (i // tps, 0, j))],
        out_specs=pl.BlockSpec((tm, tn), lambda i, j: (i, j)),
        out_shape=jax.ShapeDtypeStruct((m, n), F32),
        compiler_params=_params("parallel", "arbitrary"),
        name=name,
    )(a, w, x, gate)


def _ffn_kernel(x_ref, xp_ref, g_ref, sh_ref, sc_ref, wg_ref, wu_ref, cwg_ref, cwu_ref, cbg_ref, cbu_ref,
                wd_ref, gate_ref, fg_ref, o_ref, h_ref, acc_ref, ug_scr, uu_scr, *, tiles_per_seq, final_norm):
    i, j = pl.program_id(0), pl.program_id(1)

    @pl.when(j == 0)
    def _():
        def nm(x):
            return _rms(x, g_ref[...]) * (1.0 + sc_ref[0]) + sh_ref[0]
        h_ref[HALO:, :] = nm(x_ref[...]).astype(BF16)
        keep = jnp.where(i % tiles_per_seq == 0, 0.0, 1.0)
        h_ref[:HALO, :] = (nm(xp_ref[...]) * keep).astype(BF16)
        acc_ref[...] = jnp.zeros_like(acc_ref)

    h = h_ref[...]

    def branch(w_ref, cw_ref, cb_ref, u_scr):
        u_scr[...] = _dot(h, w_ref[...])
        tm = u_scr.shape[0] - HALO
        cw = cw_ref[...]
        return (cw[0:1] * u_scr[pl.ds(HALO - 2, tm), :] + cw[1:2] * u_scr[pl.ds(HALO - 1, tm), :]
                + cw[2:3] * u_scr[pl.ds(HALO, tm), :] + cb_ref[...])

    gt = branch(wg_ref, cwg_ref, cbg_ref, ug_scr)
    up = branch(wu_ref, cwu_ref, cbu_ref, uu_scr)
    a = gt * jax.nn.sigmoid(gt) * up
    acc_ref[...] += _dot(a.astype(BF16), wd_ref[...])

    @pl.when(j == pl.num_programs(1) - 1)
    def _():
        y = x_ref[...] + gate_ref[0] * acc_ref[...]
        if final_norm:
            y = _rms(y, fg_ref[...])
        o_ref[...] = y


def _conv_ffn(x, g, shift, scale, w_up, conv_w, conv_b, w_down, gate, final_g, *, seq_len, final_norm,
              tm=512, tf=512):
    m, d = x.shape
    f = w_down.shape[0]
    nf = f // tf
    tps = seq_len // tm
    hb = tm // HALO
    row = lambda i, j: (i, 0)
    per_batch = lambda i, j: (i // tps, 0, 0)
    return pl.pallas_call(
        functools.partial(_ffn_kernel, tiles_per_seq=tps, final_norm=final_norm),
        grid=(m // tm, nf),
        in_specs=[pl.BlockSpec((tm, d), row),
                  pl.BlockSpec((HALO, d), lambda i, j: (jnp.maximum(i * hb - 1, 0), 0)),
                  pl.BlockSpec((1, d), lambda i, j: (0, 0)),
                  pl.BlockSpec((1, 1, d), per_batch),
                  pl.BlockSpec((1, 1, d), per_batch),
                  pl.BlockSpec((d, tf), lambda i, j: (0, j)),
                  pl.BlockSpec((d, tf), lambda i, j: (0, j + nf)),
                  pl.BlockSpec((CONV_WIDTH, tf), lambda i, j: (0, j)),
                  pl.BlockSpec((CONV_WIDTH, tf), lambda i, j: (0, j + nf)),
                  pl.BlockSpec((1, tf), lambda i, j: (0, j)),
                  pl.BlockSpec((1, tf), lambda i, j: (0, j + nf)),
                  pl.BlockSpec((tf, d), lambda i, j: (j, 0)),
                  pl.BlockSpec((1, 1, d), per_batch),
                  pl.BlockSpec((1, d), lambda i, j: (0, 0))],
        out_specs=pl.BlockSpec((tm, d), row),
        out_shape=jax.ShapeDtypeStruct((m, d), F32),
        scratch_shapes=[pltpu.VMEM((HALO + tm, d), BF16), pltpu.VMEM((tm, d), F32),
                        pltpu.VMEM((HALO + tm, tf), F32), pltpu.VMEM((HALO + tm, tf), F32)],
        compiler_params=_params("parallel", "arbitrary"),
        name="conv_ffn",
    )(x, x, g.reshape(1, d), shift, scale, w_up, w_up, conv_w, conv_w, conv_b.reshape(1, -1),
      conv_b.reshape(1, -1), w_down, gate, final_g.reshape(1, d))


def _t5_bucket_np(dist):
    n = np.maximum(dist, 0)
    max_exact = T5_BUCKETS // 2
    nf = np.maximum(n, 1).astype(np.float64)
    val = np.log(nf / max_exact) / math.log(T5_MAX_DIST / max_exact) * (T5_BUCKETS - max_exact)
    large = max_exact + np.trunc(val + 1e-6).astype(np.int64)
    return np.where(n < max_exact, n, np.minimum(large, T5_BUCKETS - 1)).astype(np.int32)


def _t5_gather_kernel(t5_ref, bkt_ref, o_ref, *, mult):
    h = pl.program_id(0)
    bk = bkt_ref[...]
    acc = jnp.zeros(bk.shape, F32)
    for b in range(T5_BUCKETS):
        acc = jnp.where(bk == b, t5_ref[b, h], acc)
    o_ref[0] = jnp.where(bk == T5_MASKED, MASKED, acc * mult)


def _t5_gather(t5_bias, bkt, tr, mult=1.0):
    rows, cols = bkt.shape
    heads = t5_bias.shape[1]
    return pl.pallas_call(
        functools.partial(_t5_gather_kernel, mult=mult),
        grid=(heads, rows // tr),
        in_specs=[pl.BlockSpec(memory_space=pltpu.SMEM),
                  pl.BlockSpec((tr, cols), lambda h, r: (r, 0))],
        out_specs=pl.BlockSpec((1, tr, cols), lambda h, r: (h, r, 0)),
        out_shape=jax.ShapeDtypeStruct((heads, rows, cols), F32),
        compiler_params=_params("parallel", "parallel"),
        name="t5_gather",
    )(t5_bias, jnp.asarray(bkt))


TAB_DIAG, TAB_SUB, TAB_FAR, TAB_EDGE = 0, 1, 2, 3


def _attention_tables(t5_bias, t):
    i = np.arange(t)[:, None]
    j = np.arange(t)[None, :]
    assert int(_t5_bucket_np(np.array(t + 1))) == T5_BUCKETS - 1
    far = np.full((t, t), T5_BUCKETS - 1, np.int32)
    bkt = np.concatenate([np.where(j <= i, _t5_bucket_np(i - j), T5_MASKED), _t5_bucket_np(t + i - j), far,
                          np.where(j > i, far, T5_MASKED)], axis=0).astype(np.int32)
    return _t5_gather(t5_bias, bkt, tr=t, mult=LOG2E)


def _tile_iota(t):
    return lax.broadcasted_iota(jnp.int32, (t, t), 0), lax.broadcasted_iota(jnp.int32, (t, t), 1)


def _rect_iota(i, tq, start, tk):
    rowg = i * tq + lax.broadcasted_iota(jnp.int32, (tq, tk), 0)
    colg = start + lax.broadcasted_iota(jnp.int32, (tq, tk), 1)
    return rowg, colg


def _osm_reset(m_scr, l_scr, acc_scr):
    m_scr[...] = jnp.full(m_scr.shape, M_INIT, F32)
    l_scr[...] = jnp.zeros(l_scr.shape, F32)
    acc_scr[...] = jnp.zeros(acc_scr.shape, F32)


def _osm_update(t2, rows, slot, p_scr, m_scr, l_scr, acc_scr):
    m_prev = m_scr[rows, :]
    m_new = jnp.maximum(m_prev, jnp.max(t2, axis=-1, keepdims=True))
    p = jnp.exp2(t2 - pltpu.repeat(m_new, t2.shape[1] // LANE, axis=1))
    alpha = jnp.exp2(m_prev - m_new)
    l_scr[rows, :] = alpha * l_scr[rows, :] + jnp.sum(p, axis=-1, keepdims=True)
    m_scr[rows, :] = m_new
    p_scr[slot, rows, :] = p.astype(BF16)
    acc_scr[rows, :] = alpha * acc_scr[rows, :]


def _pipelined_tiles(n_tiles, scores, update, update_last=None):
    update_last = update if update_last is None else update_last
    last = n_tiles - 1
    n_pairs = last // 2
    scores(0, 0)

    def pair(n, cr):
        scores(2 * n + 1, 1)
        update(2 * n, 0)
        scores(2 * n + 2, 0)
        update(2 * n + 1, 1)
        return cr

    lax.fori_loop(0, n_pairs, pair, 0)

    @pl.when(last % 2 == 0)
    def _():
        update_last(last, 0)

    @pl.when(last % 2 == 1)
    def _():
        scores(last, 1)
        update(last - 1, 0)
        update_last(last, 1)


def _att_scratch(rows, tk, slots=2):
    return [pltpu.VMEM((slots, rows, tk), F32), pltpu.VMEM((slots, rows, tk), BF16), pltpu.VMEM((rows, LANE), F32),
            pltpu.VMEM((rows, LANE), F32), pltpu.VMEM((rows, HEAD_DIM), F32)]


def _sb_kernel(q_ref, k_ref, v_ref, o_ref, s_scr, c_scr, acc_scr, *, tq, tk):
    i = pl.program_id(2)
    q = q_ref[...]
    row, col = _tile_iota(tk)
    upper = jnp.where(row > col, 1.0, 0.0).astype(BF16)
    c_scr[...] = jnp.zeros(c_scr.shape, F32)
    acc_scr[...] = jnp.zeros(acc_scr.shape, F32)
    reps = tk // LANE

    def scores(kb, slot):
        start = pl.multiple_of(kb * tk, tk)
        s_scr[slot] = _dot_nt(q, k_ref[pl.ds(start, tk), :])

    def update(kb, slot, diag):
        start = pl.multiple_of(kb * tk, tk)
        nz = s_scr[slot]
        e = jnp.exp2(jnp.abs(nz) * (-LOG2E))
        lk = jnp.minimum(nz, 0.0) - jnp.log(1.0 + e)
        if diag:
            rowg, colg = _rect_iota(i, tq, start, tk)
            past = colg < rowg
            lk = jnp.where(past, lk, 0.0)
        hi, lo = _split_bf16(lk)
        c = c_scr[...]
        later = _dot(hi, upper) + _dot(lo, upper) + pltpu.repeat(c, reps, axis=1)
        a = jnp.exp(lk - nz + later)
        if diag:
            a = jnp.where(past, a, 0.0)
        acc_scr[...] += _dot(a.astype(BF16), v_ref[pl.ds(start, tk), :])
        c_scr[...] = c + jnp.sum(lk, axis=-1, keepdims=True)

    assert tq == 2 * tk
    scores(2 * i + 1, 1)
    scores(2 * i, 0)
    update(2 * i + 1, 1, True)
    scores(jnp.maximum(2 * i - 1, 0), 1)
    update(2 * i, 0, True)

    def more(carry):
        n, c_max = carry
        return (n < i) & (c_max > SB_DEAD)

    def pair(carry):
        n, _ = carry
        kb = 2 * (i - n) - 1
        scores(kb - 1, 0)
        update(kb, 1, False)
        scores(jnp.maximum(kb - 2, 0), 1)
        update(kb - 1, 0, False)
        return n + 1, jnp.max(c_scr[...])

    lax.while_loop(more, pair, (0, jnp.max(c_scr[...])))
    o_ref[...] = acc_scr[...].astype(o_ref.dtype)


def _sb_attention(qkv, *, batch, seq_len, tq=ATT_TQ, tk=ATT_TK):
    h = N_HEADS
    nq = seq_len // tq
    return pl.pallas_call(
        functools.partial(_sb_kernel, tq=tq, tk=tk),
        grid=(batch, h, nq),
        in_specs=[pl.BlockSpec((tq, HEAD_DIM), lambda b, hh, i: (b * nq + i, hh)),
                  pl.BlockSpec((seq_len, HEAD_DIM), lambda b, hh, i: (b, h + hh)),
                  pl.BlockSpec((seq_len, HEAD_DIM), lambda b, hh, i: (b, 2 * h + hh))],
        out_specs=pl.BlockSpec((tq, HEAD_DIM), lambda b, hh, i: (b * nq + i, hh)),
        out_shape=jax.ShapeDtypeStruct((batch * seq_len, h * HEAD_DIM), BF16),
        scratch_shapes=[pltpu.VMEM((2, tq, tk), F32), pltpu.VMEM((tq, LANE), F32), pltpu.VMEM((tq, HEAD_DIM), F32)],
        compiler_params=_params("parallel", "parallel", "arbitrary"),
        name="sb_attention",
    )(qkv, qkv, qkv)


def _table_offset(kb, i, t):
    return pl.multiple_of(jnp.where(kb == i, TAB_DIAG * t, jnp.where(kb == i - 1, TAB_SUB * t, TAB_FAR * t)), t)


def _diff_kernel(q_ref, k_ref, v_ref, tab_ref, lam_ref, hg_ref, o_ref, s_scr, p_scr, m_scr, l_scr, acc_scr, *, t,
                 lambda_init):
    i = pl.program_id(2)
    q = q_ref[...]
    lane = lax.broadcasted_iota(jnp.int32, q.shape, 1)
    zero = jnp.zeros_like(q)
    q2 = jnp.concatenate([jnp.where(lane < DIFF_DIM, q, zero), jnp.where(lane >= DIFF_DIM, q, zero)], axis=0)
    _osm_reset(m_scr, l_scr, acc_scr)

    def scores(kb, slot):
        s_scr[slot] = _dot_nt(q2, k_ref[pl.ds(pl.multiple_of(kb * t, t), t), :])

    def update(kb, slot):
        off = _table_offset(kb, i, t)
        for half in range(2):
            rows = pl.ds(half * t, t)
            t2 = s_scr[slot, rows, :] + tab_ref[0, pl.ds(off, t), :]
            _osm_update(t2, rows, slot, p_scr, m_scr, l_scr, acc_scr)
        acc_scr[...] += _dot(p_scr[slot], v_ref[pl.ds(pl.multiple_of(kb * t, t), t), :])

    _pipelined_tiles(i + 1, scores, update)

    lam = lam_ref[...]
    lmbda = (jnp.exp(jnp.sum(lam[0:1] * lam[1:2], axis=-1, keepdims=True))
             - jnp.exp(jnp.sum(lam[2:3] * lam[3:4], axis=-1, keepdims=True)) + lambda_init)
    o = acc_scr[...] / l_scr[...]
    o = o[:t] - lmbda * o[t:]
    o_ref[...] = (_rms(o, hg_ref[...]) * (1.0 - lambda_init)).astype(o_ref.dtype)


def _diff_attention(qkv, tables, lam, head_g, *, batch, seq_len, lambda_init, t=DIFF_T):
    h = N_HEADS
    nq = seq_len // t
    return pl.pallas_call(
        functools.partial(_diff_kernel, t=t, lambda_init=lambda_init),
        grid=(batch, h, nq),
        in_specs=[pl.BlockSpec((t, HEAD_DIM), lambda b, hh, i: (b * nq + i, hh)),
                  pl.BlockSpec((seq_len, HEAD_DIM), lambda b, hh, i: (b, h + hh)),
                  pl.BlockSpec((seq_len, HEAD_DIM), lambda b, hh, i: (b, 2 * h + hh)),
                  pl.BlockSpec((1, 4 * t, t), lambda b, hh, i: (hh, 0, 0)),
                  pl.BlockSpec((4, DIFF_DIM), lambda b, hh, i: (0, 0)),
                  pl.BlockSpec((1, HEAD_DIM), lambda b, hh, i: (0, 0))],
        out_specs=pl.BlockSpec((t, HEAD_DIM), lambda b, hh, i: (b * nq + i, hh)),
        out_shape=jax.ShapeDtypeStruct((batch * seq_len, h * HEAD_DIM), BF16),
        scratch_shapes=_att_scratch(2 * t, t),
        compiler_params=_params("parallel", "parallel", "arbitrary"),
        name="diff_attention",
    )(qkv, qkv, qkv, tables, lam, head_g.reshape(1, HEAD_DIM))


def _mla_kernel(qn_ref, qr_ref, kn_ref, kr_ref, v_ref, o_ref, s_scr, p_scr, m_scr, l_scr, acc_scr, *, tq, tk):
    i = pl.program_id(2)
    q = jnp.concatenate([qn_ref[...], qr_ref[...]], axis=-1)
    _osm_reset(m_scr, l_scr, acc_scr)

    def scores(kb, slot):
        start = pl.multiple_of(kb * tk, tk)
        k = jnp.concatenate([kn_ref[pl.ds(start, tk), :], kr_ref[pl.ds(start, tk), :]], axis=-1)
        s_scr[slot] = _dot_nt(q, k)

    def update(kb, slot, mask):
        start = pl.multiple_of(kb * tk, tk)
        t2 = s_scr[slot]
        if mask:
            rowg, colg = _rect_iota(i, tq, start, tk)
            t2 = jnp.where(colg <= rowg, t2, MASKED)
        _osm_update(t2, pl.ds(0, tq), slot, p_scr, m_scr, l_scr, acc_scr)
        acc_scr[...] += _dot(p_scr[slot], v_ref[pl.ds(start, tk), :])

    assert tq == tk
    _pipelined_tiles(i + 1, scores, lambda kb, slot: update(kb, slot, False), lambda kb, slot: update(kb, slot, True))
    o_ref[...] = (acc_scr[...] / l_scr[...]).astype(o_ref.dtype)


def _mla_attention(qn, qr, kv, kr, *, batch, seq_len, tq=MLA_T, tk=MLA_T):
    h = N_HEADS
    nq = seq_len // tq
    qspec = pl.BlockSpec((tq, HEAD_DIM), lambda b, hh, i: (b * nq + i, hh))
    return pl.pallas_call(
        functools.partial(_mla_kernel, tq=tq, tk=tk),
        grid=(batch, h, nq),
        in_specs=[qspec, qspec,
                  pl.BlockSpec((seq_len, HEAD_DIM), lambda b, hh, i: (b, 2 * hh)),
                  pl.BlockSpec((seq_len, HEAD_DIM), lambda b, hh, i: (b, 0)),
                  pl.BlockSpec((seq_len, HEAD_DIM), lambda b, hh, i: (b, 2 * hh + 1))],
        out_specs=qspec,
        out_shape=jax.ShapeDtypeStruct((batch * seq_len, h * HEAD_DIM), BF16),
        scratch_shapes=_att_scratch(tq, tk),
        compiler_params=_params("parallel", "parallel", "arbitrary"),
        name="mla_attention",
    )(qn, qr, kv, kr, kv)


def _compress_kernel(raw_ref, pe_ref, w1_ref, w2_ref, o_ref, *, n_slots):
    half = NSA_CMP_BLOCK // 2
    p1 = jnp.zeros((n_slots, HEAD_DIM), F32)
    p2 = jnp.zeros((n_slots, HEAD_DIM), F32)
    for l in range(half):
        a = raw_ref[pl.ds(l, n_slots, stride=NSA_CMP_STRIDE), :]
        p1 = p1 + _dot((a + pe_ref[0, l:l + 1, :]).astype(BF16), w1_ref[0, l])
        p2 = p2 + _dot((a + pe_ref[0, half + l:half + l + 1, :]).astype(BF16), w1_ref[0, half + l])
    pre = p1 + pltpu.roll(p2, n_slots - 1, 0)
    hid = pre * jax.nn.sigmoid(pre)
    o_ref[0, 0, 0] = _dot(hid.astype(BF16), w2_ref[0]).astype(o_ref.dtype)


def _nsa_compress(raw, pe, w1, w2, *, batch, seq_len):
    g = NSA_GROUPS
    n_slots = seq_len // NSA_CMP_STRIDE
    return pl.pallas_call(
        functools.partial(_compress_kernel, n_slots=n_slots),
        grid=(batch, 2, g),
        in_specs=[pl.BlockSpec((seq_len, HEAD_DIM), lambda b, kv, gg: (b, kv * g + gg)),
                  pl.BlockSpec((1, NSA_CMP_BLOCK, HEAD_DIM), lambda b, kv, gg: (kv, 0, 0)),
                  pl.BlockSpec((1, NSA_CMP_BLOCK, HEAD_DIM, HEAD_DIM), lambda b, kv, gg: (kv, 0, 0, 0)),
                  pl.BlockSpec((1, HEAD_DIM, HEAD_DIM), lambda b, kv, gg: (kv, 0, 0))],
        out_specs=pl.BlockSpec((1, 1, 1, n_slots, HEAD_DIM), lambda b, kv, gg: (b, kv, gg, 0, 0)),
        out_shape=jax.ShapeDtypeStruct((batch, 2, g, n_slots, HEAD_DIM), BF16),
        compiler_params=_params("parallel", "parallel", "parallel"),
        name="nsa_compress",
    )(raw, pe, w1, w2)


def _nsa_cmp_kernel(q_ref, kc_ref, vc_ref, bias_ref, gates_ref, ovt_ref, oc_ref, sel_ref, *, t, n_slots, n_sel,
                    n_top):
    i = pl.program_id(2)
    kc = kc_ref[0, 0, 0]
    vc = vc_ref[0, 0, 0]
    qpos = i * t + lax.broadcasted_iota(jnp.int32, (t, n_slots), 0)
    cmp_end = NSA_CMP_STRIDE * lax.broadcasted_iota(jnp.int32, (t, n_slots), 1) + (NSA_CMP_BLOCK - 1)
    valid = cmp_end <= qpos
    gates = gates_ref[...]
    psum = jnp.zeros((t, n_slots), F32)
    for r in range(NSA_REP):
        q = q_ref[:, r * HEAD_DIM:(r + 1) * HEAD_DIM]
        s = jnp.where(valid, _dot_nt(q, kc) + bias_ref[r], NEG)
        m = jnp.max(s, axis=-1, keepdims=True)
        p = jnp.where(valid, jnp.exp2(s - m), 0.0)
        p = p / jnp.maximum(jnp.sum(p, axis=-1, keepdims=True), 1e-30)
        psum = psum + p
        oc_ref[:, r * HEAD_DIM:(r + 1) * HEAD_DIM] = gates[:, r:r + 1] * _dot(p.astype(BF16), vc)

    hi, lo = _split_bf16(psum)
    ovt = ovt_ref[...]
    imp = _dot_nt(ovt, hi) + _dot_nt(ovt, lo)
    blk = lax.broadcasted_iota(jnp.int32, (n_sel, t), 0)
    tpos = i * t + lax.broadcasted_iota(jnp.int32, (n_sel, t), 1)
    cur = tpos // NSA_SEL_BLOCK
    forced = (blk == 0) | (blk == cur) | (blk == cur - 1)
    score = jnp.where(blk * NSA_SEL_BLOCK <= tpos, jnp.where(forced, FORCED_SCORE, imp), -1.0)
    rank = jnp.zeros((n_sel, t), F32)
    for mm in range(n_sel):
        sm = score[mm:mm + 1, :]
        ahead = (sm > score) | ((sm == score) & (blk > mm))
        rank = rank + jnp.where(ahead, 1.0, 0.0)
    sel_t = jnp.where(rank < n_top, 1.0, 0.0).astype(BF16)
    row, col = _tile_iota(t)
    eye = jnp.where(row == col, 1.0, 0.0).astype(BF16)
    sel_ref[0, 0] = _dot_nt(eye, sel_t).astype(sel_ref.dtype)


def _nsa_cmp_attention(q_all, kvc, bias_c, gates, *, batch, seq_len, t=ATT_T):
    g = NSA_GROUPS
    nq = seq_len // t
    n_slots = seq_len // NSA_CMP_STRIDE
    n_sel = seq_len // NSA_SEL_BLOCK
    c0 = NSA_CMP_STRIDE * np.arange(n_slots)[:, None]
    s0 = NSA_SEL_BLOCK * np.arange(n_sel)[None, :]
    overlap = (c0 < s0 + NSA_SEL_BLOCK) & (c0 + NSA_CMP_BLOCK > s0)
    ovt = jnp.asarray(overlap.T.astype(np.float32), dtype=BF16)
    gw = NSA_REP * HEAD_DIM
    return pl.pallas_call(
        functools.partial(_nsa_cmp_kernel, t=t, n_slots=n_slots, n_sel=n_sel, n_top=min(NSA_TOPN, n_sel)),
        grid=(batch, g, nq),
        in_specs=[pl.BlockSpec((t, gw), lambda b, gg, i: (b * nq + i, gg)),
                  pl.BlockSpec((1, 1, 1, n_slots, HEAD_DIM), lambda b, gg, i: (b, 0, gg, 0, 0)),
                  pl.BlockSpec((1, 1, 1, n_slots, HEAD_DIM), lambda b, gg, i: (b, 1, gg, 0, 0)),
                  pl.BlockSpec((NSA_REP, t, n_slots), lambda b, gg, i: (gg, i, 0)),
                  pl.BlockSpec((t, LANE), lambda b, gg, i: (b * nq + i, gg)),
                  pl.BlockSpec((n_sel, n_slots), lambda b, gg, i: (0, 0))],
        out_specs=[pl.BlockSpec((t, gw), lambda b, gg, i: (b * nq + i, gg)),
                   pl.BlockSpec((1, 1, t, n_sel), lambda b, gg, i: (b, gg, i, 0))],
        out_shape=[jax.ShapeDtypeStruct((batch * seq_len, g * gw), F32),
                   jax.ShapeDtypeStruct((batch, g, seq_len, n_sel), BF16)],
        compiler_params=_params("parallel", "parallel", "arbitrary"),
        name="nsa_cmp_attention",
    )(q_all, kvc, kvc, bias_c, gates, ovt)


def _nsa_main_kernel(q_ref, ks_ref, vs_ref, kw_ref, vw_ref, sel_ref, tab_ref, gates_ref, oc_ref, o_ref,
                     s_scr, p_scr, m_scr, l_scr, acc_scr, os_scr, mk_scr, *, t):
    i = pl.program_id(2)
    rep = NSA_REP
    q = jnp.concatenate([q_ref[:, r * HEAD_DIM:(r + 1) * HEAD_DIM] for r in range(rep)], axis=0)

    sel = sel_ref[0, 0]
    n_sel = sel.shape[1]
    blk_row = lax.broadcasted_iota(jnp.int32, (n_sel, t), 0)
    key_col = lax.broadcasted_iota(jnp.int32, (n_sel, t), 1)
    _osm_reset(m_scr, l_scr, acc_scr)

    def sel_scores(kb, slot):
        s_scr[slot] = _dot_nt(q, ks_ref[pl.ds(pl.multiple_of(kb * t, t), t), :])

    def sel_update(kb, slot):
        start = pl.multiple_of(kb * t, t)
        expand = jnp.where((start + key_col) // NSA_SEL_BLOCK == blk_row, 1.0, 0.0).astype(BF16)
        mk_scr[...] = _dot(sel, expand)
        off = _table_offset(kb, i, t)
        for r in range(rep):
            rows = pl.ds(r * t, t)
            t2 = s_scr[slot, rows, :] + tab_ref[r, pl.ds(off, t), :]
            t2 = jnp.where(mk_scr[...] > 0.5, t2, MASKED)
            _osm_update(t2, rows, slot, p_scr, m_scr, l_scr, acc_scr)
        acc_scr[...] += _dot(p_scr[slot], vs_ref[pl.ds(start, t), :])

    _pipelined_tiles(i + 1, sel_scores, sel_update)
    os_scr[...] = acc_scr[...] / l_scr[...]

    n_back = NSA_WINDOW // t
    assert n_back == 2
    _osm_reset(m_scr, l_scr, acc_scr)

    def win_scores(kb, slot):
        s_scr[slot] = _dot_nt(q, kw_ref[pl.ds(pl.multiple_of(kb * t, t), t), :])

    def win_update(kb, slot, region):
        for r in range(rep):
            rows = pl.ds(r * t, t)
            t2 = s_scr[slot, rows, :] + tab_ref[r, pl.ds(region * t, t), :]
            _osm_update(t2, rows, slot, p_scr, m_scr, l_scr, acc_scr)
        acc_scr[...] += _dot(p_scr[slot], vw_ref[pl.ds(pl.multiple_of(kb * t, t), t), :])

    def window(n_tiles):
        regions = (TAB_DIAG, TAB_SUB, TAB_EDGE)
        for back in range(n_tiles):
            win_scores(i - back, back)
        for back in reversed(range(n_tiles)):
            win_update(i - back, back, regions[back])

    pl.when(i >= 2)(lambda: window(3))
    pl.when(i == 1)(lambda: window(2))
    pl.when(i == 0)(lambda: window(1))
    o_w = acc_scr[...] / l_scr[...]
    o_s = os_scr[...]

    gates = gates_ref[...]
    for r in range(rep):
        rows = slice(r * t, (r + 1) * t)
        cols = slice(r * HEAD_DIM, (r + 1) * HEAD_DIM)
        o = oc_ref[:, cols] + gates[:, rep + r:rep + r + 1] * o_s[rows] + gates[:, 2 * rep + r:2 * rep + r + 1] * o_w[rows]
        o_ref[:, cols] = o.astype(o_ref.dtype)


def _nsa_main_attention(qkv, sel, tables, gates, oc, *, batch, seq_len, t=ATT_T):
    g = NSA_GROUPS
    nq = seq_len // t
    n_sel = seq_len // NSA_SEL_BLOCK
    gw = NSA_REP * HEAD_DIM
    qb = N_HEADS
    kv = lambda which: pl.BlockSpec((seq_len, HEAD_DIM), lambda b, gg, i: (b, qb + which * g + gg))
    tile = pl.BlockSpec((t, gw), lambda b, gg, i: (b * nq + i, gg))
    rows = NSA_REP * t
    return pl.pallas_call(
        functools.partial(_nsa_main_kernel, t=t),
        grid=(batch, g, nq),
        in_specs=[tile, kv(0), kv(1), kv(2), kv(3),
                  pl.BlockSpec((1, 1, t, n_sel), lambda b, gg, i: (b, gg, i, 0)),
                  pl.BlockSpec((NSA_REP, 4 * t, t), lambda b, gg, i: (gg, 0, 0)),
                  pl.BlockSpec((t, LANE), lambda b, gg, i: (b * nq + i, gg)),
                  tile],
        out_specs=tile,
        out_shape=jax.ShapeDtypeStruct((batch * seq_len, g * gw), BF16),
        scratch_shapes=_att_scratch(rows, t, slots=3) + [pltpu.VMEM((rows, HEAD_DIM), F32), pltpu.VMEM((t, t), F32)],
        compiler_params=_params("parallel", "parallel", "arbitrary"),
        name="nsa_main_attention",
    )(qkv, qkv, qkv, qkv, qkv, sel, tables, gates, oc)


def _rope_tables(seq_len, width):
    half = MLA_ROPE // 2
    inv = np.power(ROPE_THETA, -np.arange(half, dtype=np.float32) / half).astype(np.float32)
    ang = np.arange(seq_len, dtype=np.float32)[:, None] * inv[None, :]
    pad = np.zeros((seq_len, LANE - MLA_ROPE), np.float32)
    cos = np.concatenate([np.cos(ang), np.cos(ang), pad], axis=1)
    sin = np.concatenate([np.sin(ang), np.sin(ang), pad], axis=1)
    reps = width // LANE
    return jnp.asarray(np.tile(cos, (1, reps))), jnp.asarray(np.tile(sin, (1, reps)))


def _rope_weights(w):
    k, n, _ = w.shape
    half = MLA_ROPE // 2
    pad = jnp.zeros((k, n, LANE - MLA_ROPE), w.dtype)
    wa = jnp.concatenate([w, pad], axis=-1)
    wb = jnp.concatenate([-w[..., half:], w[..., :half], pad], axis=-1)
    return wa.reshape(k, n * LANE).astype(BF16), wb.reshape(k, n * LANE).astype(BF16)


def kernel(x, c, t5_bias, ada_w, ada_b, norm_g, final_g, ffn_w_up, ffn_conv_w, ffn_conv_b, ffn_w_down, sb_w_in, sb_w_out, nsa_w_in, nsa_cmp_pe, nsa_cmp_w1, nsa_cmp_w2, nsa_w_out, diff_w_in, diff_lambda, diff_head_g, diff_w_out, mla_w_in, mla_q_g, mla_w_qb, mla_kv_g, mla_w_kvb, mla_w_out):
    batch, seq_len, d = x.shape
    depth = ada_w.shape[0]
    h, dh, g = N_HEADS, HEAD_DIM, NSA_GROUPS
    sizes = dict(batch=batch, seq_len=seq_len)

    mod = _ada_mod(c, ada_w, ada_b)
    tables = _attention_tables(t5_bias, ATT_T)

    xf = x.reshape(batch * seq_len, d)
    for i in range(depth):
        mixer, j = i % 4, i // 4
        sh1, sc1, gt1, sh2, sc2, gt2 = (mod[i, :, n * d:(n + 1) * d].reshape(batch, 1, d) for n in range(6))
        nm = functools.partial(_norm_mm, xf, norm_g[i, 0], seq_len=seq_len, shift=sh1, scale=sc1)
        if mixer == 0:
            w_in = sb_w_in[j]
            w_in = jnp.concatenate([w_in[:, :h * dh] * SB_QSCALE, w_in[:, h * dh:]], axis=1)
            qkv = nm(w_in.astype(BF16), name="sb_in")
            o = _sb_attention(qkv, **sizes)
            w_out = sb_w_out[j]
        elif mixer == 1:
            w_in = nsa_w_in[j]
            n_q, n_kv = h * dh, g * dh
            w_att = jnp.concatenate([w_in[:, :n_q] * NSA_QSCALE, w_in[:, n_q + 2 * n_kv:n_q + 6 * n_kv]], axis=1)
            w_cmp = w_in[:, n_q:n_q + 2 * n_kv]
            w_g = w_in[:, n_q + 6 * n_kv:].reshape(d, 3, g, NSA_REP).transpose(0, 2, 1, 3).reshape(d, g, 3 * NSA_REP)
            w_g = jnp.pad(w_g, ((0, 0), (0, 0), (0, LANE - 3 * NSA_REP))).reshape(d, g * LANE)
            qkv = nm(w_att.astype(BF16), name="nsa_in")
            raw = nm(w_cmp.astype(BF16), out_dtype=F32, name="nsa_in_cmp")
            gates = nm(w_g.astype(BF16), out_dtype=F32, act="sigmoid", name="nsa_in_gates")
            kvc = _nsa_compress(raw, nsa_cmp_pe[j], nsa_cmp_w1[j].reshape(2, NSA_CMP_BLOCK, dh, dh).astype(BF16),
                                nsa_cmp_w2[j].astype(BF16), **sizes)
            n_slots = seq_len // NSA_CMP_STRIDE
            dist_c = np.arange(seq_len)[:, None] - (NSA_CMP_STRIDE * np.arange(n_slots)[None, :] + NSA_CMP_BLOCK - 1)
            bias_c = _t5_gather(t5_bias, _t5_bucket_np(dist_c), tr=min(seq_len, 512), mult=LOG2E)
            oc, sel = _nsa_cmp_attention(qkv, kvc, bias_c, gates, **sizes)
            o = _nsa_main_attention(qkv, sel, tables, gates, oc, **sizes)
            w_out = nsa_w_out[j]
        elif mixer == 2:
            lambda_init = 0.8 - 0.6 * math.exp(-0.3 * i)
            w_in = diff_w_in[j]
            w_in = jnp.concatenate([w_in[:, :h * dh] * DIFF_QSCALE, w_in[:, h * dh:]], axis=1)
            qkv = nm(w_in.astype(BF16), name="diff_in")
            o = _diff_attention(qkv, _attention_tables(t5_bias, DIFF_T), diff_lambda[j], diff_head_g[j],
                                lambda_init=lambda_init, **sizes)
            w_out = diff_w_out[j]
        else:
            w_in = mla_w_in[j]
            nq_l, nkv_l = MLA_Q_LORA, MLA_KV_LORA
            w_lat = jnp.concatenate([w_in[:, :nq_l], jnp.zeros((d, 2 * nkv_l - nq_l), w_in.dtype),
                                     w_in[:, nq_l:nq_l + nkv_l]], axis=1)
            lat = nm(w_lat.astype(BF16), out_dtype=F32, name="mla_in")
            cos1, sin1 = _rope_tables(seq_len, LANE)
            wa, wb = _rope_weights(w_in[:, nq_l + nkv_l:].reshape(d, 1, MLA_ROPE))
            kr = nm(wa, wb=wb, cos=cos1, sin=sin1, name="mla_in_rope")
            w_qb = (mla_w_qb[j] * MLA_QSCALE).reshape(nq_l, h, MLA_NOPE + MLA_ROPE)
            qn = _norm_mm(lat, mla_q_g[j], w_qb[:, :, :MLA_NOPE].reshape(nq_l, h * MLA_NOPE).astype(BF16),
                          seq_len=seq_len, x_cols=nq_l, x_col_block=0, name="mla_q_nope")
            cosh, sinh = _rope_tables(seq_len, h * LANE)
            wa, wb = _rope_weights(w_qb[:, :, MLA_NOPE:])
            qr = _norm_mm(lat, mla_q_g[j], wa, wb=wb, cos=cosh, sin=sinh, seq_len=seq_len, x_cols=nq_l,
                          x_col_block=0, name="mla_q_rope")
            kv = _norm_mm(lat, mla_kv_g[j], mla_w_kvb[j].astype(BF16), seq_len=seq_len, x_cols=nkv_l,
                          x_col_block=2, name="mla_kv")
            o = _mla_attention(qn, qr, kv, kr, **sizes)
            w_out = mla_w_out[j]
        xf = _mm_residual(o, w_out.astype(BF16), xf, gt1, seq_len=seq_len)
        xf = _conv_ffn(xf, norm_g[i, 1], sh2, sc2, ffn_w_up[i].astype(BF16), ffn_conv_w[i], ffn_conv_b[i],
                       ffn_w_down[i].astype(BF16), gt2, final_g, seq_len=seq_len, final_norm=(i == depth - 1))
    return xf.reshape(batch, seq_len, d)
```

```python
import functools
import math

import numpy as np
import jax
import jax.numpy as jnp
from jax import lax
from jax.experimental import pallas as pl
from jax.experimental.pallas import tpu as pltpu

F32 = jnp.float32
BF16 = jnp.bfloat16
EPS = 1e-6
NEG = -1e30

LANE = 128
HALO = 16
VMEM_LIMIT = 56 * 2**20

T5_BUCKETS = 32
T5_MAX_DIST = 128
N_HEADS = 16
HEAD_DIM = 128
NSA_GROUPS = 4
NSA_REP = 4
NSA_CMP_BLOCK = 32
NSA_CMP_STRIDE = 16
NSA_SEL_BLOCK = 64
NSA_TOPN = 16
NSA_WINDOW = 512
FORCED_SCORE = 1e9
DIFF_DIM = 64
MLA_Q_LORA = 768
MLA_KV_LORA = 512
MLA_NOPE = 128
MLA_ROPE = 64
ROPE_THETA = 10000.0
CONV_WIDTH = 3
ATT_T = 256
DIFF_T = 512
MLA_T = 512
ATT_TQ = 512
ATT_TK = 256
LOG2E = 1.4426950408889634
SB_QSCALE = -(HEAD_DIM ** -0.5)
NSA_QSCALE = HEAD_DIM ** -0.5 * LOG2E
DIFF_QSCALE = DIFF_DIM ** -0.5 * LOG2E
MLA_QSCALE = (MLA_NOPE + MLA_ROPE) ** -0.5 * LOG2E
M_INIT = -1e30
MASKED = -2e30
SB_DEAD = -120.0
T5_MASKED = T5_BUCKETS


def _params(*sem):
    return pltpu.CompilerParams(dimension_semantics=sem, vmem_limit_bytes=VMEM_LIMIT)


def _dot(a, b):
    return jnp.dot(a, b, preferred_element_type=F32)


def _dot_nt(a, b):
    return lax.dot_general(a, b, (((1,), (1,)), ((), ())), preferred_element_type=F32)


def _split_bf16(x):
    hi = x.astype(BF16)
    lo = (x - hi.astype(F32)).astype(BF16)
    return hi, lo


def _rms(x, g):
    ms = jnp.mean(x * x, axis=-1, keepdims=True)
    return x * lax.rsqrt(ms + EPS) * g


def _ada_kernel(c_ref, w_ref, b_ref, o_ref):
    c = c_ref[...]
    a = c * jax.nn.sigmoid(c)
    a_hi, a_lo = _split_bf16(a)
    w_hi, w_lo = _split_bf16(w_ref[0])
    o_ref[0] = _dot(a_hi, w_hi) + _dot(a_lo, w_hi) + _dot(a_hi, w_lo) + b_ref[0]


def _ada_mod(c, ada_w, ada_b, tn=1024):
    depth, d, n = ada_w.shape
    b = c.shape[0]
    return pl.pallas_call(
        _ada_kernel,
        grid=(depth, n // tn),
        in_specs=[pl.BlockSpec((b, d), lambda l, j: (0, 0)),
                  pl.BlockSpec((1, d, tn), lambda l, j: (l, 0, j)),
                  pl.BlockSpec((1, 1, tn), lambda l, j: (l, 0, j))],
        out_specs=pl.BlockSpec((1, b, tn), lambda l, j: (l, 0, j)),
        out_shape=jax.ShapeDtypeStruct((depth, b, n), F32),
        compiler_params=_params("parallel", "parallel"),
        name="ada_mod",
    )(c, ada_w, ada_b.reshape(depth, 1, n))


def _norm_mm_kernel(*refs, modulated, rope, act):
    it = iter(refs)
    x_ref, g_ref = next(it), next(it)
    sh_ref = sc_ref = wb_ref = cos_ref = sin_ref = None
    if modulated:
        sh_ref, sc_ref = next(it), next(it)
    w_ref = next(it)
    if rope:
        wb_ref, cos_ref, sin_ref = next(it), next(it), next(it)
    o_ref, h_ref = next(it), next(it)

    @pl.when(pl.program_id(1) == 0)
    def _():
        y = _rms(x_ref[...], g_ref[...])
        if modulated:
            y = y * (1.0 + sc_ref[0]) + sh_ref[0]
        h_ref[...] = y.astype(BF16)

    h = h_ref[...]
    acc = _dot(h, w_ref[...])
    if rope:
        acc = acc * cos_ref[...] + _dot(h, wb_ref[...]) * sin_ref[...]
    if act == "sigmoid":
        acc = jax.nn.sigmoid(acc)
    o_ref[...] = acc.astype(o_ref.dtype)


def _norm_mm(x, g, w, *, seq_len, shift=None, scale=None, wb=None, cos=None, sin=None, act=None,
             out_dtype=BF16, x_cols=None, x_col_block=0, tm=1024, tn=512, name="norm_mm"):
    m = x.shape[0]
    k = x.shape[1] if x_cols is None else x_cols
    n = w.shape[1]
    tn = min(tn, n)
    modulated, rope = shift is not None, wb is not None
    tps = seq_len // tm
    in_specs = [pl.BlockSpec((tm, k), lambda i, j: (i, x_col_block)),
                pl.BlockSpec((1, k), lambda i, j: (0, 0))]
    args = [x, g.reshape(1, k)]
    if modulated:
        in_specs += [pl.BlockSpec((1, 1, k), lambda i, j: (i // tps, 0, 0))] * 2
        args += [shift, scale]
    in_specs.append(pl.BlockSpec((k, tn), lambda i, j: (0, j)))
    args.append(w)
    if rope:
        in_specs.append(pl.BlockSpec((k, tn), lambda i, j: (0, j)))
        in_specs += [pl.BlockSpec((tm, tn), lambda i, j: (i % tps, j))] * 2
        args += [wb, cos, sin]
    return pl.pallas_call(
        functools.partial(_norm_mm_kernel, modulated=modulated, rope=rope, act=act),
        grid=(m // tm, n // tn),
        in_specs=in_specs,
        out_specs=pl.BlockSpec((tm, tn), lambda i, j: (i, j)),
        out_shape=jax.ShapeDtypeStruct((m, n), out_dtype),
        scratch_shapes=[pltpu.VMEM((tm, k), BF16)],
        compiler_params=_params("parallel", "arbitrary"),
        name=name,
    )(*args)


def _mm_res_kernel(a_ref, w_ref, x_ref, gate_ref, o_ref):
    o_ref[...] = x_ref[...] + gate_ref[0] * _dot(a_ref[...], w_ref[...])


def _mm_residual(a, w, x, gate, *, seq_len, tm=1024, tn=512, name="mm_residual"):
    m, k = a.shape
    n = w.shape[1]
    tps = seq_len // tm
    return pl.pallas_call(
        _mm_res_kernel,
        grid=(m // tm, n // tn),
        in_specs=[pl.BlockSpec((tm, k), lambda i, j: (i, 0)),
                  pl.BlockSpec((k, tn), lambda i, j: (0, j)),
                  pl.BlockSpec((tm, tn), lambda i, j: (i, j)),
                  pl.BlockSpec((1, 1, tn), lambda i, j: (i // tps, 0, j))],
        out_specs=pl.BlockSpec((tm, tn), lambda i, j: (i, j)),
        out_shape=jax.ShapeDtypeStruct((m, n), F32),
        compiler_params=_params("parallel", "arbitrary"),
        name=name,
    )(a, w, x, gate)


def _ffn_kernel(x_ref, xp_ref, g_ref, sh_ref, sc_ref, wg_ref, wu_ref, cwg_ref, cwu_ref, cbg_ref, cbu_ref,
                wd_ref, gate_ref, fg_ref, o_ref, h_ref, acc_ref, ug_scr, uu_scr, *, tiles_per_seq, final_norm):
    i, j = pl.program_id(0), pl.program_id(1)

    @pl.when(j == 0)
    def _():
        def nm(x):
            return _rms(x, g_ref[...]) * (1.0 + sc_ref[0]) + sh_ref[0]
        h_ref[HALO:, :] = nm(x_ref[...]).astype(BF16)
        keep = jnp.where(i % tiles_per_seq == 0, 0.0, 1.0)
        h_ref[:HALO, :] = (nm(xp_ref[...]) * keep).astype(BF16)
        acc_ref[...] = jnp.zeros_like(acc_ref)

    h = h_ref[...]

    def branch(w_ref, cw_ref, cb_ref, u_scr):
        u_scr[...] = _dot(h, w_ref[...])
        tm = u_scr.shape[0] - HALO
        cw = cw_ref[...]
        return (cw[0:1] * u_scr[pl.ds(HALO - 2, tm), :] + cw[1:2] * u_scr[pl.ds(HALO - 1, tm), :]
                + cw[2:3] * u_scr[pl.ds(HALO, tm), :] + cb_ref[...])

    gt = branch(wg_ref, cwg_ref, cbg_ref, ug_scr)
    up = branch(wu_ref, cwu_ref, cbu_ref, uu_scr)
    a = gt * jax.nn.sigmoid(gt) * up
    acc_ref[...] += _dot(a.astype(BF16), wd_ref[...])

    @pl.when(j == pl.num_programs(1) - 1)
    def _():
        y = x_ref[...] + gate_ref[0] * acc_ref[...]
        if final_norm:
            y = _rms(y, fg_ref[...])
        o_ref[...] = y


def _conv_ffn(x, g, shift, scale, w_up, conv_w, conv_b, w_down, gate, final_g, *, seq_len, final_norm,
              tm=512, tf=512):
    m, d = x.shape
    f = w_down.shape[0]
    nf = f // tf
    tps = seq_len // tm
    hb = tm // HALO
    row = lambda i, j: (i, 0)
    per_batch = lambda i, j: (i // tps, 0, 0)
    return pl.pallas_call(
        functools.partial(_ffn_kernel, tiles_per_seq=tps, final_norm=final_norm),
        grid=(m // tm, nf),
        in_specs=[pl.BlockSpec((tm, d), row),
                  pl.BlockSpec((HALO, d), lambda i, j: (jnp.maximum(i * hb - 1, 0), 0)),
                  pl.BlockSpec((1, d), lambda i, j: (0, 0)),
                  pl.BlockSpec((1, 1, d), per_batch),
                  pl.BlockSpec((1, 1, d), per_batch),
                  pl.BlockSpec((d, tf), lambda i, j: (0, j)),
                  pl.BlockSpec((d, tf), lambda i, j: (0, j + nf)),
                  pl.BlockSpec((CONV_WIDTH, tf), lambda i, j: (0, j)),
                  pl.BlockSpec((CONV_WIDTH, tf), lambda i, j: (0, j + nf)),
                  pl.BlockSpec((1, tf), lambda i, j: (0, j)),
                  pl.BlockSpec((1, tf), lambda i, j: (0, j + nf)),
                  pl.BlockSpec((tf, d), lambda i, j: (j, 0)),
                  pl.BlockSpec((1, 1, d), per_batch),
                  pl.BlockSpec((1, d), lambda i, j: (0, 0))],
        out_specs=pl.BlockSpec((tm, d), row),
        out_shape=jax.ShapeDtypeStruct((m, d), F32),
        scratch_shapes=[pltpu.VMEM((HALO + tm, d), BF16), pltpu.VMEM((tm, d), F32),
                        pltpu.VMEM((HALO + tm, tf), F32), pltpu.VMEM((HALO + tm, tf), F32)],
        compiler_params=_params("parallel", "arbitrary"),
        name="conv_ffn",
    )(x, x, g.reshape(1, d), shift, scale, w_up, w_up, conv_w, conv_w, conv_b.reshape(1, -1),
      conv_b.reshape(1, -1), w_down, gate, final_g.reshape(1, d))


def _t5_bucket_np(dist):
    n = np.maximum(dist, 0)
    max_exact = T5_BUCKETS // 2
    nf = np.maximum(n, 1).astype(np.float64)
    val = np.log(nf / max_exact) / math.log(T5_MAX_DIST / max_exact) * (T5_BUCKETS - max_exact)
    large = max_exact + np.trunc(val + 1e-6).astype(np.int64)
    return np.where(n < max_exact, n, np.minimum(large, T5_BUCKETS - 1)).astype(np.int32)


def _t5_gather_kernel(t5_ref, bkt_ref, o_ref, *, mult):
    h = pl.program_id(0)
    bk = bkt_ref[...]
    acc = jnp.zeros(bk.shape, F32)
    for b in range(T5_BUCKETS):
        acc = jnp.where(bk == b, t5_ref[b, h], acc)
    o_ref[0] = jnp.where(bk == T5_MASKED, MASKED, acc * mult)


def _t5_gather(t5_bias, bkt, tr, mult=1.0):
    rows, cols = bkt.shape
    heads = t5_bias.shape[1]
    return pl.pallas_call(
        functools.partial(_t5_gather_kernel, mult=mult),
        grid=(heads, rows // tr),
        in_specs=[pl.BlockSpec(memory_space=pltpu.SMEM),
                  pl.BlockSpec((tr, cols), lambda h, r: (r, 0))],
        out_specs=pl.BlockSpec((1, tr, cols), lambda h, r: (h, r, 0)),
        out_shape=jax.ShapeDtypeStruct((heads, rows, cols), F32),
        compiler_params=_params("parallel", "parallel"),
        name="t5_gather",
    )(t5_bias, jnp.asarray(bkt))


TAB_DIAG, TAB_SUB, TAB_FAR, TAB_EDGE = 0, 1, 2, 3


def _attention_tables(t5_bias, t):
    i = np.arange(t)[:, None]
    j = np.arange(t)[None, :]
    assert int(_t5_bucket_np(np.array(t + 1))) == T5_BUCKETS - 1
    far = np.full((t, t), T5_BUCKETS - 1, np.int32)
    bkt = np.concatenate([np.where(j <= i, _t5_bucket_np(i - j), T5_MASKED), _t5_bucket_np(t + i - j), far,
                          np.where(j > i, far, T5_MASKED)], axis=0).astype(np.int32)
    return _t5_gather(t5_bias, bkt, tr=t, mult=LOG2E)


def _tile_iota(t):
    return lax.broadcasted_iota(jnp.int32, (t, t), 0), lax.broadcasted_iota(jnp.int32, (t, t), 1)


def _rect_iota(i, tq, start, tk):
    rowg = i * tq + lax.broadcasted_iota(jnp.int32, (tq, tk), 0)
    colg = start + lax.broadcasted_iota(jnp.int32, (tq, tk), 1)
    return rowg, colg


def _osm_reset(m_scr, l_scr, acc_scr):
    m_scr[...] = jnp.full(m_scr.shape, M_INIT, F32)
    l_scr[...] = jnp.zeros(l_scr.shape, F32)
    acc_scr[...] = jnp.zeros(acc_scr.shape, F32)


def _osm_update(t2, rows, slot, p_scr, m_scr, l_scr, acc_scr):
    m_prev = m_scr[rows, :]
    m_new = jnp.maximum(m_prev, jnp.max(t2, axis=-1, keepdims=True))
    p = jnp.exp2(t2 - pltpu.repeat(m_new, t2.shape[1] // LANE, axis=1))
    alpha = jnp.exp2(m_prev - m_new)
    l_scr[rows, :] = alpha * l_scr[rows, :]
    m_scr[rows, :] = m_new
    p_scr[slot, rows, :] = p.astype(BF16)
    acc_scr[rows, :] = alpha * acc_scr[rows, :]


def _osm_accum(v, slot, p_scr, l_scr, acc_scr):
    vext = jnp.concatenate([v, jnp.ones(v.shape, v.dtype)], axis=-1)
    half = p_scr.shape[1] // 2
    for part in range(2):
        rows = pl.ds(part * half, half)
        pv = _dot(p_scr[slot, rows, :], vext)
        acc_scr[rows, :] += pv[:, :HEAD_DIM]
        l_scr[rows, :] += pv[:, HEAD_DIM:]


def _pipelined_tiles(n_tiles, scores, update, update_last=None):
    update_last = update if update_last is None else update_last
    last = n_tiles - 1
    n_pairs = last // 2
    scores(0, 0)

    def pair(n, cr):
        scores(2 * n + 1, 1)
        update(2 * n, 0)
        scores(2 * n + 2, 0)
        update(2 * n + 1, 1)
        return cr

    lax.fori_loop(0, n_pairs, pair, 0)

    @pl.when(last % 2 == 0)
    def _():
        update_last(last, 0)

    @pl.when(last % 2 == 1)
    def _():
        scores(last, 1)
        update(last - 1, 0)
        update_last(last, 1)


def _att_scratch(rows, tk, slots=2):
    return [pltpu.VMEM((slots, rows, tk), F32), pltpu.VMEM((slots, rows, tk), BF16), pltpu.VMEM((rows, LANE), F32),
            pltpu.VMEM((rows, LANE), F32), pltpu.VMEM((rows, HEAD_DIM), F32)]


def _sb_kernel(q_ref, k_ref, v_ref, o_ref, s_scr, c_scr, acc_scr, *, tq, tk):
    i = pl.program_id(2)
    q = q_ref[...]
    row, col = _tile_iota(tk)
    upper = jnp.where(row > col, 1.0, 0.0).astype(BF16)
    c_scr[...] = jnp.zeros(c_scr.shape, F32)
    acc_scr[...] = jnp.zeros(acc_scr.shape, F32)
    reps = tk // LANE

    def scores(kb, slot):
        start = pl.multiple_of(kb * tk, tk)
        s_scr[slot] = _dot_nt(q, k_ref[pl.ds(start, tk), :])

    def update(kb, slot, diag):
        start = pl.multiple_of(kb * tk, tk)
        nz = s_scr[slot]
        e = jnp.exp2(jnp.abs(nz) * (-LOG2E))
        lk = jnp.minimum(nz, 0.0) - jnp.log(1.0 + e)
        if diag:
            rowg, colg = _rect_iota(i, tq, start, tk)
            past = colg < rowg
            lk = jnp.where(past, lk, 0.0)
        hi, lo = _split_bf16(lk)
        c = c_scr[...]
        later = _dot(hi, upper) + _dot(lo, upper) + pltpu.repeat(c, reps, axis=1)
        a = jnp.exp(lk - nz + later)
        if diag:
            a = jnp.where(past, a, 0.0)
        acc_scr[...] += _dot(a.astype(BF16), v_ref[pl.ds(start, tk), :])
        c_scr[...] = c + jnp.sum(lk, axis=-1, keepdims=True)

    assert tq == 2 * tk
    scores(2 * i + 1, 1)
    scores(2 * i, 0)
    update(2 * i + 1, 1, True)
    scores(jnp.maximum(2 * i - 1, 0), 1)
    update(2 * i, 0, True)

    def more(carry):
        n, c_max = carry
        return (n < i) & (c_max > SB_DEAD)

    def pair(carry):
        n, _ = carry
        kb = 2 * (i - n) - 1
        scores(kb - 1, 0)
        update(kb, 1, False)
        scores(jnp.maximum(kb - 2, 0), 1)
        update(kb - 1, 0, False)
        return n + 1, jnp.max(c_scr[...])

    lax.while_loop(more, pair, (0, jnp.max(c_scr[...])))
    o_ref[...] = acc_scr[...].astype(o_ref.dtype)


def _sb_attention(qkv, *, batch, seq_len, tq=ATT_TQ, tk=ATT_TK):
    h = N_HEADS
    nq = seq_len // tq
    return pl.pallas_call(
        functools.partial(_sb_kernel, tq=tq, tk=tk),
        grid=(batch, h, nq),
        in_specs=[pl.BlockSpec((tq, HEAD_DIM), lambda b, hh, i: (b * nq + i, hh)),
                  pl.BlockSpec((seq_len, HEAD_DIM), lambda b, hh, i: (b, h + hh)),
                  pl.BlockSpec((seq_len, HEAD_DIM), lambda b, hh, i: (b, 2 * h + hh))],
        out_specs=pl.BlockSpec((tq, HEAD_DIM), lambda b, hh, i: (b * nq + i, hh)),
        out_shape=jax.ShapeDtypeStruct((batch * seq_len, h * HEAD_DIM), BF16),
        scratch_shapes=[pltpu.VMEM((2, tq, tk), F32), pltpu.VMEM((tq, LANE), F32), pltpu.VMEM((tq, HEAD_DIM), F32)],
        compiler_params=_params("parallel", "parallel", "arbitrary"),
        name="sb_attention",
    )(qkv, qkv, qkv)


def _table_offset(kb, i, t):
    return pl.multiple_of(jnp.where(kb == i, TAB_DIAG * t, jnp.where(kb == i - 1, TAB_SUB * t, TAB_FAR * t)), t)


def _diff_kernel(q_ref, k_ref, v_ref, tab_ref, lam_ref, hg_ref, o_ref, s_scr, p_scr, m_scr, l_scr, acc_scr, *, t,
                 lambda_init):
    i = pl.program_id(2)
    q = q_ref[...]
    lane = lax.broadcasted_iota(jnp.int32, q.shape, 1)
    zero = jnp.zeros_like(q)
    q2 = jnp.concatenate([jnp.where(lane < DIFF_DIM, q, zero), jnp.where(lane >= DIFF_DIM, q, zero)], axis=0)
    _osm_reset(m_scr, l_scr, acc_scr)

    def scores(kb, slot):
        s_scr[slot] = _dot_nt(q2, k_ref[pl.ds(pl.multiple_of(kb * t, t), t), :])

    def update(kb, slot):
        off = _table_offset(kb, i, t)
        for half in range(2):
            rows = pl.ds(half * t, t)
            t2 = s_scr[slot, rows, :] + tab_ref[0, pl.ds(off, t), :]
            _osm_update(t2, rows, slot, p_scr, m_scr, l_scr, acc_scr)
        _osm_accum(v_ref[pl.ds(pl.multiple_of(kb * t, t), t), :], slot, p_scr, l_scr, acc_scr)

    _pipelined_tiles(i + 1, scores, update)

    lam = lam_ref[...]
    lmbda = (jnp.exp(jnp.sum(lam[0:1] * lam[1:2], axis=-1, keepdims=True))
             - jnp.exp(jnp.sum(lam[2:3] * lam[3:4], axis=-1, keepdims=True)) + lambda_init)
    o = acc_scr[...] / l_scr[...]
    o = o[:t] - lmbda * o[t:]
    o_ref[...] = (_rms(o, hg_ref[...]) * (1.0 - lambda_init)).astype(o_ref.dtype)


def _diff_attention(qkv, tables, lam, head_g, *, batch, seq_len, lambda_init, t=DIFF_T):
    h = N_HEADS
    nq = seq_len // t
    return pl.pallas_call(
        functools.partial(_diff_kernel, t=t, lambda_init=lambda_init),
        grid=(batch, h, nq),
        in_specs=[pl.BlockSpec((t, HEAD_DIM), lambda b, hh, i: (b * nq + i, hh)),
                  pl.BlockSpec((seq_len, HEAD_DIM), lambda b, hh, i: (b, h + hh)),
                  pl.BlockSpec((seq_len, HEAD_DIM), lambda b, hh, i: (b, 2 * h + hh)),
                  pl.BlockSpec((1, 4 * t, t), lambda b, hh, i: (hh, 0, 0)),
                  pl.BlockSpec((4, DIFF_DIM), lambda b, hh, i: (0, 0)),
                  pl.BlockSpec((1, HEAD_DIM), lambda b, hh, i: (0, 0))],
        out_specs=pl.BlockSpec((t, HEAD_DIM), lambda b, hh, i: (b * nq + i, hh)),
        out_shape=jax.ShapeDtypeStruct((batch * seq_len, h * HEAD_DIM), BF16),
        scratch_shapes=_att_scratch(2 * t, t),
        compiler_params=_params("parallel", "parallel", "arbitrary"),
        name="diff_attention",
    )(qkv, qkv, qkv, tables, lam, head_g.reshape(1, HEAD_DIM))


def _mla_kernel(qn_ref, qr_ref, kn_ref, kr_ref, v_ref, o_ref, s_scr, p_scr, m_scr, l_scr, acc_scr, *, tq, tk):
    i = pl.program_id(2)
    q = jnp.concatenate([qn_ref[...], qr_ref[...]], axis=-1)
    _osm_reset(m_scr, l_scr, acc_scr)

    def scores(kb, slot):
        start = pl.multiple_of(kb * tk, tk)
        k = jnp.concatenate([kn_ref[pl.ds(start, tk), :], kr_ref[pl.ds(start, tk), :]], axis=-1)
        s_scr[slot] = _dot_nt(q, k)

    def update(kb, slot, mask):
        start = pl.multiple_of(kb * tk, tk)
        t2 = s_scr[slot]
        if mask:
            rowg, colg = _rect_iota(i, tq, start, tk)
            t2 = jnp.where(colg <= rowg, t2, MASKED)
        _osm_update(t2, pl.ds(0, tq), slot, p_scr, m_scr, l_scr, acc_scr)
        _osm_accum(v_ref[pl.ds(start, tk), :], slot, p_scr, l_scr, acc_scr)

    assert tq == tk
    _pipelined_tiles(i + 1, scores, lambda kb, slot: update(kb, slot, False), lambda kb, slot: update(kb, slot, True))
    o_ref[...] = (acc_scr[...] / l_scr[...]).astype(o_ref.dtype)


def _mla_attention(qn, qr, kv, kr, *, batch, seq_len, tq=MLA_T, tk=MLA_T):
    h = N_HEADS
    nq = seq_len // tq
    qspec = pl.BlockSpec((tq, HEAD_DIM), lambda b, hh, i: (b * nq + i, hh))
    return pl.pallas_call(
        functools.partial(_mla_kernel, tq=tq, tk=tk),
        grid=(batch, h, nq),
        in_specs=[qspec, qspec,
                  pl.BlockSpec((seq_len, HEAD_DIM), lambda b, hh, i: (b, 2 * hh)),
                  pl.BlockSpec((seq_len, HEAD_DIM), lambda b, hh, i: (b, 0)),
                  pl.BlockSpec((seq_len, HEAD_DIM), lambda b, hh, i: (b, 2 * hh + 1))],
        out_specs=qspec,
        out_shape=jax.ShapeDtypeStruct((batch * seq_len, h * HEAD_DIM), BF16),
        scratch_shapes=_att_scratch(tq, tk),
        compiler_params=_params("parallel", "parallel", "arbitrary"),
        name="mla_attention",
    )(qn, qr, kv, kr, kv)


def _compress_kernel(raw_ref, pe_ref, w1_ref, w2_ref, o_ref, *, n_slots):
    half = NSA_CMP_BLOCK // 2
    p1 = jnp.zeros((n_slots, HEAD_DIM), F32)
    p2 = jnp.zeros((n_slots, HEAD_DIM), F32)
    for l in range(half):
        a = raw_ref[pl.ds(l, n_slots, stride=NSA_CMP_STRIDE), :]
        p1 = p1 + _dot((a + pe_ref[0, l:l + 1, :]).astype(BF16), w1_ref[0, l])
        p2 = p2 + _dot((a + pe_ref[0, half + l:half + l + 1, :]).astype(BF16), w1_ref[0, half + l])
    pre = p1 + pltpu.roll(p2, n_slots - 1, 0)
    hid = pre * jax.nn.sigmoid(pre)
    o_ref[0, 0, 0] = _dot(hid.astype(BF16), w2_ref[0]).astype(o_ref.dtype)


def _nsa_compress(raw, pe, w1, w2, *, batch, seq_len):
    g = NSA_GROUPS
    n_slots = seq_len // NSA_CMP_STRIDE
    return pl.pallas_call(
        functools.partial(_compress_kernel, n_slots=n_slots),
        grid=(batch, 2, g),
        in_specs=[pl.BlockSpec((seq_len, HEAD_DIM), lambda b, kv, gg: (b, kv * g + gg)),
                  pl.BlockSpec((1, NSA_CMP_BLOCK, HEAD_DIM), lambda b, kv, gg: (kv, 0, 0)),
                  pl.BlockSpec((1, NSA_CMP_BLOCK, HEAD_DIM, HEAD_DIM), lambda b, kv, gg: (kv, 0, 0, 0)),
                  pl.BlockSpec((1, HEAD_DIM, HEAD_DIM), lambda b, kv, gg: (kv, 0, 0))],
        out_specs=pl.BlockSpec((1, 1, 1, n_slots, HEAD_DIM), lambda b, kv, gg: (b, kv, gg, 0, 0)),
        out_shape=jax.ShapeDtypeStruct((batch, 2, g, n_slots, HEAD_DIM), BF16),
        compiler_params=_params("parallel", "parallel", "parallel"),
        name="nsa_compress",
    )(raw, pe, w1, w2)


def _nsa_cmp_kernel(q_ref, kc_ref, vc_ref, bias_ref, gates_ref, ovt_ref, oc_ref, sel_ref, *, t, n_slots, n_sel,
                    n_top):
    i = pl.program_id(2)
    kc = kc_ref[0, 0, 0]
    vc = vc_ref[0, 0, 0]
    qpos = i * t + lax.broadcasted_iota(jnp.int32, (t, n_slots), 0)
    cmp_end = NSA_CMP_STRIDE * lax.broadcasted_iota(jnp.int32, (t, n_slots), 1) + (NSA_CMP_BLOCK - 1)
    valid = cmp_end <= qpos
    gates = gates_ref[...]
    psum = jnp.zeros((t, n_slots), F32)
    for r in range(NSA_REP):
        q = q_ref[:, r * HEAD_DIM:(r + 1) * HEAD_DIM]
        s = jnp.where(valid, _dot_nt(q, kc) + bias_ref[r], NEG)
        m = jnp.max(s, axis=-1, keepdims=True)
        p = jnp.where(valid, jnp.exp2(s - m), 0.0)
        p = p / jnp.maximum(jnp.sum(p, axis=-1, keepdims=True), 1e-30)
        psum = psum + p
        oc_ref[:, r * HEAD_DIM:(r + 1) * HEAD_DIM] = gates[:, r:r + 1] * _dot(p.astype(BF16), vc)

    hi, lo = _split_bf16(psum)
    ovt = ovt_ref[...]
    imp = _dot_nt(ovt, hi) + _dot_nt(ovt, lo)
    blk = lax.broadcasted_iota(jnp.int32, (n_sel, t), 0)
    tpos = i * t + lax.broadcasted_iota(jnp.int32, (n_sel, t), 1)
    cur = tpos // NSA_SEL_BLOCK
    forced = (blk == 0) | (blk == cur) | (blk == cur - 1)
    score = jnp.where(blk * NSA_SEL_BLOCK <= tpos, jnp.where(forced, FORCED_SCORE, imp), -1.0)
    rank = jnp.zeros((n_sel, t), F32)
    for mm in range(n_sel):
        sm = score[mm:mm + 1, :]
        ahead = (sm > score) | ((sm == score) & (blk > mm))
        rank = rank + jnp.where(ahead, 1.0, 0.0)
    sel_t = jnp.where(rank < n_top, 1.0, 0.0).astype(BF16)
    row, col = _tile_iota(t)
    eye = jnp.where(row == col, 1.0, 0.0).astype(BF16)
    sel_ref[0, 0] = _dot_nt(eye, sel_t).astype(sel_ref.dtype)


def _nsa_cmp_attention(q_all, kvc, bias_c, gates, *, batch, seq_len, t=ATT_T):
    g = NSA_GROUPS
    nq = seq_len // t
    n_slots = seq_len // NSA_CMP_STRIDE
    n_sel = seq_len // NSA_SEL_BLOCK
    c0 = NSA_CMP_STRIDE * np.arange(n_slots)[:, None]
    s0 = NSA_SEL_BLOCK * np.arange(n_sel)[None, :]
    overlap = (c0 < s0 + NSA_SEL_BLOCK) & (c0 + NSA_CMP_BLOCK > s0)
    ovt = jnp.asarray(overlap.T.astype(np.float32), dtype=BF16)
    gw = NSA_REP * HEAD_DIM
    return pl.pallas_call(
        functools.partial(_nsa_cmp_kernel, t=t, n_slots=n_slots, n_sel=n_sel, n_top=min(NSA_TOPN, n_sel)),
        grid=(batch, g, nq),
        in_specs=[pl.BlockSpec((t, gw), lambda b, gg, i: (b * nq + i, gg)),
                  pl.BlockSpec((1, 1, 1, n_slots, HEAD_DIM), lambda b, gg, i: (b, 0, gg, 0, 0)),
                  pl.BlockSpec((1, 1, 1, n_slots, HEAD_DIM), lambda b, gg, i: (b, 1, gg, 0, 0)),
                  pl.BlockSpec((NSA_REP, t, n_slots), lambda b, gg, i: (gg, i, 0)),
                  pl.BlockSpec((t, LANE), lambda b, gg, i: (b * nq + i, gg)),
                  pl.BlockSpec((n_sel, n_slots), lambda b, gg, i: (0, 0))],
        out_specs=[pl.BlockSpec((t, gw), lambda b, gg, i: (b * nq + i, gg)),
                   pl.BlockSpec((1, 1, t, n_sel), lambda b, gg, i: (b, gg, i, 0))],
        out_shape=[jax.ShapeDtypeStruct((batch * seq_len, g * gw), F32),
                   jax.ShapeDtypeStruct((batch, g, seq_len, n_sel), BF16)],
        compiler_params=_params("parallel", "parallel", "arbitrary"),
        name="nsa_cmp_attention",
    )(q_all, kvc, kvc, bias_c, gates, ovt)


def _nsa_main_kernel(q_ref, ks_ref, vs_ref, kw_ref, vw_ref, sel_ref, tab_ref, gates_ref, oc_ref, o_ref,
                     s_scr, p_scr, m_scr, l_scr, acc_scr, os_scr, mk_scr, *, t):
    i = pl.program_id(2)
    rep = NSA_REP
    q = jnp.concatenate([q_ref[:, r * HEAD_DIM:(r + 1) * HEAD_DIM] for r in range(rep)], axis=0)

    sel = sel_ref[0, 0]
    n_sel = sel.shape[1]
    blk_row = lax.broadcasted_iota(jnp.int32, (n_sel, t), 0)
    key_col = lax.broadcasted_iota(jnp.int32, (n_sel, t), 1)
    _osm_reset(m_scr, l_scr, acc_scr)

    def sel_scores(kb, slot):
        s_scr[slot] = _dot_nt(q, ks_ref[pl.ds(pl.multiple_of(kb * t, t), t), :])

    def sel_update(kb, slot):
        start = pl.multiple_of(kb * t, t)
        expand = jnp.where((start + key_col) // NSA_SEL_BLOCK == blk_row, 1.0, 0.0).astype(BF16)
        mk_scr[...] = _dot(sel, expand)
        off = _table_offset(kb, i, t)
        for r in range(rep):
            rows = pl.ds(r * t, t)
            t2 = s_scr[slot, rows, :] + tab_ref[r, pl.ds(off, t), :]
            t2 = jnp.where(mk_scr[...] > 0.5, t2, MASKED)
            _osm_update(t2, rows, slot, p_scr, m_scr, l_scr, acc_scr)
        _osm_accum(vs_ref[pl.ds(start, t), :], slot, p_scr, l_scr, acc_scr)

    _pipelined_tiles(i + 1, sel_scores, sel_update)
    os_scr[...] = acc_scr[...] / l_scr[...]

    n_back = NSA_WINDOW // t
    assert n_back == 2
    _osm_reset(m_scr, l_scr, acc_scr)

    def win_scores(kb, slot):
        s_scr[slot] = _dot_nt(q, kw_ref[pl.ds(pl.multiple_of(kb * t, t), t), :])

    def win_update(kb, slot, region):
        for r in range(rep):
            rows = pl.ds(r * t, t)
            t2 = s_scr[slot, rows, :] + tab_ref[r, pl.ds(region * t, t), :]
            _osm_update(t2, rows, slot, p_scr, m_scr, l_scr, acc_scr)
        _osm_accum(vw_ref[pl.ds(pl.multiple_of(kb * t, t), t), :], slot, p_scr, l_scr, acc_scr)

    def window(n_tiles):
        regions = (TAB_DIAG, TAB_SUB, TAB_EDGE)
        for back in range(n_tiles):
            win_scores(i - back, back)
        for back in reversed(range(n_tiles)):
            win_update(i - back, back, regions[back])

    pl.when(i >= 2)(lambda: window(3))
    pl.when(i == 1)(lambda: window(2))
    pl.when(i == 0)(lambda: window(1))
    o_w = acc_scr[...] / l_scr[...]
    o_s = os_scr[...]

    gates = gates_ref[...]
    for r in range(rep):
        rows = slice(r * t, (r + 1) * t)
        cols = slice(r * HEAD_DIM, (r + 1) * HEAD_DIM)
        o = oc_ref[:, cols] + gates[:, rep + r:rep + r + 1] * o_s[rows] + gates[:, 2 * rep + r:2 * rep + r + 1] * o_w[rows]
        o_ref[:, cols] = o.astype(o_ref.dtype)


def _nsa_main_attention(qkv, sel, tables, gates, oc, *, batch, seq_len, t=ATT_T):
    g = NSA_GROUPS
    nq = seq_len // t
    n_sel = seq_len // NSA_SEL_BLOCK
    gw = NSA_REP * HEAD_DIM
    qb = N_HEADS
    kv = lambda which: pl.BlockSpec((seq_len, HEAD_DIM), lambda b, gg, i: (b, qb + which * g + gg))
    tile = pl.BlockSpec((t, gw), lambda b, gg, i: (b * nq + i, gg))
    rows = NSA_REP * t
    return pl.pallas_call(
        functools.partial(_nsa_main_kernel, t=t),
        grid=(batch, g, nq),
        in_specs=[tile, kv(0), kv(1), kv(2), kv(3),
                  pl.BlockSpec((1, 1, t, n_sel), lambda b, gg, i: (b, gg, i, 0)),
                  pl.BlockSpec((NSA_REP, 4 * t, t), lambda b, gg, i: (gg, 0, 0)),
                  pl.BlockSpec((t, LANE), lambda b, gg, i: (b * nq + i, gg)),
                  tile],
        out_specs=tile,
        out_shape=jax.ShapeDtypeStruct((batch * seq_len, g * gw), BF16),
        scratch_shapes=_att_scratch(rows, t, slots=3) + [pltpu.VMEM((rows, HEAD_DIM), F32), pltpu.VMEM((t, t), F32)],
        compiler_params=_params("parallel", "parallel", "arbitrary"),
        name="nsa_main_attention",
    )(qkv, qkv, qkv, qkv, qkv, sel, tables, gates, oc)


def _rope_tables(seq_len, width):
    half = MLA_ROPE // 2
    inv = np.power(ROPE_THETA, -np.arange(half, dtype=np.float32) / half).astype(np.float32)
    ang = np.arange(seq_len, dtype=np.float32)[:, None] * inv[None, :]
    pad = np.zeros((seq_len, LANE - MLA_ROPE), np.float32)
    cos = np.concatenate([np.cos(ang), np.cos(ang), pad], axis=1)
    sin = np.concatenate([np.sin(ang), np.sin(ang), pad], axis=1)
    reps = width // LANE
    return jnp.asarray(np.tile(cos, (1, reps))), jnp.asarray(np.tile(sin, (1, reps)))


def _rope_weights(w):
    k, n, _ = w.shape
    half = MLA_ROPE // 2
    pad = jnp.zeros((k, n, LANE - MLA_ROPE), w.dtype)
    wa = jnp.concatenate([w, pad], axis=-1)
    wb = jnp.concatenate([-w[..., half:], w[..., :half], pad], axis=-1)
    return wa.reshape(k, n * LANE).astype(BF16), wb.reshape(k, n * LANE).astype(BF16)


def kernel(x, c, t5_bias, ada_w, ada_b, norm_g, final_g, ffn_w_up, ffn_conv_w, ffn_conv_b, ffn_w_down, sb_w_in, sb_w_out, nsa_w_in, nsa_cmp_pe, nsa_cmp_w1, nsa_cmp_w2, nsa_w_out, diff_w_in, diff_lambda, diff_head_g, diff_w_out, mla_w_in, mla_q_g, mla_w_qb, mla_kv_g, mla_w_kvb, mla_w_out):
    batch, seq_len, d = x.shape
    depth = ada_w.shape[0]
    h, dh, g = N_HEADS, HEAD_DIM, NSA_GROUPS
    sizes = dict(batch=batch, seq_len=seq_len)

    mod = _ada_mod(c, ada_w, ada_b)
    tables = _attention_tables(t5_bias, ATT_T)

    xf = x.reshape(batch * seq_len, d)
    for i in range(depth):
        mixer, j = i % 4, i // 4
        sh1, sc1, gt1, sh2, sc2, gt2 = (mod[i, :, n * d:(n + 1) * d].reshape(batch, 1, d) for n in range(6))
        nm = functools.partial(_norm_mm, xf, norm_g[i, 0], seq_len=seq_len, shift=sh1, scale=sc1)
        if mixer == 0:
            w_in = sb_w_in[j]
            w_in = jnp.concatenate([w_in[:, :h * dh] * SB_QSCALE, w_in[:, h * dh:]], axis=1)
            qkv = nm(w_in.astype(BF16), name="sb_in")
            o = _sb_attention(qkv, **sizes)
            w_out = sb_w_out[j]
        elif mixer == 1:
            w_in = nsa_w_in[j]
            n_q, n_kv = h * dh, g * dh
            w_att = jnp.concatenate([w_in[:, :n_q] * NSA_QSCALE, w_in[:, n_q + 2 * n_kv:n_q + 6 * n_kv]], axis=1)
            w_cmp = w_in[:, n_q:n_q + 2 * n_kv]
            w_g = w_in[:, n_q + 6 * n_kv:].reshape(d, 3, g, NSA_REP).transpose(0, 2, 1, 3).reshape(d, g, 3 * NSA_REP)
            w_g = jnp.pad(w_g, ((0, 0), (0, 0), (0, LANE - 3 * NSA_REP))).reshape(d, g * LANE)
            qkv = nm(w_att.astype(BF16), name="nsa_in")
            raw = nm(w_cmp.astype(BF16), out_dtype=F32, name="nsa_in_cmp")
            gates = nm(w_g.astype(BF16), out_dtype=F32, act="sigmoid", name="nsa_in_gates")
            kvc = _nsa_compress(raw, nsa_cmp_pe[j], nsa_cmp_w1[j].reshape(2, NSA_CMP_BLOCK, dh, dh).astype(BF16),
                                nsa_cmp_w2[j].astype(BF16), **sizes)
            n_slots = seq_len // NSA_CMP_STRIDE
            dist_c = np.arange(seq_len)[:, None] - (NSA_CMP_STRIDE * np.arange(n_slots)[None, :] + NSA_CMP_BLOCK - 1)
            bias_c = _t5_gather(t5_bias, _t5_bucket_np(dist_c), tr=min(seq_len, 512), mult=LOG2E)
            oc, sel = _nsa_cmp_attention(qkv, kvc, bias_c, gates, **sizes)
            o = _nsa_main_attention(qkv, sel, tables, gates, oc, **sizes)
            w_out = nsa_w_out[j]
        elif mixer == 2:
            lambda_init = 0.8 - 0.6 * math.exp(-0.3 * i)
            w_in = diff_w_in[j]
            w_in = jnp.concatenate([w_in[:, :h * dh] * DIFF_QSCALE, w_in[:, h * dh:]], axis=1)
            qkv = nm(w_in.astype(BF16), name="diff_in")
            o = _diff_attention(qkv, _attention_tables(t5_bias, DIFF_T), diff_lambda[j], diff_head_g[j],
                                lambda_init=lambda_init, **sizes)
            w_out = diff_w_out[j]
        else:
            w_in = mla_w_in[j]
            nq_l, nkv_l = MLA_Q_LORA, MLA_KV_LORA
            w_lat = jnp.concatenate([w_in[:, :nq_l], jnp.zeros((d, 2 * nkv_l - nq_l), w_in.dtype),
                                     w_in[:, nq_l:nq_l + nkv_l]], axis=1)
            lat = nm(w_lat.astype(BF16), out_dtype=F32, name="mla_in")
            cos1, sin1 = _rope_tables(seq_len, LANE)
            wa, wb = _rope_weights(w_in[:, nq_l + nkv_l:].reshape(d, 1, MLA_ROPE))
            kr = nm(wa, wb=wb, cos=cos1, sin=sin1, name="mla_in_rope")
            w_qb = (mla_w_qb[j] * MLA_QSCALE).reshape(nq_l, h, MLA_NOPE + MLA_ROPE)
            qn = _norm_mm(lat, mla_q_g[j], w_qb[:, :, :MLA_NOPE].reshape(nq_l, h * MLA_NOPE).astype(BF16),
                          seq_len=seq_len, x_cols=nq_l, x_col_block=0, name="mla_q_nope")
            cosh, sinh = _rope_tables(seq_len, h * LANE)
            wa, wb = _rope_weights(w_qb[:, :, MLA_NOPE:])
            qr = _norm_mm(lat, mla_q_g[j], wa, wb=wb, cos=cosh, sin=sinh, seq_len=seq_len, x_cols=nq_l,
                          x_col_block=0, name="mla_q_rope")
            kv = _norm_mm(lat, mla_kv_g[j], mla_w_kvb[j].astype(BF16), seq_len=seq_len, x_cols=nkv_l,
                          x_col_block=2, name="mla_kv")
            o = _mla_attention(qn, qr, kv, kr, **sizes)
            w_out = mla_w_out[j]
        xf = _mm_residual(o, w_out.astype(BF16), xf, gt1, seq_len=seq_len)
        xf = _conv_ffn(xf, norm_g[i, 1], sh2, sc2, ffn_w_up[i].astype(BF16), ffn_conv_w[i], ffn_conv_b[i],
                       ffn_w_down[i].astype(BF16), gt2, final_g, seq_len=seq_len, final_norm=(i == depth - 1))
    return xf.reshape(batch, seq_len, d)
```

```python
import functools
import math

import numpy as np
import jax
import jax.numpy as jnp
from jax import lax
from jax.experimental import pallas as pl
from jax.experimental.pallas import tpu as pltpu

F32 = jnp.float32
BF16 = jnp.bfloat16
EPS = 1e-6
NEG = -1e30

LANE = 128
HALO = 16
VMEM_LIMIT = 56 * 2**20

T5_BUCKETS = 32
T5_MAX_DIST = 128
N_HEADS = 16
HEAD_DIM = 128
NSA_GROUPS = 4
NSA_REP = 4
NSA_CMP_BLOCK = 32
NSA_CMP_STRIDE = 16
NSA_SEL_BLOCK = 64
NSA_TOPN = 16
NSA_WINDOW = 512
FORCED_SCORE = 1e9
DIFF_DIM = 64
MLA_Q_LORA = 768
MLA_KV_LORA = 512
MLA_NOPE = 128
MLA_ROPE = 64
ROPE_THETA = 10000.0
CONV_WIDTH = 3
ATT_T = 256
DIFF_T = 512
MLA_T = 512
ATT_TQ = 512
ATT_TK = 256
LOG2E = 1.4426950408889634
SB_QSCALE = -(HEAD_DIM ** -0.5)
NSA_QSCALE = HEAD_DIM ** -0.5 * LOG2E
DIFF_QSCALE = DIFF_DIM ** -0.5 * LOG2E
MLA_QSCALE = (MLA_NOPE + MLA_ROPE) ** -0.5 * LOG2E
M_INIT = -1e30
MASKED = -2e30
SB_DEAD = -120.0
T5_MASKED = T5_BUCKETS


def _params(*sem):
    return pltpu.CompilerParams(dimension_semantics=sem, vmem_limit_bytes=VMEM_LIMIT)


def _dot(a, b):
    return jnp.dot(a, b, preferred_element_type=F32)


def _dot_nt(a, b):
    return lax.dot_general(a, b, (((1,), (1,)), ((), ())), preferred_element_type=F32)


def _split_bf16(x):
    hi = x.astype(BF16)
    lo = (x - hi.astype(F32)).astype(BF16)
    return hi, lo


def _rms(x, g):
    ms = jnp.mean(x * x, axis=-1, keepdims=True)
    return x * lax.rsqrt(ms + EPS) * g


def _ada_kernel(c_ref, w_ref, b_ref, o_ref):
    c = c_ref[...]
    a = c * jax.nn.sigmoid(c)
    a_hi, a_lo = _split_bf16(a)
    w_hi, w_lo = _split_bf16(w_ref[0])
    o_ref[0] = _dot(a_hi, w_hi) + _dot(a_lo, w_hi) + _dot(a_hi, w_lo) + b_ref[0]


def _ada_mod(c, ada_w, ada_b, tn=1024):
    depth, d, n = ada_w.shape
    b = c.shape[0]
    return pl.pallas_call(
        _ada_kernel,
        grid=(depth, n // tn),
        in_specs=[pl.BlockSpec((b, d), lambda l, j: (0, 0)),
                  pl.BlockSpec((1, d, tn), lambda l, j: (l, 0, j)),
                  pl.BlockSpec((1, 1, tn), lambda l, j: (l, 0, j))],
        out_specs=pl.BlockSpec((1, b, tn), lambda l, j: (l, 0, j)),
        out_shape=jax.ShapeDtypeStruct((depth, b, n), F32),
        compiler_params=_params("parallel", "parallel"),
        name="ada_mod",
    )(c, ada_w, ada_b.reshape(depth, 1, n))


def _norm_mm_kernel(*refs, modulated, rope, act, normed):
    it = iter(refs)
    x_ref = next(it)
    g_ref = sh_ref = sc_ref = wb_ref = cos_ref = sin_ref = None
    if not normed:
        g_ref = next(it)
        if modulated:
            sh_ref, sc_ref = next(it), next(it)
    w_ref = next(it)
    if rope:
        wb_ref, cos_ref, sin_ref = next(it), next(it), next(it)
    o_ref = next(it)
    if normed:
        h = x_ref[...]
    else:
        h_ref = next(it)

        @pl.when(pl.program_id(1) == 0)
        def _():
            y = _rms(x_ref[...], g_ref[...])
            if modulated:
                y = y * (1.0 + sc_ref[0]) + sh_ref[0]
            h_ref[...] = y.astype(BF16)

        h = h_ref[...]
    acc = _dot(h, w_ref[...])
    if rope:
        acc = acc * cos_ref[...] + _dot(h, wb_ref[...]) * sin_ref[...]
    if act == "sigmoid":
        acc = jax.nn.sigmoid(acc)
    o_ref[...] = acc.astype(o_ref.dtype)


def _norm_mm(x, g, w, *, seq_len, shift=None, scale=None, wb=None, cos=None, sin=None, act=None, normed=False,
             emit_h=False, out_dtype=BF16, x_cols=None, x_col_block=0, tm=1024, tn=512, name="norm_mm"):
    m = x.shape[0]
    k = x.shape[1] if x_cols is None else x_cols
    n = w.shape[1]
    tn = min(tn, n)
    modulated, rope = shift is not None, wb is not None
    tps = seq_len // tm
    in_specs = [pl.BlockSpec((tm, k), lambda i, j: (i, x_col_block))]
    args = [x]
    if not normed:
        in_specs.append(pl.BlockSpec((1, k), lambda i, j: (0, 0)))
        args.append(g.reshape(1, k))
        if modulated:
            in_specs += [pl.BlockSpec((1, 1, k), lambda i, j: (i // tps, 0, 0))] * 2
            args += [shift, scale]
    in_specs.append(pl.BlockSpec((k, tn), lambda i, j: (0, j)))
    args.append(w)
    if rope:
        in_specs.append(pl.BlockSpec((k, tn), lambda i, j: (0, j)))
        in_specs += [pl.BlockSpec((tm, tn), lambda i, j: (i % tps, j))] * 2
        args += [wb, cos, sin]
    out_specs = [pl.BlockSpec((tm, tn), lambda i, j: (i, j))]
    out_shape = [jax.ShapeDtypeStruct((m, n), out_dtype)]
    scratch = []
    if emit_h:
        out_specs.append(pl.BlockSpec((tm, k), lambda i, j: (i, 0)))
        out_shape.append(jax.ShapeDtypeStruct((m, k), BF16))
    elif not normed:
        scratch.append(pltpu.VMEM((tm, k), BF16))
    out = pl.pallas_call(
        functools.partial(_norm_mm_kernel, modulated=modulated, rope=rope, act=act, normed=normed),
        grid=(m // tm, n // tn),
        in_specs=in_specs,
        out_specs=out_specs,
        out_shape=out_shape,
        scratch_shapes=scratch,
        compiler_params=_params("parallel", "arbitrary"),
        name=name,
    )(*args)
    return out if emit_h else out[0]


def _mm_res_kernel(a_ref, w_ref, x_ref, gate_ref, o_ref):
    o_ref[...] = x_ref[...] + gate_ref[0] * _dot(a_ref[...], w_ref[...])


def _mm_residual(a, w, x, gate, *, seq_len, tm=1024, tn=512, name="mm_residual"):
    m, k = a.shape
    n = w.shape[1]
    tps = seq_len // tm
    return pl.pallas_call(
        _mm_res_kernel,
        grid=(m // tm, n // tn),
        in_specs=[pl.BlockSpec((tm, k), lambda i, j: (i, 0)),
                  pl.BlockSpec((k, tn), lambda i, j: (0, j)),
                  pl.BlockSpec((tm, tn), lambda i, j: (i, j)),
                  pl.BlockSpec((1, 1, tn), lambda i, j: (i // tps, 0, j))],
        out_specs=pl.BlockSpec((tm, tn), lambda i, j: (i, j)),
        out_shape=jax.ShapeDtypeStruct((m, n), F32),
        compiler_params=_params("parallel", "arbitrary"),
        name=name,
    )(a, w, x, gate)


def _ffn_kernel(x_ref, xp_ref, g_ref, sh_ref, sc_ref, wg_ref, wu_ref, cwg_ref, cwu_ref, cbg_ref, cbu_ref,
                wd_ref, gate_ref, fg_ref, o_ref, h_ref, acc_ref, ug_scr, uu_scr, *, tiles_per_seq, final_norm):
    i, j = pl.program_id(0), pl.program_id(1)

    @pl.when(j == 0)
    def _():
        def nm(x):
            return _rms(x, g_ref[...]) * (1.0 + sc_ref[0]) + sh_ref[0]
        h_ref[HALO:, :] = nm(x_ref[...]).astype(BF16)
        keep = jnp.where(i % tiles_per_seq == 0, 0.0, 1.0)
        h_ref[:HALO, :] = (nm(xp_ref[...]) * keep).astype(BF16)
        acc_ref[...] = jnp.zeros_like(acc_ref)

    h = h_ref[...]

    def branch(w_ref, cw_ref, cb_ref, u_scr):
        u_scr[...] = _dot(h, w_ref[...])
        tm = u_scr.shape[0] - HALO
        cw = cw_ref[...]
        return (cw[0:1] * u_scr[pl.ds(HALO - 2, tm), :] + cw[1:2] * u_scr[pl.ds(HALO - 1, tm), :]
                + cw[2:3] * u_scr[pl.ds(HALO, tm), :] + cb_ref[...])

    gt = branch(wg_ref, cwg_ref, cbg_ref, ug_scr)
    up = branch(wu_ref, cwu_ref, cbu_ref, uu_scr)
    a = gt * jax.nn.sigmoid(gt) * up
    acc_ref[...] += _dot(a.astype(BF16), wd_ref[...])

    @pl.when(j == pl.num_programs(1) - 1)
    def _():
        y = x_ref[...] + gate_ref[0] * acc_ref[...]
        if final_norm:
            y = _rms(y, fg_ref[...])
        o_ref[...] = y


def _conv_ffn(x, g, shift, scale, w_up, conv_w, conv_b, w_down, gate, final_g, *, seq_len, final_norm,
              tm=512, tf=512):
    m, d = x.shape
    f = w_down.shape[0]
    nf = f // tf
    tps = seq_len // tm
    hb = tm // HALO
    row = lambda i, j: (i, 0)
    per_batch = lambda i, j: (i // tps, 0, 0)
    return pl.pallas_call(
        functools.partial(_ffn_kernel, tiles_per_seq=tps, final_norm=final_norm),
        grid=(m // tm, nf),
        in_specs=[pl.BlockSpec((tm, d), row),
                  pl.BlockSpec((HALO, d), lambda i, j: (jnp.maximum(i * hb - 1, 0), 0)),
                  pl.BlockSpec((1, d), lambda i, j: (0, 0)),
                  pl.BlockSpec((1, 1, d), per_batch),
                  pl.BlockSpec((1, 1, d), per_batch),
                  pl.BlockSpec((d, tf), lambda i, j: (0, j)),
                  pl.BlockSpec((d, tf), lambda i, j: (0, j + nf)),
                  pl.BlockSpec((CONV_WIDTH, tf), lambda i, j: (0, j)),
                  pl.BlockSpec((CONV_WIDTH, tf), lambda i, j: (0, j + nf)),
                  pl.BlockSpec((1, tf), lambda i, j: (0, j)),
                  pl.BlockSpec((1, tf), lambda i, j: (0, j + nf)),
                  pl.BlockSpec((tf, d), lambda i, j: (j, 0)),
                  pl.BlockSpec((1, 1, d), per_batch),
                  pl.BlockSpec((1, d), lambda i, j: (0, 0))],
        out_specs=pl.BlockSpec((tm, d), row),
        out_shape=jax.ShapeDtypeStruct((m, d), F32),
        scratch_shapes=[pltpu.VMEM((HALO + tm, d), BF16), pltpu.VMEM((tm, d), F32),
                        pltpu.VMEM((HALO + tm, tf), F32), pltpu.VMEM((HALO + tm, tf), F32)],
        compiler_params=_params("parallel", "arbitrary"),
        name="conv_ffn",
    )(x, x, g.reshape(1, d), shift, scale, w_up, w_up, conv_w, conv_w, conv_b.reshape(1, -1),
      conv_b.reshape(1, -1), w_down, gate, final_g.reshape(1, d))


def _t5_bucket_np(dist):
    n = np.maximum(dist, 0)
    max_exact = T5_BUCKETS // 2
    nf = np.maximum(n, 1).astype(np.float64)
    val = np.log(nf / max_exact) / math.log(T5_MAX_DIST / max_exact) * (T5_BUCKETS - max_exact)
    large = max_exact + np.trunc(val + 1e-6).astype(np.int64)
    return np.where(n < max_exact, n, np.minimum(large, T5_BUCKETS - 1)).astype(np.int32)


def _t5_gather_kernel(t5_ref, bkt_ref, o_ref, *, mult):
    h = pl.program_id(0)
    bk = bkt_ref[...]
    acc = jnp.zeros(bk.shape, F32)
    for b in range(T5_BUCKETS):
        acc = jnp.where(bk == b, t5_ref[b, h], acc)
    o_ref[0] = jnp.where(bk == T5_MASKED, MASKED, acc * mult)


def _t5_gather(t5_bias, bkt, tr, mult=1.0):
    rows, cols = bkt.shape
    heads = t5_bias.shape[1]
    return pl.pallas_call(
        functools.partial(_t5_gather_kernel, mult=mult),
        grid=(heads, rows // tr),
        in_specs=[pl.BlockSpec(memory_space=pltpu.SMEM),
                  pl.BlockSpec((tr, cols), lambda h, r: (r, 0))],
        out_specs=pl.BlockSpec((1, tr, cols), lambda h, r: (h, r, 0)),
        out_shape=jax.ShapeDtypeStruct((heads, rows, cols), F32),
        compiler_params=_params("parallel", "parallel"),
        name="t5_gather",
    )(t5_bias, jnp.asarray(bkt))


TAB_DIAG, TAB_SUB, TAB_FAR, TAB_EDGE = 0, 1, 2, 3


def _attention_tables(t5_bias, t):
    i = np.arange(t)[:, None]
    j = np.arange(t)[None, :]
    assert int(_t5_bucket_np(np.array(t + 1))) == T5_BUCKETS - 1
    far = np.full((t, t), T5_BUCKETS - 1, np.int32)
    bkt = np.concatenate([np.where(j <= i, _t5_bucket_np(i - j), T5_MASKED), _t5_bucket_np(t + i - j), far,
                          np.where(j > i, far, T5_MASKED)], axis=0).astype(np.int32)
    return _t5_gather(t5_bias, bkt, tr=t, mult=LOG2E)


def _tile_iota(t):
    return lax.broadcasted_iota(jnp.int32, (t, t), 0), lax.broadcasted_iota(jnp.int32, (t, t), 1)


def _rect_iota(i, tq, start, tk):
    rowg = i * tq + lax.broadcasted_iota(jnp.int32, (tq, tk), 0)
    colg = start + lax.broadcasted_iota(jnp.int32, (tq, tk), 1)
    return rowg, colg


def _osm_reset(m_scr, l_scr, acc_scr):
    m_scr[...] = jnp.full(m_scr.shape, M_INIT, F32)
    l_scr[...] = jnp.zeros(l_scr.shape, F32)
    acc_scr[...] = jnp.zeros(acc_scr.shape, F32)


def _osm_update(t2, rows, slot, p_scr, m_scr, l_scr, acc_scr):
    m_prev = m_scr[rows, :]
    m_new = jnp.maximum(m_prev, jnp.max(t2, axis=-1, keepdims=True))
    p = jnp.exp2(t2 - pltpu.repeat(m_new, t2.shape[1] // LANE, axis=1))
    alpha = jnp.exp2(m_prev - m_new)
    l_scr[rows, :] = alpha * l_scr[rows, :]
    m_scr[rows, :] = m_new
    p_scr[slot, rows, :] = p.astype(BF16)
    acc_scr[rows, :] = alpha * acc_scr[rows, :]


def _osm_accum(v, slot, p_scr, l_scr, acc_scr):
    vext = jnp.concatenate([v, jnp.ones(v.shape, v.dtype)], axis=-1)
    half = p_scr.shape[1] // 2
    for part in range(2):
        rows = pl.ds(part * half, half)
        pv = _dot(p_scr[slot, rows, :], vext)
        acc_scr[rows, :] += pv[:, :HEAD_DIM]
        l_scr[rows, :] += pv[:, HEAD_DIM:]


def _pipelined_tiles(n_tiles, scores, update, update_last=None):
    update_last = update if update_last is None else update_last
    last = n_tiles - 1
    n_pairs = last // 2
    scores(0, 0)

    def pair(n, cr):
        scores(2 * n + 1, 1)
        update(2 * n, 0)
        scores(2 * n + 2, 0)
        update(2 * n + 1, 1)
        return cr

    lax.fori_loop(0, n_pairs, pair, 0)

    @pl.when(last % 2 == 0)
    def _():
        update_last(last, 0)

    @pl.when(last % 2 == 1)
    def _():
        scores(last, 1)
        update(last - 1, 0)
        update_last(last, 1)


def _att_scratch(rows, tk, slots=2):
    return [pltpu.VMEM((slots, rows, tk), F32), pltpu.VMEM((slots, rows, tk), BF16), pltpu.VMEM((rows, LANE), F32),
            pltpu.VMEM((rows, LANE), F32), pltpu.VMEM((rows, HEAD_DIM), F32)]


def _sb_kernel(q_ref, k_ref, v_ref, o_ref, s_scr, c_scr, acc_scr, *, tq, tk):
    i = pl.program_id(2)
    q = q_ref[...]
    row, col = _tile_iota(tk)
    upper = jnp.where(row > col, 1.0, 0.0).astype(BF16)
    c_scr[...] = jnp.zeros(c_scr.shape, F32)
    acc_scr[...] = jnp.zeros(acc_scr.shape, F32)
    reps = tk // LANE

    def scores(kb, slot):
        start = pl.multiple_of(kb * tk, tk)
        s_scr[slot] = _dot_nt(q, k_ref[pl.ds(start, tk), :])

    def update(kb, slot, diag):
        start = pl.multiple_of(kb * tk, tk)
        nz = s_scr[slot]
        e = jnp.exp2(jnp.abs(nz) * (-LOG2E))
        lk = jnp.minimum(nz, 0.0) - jnp.log(1.0 + e)
        if diag:
            rowg, colg = _rect_iota(i, tq, start, tk)
            past = colg < rowg
            lk = jnp.where(past, lk, 0.0)
        hi, lo = _split_bf16(lk)
        c = c_scr[...]
        later = _dot(hi, upper) + _dot(lo, upper) + pltpu.repeat(c, reps, axis=1)
        a = jnp.exp(lk - nz + later)
        if diag:
            a = jnp.where(past, a, 0.0)
        acc_scr[...] += _dot(a.astype(BF16), v_ref[pl.ds(start, tk), :])
        c_scr[...] = c + jnp.sum(lk, axis=-1, keepdims=True)

    assert tq == 2 * tk
    scores(2 * i + 1, 1)
    scores(2 * i, 0)
    update(2 * i + 1, 1, True)
    scores(jnp.maximum(2 * i - 1, 0), 1)
    update(2 * i, 0, True)

    def more(carry):
        n, c_max = carry
        return (n < i) & (c_max > SB_DEAD)

    def pair(carry):
        n, _ = carry
        kb = 2 * (i - n) - 1
        scores(kb - 1, 0)
        update(kb, 1, False)
        c_mid = jnp.max(c_scr[...])

        @pl.when(c_mid > SB_DEAD)
        def _():
            scores(jnp.maximum(kb - 2, 0), 1)
            update(kb - 1, 0, False)

        return n + 1, jnp.where(c_mid > SB_DEAD, jnp.max(c_scr[...]), c_mid)

    lax.while_loop(more, pair, (0, jnp.max(c_scr[...])))
    o_ref[...] = acc_scr[...].astype(o_ref.dtype)


def _sb_attention(qkv, *, batch, seq_len, tq=ATT_TQ, tk=ATT_TK):
    h = N_HEADS
    nq = seq_len // tq
    return pl.pallas_call(
        functools.partial(_sb_kernel, tq=tq, tk=tk),
        grid=(batch, h, nq),
        in_specs=[pl.BlockSpec((tq, HEAD_DIM), lambda b, hh, i: (b * nq + i, hh)),
                  pl.BlockSpec((seq_len, HEAD_DIM), lambda b, hh, i: (b, h + hh)),
                  pl.BlockSpec((seq_len, HEAD_DIM), lambda b, hh, i: (b, 2 * h + hh))],
        out_specs=pl.BlockSpec((tq, HEAD_DIM), lambda b, hh, i: (b * nq + i, hh)),
        out_shape=jax.ShapeDtypeStruct((batch * seq_len, h * HEAD_DIM), BF16),
        scratch_shapes=[pltpu.VMEM((2, tq, tk), F32), pltpu.VMEM((tq, LANE), F32), pltpu.VMEM((tq, HEAD_DIM), F32)],
        compiler_params=_params("parallel", "parallel", "arbitrary"),
        name="sb_attention",
    )(qkv, qkv, qkv)


def _table_offset(kb, i, t):
    return pl.multiple_of(jnp.where(kb == i, TAB_DIAG * t, jnp.where(kb == i - 1, TAB_SUB * t, TAB_FAR * t)), t)


def _diff_kernel(q_ref, k_ref, v_ref, tab_ref, lam_ref, hg_ref, o_ref, s_scr, p_scr, m_scr, l_scr, acc_scr, *, t,
                 lambda_init):
    i = pl.program_id(2)
    q = q_ref[...]
    lane = lax.broadcasted_iota(jnp.int32, q.shape, 1)
    zero = jnp.zeros_like(q)
    q2 = jnp.concatenate([jnp.where(lane < DIFF_DIM, q, zero), jnp.where(lane >= DIFF_DIM, q, zero)], axis=0)
    _osm_reset(m_scr, l_scr, acc_scr)

    def scores(kb, slot):
        s_scr[slot] = _dot_nt(q2, k_ref[pl.ds(pl.multiple_of(kb * t, t), t), :])

    def update(kb, slot):
        off = _table_offset(kb, i, t)
        for half in range(2):
            rows = pl.ds(half * t, t)
            t2 = s_scr[slot, rows, :] + tab_ref[0, pl.ds(off, t), :]
            _osm_update(t2, rows, slot, p_scr, m_scr, l_scr, acc_scr)
        _osm_accum(v_ref[pl.ds(pl.multiple_of(kb * t, t), t), :], slot, p_scr, l_scr, acc_scr)

    _pipelined_tiles(i + 1, scores, update)

    lam = lam_ref[...]
    lmbda = (jnp.exp(jnp.sum(lam[0:1] * lam[1:2], axis=-1, keepdims=True))
             - jnp.exp(jnp.sum(lam[2:3] * lam[3:4], axis=-1, keepdims=True)) + lambda_init)
    o = acc_scr[...] / l_scr[...]
    o = o[:t] - lmbda * o[t:]
    o_ref[...] = (_rms(o, hg_ref[...]) * (1.0 - lambda_init)).astype(o_ref.dtype)


def _diff_attention(qkv, tables, lam, head_g, *, batch, seq_len, lambda_init, t=DIFF_T):
    h = N_HEADS
    nq = seq_len // t
    return pl.pallas_call(
        functools.partial(_diff_kernel, t=t, lambda_init=lambda_init),
        grid=(batch, h, nq),
        in_specs=[pl.BlockSpec((t, HEAD_DIM), lambda b, hh, i: (b * nq + i, hh)),
                  pl.BlockSpec((seq_len, HEAD_DIM), lambda b, hh, i: (b, h + hh)),
                  pl.BlockSpec((seq_len, HEAD_DIM), lambda b, hh, i: (b, 2 * h + hh)),
                  pl.BlockSpec((1, 4 * t, t), lambda b, hh, i: (hh, 0, 0)),
                  pl.BlockSpec((4, DIFF_DIM), lambda b, hh, i: (0, 0)),
                  pl.BlockSpec((1, HEAD_DIM), lambda b, hh, i: (0, 0))],
        out_specs=pl.BlockSpec((t, HEAD_DIM), lambda b, hh, i: (b * nq + i, hh)),
        out_shape=jax.ShapeDtypeStruct((batch * seq_len, h * HEAD_DIM), BF16),
        scratch_shapes=_att_scratch(2 * t, t),
        compiler_params=_params("parallel", "parallel", "arbitrary"),
        name="diff_attention",
    )(qkv, qkv, qkv, tables, lam, head_g.reshape(1, HEAD_DIM))


def _mla_kernel(qn_ref, qr_ref, kn_ref, kr_ref, v_ref, o_ref, s_scr, p_scr, m_scr, l_scr, acc_scr, *, tq, tk):
    i = pl.program_id(2)
    q = jnp.concatenate([qn_ref[...], qr_ref[...]], axis=-1)
    _osm_reset(m_scr, l_scr, acc_scr)

    def scores(kb, slot):
        start = pl.multiple_of(kb * tk, tk)
        k = jnp.concatenate([kn_ref[pl.ds(start, tk), :], kr_ref[pl.ds(start, tk), :]], axis=-1)
        s_scr[slot] = _dot_nt(q, k)

    def update(kb, slot, mask):
        start = pl.multiple_of(kb * tk, tk)
        t2 = s_scr[slot]
        if mask:
            rowg, colg = _rect_iota(i, tq, start, tk)
            t2 = jnp.where(colg <= rowg, t2, MASKED)
        _osm_update(t2, pl.ds(0, tq), slot, p_scr, m_scr, l_scr, acc_scr)
        _osm_accum(v_ref[pl.ds(start, tk), :], slot, p_scr, l_scr, acc_scr)

    assert tq == tk
    _pipelined_tiles(i + 1, scores, lambda kb, slot: update(kb, slot, False), lambda kb, slot: update(kb, slot, True))
    o_ref[...] = (acc_scr[...] / l_scr[...]).astype(o_ref.dtype)


def _mla_attention(qn, qr, kv, kr, *, batch, seq_len, tq=MLA_T, tk=MLA_T):
    h = N_HEADS
    nq = seq_len // tq
    qspec = pl.BlockSpec((tq, HEAD_DIM), lambda b, hh, i: (b * nq + i, hh))
    return pl.pallas_call(
        functools.partial(_mla_kernel, tq=tq, tk=tk),
        grid=(batch, h, nq),
        in_specs=[qspec, qspec,
                  pl.BlockSpec((seq_len, HEAD_DIM), lambda b, hh, i: (b, 2 * hh)),
                  pl.BlockSpec((seq_len, HEAD_DIM), lambda b, hh, i: (b, 0)),
                  pl.BlockSpec((seq_len, HEAD_DIM), lambda b, hh, i: (b, 2 * hh + 1))],
        out_specs=qspec,
        out_shape=jax.ShapeDtypeStruct((batch * seq_len, h * HEAD_DIM), BF16),
        scratch_shapes=_att_scratch(tq, tk),
        compiler_params=_params("parallel", "parallel", "arbitrary"),
        name="mla_attention",
    )(qn, qr, kv, kr, kv)


def _compress_kernel(raw_ref, pe_ref, w1_ref, w2_ref, o_ref, *, n_slots):
    half = NSA_CMP_BLOCK // 2
    p1 = jnp.zeros((n_slots, HEAD_DIM), F32)
    p2 = jnp.zeros((n_slots, HEAD_DIM), F32)
    for l in range(half):
        a = raw_ref[pl.ds(l, n_slots, stride=NSA_CMP_STRIDE), :]
        p1 = p1 + _dot((a + pe_ref[0, l:l + 1, :]).astype(BF16), w1_ref[0, l])
        p2 = p2 + _dot((a + pe_ref[0, half + l:half + l + 1, :]).astype(BF16), w1_ref[0, half + l])
    pre = p1 + pltpu.roll(p2, n_slots - 1, 0)
    hid = pre * jax.nn.sigmoid(pre)
    o_ref[0, 0, 0] = _dot(hid.astype(BF16), w2_ref[0]).astype(o_ref.dtype)


def _nsa_compress(raw, pe, w1, w2, *, batch, seq_len):
    g = NSA_GROUPS
    n_slots = seq_len // NSA_CMP_STRIDE
    return pl.pallas_call(
        functools.partial(_compress_kernel, n_slots=n_slots),
        grid=(batch, 2, g),
        in_specs=[pl.BlockSpec((seq_len, HEAD_DIM), lambda b, kv, gg: (b, kv * g + gg)),
                  pl.BlockSpec((1, NSA_CMP_BLOCK, HEAD_DIM), lambda b, kv, gg: (kv, 0, 0)),
                  pl.BlockSpec((1, NSA_CMP_BLOCK, HEAD_DIM, HEAD_DIM), lambda b, kv, gg: (kv, 0, 0, 0)),
                  pl.BlockSpec((1, HEAD_DIM, HEAD_DIM), lambda b, kv, gg: (kv, 0, 0))],
        out_specs=pl.BlockSpec((1, 1, 1, n_slots, HEAD_DIM), lambda b, kv, gg: (b, kv, gg, 0, 0)),
        out_shape=jax.ShapeDtypeStruct((batch, 2, g, n_slots, HEAD_DIM), BF16),
        compiler_params=_params("parallel", "parallel", "parallel"),
        name="nsa_compress",
    )(raw, pe, w1, w2)


def _nsa_cmp_kernel(q_ref, kc_ref, vc_ref, bias_ref, gates_ref, ovt_ref, oc_ref, sel_ref, *, t, n_slots, n_sel,
                    n_top):
    i = pl.program_id(2)
    kc = kc_ref[0, 0, 0]
    vc = vc_ref[0, 0, 0]
    qpos = i * t + lax.broadcasted_iota(jnp.int32, (t, n_slots), 0)
    cmp_end = NSA_CMP_STRIDE * lax.broadcasted_iota(jnp.int32, (t, n_slots), 1) + (NSA_CMP_BLOCK - 1)
    valid = cmp_end <= qpos
    gates = gates_ref[...]
    psum = jnp.zeros((t, n_slots), F32)
    for r in range(NSA_REP):
        q = q_ref[:, r * HEAD_DIM:(r + 1) * HEAD_DIM]
        s = jnp.where(valid, _dot_nt(q, kc) + bias_ref[r], NEG)
        m = jnp.max(s, axis=-1, keepdims=True)
        p = jnp.where(valid, jnp.exp2(s - m), 0.0)
        p = p / jnp.maximum(jnp.sum(p, axis=-1, keepdims=True), 1e-30)
        psum = psum + p
        oc_ref[:, r * HEAD_DIM:(r + 1) * HEAD_DIM] = gates[:, r:r + 1] * _dot(p.astype(BF16), vc)

    hi, lo = _split_bf16(psum)
    ovt = ovt_ref[...]
    imp = _dot_nt(ovt, hi) + _dot_nt(ovt, lo)
    blk = lax.broadcasted_iota(jnp.int32, (n_sel, t), 0)
    tpos = i * t + lax.broadcasted_iota(jnp.int32, (n_sel, t), 1)
    cur = tpos // NSA_SEL_BLOCK
    forced = (blk == 0) | (blk == cur) | (blk == cur - 1)
    score = jnp.where(blk * NSA_SEL_BLOCK <= tpos, jnp.where(forced, FORCED_SCORE, imp), -1.0)
    rank = jnp.zeros((n_sel, t), F32)
    for mm in range(n_sel):
        sm = score[mm:mm + 1, :]
        ahead = (sm > score) | ((sm == score) & (blk > mm))
        rank = rank + jnp.where(ahead, 1.0, 0.0)
    sel_t = jnp.where(rank < n_top, 1.0, 0.0).astype(BF16)
    row, col = _tile_iota(t)
    eye = jnp.where(row == col, 1.0, 0.0).astype(BF16)
    sel_ref[0, 0] = _dot_nt(eye, sel_t).astype(sel_ref.dtype)


def _nsa_cmp_attention(q_all, kvc, bias_c, gates, *, batch, seq_len, t=ATT_T):
    g = NSA_GROUPS
    nq = seq_len // t
    n_slots = seq_len // NSA_CMP_STRIDE
    n_sel = seq_len // NSA_SEL_BLOCK
    c0 = NSA_CMP_STRIDE * np.arange(n_slots)[:, None]
    s0 = NSA_SEL_BLOCK * np.arange(n_sel)[None, :]
    overlap = (c0 < s0 + NSA_SEL_BLOCK) & (c0 + NSA_CMP_BLOCK > s0)
    ovt = jnp.asarray(overlap.T.astype(np.float32), dtype=BF16)
    gw = NSA_REP * HEAD_DIM
    return pl.pallas_call(
        functools.partial(_nsa_cmp_kernel, t=t, n_slots=n_slots, n_sel=n_sel, n_top=min(NSA_TOPN, n_sel)),
        grid=(batch, g, nq),
        in_specs=[pl.BlockSpec((t, gw), lambda b, gg, i: (b * nq + i, gg)),
                  pl.BlockSpec((1, 1, 1, n_slots, HEAD_DIM), lambda b, gg, i: (b, 0, gg, 0, 0)),
                  pl.BlockSpec((1, 1, 1, n_slots, HEAD_DIM), lambda b, gg, i: (b, 1, gg, 0, 0)),
                  pl.BlockSpec((NSA_REP, t, n_slots), lambda b, gg, i: (gg, i, 0)),
                  pl.BlockSpec((t, LANE), lambda b, gg, i: (b * nq + i, gg)),
                  pl.BlockSpec((n_sel, n_slots), lambda b, gg, i: (0, 0))],
        out_specs=[pl.BlockSpec((t, gw), lambda b, gg, i: (b * nq + i, gg)),
                   pl.BlockSpec((1, 1, t, n_sel), lambda b, gg, i: (b, gg, i, 0))],
        out_shape=[jax.ShapeDtypeStruct((batch * seq_len, g * gw), F32),
                   jax.ShapeDtypeStruct((batch, g, seq_len, n_sel), BF16)],
        compiler_params=_params("parallel", "parallel", "arbitrary"),
        name="nsa_cmp_attention",
    )(q_all, kvc, kvc, bias_c, gates, ovt)


def _nsa_main_kernel(q_ref, ks_ref, vs_ref, kw_ref, vw_ref, sel_ref, tab_ref, gates_ref, oc_ref, o_ref,
                     s_scr, p_scr, m_scr, l_scr, acc_scr, os_scr, mk_scr, *, t):
    i = pl.program_id(2)
    rep = NSA_REP
    q = jnp.concatenate([q_ref[:, r * HEAD_DIM:(r + 1) * HEAD_DIM] for r in range(rep)], axis=0)

    sel = sel_ref[0, 0]
    n_sel = sel.shape[1]
    blk_row = lax.broadcasted_iota(jnp.int32, (n_sel, t), 0)
    key_col = lax.broadcasted_iota(jnp.int32, (n_sel, t), 1)
    _osm_reset(m_scr, l_scr, acc_scr)

    def sel_scores(kb, slot):
        s_scr[slot] = _dot_nt(q, ks_ref[pl.ds(pl.multiple_of(kb * t, t), t), :])

    def sel_update(kb, slot):
        start = pl.multiple_of(kb * t, t)
        expand = jnp.where((start + key_col) // NSA_SEL_BLOCK == blk_row, 1.0, 0.0).astype(BF16)
        mk_scr[...] = _dot(sel, expand)
        off = _table_offset(kb, i, t)
        for r in range(rep):
            rows = pl.ds(r * t, t)
            t2 = s_scr[slot, rows, :] + tab_ref[r, pl.ds(off, t), :]
            t2 = jnp.where(mk_scr[...] > 0.5, t2, MASKED)
            _osm_update(t2, rows, slot, p_scr, m_scr, l_scr, acc_scr)
        _osm_accum(vs_ref[pl.ds(start, t), :], slot, p_scr, l_scr, acc_scr)

    _pipelined_tiles(i + 1, sel_scores, sel_update)
    os_scr[...] = acc_scr[...] / l_scr[...]

    n_back = NSA_WINDOW // t
    assert n_back == 2
    _osm_reset(m_scr, l_scr, acc_scr)

    def win_scores(kb, slot):
        s_scr[slot] = _dot_nt(q, kw_ref[pl.ds(pl.multiple_of(kb * t, t), t), :])

    def win_update(kb, slot, region):
        for r in range(rep):
            rows = pl.ds(r * t, t)
            t2 = s_scr[slot, rows, :] + tab_ref[r, pl.ds(region * t, t), :]
            _osm_update(t2, rows, slot, p_scr, m_scr, l_scr, acc_scr)
        _osm_accum(vw_ref[pl.ds(pl.multiple_of(kb * t, t), t), :], slot, p_scr, l_scr, acc_scr)

    def window(n_tiles):
        regions = (TAB_DIAG, TAB_SUB, TAB_EDGE)
        for back in range(n_tiles):
            win_scores(i - back, back)
        for back in reversed(range(n_tiles)):
            win_update(i - back, back, regions[back])

    pl.when(i >= 2)(lambda: window(3))
    pl.when(i == 1)(lambda: window(2))
    pl.when(i == 0)(lambda: window(1))
    o_w = acc_scr[...] / l_scr[...]
    o_s = os_scr[...]

    gates = gates_ref[...]
    for r in range(rep):
        rows = slice(r * t, (r + 1) * t)
        cols = slice(r * HEAD_DIM, (r + 1) * HEAD_DIM)
        o = oc_ref[:, cols] + gates[:, rep + r:rep + r + 1] * o_s[rows] + gates[:, 2 * rep + r:2 * rep + r + 1] * o_w[rows]
        o_ref[:, cols] = o.astype(o_ref.dtype)


def _nsa_main_attention(qkv, sel, tables, gates, oc, *, batch, seq_len, t=ATT_T):
    g = NSA_GROUPS
    nq = seq_len // t
    n_sel = seq_len // NSA_SEL_BLOCK
    gw = NSA_REP * HEAD_DIM
    qb = N_HEADS
    kv = lambda which: pl.BlockSpec((seq_len, HEAD_DIM), lambda b, gg, i: (b, qb + which * g + gg))
    tile = pl.BlockSpec((t, gw), lambda b, gg, i: (b * nq + i, gg))
    rows = NSA_REP * t
    return pl.pallas_call(
        functools.partial(_nsa_main_kernel, t=t),
        grid=(batch, g, nq),
        in_specs=[tile, kv(0), kv(1), kv(2), kv(3),
                  pl.BlockSpec((1, 1, t, n_sel), lambda b, gg, i: (b, gg, i, 0)),
                  pl.BlockSpec((NSA_REP, 4 * t, t), lambda b, gg, i: (gg, 0, 0)),
                  pl.BlockSpec((t, LANE), lambda b, gg, i: (b * nq + i, gg)),
                  tile],
        out_specs=tile,
        out_shape=jax.ShapeDtypeStruct((batch * seq_len, g * gw), BF16),
        scratch_shapes=_att_scratch(rows, t, slots=3) + [pltpu.VMEM((rows, HEAD_DIM), F32), pltpu.VMEM((t, t), F32)],
        compiler_params=_params("parallel", "parallel", "arbitrary"),
        name="nsa_main_attention",
    )(qkv, qkv, qkv, qkv, qkv, sel, tables, gates, oc)


def _rope_tables(seq_len, width):
    half = MLA_ROPE // 2
    inv = np.power(ROPE_THETA, -np.arange(half, dtype=np.float32) / half).astype(np.float32)
    ang = np.arange(seq_len, dtype=np.float32)[:, None] * inv[None, :]
    pad = np.zeros((seq_len, LANE - MLA_ROPE), np.float32)
    cos = np.concatenate([np.cos(ang), np.cos(ang), pad], axis=1)
    sin = np.concatenate([np.sin(ang), np.sin(ang), pad], axis=1)
    reps = width // LANE
    return jnp.asarray(np.tile(cos, (1, reps))), jnp.asarray(np.tile(sin, (1, reps)))


def _rope_weights(w):
    k, n, _ = w.shape
    half = MLA_ROPE // 2
    pad = jnp.zeros((k, n, LANE - MLA_ROPE), w.dtype)
    wa = jnp.concatenate([w, pad], axis=-1)
    wb = jnp.concatenate([-w[..., half:], w[..., :half], pad], axis=-1)
    return wa.reshape(k, n * LANE).astype(BF16), wb.reshape(k, n * LANE).astype(BF16)


def kernel(x, c, t5_bias, ada_w, ada_b, norm_g, final_g, ffn_w_up, ffn_conv_w, ffn_conv_b, ffn_w_down, sb_w_in, sb_w_out, nsa_w_in, nsa_cmp_pe, nsa_cmp_w1, nsa_cmp_w2, nsa_w_out, diff_w_in, diff_lambda, diff_head_g, diff_w_out, mla_w_in, mla_q_g, mla_w_qb, mla_kv_g, mla_w_kvb, mla_w_out):
    batch, seq_len, d = x.shape
    depth = ada_w.shape[0]
    h, dh, g = N_HEADS, HEAD_DIM, NSA_GROUPS
    sizes = dict(batch=batch, seq_len=seq_len)

    mod = _ada_mod(c, ada_w, ada_b)
    tables = _attention_tables(t5_bias, ATT_T)

    xf = x.reshape(batch * seq_len, d)
    for i in range(depth):
        mixer, j = i % 4, i // 4
        sh1, sc1, gt1, sh2, sc2, gt2 = (mod[i, :, n * d:(n + 1) * d].reshape(batch, 1, d) for n in range(6))
        nm = functools.partial(_norm_mm, xf, norm_g[i, 0], seq_len=seq_len, shift=sh1, scale=sc1)
        if mixer == 0:
            w_in = sb_w_in[j]
            w_in = jnp.concatenate([w_in[:, :h * dh] * SB_QSCALE, w_in[:, h * dh:]], axis=1)
            qkv = nm(w_in.astype(BF16), name="sb_in")
            o = _sb_attention(qkv, **sizes)
            w_out = sb_w_out[j]
        elif mixer == 1:
            w_in = nsa_w_in[j]
            n_q, n_kv = h * dh, g * dh
            w_att = jnp.concatenate([w_in[:, :n_q] * NSA_QSCALE, w_in[:, n_q + 2 * n_kv:n_q + 6 * n_kv]], axis=1)
            w_cmp = w_in[:, n_q:n_q + 2 * n_kv]
            w_g = w_in[:, n_q + 6 * n_kv:].reshape(d, 3, g, NSA_REP).transpose(0, 2, 1, 3).reshape(d, g, 3 * NSA_REP)
            w_g = jnp.pad(w_g, ((0, 0), (0, 0), (0, LANE - 3 * NSA_REP))).reshape(d, g * LANE)
            qkv, hn = nm(w_att.astype(BF16), emit_h=True, name="nsa_in")
            hmm = functools.partial(_norm_mm, hn, None, seq_len=seq_len, normed=True)
            raw = hmm(w_cmp.astype(BF16), out_dtype=F32, name="nsa_in_cmp")
            gates = hmm(w_g.astype(BF16), out_dtype=F32, act="sigmoid", name="nsa_in_gates")
            kvc = _nsa_compress(raw, nsa_cmp_pe[j], nsa_cmp_w1[j].reshape(2, NSA_CMP_BLOCK, dh, dh).astype(BF16),
                                nsa_cmp_w2[j].astype(BF16), **sizes)
            n_slots = seq_len // NSA_CMP_STRIDE
            dist_c = np.arange(seq_len)[:, None] - (NSA_CMP_STRIDE * np.arange(n_slots)[None, :] + NSA_CMP_BLOCK - 1)
            bias_c = _t5_gather(t5_bias, _t5_bucket_np(dist_c), tr=min(seq_len, 512), mult=LOG2E)
            oc, sel = _nsa_cmp_attention(qkv, kvc, bias_c, gates, **sizes)
            o = _nsa_main_attention(qkv, sel, tables, gates, oc, **sizes)
            w_out = nsa_w_out[j]
        elif mixer == 2:
            lambda_init = 0.8 - 0.6 * math.exp(-0.3 * i)
            w_in = diff_w_in[j]
            w_in = jnp.concatenate([w_in[:, :h * dh] * DIFF_QSCALE, w_in[:, h * dh:]], axis=1)
            qkv = nm(w_in.astype(BF16), name="diff_in")
            o = _diff_attention(qkv, _attention_tables(t5_bias, DIFF_T), diff_lambda[j], diff_head_g[j],
                                lambda_init=lambda_init, **sizes)
            w_out = diff_w_out[j]
        else:
            w_in = mla_w_in[j]
            nq_l, nkv_l = MLA_Q_LORA, MLA_KV_LORA
            w_lat = jnp.concatenate([w_in[:, :nq_l], jnp.zeros((d, 2 * nkv_l - nq_l), w_in.dtype),
                                     w_in[:, nq_l:nq_l + nkv_l]], axis=1)
            lat, hn = nm(w_lat.astype(BF16), out_dtype=F32, emit_h=True, name="mla_in")
            cos1, sin1 = _rope_tables(seq_len, LANE)
            wa, wb = _rope_weights(w_in[:, nq_l + nkv_l:].reshape(d, 1, MLA_ROPE))
            kr = _norm_mm(hn, None, wa, wb=wb, cos=cos1, sin=sin1, seq_len=seq_len, normed=True, name="mla_in_rope")
            w_qb = (mla_w_qb[j] * MLA_QSCALE).reshape(nq_l, h, MLA_NOPE + MLA_ROPE)
            qn = _norm_mm(lat, mla_q_g[j], w_qb[:, :, :MLA_NOPE].reshape(nq_l, h * MLA_NOPE).astype(BF16),
                          seq_len=seq_len, x_cols=nq_l, x_col_block=0, name="mla_q_nope")
            cosh, sinh = _rope_tables(seq_len, h * LANE)
            wa, wb = _rope_weights(w_qb[:, :, MLA_NOPE:])
            qr = _norm_mm(lat, mla_q_g[j], wa, wb=wb, cos=cosh, sin=sinh, seq_len=seq_len, x_cols=nq_l,
                          x_col_block=0, name="mla_q_rope")
            kv = _norm_mm(lat, mla_kv_g[j], mla_w_kvb[j].astype(BF16), seq_len=seq_len, x_cols=nkv_l,
                          x_col_block=2, name="mla_kv")
            o = _mla_attention(qn, qr, kv, kr, **sizes)
            w_out = mla_w_out[j]
        xf = _mm_residual(o, w_out.astype(BF16), xf, gt1, seq_len=seq_len)
        xf = _conv_ffn(xf, norm_g[i, 1], sh2, sc2, ffn_w_up[i].astype(BF16), ffn_conv_w[i], ffn_conv_b[i],
                       ffn_w_down[i].astype(BF16), gt2, final_g, seq_len=seq_len, final_norm=(i == depth - 1))
    return xf.reshape(batch, seq_len, d)
```

```python
import functools
import math

import numpy as np
import jax
import jax.numpy as jnp
from jax import lax
from jax.experimental import pallas as pl
from jax.experimental.pallas import tpu as pltpu

F32 = jnp.float32
BF16 = jnp.bfloat16
EPS = 1e-6
NEG = -1e30

LANE = 128
HALO = 16
VMEM_LIMIT = 56 * 2**20

T5_BUCKETS = 32
T5_MAX_DIST = 128
N_HEADS = 16
HEAD_DIM = 128
NSA_GROUPS = 4
NSA_REP = 4
NSA_CMP_BLOCK = 32
NSA_CMP_STRIDE = 16
NSA_SEL_BLOCK = 64
NSA_TOPN = 16
NSA_WINDOW = 512
FORCED_SCORE = 1e9
DIFF_DIM = 64
MLA_Q_LORA = 768
MLA_KV_LORA = 512
MLA_NOPE = 128
MLA_ROPE = 64
ROPE_THETA = 10000.0
CONV_WIDTH = 3
ATT_T = 256
DIFF_T = 512
MLA_T = 512
ATT_TQ = 512
ATT_TK = 256
LOG2E = 1.4426950408889634
SB_QSCALE = -(HEAD_DIM ** -0.5)
NSA_QSCALE = HEAD_DIM ** -0.5 * LOG2E
DIFF_QSCALE = DIFF_DIM ** -0.5 * LOG2E
MLA_QSCALE = (MLA_NOPE + MLA_ROPE) ** -0.5 * LOG2E
M_INIT = -1e30
MASKED = -2e30
SB_DEAD = -120.0
T5_MASKED = T5_BUCKETS


def _params(*sem):
    return pltpu.CompilerParams(dimension_semantics=sem, vmem_limit_bytes=VMEM_LIMIT)


def _dot(a, b):
    return jnp.dot(a, b, preferred_element_type=F32)


def _dot_nt(a, b):
    return lax.dot_general(a, b, (((1,), (1,)), ((), ())), preferred_element_type=F32)


def _lane_tile(x, reps):
    return jnp.concatenate([x] * reps, axis=1)


def _split_bf16(x):
    hi = x.astype(BF16)
    lo = (x - hi.astype(F32)).astype(BF16)
    return hi, lo


def _rms(x, g):
    ms = jnp.mean(x * x, axis=-1, keepdims=True)
    return x * lax.rsqrt(ms + EPS) * g


def _ada_kernel(c_ref, w_ref, b_ref, o_ref):
    c = c_ref[...]
    a = c * jax.nn.sigmoid(c)
    a_hi, a_lo = _split_bf16(a)
    w_hi, w_lo = _split_bf16(w_ref[0])
    o_ref[0] = _dot(a_hi, w_hi) + _dot(a_lo, w_hi) + _dot(a_hi, w_lo) + b_ref[0]


def _ada_mod(c, ada_w, ada_b, tn=1024):
    depth, d, n = ada_w.shape
    b = c.shape[0]
    return pl.pallas_call(
        _ada_kernel,
        grid=(depth, n // tn),
        in_specs=[pl.BlockSpec((b, d), lambda l, j: (0, 0)),
                  pl.BlockSpec((1, d, tn), lambda l, j: (l, 0, j)),
                  pl.BlockSpec((1, 1, tn), lambda l, j: (l, 0, j))],
        out_specs=pl.BlockSpec((1, b, tn), lambda l, j: (l, 0, j)),
        out_shape=jax.ShapeDtypeStruct((depth, b, n), F32),
        compiler_params=_params("parallel", "parallel"),
        name="ada_mod",
    )(c, ada_w, ada_b.reshape(depth, 1, n))


def _norm_mm_kernel(*refs, modulated, rope, act, normed):
    it = iter(refs)
    x_ref = next(it)
    g_ref = sh_ref = sc_ref = wb_ref = cos_ref = sin_ref = None
    if not normed:
        g_ref = next(it)
        if modulated:
            sh_ref, sc_ref = next(it), next(it)
    w_ref = next(it)
    if rope:
        wb_ref, cos_ref, sin_ref = next(it), next(it), next(it)
    o_ref = next(it)
    if normed:
        h = x_ref[...]
    else:
        h_ref = next(it)

        @pl.when(pl.program_id(1) == 0)
        def _():
            y = _rms(x_ref[...], g_ref[...])
            if modulated:
                y = y * (1.0 + sc_ref[0]) + sh_ref[0]
            h_ref[...] = y.astype(BF16)

        h = h_ref[...]
    acc = _dot(h, w_ref[...])
    if rope:
        acc = acc * cos_ref[...] + _dot(h, wb_ref[...]) * sin_ref[...]
    if act == "sigmoid":
        acc = jax.nn.sigmoid(acc)
    o_ref[...] = acc.astype(o_ref.dtype)


def _norm_mm(x, g, w, *, seq_len, shift=None, scale=None, wb=None, cos=None, sin=None, act=None, normed=False,
             emit_h=False, out_dtype=BF16, x_cols=None, x_col_block=0, tm=1024, tn=512, name="norm_mm"):
    m = x.shape[0]
    k = x.shape[1] if x_cols is None else x_cols
    n = w.shape[1]
    tn = min(tn, n)
    modulated, rope = shift is not None, wb is not None
    tps = seq_len // tm
    in_specs = [pl.BlockSpec((tm, k), lambda i, j: (i, x_col_block))]
    args = [x]
    if not normed:
        in_specs.append(pl.BlockSpec((1, k), lambda i, j: (0, 0)))
        args.append(g.reshape(1, k))
        if modulated:
            in_specs += [pl.BlockSpec((1, 1, k), lambda i, j: (i // tps, 0, 0))] * 2
            args += [shift, scale]
    in_specs.append(pl.BlockSpec((k, tn), lambda i, j: (0, j)))
    args.append(w)
    if rope:
        in_specs.append(pl.BlockSpec((k, tn), lambda i, j: (0, j)))
        in_specs += [pl.BlockSpec((tm, tn), lambda i, j: (i % tps, j))] * 2
        args += [wb, cos, sin]
    out_specs = [pl.BlockSpec((tm, tn), lambda i, j: (i, j))]
    out_shape = [jax.ShapeDtypeStruct((m, n), out_dtype)]
    scratch = []
    if emit_h:
        out_specs.append(pl.BlockSpec((tm, k), lambda i, j: (i, 0)))
        out_shape.append(jax.ShapeDtypeStruct((m, k), BF16))
    elif not normed:
        scratch.append(pltpu.VMEM((tm, k), BF16))
    out = pl.pallas_call(
        functools.partial(_norm_mm_kernel, modulated=modulated, rope=rope, act=act, normed=normed),
        grid=(m // tm, n // tn),
        in_specs=in_specs,
        out_specs=out_specs,
        out_shape=out_shape,
        scratch_shapes=scratch,
        compiler_params=_params("parallel", "arbitrary"),
        name=name,
    )(*args)
    return out if emit_h else out[0]


def _mm_res_kernel(a_ref, w_ref, x_ref, gate_ref, o_ref):
    o_ref[...] = x_ref[...] + gate_ref[0] * _dot(a_ref[...], w_ref[...])


def _mm_residual(a, w, x, gate, *, seq_len, tm=1024, tn=512, name="mm_residual"):
    m, k = a.shape
    n = w.shape[1]
    tps = seq_len // tm
    return pl.pallas_call(
        _mm_res_kernel,
        grid=(m // tm, n // tn),
        in_specs=[pl.BlockSpec((tm, k), lambda i, j: (i, 0)),
                  pl.BlockSpec((k, tn), lambda i, j: (0, j)),
                  pl.BlockSpec((tm, tn), lambda i, j: (i, j)),
                  pl.BlockSpec((1, 1, tn), lambda i, j: (i // tps, 0, j))],
        out_specs=pl.BlockSpec((tm, tn), lambda i, j: (i, j)),
        out_shape=jax.ShapeDtypeStruct((m, n), F32),
        compiler_params=_params("parallel", "arbitrary"),
        name=name,
    )(a, w, x, gate)


def _ffn_kernel(x_ref, xp_ref, g_ref, sh_ref, sc_ref, wg_ref, wu_ref, cwg_ref, cwu_ref, cbg_ref, cbu_ref,
                wd_ref, gate_ref, fg_ref, o_ref, h_ref, acc_ref, ug_scr, uu_scr, *, tiles_per_seq, final_norm):
    i, j = pl.program_id(0), pl.program_id(1)

    @pl.when(j == 0)
    def _():
        def nm(x):
            return _rms(x, g_ref[...]) * (1.0 + sc_ref[0]) + sh_ref[0]
        h_ref[HALO:, :] = nm(x_ref[...]).astype(BF16)
        keep = jnp.where(i % tiles_per_seq == 0, 0.0, 1.0)
        h_ref[:HALO, :] = (nm(xp_ref[...]) * keep).astype(BF16)
        acc_ref[...] = jnp.zeros_like(acc_ref)

    h = h_ref[...]

    def branch(w_ref, cw_ref, cb_ref, u_scr):
        u_scr[...] = _dot(h, w_ref[...])
        tm = u_scr.shape[0] - HALO
        cw = cw_ref[...]
        return (cw[0:1] * u_scr[pl.ds(HALO - 2, tm), :] + cw[1:2] * u_scr[pl.ds(HALO - 1, tm), :]
                + cw[2:3] * u_scr[pl.ds(HALO, tm), :] + cb_ref[...])

    gt = branch(wg_ref, cwg_ref, cbg_ref, ug_scr)
    up = branch(wu_ref, cwu_ref, cbu_ref, uu_scr)
    a = gt * jax.nn.sigmoid(gt) * up
    acc_ref[...] += _dot(a.astype(BF16), wd_ref[...])

    @pl.when(j == pl.num_programs(1) - 1)
    def _():
        y = x_ref[...] + gate_ref[0] * acc_ref[...]
        if final_norm:
            y = _rms(y, fg_ref[...])
        o_ref[...] = y


def _conv_ffn(x, g, shift, scale, w_up, conv_w, conv_b, w_down, gate, final_g, *, seq_len, final_norm,
              tm=512, tf=512):
    m, d = x.shape
    f = w_down.shape[0]
    nf = f // tf
    tps = seq_len // tm
    hb = tm // HALO
    row = lambda i, j: (i, 0)
    per_batch = lambda i, j: (i // tps, 0, 0)
    return pl.pallas_call(
        functools.partial(_ffn_kernel, tiles_per_seq=tps, final_norm=final_norm),
        grid=(m // tm, nf),
        in_specs=[pl.BlockSpec((tm, d), row),
                  pl.BlockSpec((HALO, d), lambda i, j: (jnp.maximum(i * hb - 1, 0), 0)),
                  pl.BlockSpec((1, d), lambda i, j: (0, 0)),
                  pl.BlockSpec((1, 1, d), per_batch),
                  pl.BlockSpec((1, 1, d), per_batch),
                  pl.BlockSpec((d, tf), lambda i, j: (0, j)),
                  pl.BlockSpec((d, tf), lambda i, j: (0, j + nf)),
                  pl.BlockSpec((CONV_WIDTH, tf), lambda i, j: (0, j)),
                  pl.BlockSpec((CONV_WIDTH, tf), lambda i, j: (0, j + nf)),
                  pl.BlockSpec((1, tf), lambda i, j: (0, j)),
                  pl.BlockSpec((1, tf), lambda i, j: (0, j + nf)),
                  pl.BlockSpec((tf, d), lambda i, j: (j, 0)),
                  pl.BlockSpec((1, 1, d), per_batch),
                  pl.BlockSpec((1, d), lambda i, j: (0, 0))],
        out_specs=pl.BlockSpec((tm, d), row),
        out_shape=jax.ShapeDtypeStruct((m, d), F32),
        scratch_shapes=[pltpu.VMEM((HALO + tm, d), BF16), pltpu.VMEM((tm, d), F32),
                        pltpu.VMEM((HALO + tm, tf), F32), pltpu.VMEM((HALO + tm, tf), F32)],
        compiler_params=_params("parallel", "arbitrary"),
        name="conv_ffn",
    )(x, x, g.reshape(1, d), shift, scale, w_up, w_up, conv_w, conv_w, conv_b.reshape(1, -1),
      conv_b.reshape(1, -1), w_down, gate, final_g.reshape(1, d))


def _t5_bucket_np(dist):
    n = np.maximum(dist, 0)
    max_exact = T5_BUCKETS // 2
    nf = np.maximum(n, 1).astype(np.float64)
    val = np.log(nf / max_exact) / math.log(T5_MAX_DIST / max_exact) * (T5_BUCKETS - max_exact)
    large = max_exact + np.trunc(val + 1e-6).astype(np.int64)
    return np.where(n < max_exact, n, np.minimum(large, T5_BUCKETS - 1)).astype(np.int32)


def _t5_gather_kernel(t5_ref, bkt_ref, o_ref, *, mult):
    h = pl.program_id(0)
    bk = bkt_ref[...]
    acc = jnp.zeros(bk.shape, F32)
    for b in range(T5_BUCKETS):
        acc = jnp.where(bk == b, t5_ref[b, h], acc)
    o_ref[0] = jnp.where(bk == T5_MASKED, MASKED, acc * mult)


def _t5_gather(t5_bias, bkt, tr, mult=1.0):
    rows, cols = bkt.shape
    heads = t5_bias.shape[1]
    return pl.pallas_call(
        functools.partial(_t5_gather_kernel, mult=mult),
        grid=(heads, rows // tr),
        in_specs=[pl.BlockSpec(memory_space=pltpu.SMEM),
                  pl.BlockSpec((tr, cols), lambda h, r: (r, 0))],
        out_specs=pl.BlockSpec((1, tr, cols), lambda h, r: (h, r, 0)),
        out_shape=jax.ShapeDtypeStruct((heads, rows, cols), F32),
        compiler_params=_params("parallel", "parallel"),
        name="t5_gather",
    )(t5_bias, jnp.asarray(bkt))


TAB_DIAG, TAB_SUB, TAB_FAR, TAB_EDGE = 0, 1, 2, 3


def _attention_tables(t5_bias, t):
    i = np.arange(t)[:, None]
    j = np.arange(t)[None, :]
    assert int(_t5_bucket_np(np.array(t + 1))) == T5_BUCKETS - 1
    far = np.full((t, t), T5_BUCKETS - 1, np.int32)
    bkt = np.concatenate([np.where(j <= i, _t5_bucket_np(i - j), T5_MASKED), _t5_bucket_np(t + i - j), far,
                          np.where(j > i, far, T5_MASKED)], axis=0).astype(np.int32)
    return _t5_gather(t5_bias, bkt, tr=t, mult=LOG2E)


def _tile_iota(t):
    return lax.broadcasted_iota(jnp.int32, (t, t), 0), lax.broadcasted_iota(jnp.int32, (t, t), 1)


def _rect_iota(i, tq, start, tk):
    rowg = i * tq + lax.broadcasted_iota(jnp.int32, (tq, tk), 0)
    colg = start + lax.broadcasted_iota(jnp.int32, (tq, tk), 1)
    return rowg, colg


def _osm_reset(m_scr, l_scr, acc_scr):
    m_scr[...] = jnp.full(m_scr.shape, M_INIT, F32)
    l_scr[...] = jnp.zeros(l_scr.shape, F32)
    acc_scr[...] = jnp.zeros(acc_scr.shape, F32)


def _osm_update(t2, rows, slot, p_scr, m_scr, l_scr, acc_scr):
    m_prev = m_scr[rows, :]
    m_new = jnp.maximum(m_prev, jnp.max(t2, axis=-1, keepdims=True))
    p = jnp.exp2(t2 - _lane_tile(m_new, t2.shape[1] // LANE))
    alpha = jnp.exp2(m_prev - m_new)
    l_scr[rows, :] = alpha * l_scr[rows, :]
    m_scr[rows, :] = m_new
    p_scr[slot, rows, :] = p.astype(BF16)
    acc_scr[rows, :] = alpha * acc_scr[rows, :]


def _osm_accum(v, slot, p_scr, l_scr, acc_scr):
    vext = jnp.concatenate([v, jnp.ones(v.shape, v.dtype)], axis=-1)
    half = p_scr.shape[1] // 2
    for part in range(2):
        rows = pl.ds(part * half, half)
        pv = _dot(p_scr[slot, rows, :], vext)
        acc_scr[rows, :] += pv[:, :HEAD_DIM]
        l_scr[rows, :] += pv[:, HEAD_DIM:]


def _pipelined_tiles(n_tiles, scores, update, update_last=None):
    update_last = update if update_last is None else update_last
    last = n_tiles - 1
    n_pairs = last // 2
    scores(0, 0)

    def pair(n, cr):
        scores(2 * n + 1, 1)
        update(2 * n, 0)
        scores(2 * n + 2, 0)
        update(2 * n + 1, 1)
        return cr

    lax.fori_loop(0, n_pairs, pair, 0)

    @pl.when(last % 2 == 0)
    def _():
        update_last(last, 0)

    @pl.when(last % 2 == 1)
    def _():
        scores(last, 1)
        update(last - 1, 0)
        update_last(last, 1)


def _att_scratch(rows, tk, slots=2):
    return [pltpu.VMEM((slots, rows, tk), F32), pltpu.VMEM((slots, rows, tk), BF16), pltpu.VMEM((rows, LANE), F32),
            pltpu.VMEM((rows, LANE), F32), pltpu.VMEM((rows, HEAD_DIM), F32)]


def _sb_kernel(q_ref, k_ref, v_ref, o_ref, s_scr, c_scr, acc_scr, *, tq, tk):
    i = pl.program_id(2)
    q = q_ref[...]
    row, col = _tile_iota(tk)
    upper = jnp.where(row > col, 1.0, 0.0).astype(BF16)
    c_scr[...] = jnp.zeros(c_scr.shape, F32)
    acc_scr[...] = jnp.zeros(acc_scr.shape, F32)
    reps = tk // LANE

    def scores(kb, slot):
        start = pl.multiple_of(kb * tk, tk)
        s_scr[slot] = _dot_nt(q, k_ref[pl.ds(start, tk), :])

    def update(kb, slot, diag):
        start = pl.multiple_of(kb * tk, tk)
        nz = s_scr[slot]
        e = jnp.exp2(jnp.abs(nz) * (-LOG2E))
        lk = jnp.minimum(nz, 0.0) - jnp.log(1.0 + e)
        if diag:
            rowg, colg = _rect_iota(i, tq, start, tk)
            past = colg < rowg
            lk = jnp.where(past, lk, 0.0)
        hi, lo = _split_bf16(lk)
        c = c_scr[...]
        later = _dot(hi, upper) + _dot(lo, upper) + _lane_tile(c, reps)
        a = jnp.exp(lk - nz + later)
        if diag:
            a = jnp.where(past, a, 0.0)
        acc_scr[...] += _dot(a.astype(BF16), v_ref[pl.ds(start, tk), :])
        c_scr[...] = c + jnp.sum(lk, axis=-1, keepdims=True)

    assert tq == 2 * tk
    scores(2 * i + 1, 1)
    scores(2 * i, 0)
    update(2 * i + 1, 1, True)
    scores(jnp.maximum(2 * i - 1, 0), 1)
    update(2 * i, 0, True)

    def more(carry):
        n, c_max = carry
        return (n < i) & (c_max > SB_DEAD)

    def pair(carry):
        n, _ = carry
        kb = 2 * (i - n) - 1
        scores(kb - 1, 0)
        update(kb, 1, False)
        c_mid = jnp.max(c_scr[...])

        @pl.when(c_mid > SB_DEAD)
        def _():
            scores(jnp.maximum(kb - 2, 0), 1)
            update(kb - 1, 0, False)

        return n + 1, jnp.where(c_mid > SB_DEAD, jnp.max(c_scr[...]), c_mid)

    lax.while_loop(more, pair, (0, jnp.max(c_scr[...])))
    o_ref[...] = acc_scr[...].astype(o_ref.dtype)


def _sb_attention(qkv, *, batch, seq_len, tq=ATT_TQ, tk=ATT_TK):
    h = N_HEADS
    nq = seq_len // tq
    return pl.pallas_call(
        functools.partial(_sb_kernel, tq=tq, tk=tk),
        grid=(batch, h, nq),
        in_specs=[pl.BlockSpec((tq, HEAD_DIM), lambda b, hh, i: (b * nq + i, hh)),
                  pl.BlockSpec((seq_len, HEAD_DIM), lambda b, hh, i: (b, h + hh)),
                  pl.BlockSpec((seq_len, HEAD_DIM), lambda b, hh, i: (b, 2 * h + hh))],
        out_specs=pl.BlockSpec((tq, HEAD_DIM), lambda b, hh, i: (b * nq + i, hh)),
        out_shape=jax.ShapeDtypeStruct((batch * seq_len, h * HEAD_DIM), BF16),
        scratch_shapes=[pltpu.VMEM((2, tq, tk), F32), pltpu.VMEM((tq, LANE), F32), pltpu.VMEM((tq, HEAD_DIM), F32)],
        compiler_params=_params("parallel", "parallel", "arbitrary"),
        name="sb_attention",
    )(qkv, qkv, qkv)


def _table_offset(kb, i, t):
    return pl.multiple_of(jnp.where(kb == i, TAB_DIAG * t, jnp.where(kb == i - 1, TAB_SUB * t, TAB_FAR * t)), t)


def _diff_kernel(q_ref, k_ref, v_ref, tab_ref, lam_ref, hg_ref, o_ref, s_scr, p_scr, m_scr, l_scr, acc_scr, *, t,
                 lambda_init):
    i = pl.program_id(2)
    q = q_ref[...]
    lane = lax.broadcasted_iota(jnp.int32, q.shape, 1)
    zero = jnp.zeros_like(q)
    q2 = jnp.concatenate([jnp.where(lane < DIFF_DIM, q, zero), jnp.where(lane >= DIFF_DIM, q, zero)], axis=0)
    _osm_reset(m_scr, l_scr, acc_scr)

    def scores(kb, slot):
        s_scr[slot] = _dot_nt(q2, k_ref[pl.ds(pl.multiple_of(kb * t, t), t), :])

    def update(kb, slot):
        off = _table_offset(kb, i, t)
        for half in range(2):
            rows = pl.ds(half * t, t)
            t2 = s_scr[slot, rows, :] + tab_ref[0, pl.ds(off, t), :]
            _osm_update(t2, rows, slot, p_scr, m_scr, l_scr, acc_scr)
        _osm_accum(v_ref[pl.ds(pl.multiple_of(kb * t, t), t), :], slot, p_scr, l_scr, acc_scr)

    _pipelined_tiles(i + 1, scores, update)

    lam = lam_ref[...]
    lmbda = (jnp.exp(jnp.sum(lam[0:1] * lam[1:2], axis=-1, keepdims=True))
             - jnp.exp(jnp.sum(lam[2:3] * lam[3:4], axis=-1, keepdims=True)) + lambda_init)
    o = acc_scr[...] / l_scr[...]
    o = o[:t] - lmbda * o[t:]
    o_ref[...] = (_rms(o, hg_ref[...]) * (1.0 - lambda_init)).astype(o_ref.dtype)


def _diff_attention(qkv, tables, lam, head_g, *, batch, seq_len, lambda_init, t=DIFF_T):
    h = N_HEADS
    nq = seq_len // t
    return pl.pallas_call(
        functools.partial(_diff_kernel, t=t, lambda_init=lambda_init),
        grid=(batch, h, nq),
        in_specs=[pl.BlockSpec((t, HEAD_DIM), lambda b, hh, i: (b * nq + i, hh)),
                  pl.BlockSpec((seq_len, HEAD_DIM), lambda b, hh, i: (b, h + hh)),
                  pl.BlockSpec((seq_len, HEAD_DIM), lambda b, hh, i: (b, 2 * h + hh)),
                  pl.BlockSpec((1, 4 * t, t), lambda b, hh, i: (hh, 0, 0)),
                  pl.BlockSpec((4, DIFF_DIM), lambda b, hh, i: (0, 0)),
                  pl.BlockSpec((1, HEAD_DIM), lambda b, hh, i: (0, 0))],
        out_specs=pl.BlockSpec((t, HEAD_DIM), lambda b, hh, i: (b * nq + i, hh)),
        out_shape=jax.ShapeDtypeStruct((batch * seq_len, h * HEAD_DIM), BF16),
        scratch_shapes=_att_scratch(2 * t, t),
        compiler_params=_params("parallel", "parallel", "arbitrary"),
        name="diff_attention",
    )(qkv, qkv, qkv, tables, lam, head_g.reshape(1, HEAD_DIM))


def _mla_kernel(qn_ref, qr_ref, kn_ref, kr_ref, v_ref, o_ref, s_scr, p_scr, m_scr, l_scr, acc_scr, *, tq, tk):
    i = pl.program_id(2)
    q = jnp.concatenate([qn_ref[...], qr_ref[...]], axis=-1)
    _osm_reset(m_scr, l_scr, acc_scr)

    def scores(kb, slot):
        start = pl.multiple_of(kb * tk, tk)
        k = jnp.concatenate([kn_ref[pl.ds(start, tk), :], kr_ref[pl.ds(start, tk), :]], axis=-1)
        s_scr[slot] = _dot_nt(q, k)

    def update(kb, slot, mask):
        start = pl.multiple_of(kb * tk, tk)
        t2 = s_scr[slot]
        if mask:
            rowg, colg = _rect_iota(i, tq, start, tk)
            t2 = jnp.where(colg <= rowg, t2, MASKED)
        _osm_update(t2, pl.ds(0, tq), slot, p_scr, m_scr, l_scr, acc_scr)
        _osm_accum(v_ref[pl.ds(start, tk), :], slot, p_scr, l_scr, acc_scr)

    assert tq == tk
    _pipelined_tiles(i + 1, scores, lambda kb, slot: update(kb, slot, False), lambda kb, slot: update(kb, slot, True))
    o_ref[...] = (acc_scr[...] / l_scr[...]).astype(o_ref.dtype)


def _mla_attention(qn, qr, kv, kr, *, batch, seq_len, tq=MLA_T, tk=MLA_T):
    h = N_HEADS
    nq = seq_len // tq
    qspec = pl.BlockSpec((tq, HEAD_DIM), lambda b, hh, i: (b * nq + i, hh))
    return pl.pallas_call(
        functools.partial(_mla_kernel, tq=tq, tk=tk),
        grid=(batch, h, nq),
        in_specs=[qspec,
                  pl.BlockSpec((tq, LANE), lambda b, hh, i: (b * nq + i, hh // 2)),
                  pl.BlockSpec((seq_len, HEAD_DIM), lambda b, hh, i: (b, 2 * hh)),
                  pl.BlockSpec((seq_len, LANE), lambda b, hh, i: (b, hh % 2)),
                  pl.BlockSpec((seq_len, HEAD_DIM), lambda b, hh, i: (b, 2 * hh + 1))],
        out_specs=qspec,
        out_shape=jax.ShapeDtypeStruct((batch * seq_len, h * HEAD_DIM), BF16),
        scratch_shapes=_att_scratch(tq, tk),
        compiler_params=_params("parallel", "parallel", "arbitrary"),
        name="mla_attention",
    )(qn, qr, kv, kr, kv)


def _compress_kernel(raw_ref, pe_ref, w1_ref, w2_ref, o_ref, *, n_slots):
    half = NSA_CMP_BLOCK // 2
    p1 = jnp.zeros((n_slots, HEAD_DIM), F32)
    p2 = jnp.zeros((n_slots, HEAD_DIM), F32)
    for l in range(half):
        a = raw_ref[pl.ds(l, n_slots, stride=NSA_CMP_STRIDE), :]
        p1 = p1 + _dot((a + pe_ref[0, l:l + 1, :]).astype(BF16), w1_ref[0, l])
        p2 = p2 + _dot((a + pe_ref[0, half + l:half + l + 1, :]).astype(BF16), w1_ref[0, half + l])
    pre = p1 + pltpu.roll(p2, n_slots - 1, 0)
    hid = pre * jax.nn.sigmoid(pre)
    o_ref[0, 0, 0] = _dot(hid.astype(BF16), w2_ref[0]).astype(o_ref.dtype)


def _nsa_compress(raw, pe, w1, w2, *, batch, seq_len):
    g = NSA_GROUPS
    n_slots = seq_len // NSA_CMP_STRIDE
    return pl.pallas_call(
        functools.partial(_compress_kernel, n_slots=n_slots),
        grid=(batch, 2, g),
        in_specs=[pl.BlockSpec((seq_len, HEAD_DIM), lambda b, kv, gg: (b, kv * g + gg)),
                  pl.BlockSpec((1, NSA_CMP_BLOCK, HEAD_DIM), lambda b, kv, gg: (kv, 0, 0)),
                  pl.BlockSpec((1, NSA_CMP_BLOCK, HEAD_DIM, HEAD_DIM), lambda b, kv, gg: (kv, 0, 0, 0)),
                  pl.BlockSpec((1, HEAD_DIM, HEAD_DIM), lambda b, kv, gg: (kv, 0, 0))],
        out_specs=pl.BlockSpec((1, 1, 1, n_slots, HEAD_DIM), lambda b, kv, gg: (b, kv, gg, 0, 0)),
        out_shape=jax.ShapeDtypeStruct((batch, 2, g, n_slots, HEAD_DIM), BF16),
        compiler_params=_params("parallel", "parallel", "parallel"),
        name="nsa_compress",
    )(raw, pe, w1, w2)


def _nsa_cmp_kernel(q_ref, kc_ref, vc_ref, bias_ref, gates_ref, ovt_ref, oc_ref, sel_ref, *, t, n_slots, n_sel,
                    n_top):
    i = pl.program_id(2)
    kc = kc_ref[0, 0, 0]
    vc = vc_ref[0, 0, 0]
    qpos = i * t + lax.broadcasted_iota(jnp.int32, (t, n_slots), 0)
    cmp_end = NSA_CMP_STRIDE * lax.broadcasted_iota(jnp.int32, (t, n_slots), 1) + (NSA_CMP_BLOCK - 1)
    valid = cmp_end <= qpos
    gates = gates_ref[...]
    psum = jnp.zeros((t, n_slots), F32)
    for r in range(NSA_REP):
        q = q_ref[:, r * HEAD_DIM:(r + 1) * HEAD_DIM]
        s = jnp.where(valid, _dot_nt(q, kc) + bias_ref[r], NEG)
        m = jnp.max(s, axis=-1, keepdims=True)
        p = jnp.where(valid, jnp.exp2(s - m), 0.0)
        p = p / jnp.maximum(jnp.sum(p, axis=-1, keepdims=True), 1e-30)
        psum = psum + p
        oc_ref[:, r * HEAD_DIM:(r + 1) * HEAD_DIM] = gates[:, r:r + 1] * _dot(p.astype(BF16), vc)

    hi, lo = _split_bf16(psum)
    ovt = ovt_ref[...]
    imp = _dot_nt(ovt, hi) + _dot_nt(ovt, lo)
    blk = lax.broadcasted_iota(jnp.int32, (n_sel, t), 0)
    tpos = i * t + lax.broadcasted_iota(jnp.int32, (n_sel, t), 1)
    cur = tpos // NSA_SEL_BLOCK
    forced = (blk == 0) | (blk == cur) | (blk == cur - 1)
    score = jnp.where(blk * NSA_SEL_BLOCK <= tpos, jnp.where(forced, FORCED_SCORE, imp), -1.0)
    rank = jnp.zeros((n_sel, t), F32)
    for mm in range(n_sel):
        sm = score[mm:mm + 1, :]
        ahead = (sm > score) | ((sm == score) & (blk > mm))
        rank = rank + jnp.where(ahead, 1.0, 0.0)
    sel_t = jnp.where(rank < n_top, 1.0, 0.0).astype(BF16)
    row, col = _tile_iota(t)
    eye = jnp.where(row == col, 1.0, 0.0).astype(BF16)
    sel_ref[0, 0] = _dot_nt(eye, sel_t).astype(sel_ref.dtype)


def _nsa_cmp_attention(q_all, kvc, bias_c, gates, *, batch, seq_len, t=ATT_T):
    g = NSA_GROUPS
    nq = seq_len // t
    n_slots = seq_len // NSA_CMP_STRIDE
    n_sel = seq_len // NSA_SEL_BLOCK
    c0 = NSA_CMP_STRIDE * np.arange(n_slots)[:, None]
    s0 = NSA_SEL_BLOCK * np.arange(n_sel)[None, :]
    overlap = (c0 < s0 + NSA_SEL_BLOCK) & (c0 + NSA_CMP_BLOCK > s0)
    ovt = jnp.asarray(overlap.T.astype(np.float32), dtype=BF16)
    gw = NSA_REP * HEAD_DIM
    return pl.pallas_call(
        functools.partial(_nsa_cmp_kernel, t=t, n_slots=n_slots, n_sel=n_sel, n_top=min(NSA_TOPN, n_sel)),
        grid=(batch, g, nq),
        in_specs=[pl.BlockSpec((t, gw), lambda b, gg, i: (b * nq + i, gg)),
                  pl.BlockSpec((1, 1, 1, n_slots, HEAD_DIM), lambda b, gg, i: (b, 0, gg, 0, 0)),
                  pl.BlockSpec((1, 1, 1, n_slots, HEAD_DIM), lambda b, gg, i: (b, 1, gg, 0, 0)),
                  pl.BlockSpec((NSA_REP, t, n_slots), lambda b, gg, i: (gg, i, 0)),
                  pl.BlockSpec((t, LANE), lambda b, gg, i: (b * nq + i, gg)),
                  pl.BlockSpec((n_sel, n_slots), lambda b, gg, i: (0, 0))],
        out_specs=[pl.BlockSpec((t, gw), lambda b, gg, i: (b * nq + i, gg)),
                   pl.BlockSpec((1, 1, t, n_sel), lambda b, gg, i: (b, gg, i, 0))],
        out_shape=[jax.ShapeDtypeStruct((batch * seq_len, g * gw), F32),
                   jax.ShapeDtypeStruct((batch, g, seq_len, n_sel), BF16)],
        compiler_params=_params("parallel", "parallel", "arbitrary"),
        name="nsa_cmp_attention",
    )(q_all, kvc, kvc, bias_c, gates, ovt)


def _nsa_main_kernel(q_ref, ks_ref, vs_ref, kw_ref, vw_ref, sel_ref, tab_ref, gates_ref, oc_ref, o_ref,
                     s_scr, p_scr, m_scr, l_scr, acc_scr, os_scr, mk_scr, *, t):
    i = pl.program_id(2)
    rep = NSA_REP
    q = jnp.concatenate([q_ref[:, r * HEAD_DIM:(r + 1) * HEAD_DIM] for r in range(rep)], axis=0)

    sel = sel_ref[0, 0]
    n_sel = sel.shape[1]
    blk_row = lax.broadcasted_iota(jnp.int32, (n_sel, t), 0)
    key_col = lax.broadcasted_iota(jnp.int32, (n_sel, t), 1)
    _osm_reset(m_scr, l_scr, acc_scr)

    def sel_scores(kb, slot):
        s_scr[slot] = _dot_nt(q, ks_ref[pl.ds(pl.multiple_of(kb * t, t), t), :])

    def sel_update(kb, slot):
        start = pl.multiple_of(kb * t, t)
        expand = jnp.where((start + key_col) // NSA_SEL_BLOCK == blk_row, 1.0, 0.0).astype(BF16)
        mk_scr[...] = _dot(sel, expand)
        off = _table_offset(kb, i, t)
        for r in range(rep):
            rows = pl.ds(r * t, t)
            t2 = s_scr[slot, rows, :] + tab_ref[r, pl.ds(off, t), :]
            t2 = jnp.where(mk_scr[...] > 0.5, t2, MASKED)
            _osm_update(t2, rows, slot, p_scr, m_scr, l_scr, acc_scr)
        _osm_accum(vs_ref[pl.ds(start, t), :], slot, p_scr, l_scr, acc_scr)

    _pipelined_tiles(i + 1, sel_scores, sel_update)
    os_scr[...] = acc_scr[...] / l_scr[...]

    n_back = NSA_WINDOW // t
    assert n_back == 2
    _osm_reset(m_scr, l_scr, acc_scr)

    def win_scores(kb, slot):
        s_scr[slot] = _dot_nt(q, kw_ref[pl.ds(pl.multiple_of(kb * t, t), t), :])

    def win_update(kb, slot, region):
        for r in range(rep):
            rows = pl.ds(r * t, t)
            t2 = s_scr[slot, rows, :] + tab_ref[r, pl.ds(region * t, t), :]
            _osm_update(t2, rows, slot, p_scr, m_scr, l_scr, acc_scr)
        _osm_accum(vw_ref[pl.ds(pl.multiple_of(kb * t, t), t), :], slot, p_scr, l_scr, acc_scr)

    def window(n_tiles):
        regions = (TAB_DIAG, TAB_SUB, TAB_EDGE)
        for back in range(n_tiles):
            win_scores(i - back, back)
        for back in reversed(range(n_tiles)):
            win_update(i - back, back, regions[back])

    pl.when(i >= 2)(lambda: window(3))
    pl.when(i == 1)(lambda: window(2))
    pl.when(i == 0)(lambda: window(1))
    o_w = acc_scr[...] / l_scr[...]
    o_s = os_scr[...]

    gates = gates_ref[...]
    for r in range(rep):
        rows = slice(r * t, (r + 1) * t)
        cols = slice(r * HEAD_DIM, (r + 1) * HEAD_DIM)
        o = oc_ref[:, cols] + gates[:, rep + r:rep + r + 1] * o_s[rows] + gates[:, 2 * rep + r:2 * rep + r + 1] * o_w[rows]
        o_ref[:, cols] = o.astype(o_ref.dtype)


def _nsa_main_attention(qkv, sel, tables, gates, oc, *, batch, seq_len, t=ATT_T):
    g = NSA_GROUPS
    nq = seq_len // t
    n_sel = seq_len // NSA_SEL_BLOCK
    gw = NSA_REP * HEAD_DIM
    qb = N_HEADS
    kv = lambda which: pl.BlockSpec((seq_len, HEAD_DIM), lambda b, gg, i: (b, qb + which * g + gg))
    tile = pl.BlockSpec((t, gw), lambda b, gg, i: (b * nq + i, gg))
    rows = NSA_REP * t
    return pl.pallas_call(
        functools.partial(_nsa_main_kernel, t=t),
        grid=(batch, g, nq),
        in_specs=[tile, kv(0), kv(1), kv(2), kv(3),
                  pl.BlockSpec((1, 1, t, n_sel), lambda b, gg, i: (b, gg, i, 0)),
                  pl.BlockSpec((NSA_REP, 4 * t, t), lambda b, gg, i: (gg, 0, 0)),
                  pl.BlockSpec((t, LANE), lambda b, gg, i: (b * nq + i, gg)),
                  tile],
        out_specs=tile,
        out_shape=jax.ShapeDtypeStruct((batch * seq_len, g * gw), BF16),
        scratch_shapes=_att_scratch(rows, t, slots=3) + [pltpu.VMEM((rows, HEAD_DIM), F32), pltpu.VMEM((t, t), F32)],
        compiler_params=_params("parallel", "parallel", "arbitrary"),
        name="nsa_main_attention",
    )(qkv, qkv, qkv, qkv, qkv, sel, tables, gates, oc)


def _rope_tables(seq_len, width):
    half = MLA_ROPE // 2
    inv = np.power(ROPE_THETA, -np.arange(half, dtype=np.float32) / half).astype(np.float32)
    ang = np.arange(seq_len, dtype=np.float32)[:, None] * inv[None, :]
    reps = width // half
    return jnp.asarray(np.tile(np.cos(ang), (1, reps))), jnp.asarray(np.tile(np.sin(ang), (1, reps)))


def _rope_weights(w):
    k, n, _ = w.shape
    half = MLA_ROPE // 2
    wb = jnp.concatenate([-w[..., half:], w[..., :half]], axis=-1)
    return w.reshape(k, n * MLA_ROPE).astype(BF16), wb.reshape(k, n * MLA_ROPE).astype(BF16)


def kernel(x, c, t5_bias, ada_w, ada_b, norm_g, final_g, ffn_w_up, ffn_conv_w, ffn_conv_b, ffn_w_down, sb_w_in, sb_w_out, nsa_w_in, nsa_cmp_pe, nsa_cmp_w1, nsa_cmp_w2, nsa_w_out, diff_w_in, diff_lambda, diff_head_g, diff_w_out, mla_w_in, mla_q_g, mla_w_qb, mla_kv_g, mla_w_kvb, mla_w_out):
    batch, seq_len, d = x.shape
    depth = ada_w.shape[0]
    h, dh, g = N_HEADS, HEAD_DIM, NSA_GROUPS
    sizes = dict(batch=batch, seq_len=seq_len)

    mod = _ada_mod(c, ada_w, ada_b)
    tables = _attention_tables(t5_bias, ATT_T)

    xf = x.reshape(batch * seq_len, d)
    for i in range(depth):
        mixer, j = i % 4, i // 4
        sh1, sc1, gt1, sh2, sc2, gt2 = (mod[i, :, n * d:(n + 1) * d].reshape(batch, 1, d) for n in range(6))
        nm = functools.partial(_norm_mm, xf, norm_g[i, 0], seq_len=seq_len, shift=sh1, scale=sc1)
        if mixer == 0:
            w_in = sb_w_in[j]
            w_in = jnp.concatenate([w_in[:, :h * dh] * SB_QSCALE, w_in[:, h * dh:]], axis=1)
            qkv = nm(w_in.astype(BF16), name="sb_in")
            o = _sb_attention(qkv, **sizes)
            w_out = sb_w_out[j]
        elif mixer == 1:
            w_in = nsa_w_in[j]
            n_q, n_kv = h * dh, g * dh
            w_att = jnp.concatenate([w_in[:, :n_q] * NSA_QSCALE, w_in[:, n_q + 2 * n_kv:n_q + 6 * n_kv]], axis=1)
            w_cmp = w_in[:, n_q:n_q + 2 * n_kv]
            w_g = w_in[:, n_q + 6 * n_kv:].reshape(d, 3, g, NSA_REP).transpose(0, 2, 1, 3).reshape(d, g, 3 * NSA_REP)
            w_g = jnp.pad(w_g, ((0, 0), (0, 0), (0, LANE - 3 * NSA_REP))).reshape(d, g * LANE)
            qkv, hn = nm(w_att.astype(BF16), emit_h=True, name="nsa_in")
            hmm = functools.partial(_norm_mm, hn, None, seq_len=seq_len, normed=True)
            raw = hmm(w_cmp.astype(BF16), out_dtype=F32, name="nsa_in_cmp")
            gates = hmm(w_g.astype(BF16), out_dtype=F32, act="sigmoid", name="nsa_in_gates")
            kvc = _nsa_compress(raw, nsa_cmp_pe[j], nsa_cmp_w1[j].reshape(2, NSA_CMP_BLOCK, dh, dh).astype(BF16),
                                nsa_cmp_w2[j].astype(BF16), **sizes)
            n_slots = seq_len // NSA_CMP_STRIDE
            dist_c = np.arange(seq_len)[:, None] - (NSA_CMP_STRIDE * np.arange(n_slots)[None, :] + NSA_CMP_BLOCK - 1)
            bias_c = _t5_gather(t5_bias, _t5_bucket_np(dist_c), tr=min(seq_len, 512), mult=LOG2E)
            oc, sel = _nsa_cmp_attention(qkv, kvc, bias_c, gates, **sizes)
            o = _nsa_main_attention(qkv, sel, tables, gates, oc, **sizes)
            w_out = nsa_w_out[j]
        elif mixer == 2:
            lambda_init = 0.8 - 0.6 * math.exp(-0.3 * i)
            w_in = diff_w_in[j]
            w_in = jnp.concatenate([w_in[:, :h * dh] * DIFF_QSCALE, w_in[:, h * dh:]], axis=1)
            qkv = nm(w_in.astype(BF16), name="diff_in")
            o = _diff_attention(qkv, _attention_tables(t5_bias, DIFF_T), diff_lambda[j], diff_head_g[j],
                                lambda_init=lambda_init, **sizes)
            w_out = diff_w_out[j]
        else:
            w_in = mla_w_in[j]
            nq_l, nkv_l = MLA_Q_LORA, MLA_KV_LORA
            w_lat = jnp.concatenate([w_in[:, :nq_l], jnp.zeros((d, 2 * nkv_l - nq_l), w_in.dtype),
                                     w_in[:, nq_l:nq_l + nkv_l]], axis=1)
            lat, hn = nm(w_lat.astype(BF16), out_dtype=F32, emit_h=True, name="mla_in")
            w_kr = w_in[:, nq_l + nkv_l:]
            zero = jnp.zeros_like(w_kr)
            cos2, sin2 = _rope_tables(seq_len, 2 * LANE)
            wa, wb = _rope_weights(jnp.stack([w_kr, zero, zero, w_kr], axis=1))
            kr = _norm_mm(hn, None, wa, wb=wb, cos=cos2, sin=sin2, seq_len=seq_len, normed=True, name="mla_in_rope")
            w_qb = (mla_w_qb[j] * MLA_QSCALE).reshape(nq_l, h, MLA_NOPE + MLA_ROPE)
            qn = _norm_mm(lat, mla_q_g[j], w_qb[:, :, :MLA_NOPE].reshape(nq_l, h * MLA_NOPE).astype(BF16),
                          seq_len=seq_len, x_cols=nq_l, x_col_block=0, name="mla_q_nope")
            cosh, sinh = _rope_tables(seq_len, h * MLA_ROPE)
            wa, wb = _rope_weights(w_qb[:, :, MLA_NOPE:])
            qr = _norm_mm(lat, mla_q_g[j], wa, wb=wb, cos=cosh, sin=sinh, seq_len=seq_len, x_cols=nq_l,
                          x_col_block=0, name="mla_q_rope")
            kv = _norm_mm(lat, mla_kv_g[j], mla_w_kvb[j].astype(BF16), seq_len=seq_len, x_cols=nkv_l,
                          x_col_block=2, name="mla_kv")
            o = _mla_attention(qn, qr, kv, kr, **sizes)
            w_out = mla_w_out[j]
        xf = _mm_residual(o, w_out.astype(BF16), xf, gt1, seq_len=seq_len)
        xf = _conv_ffn(xf, norm_g[i, 1], sh2, sc2, ffn_w_up[i].astype(BF16), ffn_conv_w[i], ffn_conv_b[i],
                       ffn_w_down[i].astype(BF16), gt2, final_g, seq_len=seq_len, final_norm=(i == depth - 1))
    return xf.reshape(batch, seq_len, d)
```

```python
import functools
import math

import numpy as np
import jax
import jax.numpy as jnp
from jax import lax
from jax.experimental import pallas as pl
from jax.experimental.pallas import tpu as pltpu

F32 = jnp.float32
BF16 = jnp.bfloat16
EPS = 1e-6
NEG = -1e30

LANE = 128
HALO = 16
VMEM_LIMIT = 56 * 2**20

T5_BUCKETS = 32
T5_MAX_DIST = 128
N_HEADS = 16
HEAD_DIM = 128
NSA_GROUPS = 4
NSA_REP = 4
NSA_CMP_BLOCK = 32
NSA_CMP_STRIDE = 16
NSA_SEL_BLOCK = 64
NSA_TOPN = 16
NSA_WINDOW = 512
FORCED_SCORE = 1e9
DIFF_DIM = 64
MLA_Q_LORA = 768
MLA_KV_LORA = 512
MLA_NOPE = 128
MLA_ROPE = 64
ROPE_THETA = 10000.0
CONV_WIDTH = 3
ATT_T = 256
NSA_CMP_T = 512
DIFF_T = 512
MLA_T = 512
ATT_TQ = 512
ATT_TK = 256
LOG2E = 1.4426950408889634
SB_QSCALE = -(HEAD_DIM ** -0.5)
NSA_QSCALE = HEAD_DIM ** -0.5 * LOG2E
DIFF_QSCALE = DIFF_DIM ** -0.5 * LOG2E
MLA_QSCALE = (MLA_NOPE + MLA_ROPE) ** -0.5 * LOG2E
M_INIT = -1e30
MASKED = -2e30
SB_DEAD = -120.0
T5_MASKED = T5_BUCKETS


def _params(*sem):
    return pltpu.CompilerParams(dimension_semantics=sem, vmem_limit_bytes=VMEM_LIMIT)


def _dot(a, b):
    return jnp.dot(a, b, preferred_element_type=F32)


def _dot_nt(a, b):
    return lax.dot_general(a, b, (((1,), (1,)), ((), ())), preferred_element_type=F32)


def _lane_tile(x, reps):
    return jnp.concatenate([x] * reps, axis=1)


def _split_bf16(x):
    hi = x.astype(BF16)
    lo = (x - hi.astype(F32)).astype(BF16)
    return hi, lo


def _rms(x, g):
    ms = jnp.mean(x * x, axis=-1, keepdims=True)
    return x * lax.rsqrt(ms + EPS) * g


def _ada_kernel(c_ref, w_ref, b_ref, o_ref):
    c = c_ref[...]
    a = c * jax.nn.sigmoid(c)
    a_hi, a_lo = _split_bf16(a)
    w_hi, w_lo = _split_bf16(w_ref[0])
    o_ref[0] = _dot(a_hi, w_hi) + _dot(a_lo, w_hi) + _dot(a_hi, w_lo) + b_ref[0]


def _ada_mod(c, ada_w, ada_b, tn=1024):
    depth, d, n = ada_w.shape
    b = c.shape[0]
    return pl.pallas_call(
        _ada_kernel,
        grid=(depth, n // tn),
        in_specs=[pl.BlockSpec((b, d), lambda l, j: (0, 0)),
                  pl.BlockSpec((1, d, tn), lambda l, j: (l, 0, j)),
                  pl.BlockSpec((1, 1, tn), lambda l, j: (l, 0, j))],
        out_specs=pl.BlockSpec((1, b, tn), lambda l, j: (l, 0, j)),
        out_shape=jax.ShapeDtypeStruct((depth, b, n), F32),
        compiler_params=_params("parallel", "parallel"),
        name="ada_mod",
    )(c, ada_w, ada_b.reshape(depth, 1, n))


def _norm_mm_kernel(*refs, modulated, rope, act, normed):
    it = iter(refs)
    x_ref = next(it)
    g_ref = sh_ref = sc_ref = wb_ref = cos_ref = sin_ref = None
    if not normed:
        g_ref = next(it)
        if modulated:
            sh_ref, sc_ref = next(it), next(it)
    w_ref = next(it)
    if rope:
        wb_ref, cos_ref, sin_ref = next(it), next(it), next(it)
    o_ref = next(it)
    if normed:
        h = x_ref[...]
    else:
        h_ref = next(it)

        @pl.when(pl.program_id(1) == 0)
        def _():
            y = _rms(x_ref[...], g_ref[...])
            if modulated:
                y = y * (1.0 + sc_ref[0]) + sh_ref[0]
            h_ref[...] = y.astype(BF16)

        h = h_ref[...]
    acc = _dot(h, w_ref[...])
    if rope:
        acc = acc * cos_ref[...] + _dot(h, wb_ref[...]) * sin_ref[...]
    if act == "sigmoid":
        acc = jax.nn.sigmoid(acc)
    o_ref[...] = acc.astype(o_ref.dtype)


def _norm_mm(x, g, w, *, seq_len, shift=None, scale=None, wb=None, cos=None, sin=None, act=None, normed=False,
             emit_h=False, out_dtype=BF16, x_cols=None, x_col_block=0, tm=1024, tn=512, name="norm_mm"):
    m = x.shape[0]
    k = x.shape[1] if x_cols is None else x_cols
    n = w.shape[1]
    tn = min(tn, n)
    modulated, rope = shift is not None, wb is not None
    tps = seq_len // tm
    in_specs = [pl.BlockSpec((tm, k), lambda i, j: (i, x_col_block))]
    args = [x]
    if not normed:
        in_specs.append(pl.BlockSpec((1, k), lambda i, j: (0, 0)))
        args.append(g.reshape(1, k))
        if modulated:
            in_specs += [pl.BlockSpec((1, 1, k), lambda i, j: (i // tps, 0, 0))] * 2
            args += [shift, scale]
    in_specs.append(pl.BlockSpec((k, tn), lambda i, j: (0, j)))
    args.append(w)
    if rope:
        in_specs.append(pl.BlockSpec((k, tn), lambda i, j: (0, j)))
        in_specs += [pl.BlockSpec((tm, tn), lambda i, j: (i % tps, j))] * 2
        args += [wb, cos, sin]
    out_specs = [pl.BlockSpec((tm, tn), lambda i, j: (i, j))]
    out_shape = [jax.ShapeDtypeStruct((m, n), out_dtype)]
    scratch = []
    if emit_h:
        out_specs.append(pl.BlockSpec((tm, k), lambda i, j: (i, 0)))
        out_shape.append(jax.ShapeDtypeStruct((m, k), BF16))
    elif not normed:
        scratch.append(pltpu.VMEM((tm, k), BF16))
    out = pl.pallas_call(
        functools.partial(_norm_mm_kernel, modulated=modulated, rope=rope, act=act, normed=normed),
        grid=(m // tm, n // tn),
        in_specs=in_specs,
        out_specs=out_specs,
        out_shape=out_shape,
        scratch_shapes=scratch,
        compiler_params=_params("parallel", "arbitrary"),
        name=name,
    )(*args)
    return out if emit_h else out[0]


def _mm_res_kernel(a_ref, w_ref, x_ref, gate_ref, o_ref):
    o_ref[...] = x_ref[...] + gate_ref[0] * _dot(a_ref[...], w_ref[...])


def _mm_residual(a, w, x, gate, *, seq_len, tm=1024, tn=512, name="mm_residual"):
    m, k = a.shape
    n = w.shape[1]
    tps = seq_len // tm
    return pl.pallas_call(
        _mm_res_kernel,
        grid=(m // tm, n // tn),
        in_specs=[pl.BlockSpec((tm, k), lambda i, j: (i, 0)),
                  pl.BlockSpec((k, tn), lambda i, j: (0, j)),
                  pl.BlockSpec((tm, tn), lambda i, j: (i, j)),
                  pl.BlockSpec((1, 1, tn), lambda i, j: (i // tps, 0, j))],
        out_specs=pl.BlockSpec((tm, tn), lambda i, j: (i, j)),
        out_shape=jax.ShapeDtypeStruct((m, n), F32),
        compiler_params=_params("parallel", "arbitrary"),
        name=name,
    )(a, w, x, gate)


def _ffn_kernel(x_ref, xp_ref, g_ref, sh_ref, sc_ref, wg_ref, wu_ref, cwg_ref, cwu_ref, cbg_ref, cbu_ref,
                wd_ref, gate_ref, fg_ref, o_ref, h_ref, acc_ref, ug_scr, uu_scr, *, tiles_per_seq, final_norm):
    i, j = pl.program_id(0), pl.program_id(1)

    @pl.when(j == 0)
    def _():
        def nm(x):
            return _rms(x, g_ref[...]) * (1.0 + sc_ref[0]) + sh_ref[0]
        h_ref[HALO:, :] = nm(x_ref[...]).astype(BF16)
        keep = jnp.where(i % tiles_per_seq == 0, 0.0, 1.0)
        h_ref[:HALO, :] = (nm(xp_ref[...]) * keep).astype(BF16)
        acc_ref[...] = jnp.zeros_like(acc_ref)

    h = h_ref[...]

    def branch(w_ref, cw_ref, cb_ref, u_scr):
        u_scr[...] = _dot(h, w_ref[...])
        tm = u_scr.shape[0] - HALO
        cw = cw_ref[...]
        return (cw[0:1] * u_scr[pl.ds(HALO - 2, tm), :] + cw[1:2] * u_scr[pl.ds(HALO - 1, tm), :]
                + cw[2:3] * u_scr[pl.ds(HALO, tm), :] + cb_ref[...])

    gt = branch(wg_ref, cwg_ref, cbg_ref, ug_scr)
    up = branch(wu_ref, cwu_ref, cbu_ref, uu_scr)
    a = gt * jax.nn.sigmoid(gt) * up
    acc_ref[...] += _dot(a.astype(BF16), wd_ref[...])

    @pl.when(j == pl.num_programs(1) - 1)
    def _():
        y = x_ref[...] + gate_ref[0] * acc_ref[...]
        if final_norm:
            y = _rms(y, fg_ref[...])
        o_ref[...] = y


def _conv_ffn(x, g, shift, scale, w_up, conv_w, conv_b, w_down, gate, final_g, *, seq_len, final_norm,
              tm=512, tf=512):
    m, d = x.shape
    f = w_down.shape[0]
    nf = f // tf
    tps = seq_len // tm
    hb = tm // HALO
    row = lambda i, j: (i, 0)
    per_batch = lambda i, j: (i // tps, 0, 0)
    return pl.pallas_call(
        functools.partial(_ffn_kernel, tiles_per_seq=tps, final_norm=final_norm),
        grid=(m // tm, nf),
        in_specs=[pl.BlockSpec((tm, d), row),
                  pl.BlockSpec((HALO, d), lambda i, j: (jnp.maximum(i * hb - 1, 0), 0)),
                  pl.BlockSpec((1, d), lambda i, j: (0, 0)),
                  pl.BlockSpec((1, 1, d), per_batch),
                  pl.BlockSpec((1, 1, d), per_batch),
                  pl.BlockSpec((d, tf), lambda i, j: (0, j)),
                  pl.BlockSpec((d, tf), lambda i, j: (0, j + nf)),
                  pl.BlockSpec((CONV_WIDTH, tf), lambda i, j: (0, j)),
                  pl.BlockSpec((CONV_WIDTH, tf), lambda i, j: (0, j + nf)),
                  pl.BlockSpec((1, tf), lambda i, j: (0, j)),
                  pl.BlockSpec((1, tf), lambda i, j: (0, j + nf)),
                  pl.BlockSpec((tf, d), lambda i, j: (j, 0)),
                  pl.BlockSpec((1, 1, d), per_batch),
                  pl.BlockSpec((1, d), lambda i, j: (0, 0))],
        out_specs=pl.BlockSpec((tm, d), row),
        out_shape=jax.ShapeDtypeStruct((m, d), F32),
        scratch_shapes=[pltpu.VMEM((HALO + tm, d), BF16), pltpu.VMEM((tm, d), F32),
                        pltpu.VMEM((HALO + tm, tf), F32), pltpu.VMEM((HALO + tm, tf), F32)],
        compiler_params=_params("parallel", "arbitrary"),
        name="conv_ffn",
    )(x, x, g.reshape(1, d), shift, scale, w_up, w_up, conv_w, conv_w, conv_b.reshape(1, -1),
      conv_b.reshape(1, -1), w_down, gate, final_g.reshape(1, d))


def _t5_bucket_np(dist):
    n = np.maximum(dist, 0)
    max_exact = T5_BUCKETS // 2
    nf = np.maximum(n, 1).astype(np.float64)
    val = np.log(nf / max_exact) / math.log(T5_MAX_DIST / max_exact) * (T5_BUCKETS - max_exact)
    large = max_exact + np.trunc(val + 1e-6).astype(np.int64)
    return np.where(n < max_exact, n, np.minimum(large, T5_BUCKETS - 1)).astype(np.int32)


def _t5_gather_kernel(t5_ref, bkt_ref, o_ref, *, mult):
    h = pl.program_id(0)
    bk = bkt_ref[...]
    acc = jnp.zeros(bk.shape, F32)
    for b in range(T5_BUCKETS):
        acc = jnp.where(bk == b, t5_ref[b, h], acc)
    o_ref[0] = jnp.where(bk == T5_MASKED, MASKED, acc * mult)


def _t5_gather(t5_bias, bkt, tr, mult=1.0):
    rows, cols = bkt.shape
    heads = t5_bias.shape[1]
    return pl.pallas_call(
        functools.partial(_t5_gather_kernel, mult=mult),
        grid=(heads, rows // tr),
        in_specs=[pl.BlockSpec(memory_space=pltpu.SMEM),
                  pl.BlockSpec((tr, cols), lambda h, r: (r, 0))],
        out_specs=pl.BlockSpec((1, tr, cols), lambda h, r: (h, r, 0)),
        out_shape=jax.ShapeDtypeStruct((heads, rows, cols), F32),
        compiler_params=_params("parallel", "parallel"),
        name="t5_gather",
    )(t5_bias, jnp.asarray(bkt))


TAB_DIAG, TAB_SUB, TAB_FAR, TAB_EDGE = 0, 1, 2, 3


def _attention_tables(t5_bias, t):
    i = np.arange(t)[:, None]
    j = np.arange(t)[None, :]
    assert int(_t5_bucket_np(np.array(t + 1))) == T5_BUCKETS - 1
    far = np.full((t, t), T5_BUCKETS - 1, np.int32)
    bkt = np.concatenate([np.where(j <= i, _t5_bucket_np(i - j), T5_MASKED), _t5_bucket_np(t + i - j), far,
                          np.where(j > i, far, T5_MASKED)], axis=0).astype(np.int32)
    return _t5_gather(t5_bias, bkt, tr=t, mult=LOG2E)


def _tile_iota(t):
    return lax.broadcasted_iota(jnp.int32, (t, t), 0), lax.broadcasted_iota(jnp.int32, (t, t), 1)


def _rect_iota(i, tq, start, tk):
    rowg = i * tq + lax.broadcasted_iota(jnp.int32, (tq, tk), 0)
    colg = start + lax.broadcasted_iota(jnp.int32, (tq, tk), 1)
    return rowg, colg


def _osm_reset(m_scr, l_scr, acc_scr):
    m_scr[...] = jnp.full(m_scr.shape, M_INIT, F32)
    l_scr[...] = jnp.zeros(l_scr.shape, F32)
    acc_scr[...] = jnp.zeros(acc_scr.shape, F32)


def _osm_update(t2, rows, slot, p_scr, m_scr, l_scr, acc_scr):
    m_prev = m_scr[rows, :]
    m_new = jnp.maximum(m_prev, jnp.max(t2, axis=-1, keepdims=True))
    p = jnp.exp2(t2 - _lane_tile(m_new, t2.shape[1] // LANE))
    alpha = jnp.exp2(m_prev - m_new)
    l_scr[rows, :] = alpha * l_scr[rows, :]
    m_scr[rows, :] = m_new
    p_scr[slot, rows, :] = p.astype(BF16)
    acc_scr[rows, :] = alpha * acc_scr[rows, :]


def _osm_accum(v, slot, p_scr, l_scr, acc_scr):
    vext = jnp.concatenate([v, jnp.ones(v.shape, v.dtype)], axis=-1)
    half = p_scr.shape[1] // 2
    for part in range(2):
        rows = pl.ds(part * half, half)
        pv = _dot(p_scr[slot, rows, :], vext)
        acc_scr[rows, :] += pv[:, :HEAD_DIM]
        l_scr[rows, :] += pv[:, HEAD_DIM:]


def _pipelined_tiles(n_tiles, scores, update, update_last=None):
    update_last = update if update_last is None else update_last
    last = n_tiles - 1
    n_pairs = last // 2
    scores(0, 0)

    def pair(n, cr):
        scores(2 * n + 1, 1)
        update(2 * n, 0)
        scores(2 * n + 2, 0)
        update(2 * n + 1, 1)
        return cr

    lax.fori_loop(0, n_pairs, pair, 0)

    @pl.when(last % 2 == 0)
    def _():
        update_last(last, 0)

    @pl.when(last % 2 == 1)
    def _():
        scores(last, 1)
        update(last - 1, 0)
        update_last(last, 1)


def _att_scratch(rows, tk, slots=2):
    return [pltpu.VMEM((slots, rows, tk), F32), pltpu.VMEM((slots, rows, tk), BF16), pltpu.VMEM((rows, LANE), F32),
            pltpu.VMEM((rows, LANE), F32), pltpu.VMEM((rows, HEAD_DIM), F32)]


def _sb_kernel(q_ref, k_ref, v_ref, o_ref, s_scr, c_scr, acc_scr, *, tq, tk):
    i = pl.program_id(2)
    q = q_ref[...]
    row, col = _tile_iota(tk)
    upper = jnp.where(row > col, 1.0, 0.0).astype(BF16)
    c_scr[...] = jnp.zeros(c_scr.shape, F32)
    acc_scr[...] = jnp.zeros(acc_scr.shape, F32)
    reps = tk // LANE

    def scores(kb, slot):
        start = pl.multiple_of(kb * tk, tk)
        s_scr[slot] = _dot_nt(q, k_ref[pl.ds(start, tk), :])

    def update(kb, slot, diag):
        start = pl.multiple_of(kb * tk, tk)
        nz = s_scr[slot]
        e = jnp.exp2(jnp.abs(nz) * (-LOG2E))
        lk = jnp.minimum(nz, 0.0) - jnp.log(1.0 + e)
        if diag:
            rowg, colg = _rect_iota(i, tq, start, tk)
            past = colg < rowg
            lk = jnp.where(past, lk, 0.0)
        hi, lo = _split_bf16(lk)
        c = c_scr[...]
        later = _dot(hi, upper) + _dot(lo, upper) + _lane_tile(c, reps)
        a = jnp.exp(lk - nz + later)
        if diag:
            a = jnp.where(past, a, 0.0)
        acc_scr[...] += _dot(a.astype(BF16), v_ref[pl.ds(start, tk), :])
        c_scr[...] = c + jnp.sum(lk, axis=-1, keepdims=True)

    assert tq == 2 * tk
    scores(2 * i + 1, 1)
    scores(2 * i, 0)
    update(2 * i + 1, 1, True)
    scores(jnp.maximum(2 * i - 1, 0), 1)
    update(2 * i, 0, True)

    def more(carry):
        n, c_max = carry
        return (n < i) & (c_max > SB_DEAD)

    def pair(carry):
        n, _ = carry
        kb = 2 * (i - n) - 1
        scores(kb - 1, 0)
        update(kb, 1, False)
        c_mid = jnp.max(c_scr[...])

        @pl.when(c_mid > SB_DEAD)
        def _():
            scores(jnp.maximum(kb - 2, 0), 1)
            update(kb - 1, 0, False)

        return n + 1, jnp.where(c_mid > SB_DEAD, jnp.max(c_scr[...]), c_mid)

    lax.while_loop(more, pair, (0, jnp.max(c_scr[...])))
    o_ref[...] = acc_scr[...].astype(o_ref.dtype)


def _sb_attention(qkv, *, batch, seq_len, tq=ATT_TQ, tk=ATT_TK):
    h = N_HEADS
    nq = seq_len // tq
    return pl.pallas_call(
        functools.partial(_sb_kernel, tq=tq, tk=tk),
        grid=(batch, h, nq),
        in_specs=[pl.BlockSpec((tq, HEAD_DIM), lambda b, hh, i: (b * nq + i, hh)),
                  pl.BlockSpec((seq_len, HEAD_DIM), lambda b, hh, i: (b, h + hh)),
                  pl.BlockSpec((seq_len, HEAD_DIM), lambda b, hh, i: (b, 2 * h + hh))],
        out_specs=pl.BlockSpec((tq, HEAD_DIM), lambda b, hh, i: (b * nq + i, hh)),
        out_shape=jax.ShapeDtypeStruct((batch * seq_len, h * HEAD_DIM), BF16),
        scratch_shapes=[pltpu.VMEM((2, tq, tk), F32), pltpu.VMEM((tq, LANE), F32), pltpu.VMEM((tq, HEAD_DIM), F32)],
        compiler_params=_params("parallel", "parallel", "arbitrary"),
        name="sb_attention",
    )(qkv, qkv, qkv)


def _table_offset(kb, i, t):
    return pl.multiple_of(jnp.where(kb == i, TAB_DIAG * t, jnp.where(kb == i - 1, TAB_SUB * t, TAB_FAR * t)), t)


def _diff_kernel(q_ref, k_ref, v_ref, tab_ref, lam_ref, hg_ref, o_ref, s_scr, p_scr, m_scr, l_scr, acc_scr, *, t,
                 lambda_init):
    i = pl.program_id(2)
    q = q_ref[...]
    lane = lax.broadcasted_iota(jnp.int32, q.shape, 1)
    zero = jnp.zeros_like(q)
    q2 = jnp.concatenate([jnp.where(lane < DIFF_DIM, q, zero), jnp.where(lane >= DIFF_DIM, q, zero)], axis=0)
    _osm_reset(m_scr, l_scr, acc_scr)

    def scores(kb, slot):
        s_scr[slot] = _dot_nt(q2, k_ref[pl.ds(pl.multiple_of(kb * t, t), t), :])

    def update(kb, slot):
        off = _table_offset(kb, i, t)
        for half in range(2):
            rows = pl.ds(half * t, t)
            t2 = s_scr[slot, rows, :] + tab_ref[0, pl.ds(off, t), :]
            _osm_update(t2, rows, slot, p_scr, m_scr, l_scr, acc_scr)
        _osm_accum(v_ref[pl.ds(pl.multiple_of(kb * t, t), t), :], slot, p_scr, l_scr, acc_scr)

    _pipelined_tiles(i + 1, scores, update)

    lam = lam_ref[...]
    lmbda = (jnp.exp(jnp.sum(lam[0:1] * lam[1:2], axis=-1, keepdims=True))
             - jnp.exp(jnp.sum(lam[2:3] * lam[3:4], axis=-1, keepdims=True)) + lambda_init)
    o = acc_scr[...] / l_scr[...]
    o = o[:t] - lmbda * o[t:]
    o_ref[...] = (_rms(o, hg_ref[...]) * (1.0 - lambda_init)).astype(o_ref.dtype)


def _diff_attention(qkv, tables, lam, head_g, *, batch, seq_len, lambda_init, t=DIFF_T):
    h = N_HEADS
    nq = seq_len // t
    return pl.pallas_call(
        functools.partial(_diff_kernel, t=t, lambda_init=lambda_init),
        grid=(batch, h, nq),
        in_specs=[pl.BlockSpec((t, HEAD_DIM), lambda b, hh, i: (b * nq + i, hh)),
                  pl.BlockSpec((seq_len, HEAD_DIM), lambda b, hh, i: (b, h + hh)),
                  pl.BlockSpec((seq_len, HEAD_DIM), lambda b, hh, i: (b, 2 * h + hh)),
                  pl.BlockSpec((1, 4 * t, t), lambda b, hh, i: (hh, 0, 0)),
                  pl.BlockSpec((4, DIFF_DIM), lambda b, hh, i: (0, 0)),
                  pl.BlockSpec((1, HEAD_DIM), lambda b, hh, i: (0, 0))],
        out_specs=pl.BlockSpec((t, HEAD_DIM), lambda b, hh, i: (b * nq + i, hh)),
        out_shape=jax.ShapeDtypeStruct((batch * seq_len, h * HEAD_DIM), BF16),
        scratch_shapes=_att_scratch(2 * t, t),
        compiler_params=_params("parallel", "parallel", "arbitrary"),
        name="diff_attention",
    )(qkv, qkv, qkv, tables, lam, head_g.reshape(1, HEAD_DIM))


def _mla_kernel(qn_ref, qr_ref, kn_ref, kr_ref, v_ref, o_ref, s_scr, p_scr, m_scr, l_scr, acc_scr, *, tq, tk):
    i = pl.program_id(2)
    q = jnp.concatenate([qn_ref[...], qr_ref[...]], axis=-1)
    _osm_reset(m_scr, l_scr, acc_scr)

    def scores(kb, slot):
        start = pl.multiple_of(kb * tk, tk)
        k = jnp.concatenate([kn_ref[pl.ds(start, tk), :], kr_ref[pl.ds(start, tk), :]], axis=-1)
        s_scr[slot] = _dot_nt(q, k)

    def update(kb, slot, mask):
        start = pl.multiple_of(kb * tk, tk)
        t2 = s_scr[slot]
        if mask:
            rowg, colg = _rect_iota(i, tq, start, tk)
            t2 = jnp.where(colg <= rowg, t2, MASKED)
        _osm_update(t2, pl.ds(0, tq), slot, p_scr, m_scr, l_scr, acc_scr)
        _osm_accum(v_ref[pl.ds(start, tk), :], slot, p_scr, l_scr, acc_scr)

    assert tq == tk
    _pipelined_tiles(i + 1, scores, lambda kb, slot: update(kb, slot, False), lambda kb, slot: update(kb, slot, True))
    o_ref[...] = (acc_scr[...] / l_scr[...]).astype(o_ref.dtype)


def _mla_attention(qn, qr, kv, kr, *, batch, seq_len, tq=MLA_T, tk=MLA_T):
    h = N_HEADS
    nq = seq_len // tq
    qspec = pl.BlockSpec((tq, HEAD_DIM), lambda b, hh, i: (b * nq + i, hh))
    return pl.pallas_call(
        functools.partial(_mla_kernel, tq=tq, tk=tk),
        grid=(batch, h, nq),
        in_specs=[qspec,
                  pl.BlockSpec((tq, LANE), lambda b, hh, i: (b * nq + i, hh // 2)),
                  pl.BlockSpec((seq_len, HEAD_DIM), lambda b, hh, i: (b, 2 * hh)),
                  pl.BlockSpec((seq_len, LANE), lambda b, hh, i: (b, hh % 2)),
                  pl.BlockSpec((seq_len, HEAD_DIM), lambda b, hh, i: (b, 2 * hh + 1))],
        out_specs=qspec,
        out_shape=jax.ShapeDtypeStruct((batch * seq_len, h * HEAD_DIM), BF16),
        scratch_shapes=_att_scratch(tq, tk),
        compiler_params=_params("parallel", "parallel", "arbitrary"),
        name="mla_attention",
    )(qn, qr, kv, kr, kv)


def _compress_kernel(raw_ref, pe_ref, w1_ref, w2_ref, o_ref, *, n_slots):
    half = NSA_CMP_BLOCK // 2
    p1 = jnp.zeros((n_slots, HEAD_DIM), F32)
    p2 = jnp.zeros((n_slots, HEAD_DIM), F32)
    for l in range(half):
        a = raw_ref[pl.ds(l, n_slots, stride=NSA_CMP_STRIDE), :]
        p1 = p1 + _dot((a + pe_ref[0, l:l + 1, :]).astype(BF16), w1_ref[0, l])
        p2 = p2 + _dot((a + pe_ref[0, half + l:half + l + 1, :]).astype(BF16), w1_ref[0, half + l])
    pre = p1 + pltpu.roll(p2, n_slots - 1, 0)
    hid = pre * jax.nn.sigmoid(pre)
    o_ref[0, 0, 0] = _dot(hid.astype(BF16), w2_ref[0]).astype(o_ref.dtype)


def _nsa_compress(raw, pe, w1, w2, *, batch, seq_len):
    g = NSA_GROUPS
    n_slots = seq_len // NSA_CMP_STRIDE
    return pl.pallas_call(
        functools.partial(_compress_kernel, n_slots=n_slots),
        grid=(batch, 2, g),
        in_specs=[pl.BlockSpec((seq_len, HEAD_DIM), lambda b, kv, gg: (b, kv * g + gg)),
                  pl.BlockSpec((1, NSA_CMP_BLOCK, HEAD_DIM), lambda b, kv, gg: (kv, 0, 0)),
                  pl.BlockSpec((1, NSA_CMP_BLOCK, HEAD_DIM, HEAD_DIM), lambda b, kv, gg: (kv, 0, 0, 0)),
                  pl.BlockSpec((1, HEAD_DIM, HEAD_DIM), lambda b, kv, gg: (kv, 0, 0))],
        out_specs=pl.BlockSpec((1, 1, 1, n_slots, HEAD_DIM), lambda b, kv, gg: (b, kv, gg, 0, 0)),
        out_shape=jax.ShapeDtypeStruct((batch, 2, g, n_slots, HEAD_DIM), BF16),
        compiler_params=_params("parallel", "parallel", "parallel"),
        name="nsa_compress",
    )(raw, pe, w1, w2)


def _nsa_cmp_kernel(q_ref, kc_ref, vc_ref, bias_ref, gates_ref, ovt_ref, oc_ref, sel_ref, *, t, n_slots, n_sel,
                    n_top):
    i = pl.program_id(2)
    kc = kc_ref[0, 0, 0]
    vc = vc_ref[0, 0, 0]
    qpos = i * t + lax.broadcasted_iota(jnp.int32, (t, n_slots), 0)
    cmp_end = NSA_CMP_STRIDE * lax.broadcasted_iota(jnp.int32, (t, n_slots), 1) + (NSA_CMP_BLOCK - 1)
    valid = cmp_end <= qpos
    gates = gates_ref[...]
    psum = jnp.zeros((t, n_slots), F32)
    for r in range(NSA_REP):
        q = q_ref[:, r * HEAD_DIM:(r + 1) * HEAD_DIM]
        s = jnp.where(valid, _dot_nt(q, kc) + bias_ref[r], NEG)
        m = jnp.max(s, axis=-1, keepdims=True)
        p = jnp.where(valid, jnp.exp2(s - m), 0.0)
        p = p / jnp.maximum(jnp.sum(p, axis=-1, keepdims=True), 1e-30)
        psum = psum + p
        oc_ref[:, r * HEAD_DIM:(r + 1) * HEAD_DIM] = gates[:, r:r + 1] * _dot(p.astype(BF16), vc)

    hi, lo = _split_bf16(psum)
    ovt = ovt_ref[...]
    imp = _dot_nt(ovt, hi) + _dot_nt(ovt, lo)
    blk = lax.broadcasted_iota(jnp.int32, (n_sel, t), 0)
    tpos = i * t + lax.broadcasted_iota(jnp.int32, (n_sel, t), 1)
    cur = tpos // NSA_SEL_BLOCK
    forced = (blk == 0) | (blk == cur) | (blk == cur - 1)
    score = jnp.where(blk * NSA_SEL_BLOCK <= tpos, jnp.where(forced, FORCED_SCORE, imp), -1.0)
    rank = jnp.zeros((n_sel, t), F32)
    for mm in range(n_sel):
        sm = score[mm:mm + 1, :]
        ahead = (sm > score) | ((sm == score) & (blk > mm))
        rank = rank + jnp.where(ahead, 1.0, 0.0)
    sel_t = jnp.where(rank < n_top, 1.0, 0.0).astype(BF16)
    row, col = _tile_iota(t)
    eye = jnp.where(row == col, 1.0, 0.0).astype(BF16)
    sel_ref[0, 0] = _dot_nt(eye, sel_t).astype(sel_ref.dtype)


def _nsa_cmp_attention(q_all, kvc, bias_c, gates, *, batch, seq_len, t=NSA_CMP_T):
    g = NSA_GROUPS
    nq = seq_len // t
    n_slots = seq_len // NSA_CMP_STRIDE
    n_sel = seq_len // NSA_SEL_BLOCK
    c0 = NSA_CMP_STRIDE * np.arange(n_slots)[:, None]
    s0 = NSA_SEL_BLOCK * np.arange(n_sel)[None, :]
    overlap = (c0 < s0 + NSA_SEL_BLOCK) & (c0 + NSA_CMP_BLOCK > s0)
    ovt = jnp.asarray(overlap.T.astype(np.float32), dtype=BF16)
    gw = NSA_REP * HEAD_DIM
    return pl.pallas_call(
        functools.partial(_nsa_cmp_kernel, t=t, n_slots=n_slots, n_sel=n_sel, n_top=min(NSA_TOPN, n_sel)),
        grid=(batch, g, nq),
        in_specs=[pl.BlockSpec((t, gw), lambda b, gg, i: (b * nq + i, gg)),
                  pl.BlockSpec((1, 1, 1, n_slots, HEAD_DIM), lambda b, gg, i: (b, 0, gg, 0, 0)),
                  pl.BlockSpec((1, 1, 1, n_slots, HEAD_DIM), lambda b, gg, i: (b, 1, gg, 0, 0)),
                  pl.BlockSpec((NSA_REP, t, n_slots), lambda b, gg, i: (gg, i, 0)),
                  pl.BlockSpec((t, LANE), lambda b, gg, i: (b * nq + i, gg)),
                  pl.BlockSpec((n_sel, n_slots), lambda b, gg, i: (0, 0))],
        out_specs=[pl.BlockSpec((t, gw), lambda b, gg, i: (b * nq + i, gg)),
                   pl.BlockSpec((1, 1, t, n_sel), lambda b, gg, i: (b, gg, i, 0))],
        out_shape=[jax.ShapeDtypeStruct((batch * seq_len, g * gw), F32),
                   jax.ShapeDtypeStruct((batch, g, seq_len, n_sel), BF16)],
        compiler_params=_params("parallel", "parallel", "arbitrary"),
        name="nsa_cmp_attention",
    )(q_all, kvc, kvc, bias_c, gates, ovt)


def _nsa_main_kernel(q_ref, ks_ref, vs_ref, kw_ref, vw_ref, sel_ref, tab_ref, gates_ref, oc_ref, o_ref,
                     s_scr, p_scr, m_scr, l_scr, acc_scr, os_scr, mk_scr, *, t):
    i = pl.program_id(2)
    rep = NSA_REP
    q = jnp.concatenate([q_ref[:, r * HEAD_DIM:(r + 1) * HEAD_DIM] for r in range(rep)], axis=0)

    sel = sel_ref[0, 0]
    n_sel = sel.shape[1]
    blk_row = lax.broadcasted_iota(jnp.int32, (n_sel, t), 0)
    key_col = lax.broadcasted_iota(jnp.int32, (n_sel, t), 1)
    _osm_reset(m_scr, l_scr, acc_scr)

    def sel_scores(kb, slot):
        s_scr[slot] = _dot_nt(q, ks_ref[pl.ds(pl.multiple_of(kb * t, t), t), :])

    def sel_update(kb, slot):
        start = pl.multiple_of(kb * t, t)
        expand = jnp.where((start + key_col) // NSA_SEL_BLOCK == blk_row, 1.0, 0.0).astype(BF16)
        mk_scr[...] = _dot(sel, expand)
        off = _table_offset(kb, i, t)
        for r in range(rep):
            rows = pl.ds(r * t, t)
            t2 = s_scr[slot, rows, :] + tab_ref[r, pl.ds(off, t), :]
            t2 = jnp.where(mk_scr[...] > 0.5, t2, MASKED)
            _osm_update(t2, rows, slot, p_scr, m_scr, l_scr, acc_scr)
        _osm_accum(vs_ref[pl.ds(start, t), :], slot, p_scr, l_scr, acc_scr)

    _pipelined_tiles(i + 1, sel_scores, sel_update)
    os_scr[...] = acc_scr[...] / l_scr[...]

    n_back = NSA_WINDOW // t
    assert n_back == 2
    _osm_reset(m_scr, l_scr, acc_scr)

    def win_scores(kb, slot):
        s_scr[slot] = _dot_nt(q, kw_ref[pl.ds(pl.multiple_of(kb * t, t), t), :])

    def win_update(kb, slot, region):
        for r in range(rep):
            rows = pl.ds(r * t, t)
            t2 = s_scr[slot, rows, :] + tab_ref[r, pl.ds(region * t, t), :]
            _osm_update(t2, rows, slot, p_scr, m_scr, l_scr, acc_scr)
        _osm_accum(vw_ref[pl.ds(pl.multiple_of(kb * t, t), t), :], slot, p_scr, l_scr, acc_scr)

    def window(n_tiles):
        regions = (TAB_DIAG, TAB_SUB, TAB_EDGE)
        for back in range(n_tiles):
            win_scores(i - back, back)
        for back in reversed(range(n_tiles)):
            win_update(i - back, back, regions[back])

    pl.when(i >= 2)(lambda: window(3))
    pl.when(i == 1)(lambda: window(2))
    pl.when(i == 0)(lambda: window(1))
    o_w = acc_scr[...] / l_scr[...]
    o_s = os_scr[...]

    gates = gates_ref[...]
    for r in range(rep):
        rows = slice(r * t, (r + 1) * t)
        cols = slice(r * HEAD_DIM, (r + 1) * HEAD_DIM)
        o = oc_ref[:, cols] + gates[:, rep + r:rep + r + 1] * o_s[rows] + gates[:, 2 * rep + r:2 * rep + r + 1] * o_w[rows]
        o_ref[:, cols] = o.astype(o_ref.dtype)


def _nsa_main_attention(qkv, sel, tables, gates, oc, *, batch, seq_len, t=ATT_T):
    g = NSA_GROUPS
    nq = seq_len // t
    n_sel = seq_len // NSA_SEL_BLOCK
    gw = NSA_REP * HEAD_DIM
    qb = N_HEADS
    kv = lambda which: pl.BlockSpec((seq_len, HEAD_DIM), lambda b, gg, i: (b, qb + which * g + gg))
    tile = pl.BlockSpec((t, gw), lambda b, gg, i: (b * nq + i, gg))
    rows = NSA_REP * t
    return pl.pallas_call(
        functools.partial(_nsa_main_kernel, t=t),
        grid=(batch, g, nq),
        in_specs=[tile, kv(0), kv(1), kv(2), kv(3),
                  pl.BlockSpec((1, 1, t, n_sel), lambda b, gg, i: (b, gg, i, 0)),
                  pl.BlockSpec((NSA_REP, 4 * t, t), lambda b, gg, i: (gg, 0, 0)),
                  pl.BlockSpec((t, LANE), lambda b, gg, i: (b * nq + i, gg)),
                  tile],
        out_specs=tile,
        out_shape=jax.ShapeDtypeStruct((batch * seq_len, g * gw), BF16),
        scratch_shapes=_att_scratch(rows, t, slots=3) + [pltpu.VMEM((rows, HEAD_DIM), F32), pltpu.VMEM((t, t), F32)],
        compiler_params=_params("parallel", "parallel", "arbitrary"),
        name="nsa_main_attention",
    )(qkv, qkv, qkv, qkv, qkv, sel, tables, gates, oc)


def _rope_tables(seq_len, width):
    half = MLA_ROPE // 2
    inv = np.power(ROPE_THETA, -np.arange(half, dtype=np.float32) / half).astype(np.float32)
    ang = np.arange(seq_len, dtype=np.float32)[:, None] * inv[None, :]
    reps = width // half
    return jnp.asarray(np.tile(np.cos(ang), (1, reps))), jnp.asarray(np.tile(np.sin(ang), (1, reps)))


def _rope_weights(w):
    k, n, _ = w.shape
    half = MLA_ROPE // 2
    wb = jnp.concatenate([-w[..., half:], w[..., :half]], axis=-1)
    return w.reshape(k, n * MLA_ROPE).astype(BF16), wb.reshape(k, n * MLA_ROPE).astype(BF16)


def kernel(x, c, t5_bias, ada_w, ada_b, norm_g, final_g, ffn_w_up, ffn_conv_w, ffn_conv_b, ffn_w_down, sb_w_in, sb_w_out, nsa_w_in, nsa_cmp_pe, nsa_cmp_w1, nsa_cmp_w2, nsa_w_out, diff_w_in, diff_lambda, diff_head_g, diff_w_out, mla_w_in, mla_q_g, mla_w_qb, mla_kv_g, mla_w_kvb, mla_w_out):
    batch, seq_len, d = x.shape
    depth = ada_w.shape[0]
    h, dh, g = N_HEADS, HEAD_DIM, NSA_GROUPS
    sizes = dict(batch=batch, seq_len=seq_len)

    mod = _ada_mod(c, ada_w, ada_b)
    tables = _attention_tables(t5_bias, ATT_T)

    xf = x.reshape(batch * seq_len, d)
    for i in range(depth):
        mixer, j = i % 4, i // 4
        sh1, sc1, gt1, sh2, sc2, gt2 = (mod[i, :, n * d:(n + 1) * d].reshape(batch, 1, d) for n in range(6))
        nm = functools.partial(_norm_mm, xf, norm_g[i, 0], seq_len=seq_len, shift=sh1, scale=sc1)
        if mixer == 0:
            w_in = sb_w_in[j]
            w_in = jnp.concatenate([w_in[:, :h * dh] * SB_QSCALE, w_in[:, h * dh:]], axis=1)
            qkv = nm(w_in.astype(BF16), name="sb_in")
            o = _sb_attention(qkv, **sizes)
            w_out = sb_w_out[j]
        elif mixer == 1:
            w_in = nsa_w_in[j]
            n_q, n_kv = h * dh, g * dh
            w_att = jnp.concatenate([w_in[:, :n_q] * NSA_QSCALE, w_in[:, n_q + 2 * n_kv:n_q + 6 * n_kv]], axis=1)
            w_cmp = w_in[:, n_q:n_q + 2 * n_kv]
            w_g = w_in[:, n_q + 6 * n_kv:].reshape(d, 3, g, NSA_REP).transpose(0, 2, 1, 3).reshape(d, g, 3 * NSA_REP)
            w_g = jnp.pad(w_g, ((0, 0), (0, 0), (0, LANE - 3 * NSA_REP))).reshape(d, g * LANE)
            qkv, hn = nm(w_att.astype(BF16), emit_h=True, name="nsa_in")
            hmm = functools.partial(_norm_mm, hn, None, seq_len=seq_len, normed=True)
            raw = hmm(w_cmp.astype(BF16), out_dtype=F32, name="nsa_in_cmp")
            gates = hmm(w_g.astype(BF16), out_dtype=F32, act="sigmoid", name="nsa_in_gates")
            kvc = _nsa_compress(raw, nsa_cmp_pe[j], nsa_cmp_w1[j].reshape(2, NSA_CMP_BLOCK, dh, dh).astype(BF16),
                                nsa_cmp_w2[j].astype(BF16), **sizes)
            n_slots = seq_len // NSA_CMP_STRIDE
            dist_c = np.arange(seq_len)[:, None] - (NSA_CMP_STRIDE * np.arange(n_slots)[None, :] + NSA_CMP_BLOCK - 1)
            bias_c = _t5_gather(t5_bias, _t5_bucket_np(dist_c), tr=min(seq_len, 512), mult=LOG2E)
            oc, sel = _nsa_cmp_attention(qkv, kvc, bias_c, gates, **sizes)
            o = _nsa_main_attention(qkv, sel, tables, gates, oc, **sizes)
            w_out = nsa_w_out[j]
        elif mixer == 2:
            lambda_init = 0.8 - 0.6 * math.exp(-0.3 * i)
            w_in = diff_w_in[j]
            w_in = jnp.concatenate([w_in[:, :h * dh] * DIFF_QSCALE, w_in[:, h * dh:]], axis=1)
            qkv = nm(w_in.astype(BF16), name="diff_in")
            o = _diff_attention(qkv, _attention_tables(t5_bias, DIFF_T), diff_lambda[j], diff_head_g[j],
                                lambda_init=lambda_init, **sizes)
            w_out = diff_w_out[j]
        else:
            w_in = mla_w_in[j]
            nq_l, nkv_l = MLA_Q_LORA, MLA_KV_LORA
            w_lat = jnp.concatenate([w_in[:, :nq_l], jnp.zeros((d, 2 * nkv_l - nq_l), w_in.dtype),
                                     w_in[:, nq_l:nq_l + nkv_l]], axis=1)
            lat, hn = nm(w_lat.astype(BF16), out_dtype=F32, emit_h=True, name="mla_in")
            w_kr = w_in[:, nq_l + nkv_l:]
            zero = jnp.zeros_like(w_kr)
            cos2, sin2 = _rope_tables(seq_len, 2 * LANE)
            wa, wb = _rope_weights(jnp.stack([w_kr, zero, zero, w_kr], axis=1))
            kr = _norm_mm(hn, None, wa, wb=wb, cos=cos2, sin=sin2, seq_len=seq_len, normed=True, name="mla_in_rope")
            w_qb = (mla_w_qb[j] * MLA_QSCALE).reshape(nq_l, h, MLA_NOPE + MLA_ROPE)
            qn = _norm_mm(lat, mla_q_g[j], w_qb[:, :, :MLA_NOPE].reshape(nq_l, h * MLA_NOPE).astype(BF16),
                          seq_len=seq_len, x_cols=nq_l, x_col_block=0, name="mla_q_nope")
            cosh, sinh = _rope_tables(seq_len, h * MLA_ROPE)
            wa, wb = _rope_weights(w_qb[:, :, MLA_NOPE:])
            qr = _norm_mm(lat, mla_q_g[j], wa, wb=wb, cos=cosh, sin=sinh, seq_len=seq_len, x_cols=nq_l,
                          x_col_block=0, name="mla_q_rope")
            kv = _norm_mm(lat, mla_kv_g[j], mla_w_kvb[j].astype(BF16), seq_len=seq_len, x_cols=nkv_l,
                          x_col_block=2, name="mla_kv")
            o = _mla_attention(qn, qr, kv, kr, **sizes)
            w_out = mla_w_out[j]
        xf = _mm_residual(o, w_out.astype(BF16), xf, gt1, seq_len=seq_len)
        xf = _conv_ffn(xf, norm_g[i, 1], sh2, sc2, ffn_w_up[i].astype(BF16), ffn_conv_w[i], ffn_conv_b[i],
                       ffn_w_down[i].astype(BF16), gt2, final_g, seq_len=seq_len, final_norm=(i == depth - 1))
    return xf.reshape(batch, seq_len, d)
```

```python
import functools
import math

import numpy as np
import jax
import jax.numpy as jnp
from jax import lax
from jax.experimental import pallas as pl
from jax.experimental.pallas import tpu as pltpu

F32 = jnp.float32
BF16 = jnp.bfloat16
EPS = 1e-6
NEG = -1e30

LANE = 128
HALO = 16
VMEM_LIMIT = 56 * 2**20

T5_BUCKETS = 32
T5_MAX_DIST = 128
N_HEADS = 16
HEAD_DIM = 128
NSA_GROUPS = 4
NSA_REP = 4
NSA_CMP_BLOCK = 32
NSA_CMP_STRIDE = 16
NSA_SEL_BLOCK = 64
NSA_TOPN = 16
NSA_WINDOW = 512
FORCED_SCORE = 1e9
DIFF_DIM = 64
MLA_Q_LORA = 768
MLA_KV_LORA = 512
MLA_NOPE = 128
MLA_ROPE = 64
ROPE_THETA = 10000.0
CONV_WIDTH = 3
ATT_T = 256
NSA_CMP_T = 512
DIFF_T = 512
MLA_T = 512
ATT_TQ = 512
ATT_TK = 256
LOG2E = 1.4426950408889634
SB_QSCALE = -(HEAD_DIM ** -0.5)
NSA_QSCALE = HEAD_DIM ** -0.5 * LOG2E
DIFF_QSCALE = DIFF_DIM ** -0.5 * LOG2E
MLA_QSCALE = (MLA_NOPE + MLA_ROPE) ** -0.5 * LOG2E
M_INIT = -1e30
MASKED = -2e30
SB_DEAD = -120.0
T5_MASKED = T5_BUCKETS


def _params(*sem):
    return pltpu.CompilerParams(dimension_semantics=sem, vmem_limit_bytes=VMEM_LIMIT)


def _dot(a, b):
    return jnp.dot(a, b, preferred_element_type=F32)


def _dot_nt(a, b):
    return lax.dot_general(a, b, (((1,), (1,)), ((), ())), preferred_element_type=F32)


def _lane_tile(x, reps):
    return jnp.concatenate([x] * reps, axis=1)


def _split_bf16(x):
    hi = x.astype(BF16)
    lo = (x - hi.astype(F32)).astype(BF16)
    return hi, lo


def _rms(x, g):
    ms = jnp.mean(x * x, axis=-1, keepdims=True)
    return x * lax.rsqrt(ms + EPS) * g


def _ada_kernel(c_ref, w_ref, b_ref, o_ref):
    c = c_ref[...]
    a = c * jax.nn.sigmoid(c)
    a_hi, a_lo = _split_bf16(a)
    w_hi, w_lo = _split_bf16(w_ref[0])
    o_ref[0] = _dot(a_hi, w_hi) + _dot(a_lo, w_hi) + _dot(a_hi, w_lo) + b_ref[0]


def _ada_mod(c, ada_w, ada_b, tn=1024):
    depth, d, n = ada_w.shape
    b = c.shape[0]
    return pl.pallas_call(
        _ada_kernel,
        grid=(depth, n // tn),
        in_specs=[pl.BlockSpec((b, d), lambda l, j: (0, 0)),
                  pl.BlockSpec((1, d, tn), lambda l, j: (l, 0, j)),
                  pl.BlockSpec((1, 1, tn), lambda l, j: (l, 0, j))],
        out_specs=pl.BlockSpec((1, b, tn), lambda l, j: (l, 0, j)),
        out_shape=jax.ShapeDtypeStruct((depth, b, n), F32),
        compiler_params=_params("parallel", "parallel"),
        name="ada_mod",
    )(c, ada_w, ada_b.reshape(depth, 1, n))


def _norm_mm_kernel(*refs, modulated, rope, act, normed):
    it = iter(refs)
    x_ref = next(it)
    g_ref = sh_ref = sc_ref = wb_ref = cos_ref = sin_ref = None
    if not normed:
        g_ref = next(it)
        if modulated:
            sh_ref, sc_ref = next(it), next(it)
    w_ref = next(it)
    if rope:
        wb_ref, cos_ref, sin_ref = next(it), next(it), next(it)
    o_ref = next(it)
    if normed:
        h = x_ref[...]
    else:
        h_ref = next(it)

        @pl.when(pl.program_id(1) == 0)
        def _():
            y = _rms(x_ref[...], g_ref[...])
            if modulated:
                y = y * (1.0 + sc_ref[0]) + sh_ref[0]
            h_ref[...] = y.astype(BF16)

        h = h_ref[...]
    acc = _dot(h, w_ref[...])
    if rope:
        acc = acc * cos_ref[...] + _dot(h, wb_ref[...]) * sin_ref[...]
    if act == "sigmoid":
        acc = jax.nn.sigmoid(acc)
    o_ref[...] = acc.astype(o_ref.dtype)


def _norm_mm(x, g, w, *, seq_len, shift=None, scale=None, wb=None, cos=None, sin=None, act=None, normed=False,
             emit_h=False, out_dtype=BF16, x_cols=None, x_col_block=0, tm=1024, tn=1024, name="norm_mm"):
    m = x.shape[0]
    k = x.shape[1] if x_cols is None else x_cols
    n = w.shape[1]
    while n % tn:
        tn //= 2
    modulated, rope = shift is not None, wb is not None
    tps = seq_len // tm
    in_specs = [pl.BlockSpec((tm, k), lambda i, j: (i, x_col_block))]
    args = [x]
    if not normed:
        in_specs.append(pl.BlockSpec((1, k), lambda i, j: (0, 0)))
        args.append(g.reshape(1, k))
        if modulated:
            in_specs += [pl.BlockSpec((1, 1, k), lambda i, j: (i // tps, 0, 0))] * 2
            args += [shift, scale]
    in_specs.append(pl.BlockSpec((k, tn), lambda i, j: (0, j)))
    args.append(w)
    if rope:
        in_specs.append(pl.BlockSpec((k, tn), lambda i, j: (0, j)))
        in_specs += [pl.BlockSpec((tm, tn), lambda i, j: (i % tps, j))] * 2
        args += [wb, cos, sin]
    out_specs = [pl.BlockSpec((tm, tn), lambda i, j: (i, j))]
    out_shape = [jax.ShapeDtypeStruct((m, n), out_dtype)]
    scratch = []
    if emit_h:
        out_specs.append(pl.BlockSpec((tm, k), lambda i, j: (i, 0)))
        out_shape.append(jax.ShapeDtypeStruct((m, k), BF16))
    elif not normed:
        scratch.append(pltpu.VMEM((tm, k), BF16))
    out = pl.pallas_call(
        functools.partial(_norm_mm_kernel, modulated=modulated, rope=rope, act=act, normed=normed),
        grid=(m // tm, n // tn),
        in_specs=in_specs,
        out_specs=out_specs,
        out_shape=out_shape,
        scratch_shapes=scratch,
        compiler_params=_params("parallel", "arbitrary"),
        name=name,
    )(*args)
    return out if emit_h else out[0]


def _mm_res_kernel(a_ref, w_ref, x_ref, gate_ref, o_ref):
    o_ref[...] = x_ref[...] + gate_ref[0] * _dot(a_ref[...], w_ref[...])


def _mm_residual(a, w, x, gate, *, seq_len, tm=1024, tn=1024, name="mm_residual"):
    m, k = a.shape
    n = w.shape[1]
    tps = seq_len // tm
    return pl.pallas_call(
        _mm_res_kernel,
        grid=(m // tm, n // tn),
        in_specs=[pl.BlockSpec((tm, k), lambda i, j: (i, 0)),
                  pl.BlockSpec((k, tn), lambda i, j: (0, j)),
                  pl.BlockSpec((tm, tn), lambda i, j: (i, j)),
                  pl.BlockSpec((1, 1, tn), lambda i, j: (i // tps, 0, j))],
        out_specs=pl.BlockSpec((tm, tn), lambda i, j: (i, j)),
        out_shape=jax.ShapeDtypeStruct((m, n), F32),
        compiler_params=_params("parallel", "arbitrary"),
        name=name,
    )(a, w, x, gate)


def _ffn_kernel(x_ref, xp_ref, g_ref, sh_ref, sc_ref, wg_ref, wu_ref, cwg_ref, cwu_ref, cbg_ref, cbu_ref,
                wd_ref, gate_ref, fg_ref, o_ref, h_ref, acc_ref, ug_scr, uu_scr, *, tiles_per_seq, final_norm):
    i, j = pl.program_id(0), pl.program_id(1)

    @pl.when(j == 0)
    def _():
        def nm(x):
            return _rms(x, g_ref[...]) * (1.0 + sc_ref[0]) + sh_ref[0]
        h_ref[HALO:, :] = nm(x_ref[...]).astype(BF16)
        keep = jnp.where(i % tiles_per_seq == 0, 0.0, 1.0)
        h_ref[:HALO, :] = (nm(xp_ref[...]) * keep).astype(BF16)
        acc_ref[...] = jnp.zeros_like(acc_ref)

    h = h_ref[...]

    def branch(w_ref, cw_ref, cb_ref, u_scr):
        u_scr[...] = _dot(h, w_ref[...])
        tm = u_scr.shape[0] - HALO
        cw = cw_ref[...]
        return (cw[0:1] * u_scr[pl.ds(HALO - 2, tm), :] + cw[1:2] * u_scr[pl.ds(HALO - 1, tm), :]
                + cw[2:3] * u_scr[pl.ds(HALO, tm), :] + cb_ref[...])

    gt = branch(wg_ref, cwg_ref, cbg_ref, ug_scr)
    up = branch(wu_ref, cwu_ref, cbu_ref, uu_scr)
    a = gt * jax.nn.sigmoid(gt) * up
    acc_ref[...] += _dot(a.astype(BF16), wd_ref[...])

    @pl.when(j == pl.num_programs(1) - 1)
    def _():
        y = x_ref[...] + gate_ref[0] * acc_ref[...]
        if final_norm:
            y = _rms(y, fg_ref[...])
        o_ref[...] = y


def _conv_ffn(x, g, shift, scale, w_up, conv_w, conv_b, w_down, gate, final_g, *, seq_len, final_norm,
              tm=512, tf=512):
    m, d = x.shape
    f = w_down.shape[0]
    nf = f // tf
    tps = seq_len // tm
    hb = tm // HALO
    row = lambda i, j: (i, 0)
    per_batch = lambda i, j: (i // tps, 0, 0)
    return pl.pallas_call(
        functools.partial(_ffn_kernel, tiles_per_seq=tps, final_norm=final_norm),
        grid=(m // tm, nf),
        in_specs=[pl.BlockSpec((tm, d), row),
                  pl.BlockSpec((HALO, d), lambda i, j: (jnp.maximum(i * hb - 1, 0), 0)),
                  pl.BlockSpec((1, d), lambda i, j: (0, 0)),
                  pl.BlockSpec((1, 1, d), per_batch),
                  pl.BlockSpec((1, 1, d), per_batch),
                  pl.BlockSpec((d, tf), lambda i, j: (0, j)),
                  pl.BlockSpec((d, tf), lambda i, j: (0, j + nf)),
                  pl.BlockSpec((CONV_WIDTH, tf), lambda i, j: (0, j)),
                  pl.BlockSpec((CONV_WIDTH, tf), lambda i, j: (0, j + nf)),
                  pl.BlockSpec((1, tf), lambda i, j: (0, j)),
                  pl.BlockSpec((1, tf), lambda i, j: (0, j + nf)),
                  pl.BlockSpec((tf, d), lambda i, j: (j, 0)),
                  pl.BlockSpec((1, 1, d), per_batch),
                  pl.BlockSpec((1, d), lambda i, j: (0, 0))],
        out_specs=pl.BlockSpec((tm, d), row),
        out_shape=jax.ShapeDtypeStruct((m, d), F32),
        scratch_shapes=[pltpu.VMEM((HALO + tm, d), BF16), pltpu.VMEM((tm, d), F32),
                        pltpu.VMEM((HALO + tm, tf), F32), pltpu.VMEM((HALO + tm, tf), F32)],
        compiler_params=_params("parallel", "arbitrary"),
        name="conv_ffn",
    )(x, x, g.reshape(1, d), shift, scale, w_up, w_up, conv_w, conv_w, conv_b.reshape(1, -1),
      conv_b.reshape(1, -1), w_down, gate, final_g.reshape(1, d))


def _t5_bucket_np(dist):
    n = np.maximum(dist, 0)
    max_exact = T5_BUCKETS // 2
    nf = np.maximum(n, 1).astype(np.float64)
    val = np.log(nf / max_exact) / math.log(T5_MAX_DIST / max_exact) * (T5_BUCKETS - max_exact)
    large = max_exact + np.trunc(val + 1e-6).astype(np.int64)
    return np.where(n < max_exact, n, np.minimum(large, T5_BUCKETS - 1)).astype(np.int32)


def _t5_gather_kernel(t5_ref, bkt_ref, o_ref, *, mult):
    h = pl.program_id(0)
    bk = bkt_ref[...]
    acc = jnp.zeros(bk.shape, F32)
    for b in range(T5_BUCKETS):
        acc = jnp.where(bk == b, t5_ref[b, h], acc)
    o_ref[0] = jnp.where(bk == T5_MASKED, MASKED, acc * mult)


def _t5_gather(t5_bias, bkt, tr, mult=1.0):
    rows, cols = bkt.shape
    heads = t5_bias.shape[1]
    return pl.pallas_call(
        functools.partial(_t5_gather_kernel, mult=mult),
        grid=(heads, rows // tr),
        in_specs=[pl.BlockSpec(memory_space=pltpu.SMEM),
                  pl.BlockSpec((tr, cols), lambda h, r: (r, 0))],
        out_specs=pl.BlockSpec((1, tr, cols), lambda h, r: (h, r, 0)),
        out_shape=jax.ShapeDtypeStruct((heads, rows, cols), F32),
        compiler_params=_params("parallel", "parallel"),
        name="t5_gather",
    )(t5_bias, jnp.asarray(bkt))


TAB_DIAG, TAB_SUB, TAB_FAR, TAB_EDGE = 0, 1, 2, 3


def _attention_tables(t5_bias, t):
    i = np.arange(t)[:, None]
    j = np.arange(t)[None, :]
    assert int(_t5_bucket_np(np.array(t + 1))) == T5_BUCKETS - 1
    far = np.full((t, t), T5_BUCKETS - 1, np.int32)
    bkt = np.concatenate([np.where(j <= i, _t5_bucket_np(i - j), T5_MASKED), _t5_bucket_np(t + i - j), far,
                          np.where(j > i, far, T5_MASKED)], axis=0).astype(np.int32)
    return _t5_gather(t5_bias, bkt, tr=t, mult=LOG2E)


def _tile_iota(t):
    return lax.broadcasted_iota(jnp.int32, (t, t), 0), lax.broadcasted_iota(jnp.int32, (t, t), 1)


def _rect_iota(i, tq, start, tk):
    rowg = i * tq + lax.broadcasted_iota(jnp.int32, (tq, tk), 0)
    colg = start + lax.broadcasted_iota(jnp.int32, (tq, tk), 1)
    return rowg, colg


def _osm_reset(m_scr, l_scr, acc_scr):
    m_scr[...] = jnp.full(m_scr.shape, M_INIT, F32)
    l_scr[...] = jnp.zeros(l_scr.shape, F32)
    acc_scr[...] = jnp.zeros(acc_scr.shape, F32)


def _osm_update(t2, rows, slot, p_scr, m_scr, l_scr, acc_scr):
    m_prev = m_scr[rows, :]
    m_new = jnp.maximum(m_prev, jnp.max(t2, axis=-1, keepdims=True))
    p = jnp.exp2(t2 - _lane_tile(m_new, t2.shape[1] // LANE))
    alpha = jnp.exp2(m_prev - m_new)
    l_scr[rows, :] = alpha * l_scr[rows, :]
    m_scr[rows, :] = m_new
    p_scr[slot, rows, :] = p.astype(BF16)
    acc_scr[rows, :] = alpha * acc_scr[rows, :]


def _osm_accum(v, slot, p_scr, l_scr, acc_scr):
    vext = jnp.concatenate([v, jnp.ones(v.shape, v.dtype)], axis=-1)
    half = p_scr.shape[1] // 2
    for part in range(2):
        rows = pl.ds(part * half, half)
        pv = _dot(p_scr[slot, rows, :], vext)
        acc_scr[rows, :] += pv[:, :HEAD_DIM]
        l_scr[rows, :] += pv[:, HEAD_DIM:]


def _pipelined_tiles(n_tiles, scores, update, update_last=None):
    update_last = update if update_last is None else update_last
    last = n_tiles - 1
    n_pairs = last // 2
    scores(0, 0)

    def pair(n, cr):
        scores(2 * n + 1, 1)
        update(2 * n, 0)
        scores(2 * n + 2, 0)
        update(2 * n + 1, 1)
        return cr

    lax.fori_loop(0, n_pairs, pair, 0)

    @pl.when(last % 2 == 0)
    def _():
        update_last(last, 0)

    @pl.when(last % 2 == 1)
    def _():
        scores(last, 1)
        update(last - 1, 0)
        update_last(last, 1)


def _att_scratch(rows, tk, slots=2):
    return [pltpu.VMEM((slots, rows, tk), F32), pltpu.VMEM((slots, rows, tk), BF16), pltpu.VMEM((rows, LANE), F32),
            pltpu.VMEM((rows, LANE), F32), pltpu.VMEM((rows, HEAD_DIM), F32)]


def _sb_kernel(q_ref, k_ref, v_ref, o_ref, s_scr, c_scr, acc_scr, *, tq, tk):
    i = pl.program_id(2)
    q = q_ref[...]
    row, col = _tile_iota(tk)
    upper = jnp.where(row > col, 1.0, 0.0).astype(BF16)
    c_scr[...] = jnp.zeros(c_scr.shape, F32)
    acc_scr[...] = jnp.zeros(acc_scr.shape, F32)
    reps = tk // LANE

    def scores(kb, slot):
        start = pl.multiple_of(kb * tk, tk)
        s_scr[slot] = _dot_nt(q, k_ref[pl.ds(start, tk), :])

    def update(kb, slot, diag):
        start = pl.multiple_of(kb * tk, tk)
        nz = s_scr[slot]
        e = jnp.exp2(jnp.abs(nz) * (-LOG2E))
        lk = jnp.minimum(nz, 0.0) - jnp.log(1.0 + e)
        if diag:
            rowg, colg = _rect_iota(i, tq, start, tk)
            past = colg < rowg
            lk = jnp.where(past, lk, 0.0)
        hi, lo = _split_bf16(lk)
        c = c_scr[...]
        later = _dot(hi, upper) + _dot(lo, upper) + _lane_tile(c, reps)
        a = jnp.exp(lk - nz + later)
        if diag:
            a = jnp.where(past, a, 0.0)
        acc_scr[...] += _dot(a.astype(BF16), v_ref[pl.ds(start, tk), :])
        c_scr[...] = c + jnp.sum(lk, axis=-1, keepdims=True)

    assert tq == 2 * tk
    scores(2 * i + 1, 1)
    scores(2 * i, 0)
    update(2 * i + 1, 1, True)
    scores(jnp.maximum(2 * i - 1, 0), 1)
    update(2 * i, 0, True)

    def more(carry):
        n, c_max = carry
        return (n < i) & (c_max > SB_DEAD)

    def pair(carry):
        n, _ = carry
        kb = 2 * (i - n) - 1
        scores(kb - 1, 0)
        update(kb, 1, False)
        c_mid = jnp.max(c_scr[...])

        @pl.when(c_mid > SB_DEAD)
        def _():
            scores(jnp.maximum(kb - 2, 0), 1)
            update(kb - 1, 0, False)

        return n + 1, jnp.where(c_mid > SB_DEAD, jnp.max(c_scr[...]), c_mid)

    lax.while_loop(more, pair, (0, jnp.max(c_scr[...])))
    o_ref[...] = acc_scr[...].astype(o_ref.dtype)


def _sb_attention(qkv, *, batch, seq_len, tq=ATT_TQ, tk=ATT_TK):
    h = N_HEADS
    nq = seq_len // tq
    return pl.pallas_call(
        functools.partial(_sb_kernel, tq=tq, tk=tk),
        grid=(batch, h, nq),
        in_specs=[pl.BlockSpec((tq, HEAD_DIM), lambda b, hh, i: (b * nq + i, hh)),
                  pl.BlockSpec((seq_len, HEAD_DIM), lambda b, hh, i: (b, h + hh)),
                  pl.BlockSpec((seq_len, HEAD_DIM), lambda b, hh, i: (b, 2 * h + hh))],
        out_specs=pl.BlockSpec((tq, HEAD_DIM), lambda b, hh, i: (b * nq + i, hh)),
        out_shape=jax.ShapeDtypeStruct((batch * seq_len, h * HEAD_DIM), BF16),
        scratch_shapes=[pltpu.VMEM((2, tq, tk), F32), pltpu.VMEM((tq, LANE), F32), pltpu.VMEM((tq, HEAD_DIM), F32)],
        compiler_params=_params("parallel", "parallel", "arbitrary"),
        name="sb_attention",
    )(qkv, qkv, qkv)


def _table_offset(kb, i, t):
    return pl.multiple_of(jnp.where(kb == i, TAB_DIAG * t, jnp.where(kb == i - 1, TAB_SUB * t, TAB_FAR * t)), t)


def _diff_kernel(q_ref, k_ref, v_ref, tab_ref, lam_ref, hg_ref, o_ref, s_scr, p_scr, m_scr, l_scr, acc_scr, *, t,
                 lambda_init):
    i = pl.program_id(2)
    q = q_ref[...]
    lane = lax.broadcasted_iota(jnp.int32, q.shape, 1)
    zero = jnp.zeros_like(q)
    q2 = jnp.concatenate([jnp.where(lane < DIFF_DIM, q, zero), jnp.where(lane >= DIFF_DIM, q, zero)], axis=0)
    _osm_reset(m_scr, l_scr, acc_scr)

    def scores(kb, slot):
        s_scr[slot] = _dot_nt(q2, k_ref[pl.ds(pl.multiple_of(kb * t, t), t), :])

    def update(kb, slot):
        off = _table_offset(kb, i, t)
        for half in range(2):
            rows = pl.ds(half * t, t)
            t2 = s_scr[slot, rows, :] + tab_ref[0, pl.ds(off, t), :]
            _osm_update(t2, rows, slot, p_scr, m_scr, l_scr, acc_scr)
        _osm_accum(v_ref[pl.ds(pl.multiple_of(kb * t, t), t), :], slot, p_scr, l_scr, acc_scr)

    _pipelined_tiles(i + 1, scores, update)

    lam = lam_ref[...]
    lmbda = (jnp.exp(jnp.sum(lam[0:1] * lam[1:2], axis=-1, keepdims=True))
             - jnp.exp(jnp.sum(lam[2:3] * lam[3:4], axis=-1, keepdims=True)) + lambda_init)
    o = acc_scr[...] / l_scr[...]
    o = o[:t] - lmbda * o[t:]
    o_ref[...] = (_rms(o, hg_ref[...]) * (1.0 - lambda_init)).astype(o_ref.dtype)


def _diff_attention(qkv, tables, lam, head_g, *, batch, seq_len, lambda_init, t=DIFF_T):
    h = N_HEADS
    nq = seq_len // t
    return pl.pallas_call(
        functools.partial(_diff_kernel, t=t, lambda_init=lambda_init),
        grid=(batch, h, nq),
        in_specs=[pl.BlockSpec((t, HEAD_DIM), lambda b, hh, i: (b * nq + i, hh)),
                  pl.BlockSpec((seq_len, HEAD_DIM), lambda b, hh, i: (b, h + hh)),
                  pl.BlockSpec((seq_len, HEAD_DIM), lambda b, hh, i: (b, 2 * h + hh)),
                  pl.BlockSpec((1, 4 * t, t), lambda b, hh, i: (hh, 0, 0)),
                  pl.BlockSpec((4, DIFF_DIM), lambda b, hh, i: (0, 0)),
                  pl.BlockSpec((1, HEAD_DIM), lambda b, hh, i: (0, 0))],
        out_specs=pl.BlockSpec((t, HEAD_DIM), lambda b, hh, i: (b * nq + i, hh)),
        out_shape=jax.ShapeDtypeStruct((batch * seq_len, h * HEAD_DIM), BF16),
        scratch_shapes=_att_scratch(2 * t, t),
        compiler_params=_params("parallel", "parallel", "arbitrary"),
        name="diff_attention",
    )(qkv, qkv, qkv, tables, lam, head_g.reshape(1, HEAD_DIM))


def _mla_kernel(qn_ref, qr_ref, kn_ref, kr_ref, v_ref, o_ref, s_scr, p_scr, m_scr, l_scr, acc_scr, *, tq, tk):
    i = pl.program_id(2)
    q = jnp.concatenate([qn_ref[...], qr_ref[...]], axis=-1)
    _osm_reset(m_scr, l_scr, acc_scr)

    def scores(kb, slot):
        start = pl.multiple_of(kb * tk, tk)
        k = jnp.concatenate([kn_ref[pl.ds(start, tk), :], kr_ref[pl.ds(start, tk), :]], axis=-1)
        s_scr[slot] = _dot_nt(q, k)

    def update(kb, slot, mask):
        start = pl.multiple_of(kb * tk, tk)
        t2 = s_scr[slot]
        if mask:
            rowg, colg = _rect_iota(i, tq, start, tk)
            t2 = jnp.where(colg <= rowg, t2, MASKED)
        _osm_update(t2, pl.ds(0, tq), slot, p_scr, m_scr, l_scr, acc_scr)
        _osm_accum(v_ref[pl.ds(start, tk), :], slot, p_scr, l_scr, acc_scr)

    assert tq == tk
    _pipelined_tiles(i + 1, scores, lambda kb, slot: update(kb, slot, False), lambda kb, slot: update(kb, slot, True))
    o_ref[...] = (acc_scr[...] / l_scr[...]).astype(o_ref.dtype)


def _mla_attention(qn, qr, kv, kr, *, batch, seq_len, tq=MLA_T, tk=MLA_T):
    h = N_HEADS
    nq = seq_len // tq
    qspec = pl.BlockSpec((tq, HEAD_DIM), lambda b, hh, i: (b * nq + i, hh))
    return pl.pallas_call(
        functools.partial(_mla_kernel, tq=tq, tk=tk),
        grid=(batch, h, nq),
        in_specs=[qspec,
                  pl.BlockSpec((tq, LANE), lambda b, hh, i: (b * nq + i, hh // 2)),
                  pl.BlockSpec((seq_len, HEAD_DIM), lambda b, hh, i: (b, 2 * hh)),
                  pl.BlockSpec((seq_len, LANE), lambda b, hh, i: (b, hh % 2)),
                  pl.BlockSpec((seq_len, HEAD_DIM), lambda b, hh, i: (b, 2 * hh + 1))],
        out_specs=qspec,
        out_shape=jax.ShapeDtypeStruct((batch * seq_len, h * HEAD_DIM), BF16),
        scratch_shapes=_att_scratch(tq, tk),
        compiler_params=_params("parallel", "parallel", "arbitrary"),
        name="mla_attention",
    )(qn, qr, kv, kr, kv)


def _compress_kernel(raw_ref, pe_ref, w1_ref, w2_ref, o_ref, *, n_slots):
    half = NSA_CMP_BLOCK // 2
    p1 = jnp.zeros((n_slots, HEAD_DIM), F32)
    p2 = jnp.zeros((n_slots, HEAD_DIM), F32)
    for l in range(half):
        a = raw_ref[pl.ds(l, n_slots, stride=NSA_CMP_STRIDE), :]
        p1 = p1 + _dot((a + pe_ref[0, l:l + 1, :]).astype(BF16), w1_ref[0, l])
        p2 = p2 + _dot((a + pe_ref[0, half + l:half + l + 1, :]).astype(BF16), w1_ref[0, half + l])
    pre = p1 + pltpu.roll(p2, n_slots - 1, 0)
    hid = pre * jax.nn.sigmoid(pre)
    o_ref[0, 0, 0] = _dot(hid.astype(BF16), w2_ref[0]).astype(o_ref.dtype)


def _nsa_compress(raw, pe, w1, w2, *, batch, seq_len):
    g = NSA_GROUPS
    n_slots = seq_len // NSA_CMP_STRIDE
    return pl.pallas_call(
        functools.partial(_compress_kernel, n_slots=n_slots),
        grid=(batch, 2, g),
        in_specs=[pl.BlockSpec((seq_len, HEAD_DIM), lambda b, kv, gg: (b, kv * g + gg)),
                  pl.BlockSpec((1, NSA_CMP_BLOCK, HEAD_DIM), lambda b, kv, gg: (kv, 0, 0)),
                  pl.BlockSpec((1, NSA_CMP_BLOCK, HEAD_DIM, HEAD_DIM), lambda b, kv, gg: (kv, 0, 0, 0)),
                  pl.BlockSpec((1, HEAD_DIM, HEAD_DIM), lambda b, kv, gg: (kv, 0, 0))],
        out_specs=pl.BlockSpec((1, 1, 1, n_slots, HEAD_DIM), lambda b, kv, gg: (b, kv, gg, 0, 0)),
        out_shape=jax.ShapeDtypeStruct((batch, 2, g, n_slots, HEAD_DIM), BF16),
        compiler_params=_params("parallel", "parallel", "parallel"),
        name="nsa_compress",
    )(raw, pe, w1, w2)


def _nsa_cmp_kernel(q_ref, kc_ref, vc_ref, bias_ref, gates_ref, ovt_ref, oc_ref, sel_ref, *, t, n_slots, n_sel,
                    n_top):
    i = pl.program_id(2)
    kc = kc_ref[0, 0, 0]
    vc = vc_ref[0, 0, 0]
    qpos = i * t + lax.broadcasted_iota(jnp.int32, (t, n_slots), 0)
    cmp_end = NSA_CMP_STRIDE * lax.broadcasted_iota(jnp.int32, (t, n_slots), 1) + (NSA_CMP_BLOCK - 1)
    valid = cmp_end <= qpos
    gates = gates_ref[...]
    psum = jnp.zeros((t, n_slots), F32)
    for r in range(NSA_REP):
        q = q_ref[:, r * HEAD_DIM:(r + 1) * HEAD_DIM]
        s = jnp.where(valid, _dot_nt(q, kc) + bias_ref[r], NEG)
        m = jnp.max(s, axis=-1, keepdims=True)
        p = jnp.where(valid, jnp.exp2(s - m), 0.0)
        p = p / jnp.maximum(jnp.sum(p, axis=-1, keepdims=True), 1e-30)
        psum = psum + p
        oc_ref[:, r * HEAD_DIM:(r + 1) * HEAD_DIM] = gates[:, r:r + 1] * _dot(p.astype(BF16), vc)

    hi, lo = _split_bf16(psum)
    ovt = ovt_ref[...]
    imp = _dot_nt(ovt, hi) + _dot_nt(ovt, lo)
    blk = lax.broadcasted_iota(jnp.int32, (n_sel, t), 0)
    tpos = i * t + lax.broadcasted_iota(jnp.int32, (n_sel, t), 1)
    cur = tpos // NSA_SEL_BLOCK
    forced = (blk == 0) | (blk == cur) | (blk == cur - 1)
    score = jnp.where(blk * NSA_SEL_BLOCK <= tpos, jnp.where(forced, FORCED_SCORE, imp), -1.0)
    rank = jnp.zeros((n_sel, t), F32)
    for mm in range(n_sel):
        sm = score[mm:mm + 1, :]
        ahead = (sm > score) | ((sm == score) & (blk > mm))
        rank = rank + jnp.where(ahead, 1.0, 0.0)
    sel_t = jnp.where(rank < n_top, 1.0, 0.0).astype(BF16)
    row, col = _tile_iota(t)
    eye = jnp.where(row == col, 1.0, 0.0).astype(BF16)
    sel_ref[0, 0] = _dot_nt(eye, sel_t).astype(sel_ref.dtype)


def _nsa_cmp_attention(q_all, kvc, bias_c, gates, *, batch, seq_len, t=NSA_CMP_T):
    g = NSA_GROUPS
    nq = seq_len // t
    n_slots = seq_len // NSA_CMP_STRIDE
    n_sel = seq_len // NSA_SEL_BLOCK
    c0 = NSA_CMP_STRIDE * np.arange(n_slots)[:, None]
    s0 = NSA_SEL_BLOCK * np.arange(n_sel)[None, :]
    overlap = (c0 < s0 + NSA_SEL_BLOCK) & (c0 + NSA_CMP_BLOCK > s0)
    ovt = jnp.asarray(overlap.T.astype(np.float32), dtype=BF16)
    gw = NSA_REP * HEAD_DIM
    return pl.pallas_call(
        functools.partial(_nsa_cmp_kernel, t=t, n_slots=n_slots, n_sel=n_sel, n_top=min(NSA_TOPN, n_sel)),
        grid=(batch, g, nq),
        in_specs=[pl.BlockSpec((t, gw), lambda b, gg, i: (b * nq + i, gg)),
                  pl.BlockSpec((1, 1, 1, n_slots, HEAD_DIM), lambda b, gg, i: (b, 0, gg, 0, 0)),
                  pl.BlockSpec((1, 1, 1, n_slots, HEAD_DIM), lambda b, gg, i: (b, 1, gg, 0, 0)),
                  pl.BlockSpec((NSA_REP, t, n_slots), lambda b, gg, i: (gg, i, 0)),
                  pl.BlockSpec((t, LANE), lambda b, gg, i: (b * nq + i, gg)),
                  pl.BlockSpec((n_sel, n_slots), lambda b, gg, i: (0, 0))],
        out_specs=[pl.BlockSpec((t, gw), lambda b, gg, i: (b * nq + i, gg)),
                   pl.BlockSpec((1, 1, t, n_sel), lambda b, gg, i: (b, gg, i, 0))],
        out_shape=[jax.ShapeDtypeStruct((batch * seq_len, g * gw), F32),
                   jax.ShapeDtypeStruct((batch, g, seq_len, n_sel), BF16)],
        compiler_params=_params("parallel", "parallel", "arbitrary"),
        name="nsa_cmp_attention",
    )(q_all, kvc, kvc, bias_c, gates, ovt)


def _nsa_main_kernel(q_ref, ks_ref, vs_ref, kw_ref, vw_ref, sel_ref, tab_ref, gates_ref, oc_ref, o_ref,
                     s_scr, p_scr, m_scr, l_scr, acc_scr, os_scr, mk_scr, *, t):
    i = pl.program_id(2)
    rep = NSA_REP
    q = jnp.concatenate([q_ref[:, r * HEAD_DIM:(r + 1) * HEAD_DIM] for r in range(rep)], axis=0)

    sel = sel_ref[0, 0]
    n_sel = sel.shape[1]
    blk_row = lax.broadcasted_iota(jnp.int32, (n_sel, t), 0)
    key_col = lax.broadcasted_iota(jnp.int32, (n_sel, t), 1)
    _osm_reset(m_scr, l_scr, acc_scr)

    def sel_scores(kb, slot):
        s_scr[slot] = _dot_nt(q, ks_ref[pl.ds(pl.multiple_of(kb * t, t), t), :])

    def sel_update(kb, slot):
        start = pl.multiple_of(kb * t, t)
        expand = jnp.where((start + key_col) // NSA_SEL_BLOCK == blk_row, 1.0, 0.0).astype(BF16)
        mk_scr[...] = _dot(sel, expand)
        off = _table_offset(kb, i, t)
        for r in range(rep):
            rows = pl.ds(r * t, t)
            t2 = s_scr[slot, rows, :] + tab_ref[r, pl.ds(off, t), :]
            t2 = jnp.where(mk_scr[...] > 0.5, t2, MASKED)
            _osm_update(t2, rows, slot, p_scr, m_scr, l_scr, acc_scr)
        _osm_accum(vs_ref[pl.ds(start, t), :], slot, p_scr, l_scr, acc_scr)

    _pipelined_tiles(i + 1, sel_scores, sel_update)
    os_scr[...] = acc_scr[...] / l_scr[...]

    n_back = NSA_WINDOW // t
    assert n_back == 2
    _osm_reset(m_scr, l_scr, acc_scr)

    def win_scores(kb, slot):
        s_scr[slot] = _dot_nt(q, kw_ref[pl.ds(pl.multiple_of(kb * t, t), t), :])

    def win_update(kb, slot, region):
        for r in range(rep):
            rows = pl.ds(r * t, t)
            t2 = s_scr[slot, rows, :] + tab_ref[r, pl.ds(region * t, t), :]
            _osm_update(t2, rows, slot, p_scr, m_scr, l_scr, acc_scr)
        _osm_accum(vw_ref[pl.ds(pl.multiple_of(kb * t, t), t), :], slot, p_scr, l_scr, acc_scr)

    def window(n_tiles):
        regions = (TAB_DIAG, TAB_SUB, TAB_EDGE)
        for back in range(n_tiles):
            win_scores(i - back, back)
        for back in reversed(range(n_tiles)):
            win_update(i - back, back, regions[back])

    pl.when(i >= 2)(lambda: window(3))
    pl.when(i == 1)(lambda: window(2))
    pl.when(i == 0)(lambda: window(1))
    o_w = acc_scr[...] / l_scr[...]
    o_s = os_scr[...]

    gates = gates_ref[...]
    for r in range(rep):
        rows = slice(r * t, (r + 1) * t)
        cols = slice(r * HEAD_DIM, (r + 1) * HEAD_DIM)
        o = oc_ref[:, cols] + gates[:, rep + r:rep + r + 1] * o_s[rows] + gates[:, 2 * rep + r:2 * rep + r + 1] * o_w[rows]
        o_ref[:, cols] = o.astype(o_ref.dtype)


def _nsa_main_attention(qkv, sel, tables, gates, oc, *, batch, seq_len, t=ATT_T):
    g = NSA_GROUPS
    nq = seq_len // t
    n_sel = seq_len // NSA_SEL_BLOCK
    gw = NSA_REP * HEAD_DIM
    qb = N_HEADS
    kv = lambda which: pl.BlockSpec((seq_len, HEAD_DIM), lambda b, gg, i: (b, qb + which * g + gg))
    tile = pl.BlockSpec((t, gw), lambda b, gg, i: (b * nq + i, gg))
    rows = NSA_REP * t
    return pl.pallas_call(
        functools.partial(_nsa_main_kernel, t=t),
        grid=(batch, g, nq),
        in_specs=[tile, kv(0), kv(1), kv(2), kv(3),
                  pl.BlockSpec((1, 1, t, n_sel), lambda b, gg, i: (b, gg, i, 0)),
                  pl.BlockSpec((NSA_REP, 4 * t, t), lambda b, gg, i: (gg, 0, 0)),
                  pl.BlockSpec((t, LANE), lambda b, gg, i: (b * nq + i, gg)),
                  tile],
        out_specs=tile,
        out_shape=jax.ShapeDtypeStruct((batch * seq_len, g * gw), BF16),
        scratch_shapes=_att_scratch(rows, t, slots=3) + [pltpu.VMEM((rows, HEAD_DIM), F32), pltpu.VMEM((t, t), F32)],
        compiler_params=_params("parallel", "parallel", "arbitrary"),
        name="nsa_main_attention",
    )(qkv, qkv, qkv, qkv, qkv, sel, tables, gates, oc)


def _rope_tables(seq_len, width):
    half = MLA_ROPE // 2
    inv = np.power(ROPE_THETA, -np.arange(half, dtype=np.float32) / half).astype(np.float32)
    ang = np.arange(seq_len, dtype=np.float32)[:, None] * inv[None, :]
    reps = width // half
    return jnp.asarray(np.tile(np.cos(ang), (1, reps))), jnp.asarray(np.tile(np.sin(ang), (1, reps)))


def _rope_weights(w):
    k, n, _ = w.shape
    half = MLA_ROPE // 2
    wb = jnp.concatenate([-w[..., half:], w[..., :half]], axis=-1)
    return w.reshape(k, n * MLA_ROPE).astype(BF16), wb.reshape(k, n * MLA_ROPE).astype(BF16)


def kernel(x, c, t5_bias, ada_w, ada_b, norm_g, final_g, ffn_w_up, ffn_conv_w, ffn_conv_b, ffn_w_down, sb_w_in, sb_w_out, nsa_w_in, nsa_cmp_pe, nsa_cmp_w1, nsa_cmp_w2, nsa_w_out, diff_w_in, diff_lambda, diff_head_g, diff_w_out, mla_w_in, mla_q_g, mla_w_qb, mla_kv_g, mla_w_kvb, mla_w_out):
    batch, seq_len, d = x.shape
    depth = ada_w.shape[0]
    h, dh, g = N_HEADS, HEAD_DIM, NSA_GROUPS
    sizes = dict(batch=batch, seq_len=seq_len)

    mod = _ada_mod(c, ada_w, ada_b)
    tables = _attention_tables(t5_bias, ATT_T)

    xf = x.reshape(batch * seq_len, d)
    for i in range(depth):
        mixer, j = i % 4, i // 4
        sh1, sc1, gt1, sh2, sc2, gt2 = (mod[i, :, n * d:(n + 1) * d].reshape(batch, 1, d) for n in range(6))
        nm = functools.partial(_norm_mm, xf, norm_g[i, 0], seq_len=seq_len, shift=sh1, scale=sc1)
        if mixer == 0:
            w_in = sb_w_in[j]
            w_in = jnp.concatenate([w_in[:, :h * dh] * SB_QSCALE, w_in[:, h * dh:]], axis=1)
            qkv = nm(w_in.astype(BF16), name="sb_in")
            o = _sb_attention(qkv, **sizes)
            w_out = sb_w_out[j]
        elif mixer == 1:
            w_in = nsa_w_in[j]
            n_q, n_kv = h * dh, g * dh
            w_att = jnp.concatenate([w_in[:, :n_q] * NSA_QSCALE, w_in[:, n_q + 2 * n_kv:n_q + 6 * n_kv]], axis=1)
            w_cmp = w_in[:, n_q:n_q + 2 * n_kv]
            w_g = w_in[:, n_q + 6 * n_kv:].reshape(d, 3, g, NSA_REP).transpose(0, 2, 1, 3).reshape(d, g, 3 * NSA_REP)
            w_g = jnp.pad(w_g, ((0, 0), (0, 0), (0, LANE - 3 * NSA_REP))).reshape(d, g * LANE)
            qkv, hn = nm(w_att.astype(BF16), emit_h=True, name="nsa_in")
            hmm = functools.partial(_norm_mm, hn, None, seq_len=seq_len, normed=True)
            raw = hmm(w_cmp.astype(BF16), out_dtype=F32, name="nsa_in_cmp")
            gates = hmm(w_g.astype(BF16), out_dtype=F32, act="sigmoid", name="nsa_in_gates")
            kvc = _nsa_compress(raw, nsa_cmp_pe[j], nsa_cmp_w1[j].reshape(2, NSA_CMP_BLOCK, dh, dh).astype(BF16),
                                nsa_cmp_w2[j].astype(BF16), **sizes)
            n_slots = seq_len // NSA_CMP_STRIDE
            dist_c = np.arange(seq_len)[:, None] - (NSA_CMP_STRIDE * np.arange(n_slots)[None, :] + NSA_CMP_BLOCK - 1)
            bias_c = _t5_gather(t5_bias, _t5_bucket_np(dist_c), tr=min(seq_len, 512), mult=LOG2E)
            oc, sel = _nsa_cmp_attention(qkv, kvc, bias_c, gates, **sizes)
            o = _nsa_main_attention(qkv, sel, tables, gates, oc, **sizes)
            w_out = nsa_w_out[j]
        elif mixer == 2:
            lambda_init = 0.8 - 0.6 * math.exp(-0.3 * i)
            w_in = diff_w_in[j]
            w_in = jnp.concatenate([w_in[:, :h * dh] * DIFF_QSCALE, w_in[:, h * dh:]], axis=1)
            qkv = nm(w_in.astype(BF16), name="diff_in")
            o = _diff_attention(qkv, _attention_tables(t5_bias, DIFF_T), diff_lambda[j], diff_head_g[j],
                                lambda_init=lambda_init, **sizes)
            w_out = diff_w_out[j]
        else:
            w_in = mla_w_in[j]
            nq_l, nkv_l = MLA_Q_LORA, MLA_KV_LORA
            w_lat = jnp.concatenate([w_in[:, :nq_l], jnp.zeros((d, 2 * nkv_l - nq_l), w_in.dtype),
                                     w_in[:, nq_l:nq_l + nkv_l]], axis=1)
            lat, hn = nm(w_lat.astype(BF16), out_dtype=F32, emit_h=True, name="mla_in")
            w_kr = w_in[:, nq_l + nkv_l:]
            zero = jnp.zeros_like(w_kr)
            cos2, sin2 = _rope_tables(seq_len, 2 * LANE)
            wa, wb = _rope_weights(jnp.stack([w_kr, zero, zero, w_kr], axis=1))
            kr = _norm_mm(hn, None, wa, wb=wb, cos=cos2, sin=sin2, seq_len=seq_len, normed=True, name="mla_in_rope")
            w_qb = (mla_w_qb[j] * MLA_QSCALE).reshape(nq_l, h, MLA_NOPE + MLA_ROPE)
            qn = _norm_mm(lat, mla_q_g[j], w_qb[:, :, :MLA_NOPE].reshape(nq_l, h * MLA_NOPE).astype(BF16),
                          seq_len=seq_len, x_cols=nq_l, x_col_block=0, name="mla_q_nope")
            cosh, sinh = _rope_tables(seq_len, h * MLA_ROPE)
            wa, wb = _rope_weights(w_qb[:, :, MLA_NOPE:])
            qr = _norm_mm(lat, mla_q_g[j], wa, wb=wb, cos=cosh, sin=sinh, seq_len=seq_len, x_cols=nq_l,
                          x_col_block=0, name="mla_q_rope")
            kv = _norm_mm(lat, mla_kv_g[j], mla_w_kvb[j].astype(BF16), seq_len=seq_len, x_cols=nkv_l,
                          x_col_block=2, name="mla_kv")
            o = _mla_attention(qn, qr, kv, kr, **sizes)
            w_out = mla_w_out[j]
        xf = _mm_residual(o, w_out.astype(BF16), xf, gt1, seq_len=seq_len)
        xf = _conv_ffn(xf, norm_g[i, 1], sh2, sc2, ffn_w_up[i].astype(BF16), ffn_conv_w[i], ffn_conv_b[i],
                       ffn_w_down[i].astype(BF16), gt2, final_g, seq_len=seq_len, final_norm=(i == depth - 1))
    return xf.reshape(batch, seq_len, d)
```

```python
import functools
import math

import numpy as np
import jax
import jax.numpy as jnp
from jax import lax
from jax.experimental import pallas as pl
from jax.experimental.pallas import tpu as pltpu

F32 = jnp.float32
BF16 = jnp.bfloat16
EPS = 1e-6
NEG = -1e30

LANE = 128
HALO = 16
VMEM_LIMIT = 56 * 2**20
FFN_VMEM_LIMIT = 58 * 2**20

T5_BUCKETS = 32
T5_MAX_DIST = 128
N_HEADS = 16
HEAD_DIM = 128
NSA_GROUPS = 4
NSA_REP = 4
NSA_CMP_BLOCK = 32
NSA_CMP_STRIDE = 16
NSA_SEL_BLOCK = 64
NSA_TOPN = 16
NSA_WINDOW = 512
FORCED_SCORE = 1e9
DIFF_DIM = 64
MLA_Q_LORA = 768
MLA_KV_LORA = 512
MLA_NOPE = 128
MLA_ROPE = 64
ROPE_THETA = 10000.0
CONV_WIDTH = 3
ATT_T = 256
NSA_CMP_T = 512
DIFF_T = 512
MLA_T = 512
ATT_TQ = 512
ATT_TK = 256
LOG2E = 1.4426950408889634
SB_QSCALE = -(HEAD_DIM ** -0.5)
NSA_QSCALE = HEAD_DIM ** -0.5 * LOG2E
DIFF_QSCALE = DIFF_DIM ** -0.5 * LOG2E
MLA_QSCALE = (MLA_NOPE + MLA_ROPE) ** -0.5 * LOG2E
M_INIT = -1e30
MASKED = -2e30
SB_DEAD = -120.0
T5_MASKED = T5_BUCKETS


def _params(*sem, vmem_limit=VMEM_LIMIT):
    return pltpu.CompilerParams(dimension_semantics=sem, vmem_limit_bytes=vmem_limit)


def _dot(a, b):
    return jnp.dot(a, b, preferred_element_type=F32)


def _dot_nt(a, b):
    return lax.dot_general(a, b, (((1,), (1,)), ((), ())), preferred_element_type=F32)


def _lane_tile(x, reps):
    return jnp.concatenate([x] * reps, axis=1)


def _split_bf16(x):
    hi = x.astype(BF16)
    lo = (x - hi.astype(F32)).astype(BF16)
    return hi, lo


def _rms(x, g):
    ms = jnp.mean(x * x, axis=-1, keepdims=True)
    return x * lax.rsqrt(ms + EPS) * g


def _ada_kernel(c_ref, w_ref, b_ref, o_ref):
    c = c_ref[...]
    a = c * jax.nn.sigmoid(c)
    a_hi, a_lo = _split_bf16(a)
    w_hi, w_lo = _split_bf16(w_ref[0])
    o_ref[0] = _dot(a_hi, w_hi) + _dot(a_lo, w_hi) + _dot(a_hi, w_lo) + b_ref[0]


def _ada_mod(c, ada_w, ada_b, tn=1024):
    depth, d, n = ada_w.shape
    b = c.shape[0]
    return pl.pallas_call(
        _ada_kernel,
        grid=(depth, n // tn),
        in_specs=[pl.BlockSpec((b, d), lambda l, j: (0, 0)),
                  pl.BlockSpec((1, d, tn), lambda l, j: (l, 0, j)),
                  pl.BlockSpec((1, 1, tn), lambda l, j: (l, 0, j))],
        out_specs=pl.BlockSpec((1, b, tn), lambda l, j: (l, 0, j)),
        out_shape=jax.ShapeDtypeStruct((depth, b, n), F32),
        compiler_params=_params("parallel", "parallel"),
        name="ada_mod",
    )(c, ada_w, ada_b.reshape(depth, 1, n))


def _norm_mm_kernel(*refs, modulated, rope, act, normed):
    it = iter(refs)
    x_ref = next(it)
    g_ref = sh_ref = sc_ref = wb_ref = cos_ref = sin_ref = None
    if not normed:
        g_ref = next(it)
        if modulated:
            sh_ref, sc_ref = next(it), next(it)
    w_ref = next(it)
    if rope:
        wb_ref, cos_ref, sin_ref = next(it), next(it), next(it)
    o_ref = next(it)
    if normed:
        h = x_ref[...]
    else:
        h_ref = next(it)

        @pl.when(pl.program_id(1) == 0)
        def _():
            y = _rms(x_ref[...], g_ref[...])
            if modulated:
                y = y * (1.0 + sc_ref[0]) + sh_ref[0]
            h_ref[...] = y.astype(BF16)

        h = h_ref[...]
    acc = _dot(h, w_ref[...])
    if rope:
        acc = acc * cos_ref[...] + _dot(h, wb_ref[...]) * sin_ref[...]
    if act == "sigmoid":
        acc = jax.nn.sigmoid(acc)
    o_ref[...] = acc.astype(o_ref.dtype)


def _norm_mm(x, g, w, *, seq_len, shift=None, scale=None, wb=None, cos=None, sin=None, act=None, normed=False,
             emit_h=False, out_dtype=BF16, x_cols=None, x_col_block=0, tm=1024, tn=1024, name="norm_mm"):
    m = x.shape[0]
    k = x.shape[1] if x_cols is None else x_cols
    n = w.shape[1]
    while n % tn:
        tn //= 2
    modulated, rope = shift is not None, wb is not None
    tps = seq_len // tm
    in_specs = [pl.BlockSpec((tm, k), lambda i, j: (i, x_col_block))]
    args = [x]
    if not normed:
        in_specs.append(pl.BlockSpec((1, k), lambda i, j: (0, 0)))
        args.append(g.reshape(1, k))
        if modulated:
            in_specs += [pl.BlockSpec((1, 1, k), lambda i, j: (i // tps, 0, 0))] * 2
            args += [shift, scale]
    in_specs.append(pl.BlockSpec((k, tn), lambda i, j: (0, j)))
    args.append(w)
    if rope:
        in_specs.append(pl.BlockSpec((k, tn), lambda i, j: (0, j)))
        in_specs += [pl.BlockSpec((tm, tn), lambda i, j: (i % tps, j))] * 2
        args += [wb, cos, sin]
    out_specs = [pl.BlockSpec((tm, tn), lambda i, j: (i, j))]
    out_shape = [jax.ShapeDtypeStruct((m, n), out_dtype)]
    scratch = []
    if emit_h:
        out_specs.append(pl.BlockSpec((tm, k), lambda i, j: (i, 0)))
        out_shape.append(jax.ShapeDtypeStruct((m, k), BF16))
    elif not normed:
        scratch.append(pltpu.VMEM((tm, k), BF16))
    out = pl.pallas_call(
        functools.partial(_norm_mm_kernel, modulated=modulated, rope=rope, act=act, normed=normed),
        grid=(m // tm, n // tn),
        in_specs=in_specs,
        out_specs=out_specs,
        out_shape=out_shape,
        scratch_shapes=scratch,
        compiler_params=_params("parallel", "arbitrary"),
        name=name,
    )(*args)
    return out if emit_h else out[0]


def _mm_res_kernel(a_ref, w_ref, x_ref, gate_ref, o_ref):
    o_ref[...] = x_ref[...] + gate_ref[0] * _dot(a_ref[...], w_ref[...])


def _mm_residual(a, w, x, gate, *, seq_len, tm=1024, tn=1024, name="mm_residual"):
    m, k = a.shape
    n = w.shape[1]
    tps = seq_len // tm
    return pl.pallas_call(
        _mm_res_kernel,
        grid=(m // tm, n // tn),
        in_specs=[pl.BlockSpec((tm, k), lambda i, j: (i, 0)),
                  pl.BlockSpec((k, tn), lambda i, j: (0, j)),
                  pl.BlockSpec((tm, tn), lambda i, j: (i, j)),
                  pl.BlockSpec((1, 1, tn), lambda i, j: (i // tps, 0, j))],
        out_specs=pl.BlockSpec((tm, tn), lambda i, j: (i, j)),
        out_shape=jax.ShapeDtypeStruct((m, n), F32),
        compiler_params=_params("parallel", "arbitrary"),
        name=name,
    )(a, w, x, gate)


def _ffn_kernel(x_ref, xp_ref, g_ref, sh_ref, sc_ref, wg_ref, wu_ref, cwg_ref, cwu_ref, cbg_ref, cbu_ref,
                wd_ref, gate_ref, fg_ref, o_ref, h_ref, acc_ref, ug_scr, uu_scr, *, tiles_per_seq, final_norm):
    i, j = pl.program_id(0), pl.program_id(1)

    @pl.when(j == 0)
    def _():
        def nm(x):
            return _rms(x, g_ref[...]) * (1.0 + sc_ref[0]) + sh_ref[0]
        h_ref[HALO:, :] = nm(x_ref[...]).astype(BF16)
        keep = jnp.where(i % tiles_per_seq == 0, 0.0, 1.0)
        h_ref[:HALO, :] = (nm(xp_ref[...]) * keep).astype(BF16)
        acc_ref[...] = jnp.zeros_like(acc_ref)

    h = h_ref[...]

    def branch(w_ref, cw_ref, cb_ref, u_scr):
        u_scr[...] = _dot(h, w_ref[...])
        tm = u_scr.shape[0] - HALO
        cw = cw_ref[...]
        return (cw[0:1] * u_scr[pl.ds(HALO - 2, tm), :] + cw[1:2] * u_scr[pl.ds(HALO - 1, tm), :]
                + cw[2:3] * u_scr[pl.ds(HALO, tm), :] + cb_ref[...])

    gt = branch(wg_ref, cwg_ref, cbg_ref, ug_scr)
    up = branch(wu_ref, cwu_ref, cbu_ref, uu_scr)
    a = gt * jax.nn.sigmoid(gt) * up
    acc_ref[...] += _dot(a.astype(BF16), wd_ref[...])

    @pl.when(j == pl.num_programs(1) - 1)
    def _():
        y = x_ref[...] + gate_ref[0] * acc_ref[...]
        if final_norm:
            y = _rms(y, fg_ref[...])
        o_ref[...] = y


def _conv_ffn(x, g, shift, scale, w_up, conv_w, conv_b, w_down, gate, final_g, *, seq_len, final_norm,
              tm=1024, tf=512):
    m, d = x.shape
    f = w_down.shape[0]
    nf = f // tf
    tps = seq_len // tm
    hb = tm // HALO
    row = lambda i, j: (i, 0)
    per_batch = lambda i, j: (i // tps, 0, 0)
    return pl.pallas_call(
        functools.partial(_ffn_kernel, tiles_per_seq=tps, final_norm=final_norm),
        grid=(m // tm, nf),
        in_specs=[pl.BlockSpec((tm, d), row, pipeline_mode=pl.Buffered(1)),
                  pl.BlockSpec((HALO, d), lambda i, j: (jnp.maximum(i * hb - 1, 0), 0)),
                  pl.BlockSpec((1, d), lambda i, j: (0, 0)),
                  pl.BlockSpec((1, 1, d), per_batch),
                  pl.BlockSpec((1, 1, d), per_batch),
                  pl.BlockSpec((d, tf), lambda i, j: (0, j)),
                  pl.BlockSpec((d, tf), lambda i, j: (0, j + nf)),
                  pl.BlockSpec((CONV_WIDTH, tf), lambda i, j: (0, j)),
                  pl.BlockSpec((CONV_WIDTH, tf), lambda i, j: (0, j + nf)),
                  pl.BlockSpec((1, tf), lambda i, j: (0, j)),
                  pl.BlockSpec((1, tf), lambda i, j: (0, j + nf)),
                  pl.BlockSpec((tf, d), lambda i, j: (j, 0)),
                  pl.BlockSpec((1, 1, d), per_batch),
                  pl.BlockSpec((1, d), lambda i, j: (0, 0))],
        out_specs=pl.BlockSpec((tm, d), row, pipeline_mode=pl.Buffered(1)),
        out_shape=jax.ShapeDtypeStruct((m, d), F32),
        scratch_shapes=[pltpu.VMEM((HALO + tm, d), BF16), pltpu.VMEM((tm, d), F32),
                        pltpu.VMEM((HALO + tm, tf), F32), pltpu.VMEM((HALO + tm, tf), F32)],
        compiler_params=_params("parallel", "arbitrary", vmem_limit=FFN_VMEM_LIMIT),
        name="conv_ffn",
    )(x, x, g.reshape(1, d), shift, scale, w_up, w_up, conv_w, conv_w, conv_b.reshape(1, -1),
      conv_b.reshape(1, -1), w_down, gate, final_g.reshape(1, d))


def _t5_bucket_np(dist):
    n = np.maximum(dist, 0)
    max_exact = T5_BUCKETS // 2
    nf = np.maximum(n, 1).astype(np.float64)
    val = np.log(nf / max_exact) / math.log(T5_MAX_DIST / max_exact) * (T5_BUCKETS - max_exact)
    large = max_exact + np.trunc(val + 1e-6).astype(np.int64)
    return np.where(n < max_exact, n, np.minimum(large, T5_BUCKETS - 1)).astype(np.int32)


def _t5_gather_kernel(t5_ref, bkt_ref, o_ref, *, mult):
    h = pl.program_id(0)
    bk = bkt_ref[...]
    acc = jnp.zeros(bk.shape, F32)
    for b in range(T5_BUCKETS):
        acc = jnp.where(bk == b, t5_ref[b, h], acc)
    o_ref[0] = jnp.where(bk == T5_MASKED, MASKED, acc * mult)


def _t5_gather(t5_bias, bkt, tr, mult=1.0):
    rows, cols = bkt.shape
    heads = t5_bias.shape[1]
    return pl.pallas_call(
        functools.partial(_t5_gather_kernel, mult=mult),
        grid=(heads, rows // tr),
        in_specs=[pl.BlockSpec(memory_space=pltpu.SMEM),
                  pl.BlockSpec((tr, cols), lambda h, r: (r, 0))],
        out_specs=pl.BlockSpec((1, tr, cols), lambda h, r: (h, r, 0)),
        out_shape=jax.ShapeDtypeStruct((heads, rows, cols), F32),
        compiler_params=_params("parallel", "parallel"),
        name="t5_gather",
    )(t5_bias, jnp.asarray(bkt))


TAB_DIAG, TAB_SUB, TAB_FAR, TAB_EDGE = 0, 1, 2, 3


def _attention_tables(t5_bias, t):
    i = np.arange(t)[:, None]
    j = np.arange(t)[None, :]
    assert int(_t5_bucket_np(np.array(t + 1))) == T5_BUCKETS - 1
    far = np.full((t, t), T5_BUCKETS - 1, np.int32)
    bkt = np.concatenate([np.where(j <= i, _t5_bucket_np(i - j), T5_MASKED), _t5_bucket_np(t + i - j), far,
                          np.where(j > i, far, T5_MASKED)], axis=0).astype(np.int32)
    return _t5_gather(t5_bias, bkt, tr=t, mult=LOG2E)


def _tile_iota(t):
    return lax.broadcasted_iota(jnp.int32, (t, t), 0), lax.broadcasted_iota(jnp.int32, (t, t), 1)


def _rect_iota(i, tq, start, tk):
    rowg = i * tq + lax.broadcasted_iota(jnp.int32, (tq, tk), 0)
    colg = start + lax.broadcasted_iota(jnp.int32, (tq, tk), 1)
    return rowg, colg


def _osm_reset(m_scr, l_scr, acc_scr):
    m_scr[...] = jnp.full(m_scr.shape, M_INIT, F32)
    l_scr[...] = jnp.zeros(l_scr.shape, F32)
    acc_scr[...] = jnp.zeros(acc_scr.shape, F32)


def _osm_update(t2, rows, slot, p_scr, m_scr, l_scr, acc_scr):
    m_prev = m_scr[rows, :]
    m_new = jnp.maximum(m_prev, jnp.max(t2, axis=-1, keepdims=True))
    p = jnp.exp2(t2 - _lane_tile(m_new, t2.shape[1] // LANE))
    alpha = jnp.exp2(m_prev - m_new)
    l_scr[rows, :] = alpha * l_scr[rows, :]
    m_scr[rows, :] = m_new
    p_scr[slot, rows, :] = p.astype(BF16)
    acc_scr[rows, :] = alpha * acc_scr[rows, :]


def _osm_accum(v, slot, p_scr, l_scr, acc_scr):
    vext = jnp.concatenate([v, jnp.ones(v.shape, v.dtype)], axis=-1)
    half = p_scr.shape[1] // 2
    for part in range(2):
        rows = pl.ds(part * half, half)
        pv = _dot(p_scr[slot, rows, :], vext)
        acc_scr[rows, :] += pv[:, :HEAD_DIM]
        l_scr[rows, :] += pv[:, HEAD_DIM:]


def _pipelined_tiles(n_tiles, scores, update, update_last=None):
    update_last = update if update_last is None else update_last
    last = n_tiles - 1
    n_pairs = last // 2
    scores(0, 0)

    def pair(n, cr):
        scores(2 * n + 1, 1)
        update(2 * n, 0)
        scores(2 * n + 2, 0)
        update(2 * n + 1, 1)
        return cr

    lax.fori_loop(0, n_pairs, pair, 0)

    @pl.when(last % 2 == 0)
    def _():
        update_last(last, 0)

    @pl.when(last % 2 == 1)
    def _():
        scores(last, 1)
        update(last - 1, 0)
        update_last(last, 1)


def _att_scratch(rows, tk, slots=2):
    return [pltpu.VMEM((slots, rows, tk), F32), pltpu.VMEM((slots, rows, tk), BF16), pltpu.VMEM((rows, LANE), F32),
            pltpu.VMEM((rows, LANE), F32), pltpu.VMEM((rows, HEAD_DIM), F32)]


def _sb_kernel(q_ref, k_ref, v_ref, o_ref, s_scr, c_scr, acc_scr, *, tq, tk):
    i = pl.program_id(2)
    q = q_ref[...]
    row, col = _tile_iota(tk)
    upper = jnp.where(row > col, 1.0, 0.0).astype(BF16)
    c_scr[...] = jnp.zeros(c_scr.shape, F32)
    acc_scr[...] = jnp.zeros(acc_scr.shape, F32)
    reps = tk // LANE

    def scores(kb, slot):
        start = pl.multiple_of(kb * tk, tk)
        s_scr[slot] = _dot_nt(q, k_ref[pl.ds(start, tk), :])

    def update(kb, slot, diag):
        start = pl.multiple_of(kb * tk, tk)
        nz = s_scr[slot]
        e = jnp.exp2(jnp.abs(nz) * (-LOG2E))
        lk = jnp.minimum(nz, 0.0) - jnp.log(1.0 + e)
        if diag:
            rowg, colg = _rect_iota(i, tq, start, tk)
            past = colg < rowg
            lk = jnp.where(past, lk, 0.0)
        hi, lo = _split_bf16(lk)
        c = c_scr[...]
        later = _dot(hi, upper) + _dot(lo, upper) + _lane_tile(c, reps)
        a = jnp.exp(lk - nz + later)
        if diag:
            a = jnp.where(past, a, 0.0)
        acc_scr[...] += _dot(a.astype(BF16), v_ref[pl.ds(start, tk), :])
        c_scr[...] = c + jnp.sum(lk, axis=-1, keepdims=True)

    assert tq == 2 * tk
    scores(2 * i + 1, 1)
    scores(2 * i, 0)
    update(2 * i + 1, 1, True)
    scores(jnp.maximum(2 * i - 1, 0), 1)
    update(2 * i, 0, True)

    def more(carry):
        n, c_max = carry
        return (n < i) & (c_max > SB_DEAD)

    def pair(carry):
        n, _ = carry
        kb = 2 * (i - n) - 1
        scores(kb - 1, 0)
        update(kb, 1, False)
        c_mid = jnp.max(c_scr[...])

        @pl.when(c_mid > SB_DEAD)
        def _():
            scores(jnp.maximum(kb - 2, 0), 1)
            update(kb - 1, 0, False)

        return n + 1, jnp.where(c_mid > SB_DEAD, jnp.max(c_scr[...]), c_mid)

    lax.while_loop(more, pair, (0, jnp.max(c_scr[...])))
    o_ref[...] = acc_scr[...].astype(o_ref.dtype)


def _sb_attention(qkv, *, batch, seq_len, tq=ATT_TQ, tk=ATT_TK):
    h = N_HEADS
    nq = seq_len // tq
    return pl.pallas_call(
        functools.partial(_sb_kernel, tq=tq, tk=tk),
        grid=(batch, h, nq),
        in_specs=[pl.BlockSpec((tq, HEAD_DIM), lambda b, hh, i: (b * nq + i, hh)),
                  pl.BlockSpec((seq_len, HEAD_DIM), lambda b, hh, i: (b, h + hh)),
                  pl.BlockSpec((seq_len, HEAD_DIM), lambda b, hh, i: (b, 2 * h + hh))],
        out_specs=pl.BlockSpec((tq, HEAD_DIM), lambda b, hh, i: (b * nq + i, hh)),
        out_shape=jax.ShapeDtypeStruct((batch * seq_len, h * HEAD_DIM), BF16),
        scratch_shapes=[pltpu.VMEM((2, tq, tk), F32), pltpu.VMEM((tq, LANE), F32), pltpu.VMEM((tq, HEAD_DIM), F32)],
        compiler_params=_params("parallel", "parallel", "arbitrary"),
        name="sb_attention",
    )(qkv, qkv, qkv)


def _table_offset(kb, i, t):
    return pl.multiple_of(jnp.where(kb == i, TAB_DIAG * t, jnp.where(kb == i - 1, TAB_SUB * t, TAB_FAR * t)), t)


def _diff_kernel(q_ref, k_ref, v_ref, tab_ref, lam_ref, hg_ref, o_ref, s_scr, p_scr, m_scr, l_scr, acc_scr, *, t,
                 lambda_init):
    i = pl.program_id(2)
    q = q_ref[...]
    lane = lax.broadcasted_iota(jnp.int32, q.shape, 1)
    zero = jnp.zeros_like(q)
    q2 = jnp.concatenate([jnp.where(lane < DIFF_DIM, q, zero), jnp.where(lane >= DIFF_DIM, q, zero)], axis=0)
    _osm_reset(m_scr, l_scr, acc_scr)

    def scores(kb, slot):
        s_scr[slot] = _dot_nt(q2, k_ref[pl.ds(pl.multiple_of(kb * t, t), t), :])

    def update(kb, slot):
        off = _table_offset(kb, i, t)
        for half in range(2):
            rows = pl.ds(half * t, t)
            t2 = s_scr[slot, rows, :] + tab_ref[0, pl.ds(off, t), :]
            _osm_update(t2, rows, slot, p_scr, m_scr, l_scr, acc_scr)
        _osm_accum(v_ref[pl.ds(pl.multiple_of(kb * t, t), t), :], slot, p_scr, l_scr, acc_scr)

    _pipelined_tiles(i + 1, scores, update)

    lam = lam_ref[...]
    lmbda = (jnp.exp(jnp.sum(lam[0:1] * lam[1:2], axis=-1, keepdims=True))
             - jnp.exp(jnp.sum(lam[2:3] * lam[3:4], axis=-1, keepdims=True)) + lambda_init)
    o = acc_scr[...] / l_scr[...]
    o = o[:t] - lmbda * o[t:]
    o_ref[...] = (_rms(o, hg_ref[...]) * (1.0 - lambda_init)).astype(o_ref.dtype)


def _diff_attention(qkv, tables, lam, head_g, *, batch, seq_len, lambda_init, t=DIFF_T):
    h = N_HEADS
    nq = seq_len // t
    return pl.pallas_call(
        functools.partial(_diff_kernel, t=t, lambda_init=lambda_init),
        grid=(batch, h, nq),
        in_specs=[pl.BlockSpec((t, HEAD_DIM), lambda b, hh, i: (b * nq + i, hh)),
                  pl.BlockSpec((seq_len, HEAD_DIM), lambda b, hh, i: (b, h + hh)),
                  pl.BlockSpec((seq_len, HEAD_DIM), lambda b, hh, i: (b, 2 * h + hh)),
                  pl.BlockSpec((1, 4 * t, t), lambda b, hh, i: (hh, 0, 0)),
                  pl.BlockSpec((4, DIFF_DIM), lambda b, hh, i: (0, 0)),
                  pl.BlockSpec((1, HEAD_DIM), lambda b, hh, i: (0, 0))],
        out_specs=pl.BlockSpec((t, HEAD_DIM), lambda b, hh, i: (b * nq + i, hh)),
        out_shape=jax.ShapeDtypeStruct((batch * seq_len, h * HEAD_DIM), BF16),
        scratch_shapes=_att_scratch(2 * t, t),
        compiler_params=_params("parallel", "parallel", "arbitrary"),
        name="diff_attention",
    )(qkv, qkv, qkv, tables, lam, head_g.reshape(1, HEAD_DIM))


def _mla_kernel(qn_ref, qr_ref, kn_ref, kr_ref, v_ref, o_ref, s_scr, p_scr, m_scr, l_scr, acc_scr, *, tq, tk):
    i = pl.program_id(2)
    q = jnp.concatenate([qn_ref[...], qr_ref[...]], axis=-1)
    _osm_reset(m_scr, l_scr, acc_scr)

    def scores(kb, slot):
        start = pl.multiple_of(kb * tk, tk)
        k = jnp.concatenate([kn_ref[pl.ds(start, tk), :], kr_ref[pl.ds(start, tk), :]], axis=-1)
        s_scr[slot] = _dot_nt(q, k)

    def update(kb, slot, mask):
        start = pl.multiple_of(kb * tk, tk)
        t2 = s_scr[slot]
        if mask:
            rowg, colg = _rect_iota(i, tq, start, tk)
            t2 = jnp.where(colg <= rowg, t2, MASKED)
        _osm_update(t2, pl.ds(0, tq), slot, p_scr, m_scr, l_scr, acc_scr)
        _osm_accum(v_ref[pl.ds(start, tk), :], slot, p_scr, l_scr, acc_scr)

    assert tq == tk
    _pipelined_tiles(i + 1, scores, lambda kb, slot: update(kb, slot, False), lambda kb, slot: update(kb, slot, True))
    o_ref[...] = (acc_scr[...] / l_scr[...]).astype(o_ref.dtype)


def _mla_attention(qn, qr, kv, kr, *, batch, seq_len, tq=MLA_T, tk=MLA_T):
    h = N_HEADS
    nq = seq_len // tq
    qspec = pl.BlockSpec((tq, HEAD_DIM), lambda b, hh, i: (b * nq + i, hh))
    return pl.pallas_call(
        functools.partial(_mla_kernel, tq=tq, tk=tk),
        grid=(batch, h, nq),
        in_specs=[qspec,
                  pl.BlockSpec((tq, LANE), lambda b, hh, i: (b * nq + i, hh // 2)),
                  pl.BlockSpec((seq_len, HEAD_DIM), lambda b, hh, i: (b, 2 * hh)),
                  pl.BlockSpec((seq_len, LANE), lambda b, hh, i: (b, hh % 2)),
                  pl.BlockSpec((seq_len, HEAD_DIM), lambda b, hh, i: (b, 2 * hh + 1))],
        out_specs=qspec,
        out_shape=jax.ShapeDtypeStruct((batch * seq_len, h * HEAD_DIM), BF16),
        scratch_shapes=_att_scratch(tq, tk),
        compiler_params=_params("parallel", "parallel", "arbitrary"),
        name="mla_attention",
    )(qn, qr, kv, kr, kv)


def _compress_kernel(raw_ref, pe_ref, w1_ref, w2_ref, o_ref, *, n_slots):
    half = NSA_CMP_BLOCK // 2
    p1 = jnp.zeros((n_slots, HEAD_DIM), F32)
    p2 = jnp.zeros((n_slots, HEAD_DIM), F32)
    for l in range(half):
        a = raw_ref[pl.ds(l, n_slots, stride=NSA_CMP_STRIDE), :]
        p1 = p1 + _dot((a + pe_ref[0, l:l + 1, :]).astype(BF16), w1_ref[0, l])
        p2 = p2 + _dot((a + pe_ref[0, half + l:half + l + 1, :]).astype(BF16), w1_ref[0, half + l])
    pre = p1 + pltpu.roll(p2, n_slots - 1, 0)
    hid = pre * jax.nn.sigmoid(pre)
    o_ref[0, 0, 0] = _dot(hid.astype(BF16), w2_ref[0]).astype(o_ref.dtype)


def _nsa_compress(raw, pe, w1, w2, *, batch, seq_len):
    g = NSA_GROUPS
    n_slots = seq_len // NSA_CMP_STRIDE
    return pl.pallas_call(
        functools.partial(_compress_kernel, n_slots=n_slots),
        grid=(batch, 2, g),
        in_specs=[pl.BlockSpec((seq_len, HEAD_DIM), lambda b, kv, gg: (b, kv * g + gg)),
                  pl.BlockSpec((1, NSA_CMP_BLOCK, HEAD_DIM), lambda b, kv, gg: (kv, 0, 0)),
                  pl.BlockSpec((1, NSA_CMP_BLOCK, HEAD_DIM, HEAD_DIM), lambda b, kv, gg: (kv, 0, 0, 0)),
                  pl.BlockSpec((1, HEAD_DIM, HEAD_DIM), lambda b, kv, gg: (kv, 0, 0))],
        out_specs=pl.BlockSpec((1, 1, 1, n_slots, HEAD_DIM), lambda b, kv, gg: (b, kv, gg, 0, 0)),
        out_shape=jax.ShapeDtypeStruct((batch, 2, g, n_slots, HEAD_DIM), BF16),
        compiler_params=_params("parallel", "parallel", "parallel"),
        name="nsa_compress",
    )(raw, pe, w1, w2)


def _nsa_cmp_kernel(q_ref, kc_ref, vc_ref, bias_ref, gates_ref, ovt_ref, oc_ref, sel_ref, *, t, n_slots, n_sel,
                    n_top):
    i = pl.program_id(2)
    kc = kc_ref[0, 0, 0]
    vc = vc_ref[0, 0, 0]
    qpos = i * t + lax.broadcasted_iota(jnp.int32, (t, n_slots), 0)
    cmp_end = NSA_CMP_STRIDE * lax.broadcasted_iota(jnp.int32, (t, n_slots), 1) + (NSA_CMP_BLOCK - 1)
    valid = cmp_end <= qpos
    gates = gates_ref[...]
    psum = jnp.zeros((t, n_slots), F32)
    for r in range(NSA_REP):
        q = q_ref[:, r * HEAD_DIM:(r + 1) * HEAD_DIM]
        s = jnp.where(valid, _dot_nt(q, kc) + bias_ref[r], NEG)
        m = jnp.max(s, axis=-1, keepdims=True)
        p = jnp.where(valid, jnp.exp2(s - m), 0.0)
        p = p / jnp.maximum(jnp.sum(p, axis=-1, keepdims=True), 1e-30)
        psum = psum + p
        oc_ref[:, r * HEAD_DIM:(r + 1) * HEAD_DIM] = gates[:, r:r + 1] * _dot(p.astype(BF16), vc)

    hi, lo = _split_bf16(psum)
    ovt = ovt_ref[...]
    imp = _dot_nt(ovt, hi) + _dot_nt(ovt, lo)
    blk = lax.broadcasted_iota(jnp.int32, (n_sel, t), 0)
    tpos = i * t + lax.broadcasted_iota(jnp.int32, (n_sel, t), 1)
    cur = tpos // NSA_SEL_BLOCK
    forced = (blk == 0) | (blk == cur) | (blk == cur - 1)
    score = jnp.where(blk * NSA_SEL_BLOCK <= tpos, jnp.where(forced, FORCED_SCORE, imp), -1.0)
    rank = jnp.zeros((n_sel, t), F32)
    for mm in range(n_sel):
        sm = score[mm:mm + 1, :]
        ahead = (sm > score) | ((sm == score) & (blk > mm))
        rank = rank + jnp.where(ahead, 1.0, 0.0)
    sel_t = jnp.where(rank < n_top, 1.0, 0.0).astype(BF16)
    row, col = _tile_iota(t)
    eye = jnp.where(row == col, 1.0, 0.0).astype(BF16)
    sel_ref[0, 0] = _dot_nt(eye, sel_t).astype(sel_ref.dtype)


def _nsa_cmp_attention(q_all, kvc, bias_c, gates, *, batch, seq_len, t=NSA_CMP_T):
    g = NSA_GROUPS
    nq = seq_len // t
    n_slots = seq_len // NSA_CMP_STRIDE
    n_sel = seq_len // NSA_SEL_BLOCK
    c0 = NSA_CMP_STRIDE * np.arange(n_slots)[:, None]
    s0 = NSA_SEL_BLOCK * np.arange(n_sel)[None, :]
    overlap = (c0 < s0 + NSA_SEL_BLOCK) & (c0 + NSA_CMP_BLOCK > s0)
    ovt = jnp.asarray(overlap.T.astype(np.float32), dtype=BF16)
    gw = NSA_REP * HEAD_DIM
    return pl.pallas_call(
        functools.partial(_nsa_cmp_kernel, t=t, n_slots=n_slots, n_sel=n_sel, n_top=min(NSA_TOPN, n_sel)),
        grid=(batch, g, nq),
        in_specs=[pl.BlockSpec((t, gw), lambda b, gg, i: (b * nq + i, gg)),
                  pl.BlockSpec((1, 1, 1, n_slots, HEAD_DIM), lambda b, gg, i: (b, 0, gg, 0, 0)),
                  pl.BlockSpec((1, 1, 1, n_slots, HEAD_DIM), lambda b, gg, i: (b, 1, gg, 0, 0)),
                  pl.BlockSpec((NSA_REP, t, n_slots), lambda b, gg, i: (gg, i, 0)),
                  pl.BlockSpec((t, LANE), lambda b, gg, i: (b * nq + i, gg)),
                  pl.BlockSpec((n_sel, n_slots), lambda b, gg, i: (0, 0))],
        out_specs=[pl.BlockSpec((t, gw), lambda b, gg, i: (b * nq + i, gg)),
                   pl.BlockSpec((1, 1, t, n_sel), lambda b, gg, i: (b, gg, i, 0))],
        out_shape=[jax.ShapeDtypeStruct((batch * seq_len, g * gw), F32),
                   jax.ShapeDtypeStruct((batch, g, seq_len, n_sel), BF16)],
        compiler_params=_params("parallel", "parallel", "arbitrary"),
        name="nsa_cmp_attention",
    )(q_all, kvc, kvc, bias_c, gates, ovt)


def _nsa_main_kernel(q_ref, ks_ref, vs_ref, kw_ref, vw_ref, sel_ref, tab_ref, gates_ref, oc_ref, o_ref,
                     s_scr, p_scr, m_scr, l_scr, acc_scr, os_scr, mk_scr, *, t):
    i = pl.program_id(2)
    rep = NSA_REP
    q = jnp.concatenate([q_ref[:, r * HEAD_DIM:(r + 1) * HEAD_DIM] for r in range(rep)], axis=0)

    sel = sel_ref[0, 0]
    n_sel = sel.shape[1]
    blk_row = lax.broadcasted_iota(jnp.int32, (n_sel, t), 0)
    key_col = lax.broadcasted_iota(jnp.int32, (n_sel, t), 1)
    _osm_reset(m_scr, l_scr, acc_scr)

    def sel_scores(kb, slot):
        s_scr[slot] = _dot_nt(q, ks_ref[pl.ds(pl.multiple_of(kb * t, t), t), :])

    def sel_update(kb, slot):
        start = pl.multiple_of(kb * t, t)
        expand = jnp.where((start + key_col) // NSA_SEL_BLOCK == blk_row, 1.0, 0.0).astype(BF16)
        mk_scr[...] = _dot(sel, expand)
        off = _table_offset(kb, i, t)
        for r in range(rep):
            rows = pl.ds(r * t, t)
            t2 = s_scr[slot, rows, :] + tab_ref[r, pl.ds(off, t), :]
            t2 = jnp.where(mk_scr[...] > 0.5, t2, MASKED)
            _osm_update(t2, rows, slot, p_scr, m_scr, l_scr, acc_scr)
        _osm_accum(vs_ref[pl.ds(start, t), :], slot, p_scr, l_scr, acc_scr)

    _pipelined_tiles(i + 1, sel_scores, sel_update)
    os_scr[...] = acc_scr[...] / l_scr[...]

    n_back = NSA_WINDOW // t
    assert n_back == 2
    _osm_reset(m_scr, l_scr, acc_scr)

    def win_scores(kb, slot):
        s_scr[slot] = _dot_nt(q, kw_ref[pl.ds(pl.multiple_of(kb * t, t), t), :])

    def win_update(kb, slot, region):
        for r in range(rep):
            rows = pl.ds(r * t, t)
            t2 = s_scr[slot, rows, :] + tab_ref[r, pl.ds(region * t, t), :]
            _osm_update(t2, rows, slot, p_scr, m_scr, l_scr, acc_scr)
        _osm_accum(vw_ref[pl.ds(pl.multiple_of(kb * t, t), t), :], slot, p_scr, l_scr, acc_scr)

    def window(n_tiles):
        regions = (TAB_DIAG, TAB_SUB, TAB_EDGE)
        for back in range(n_tiles):
            win_scores(i - back, back)
        for back in reversed(range(n_tiles)):
            win_update(i - back, back, regions[back])

    pl.when(i >= 2)(lambda: window(3))
    pl.when(i == 1)(lambda: window(2))
    pl.when(i == 0)(lambda: window(1))
    o_w = acc_scr[...] / l_scr[...]
    o_s = os_scr[...]

    gates = gates_ref[...]
    for r in range(rep):
        rows = slice(r * t, (r + 1) * t)
        cols = slice(r * HEAD_DIM, (r + 1) * HEAD_DIM)
        o = oc_ref[:, cols] + gates[:, rep + r:rep + r + 1] * o_s[rows] + gates[:, 2 * rep + r:2 * rep + r + 1] * o_w[rows]
        o_ref[:, cols] = o.astype(o_ref.dtype)


def _nsa_main_attention(qkv, sel, tables, gates, oc, *, batch, seq_len, t=ATT_T):
    g = NSA_GROUPS
    nq = seq_len // t
    n_sel = seq_len // NSA_SEL_BLOCK
    gw = NSA_REP * HEAD_DIM
    qb = N_HEADS
    kv = lambda which: pl.BlockSpec((seq_len, HEAD_DIM), lambda b, gg, i: (b, qb + which * g + gg))
    tile = pl.BlockSpec((t, gw), lambda b, gg, i: (b * nq + i, gg))
    rows = NSA_REP * t
    return pl.pallas_call(
        functools.partial(_nsa_main_kernel, t=t),
        grid=(batch, g, nq),
        in_specs=[tile, kv(0), kv(1), kv(2), kv(3),
                  pl.BlockSpec((1, 1, t, n_sel), lambda b, gg, i: (b, gg, i, 0)),
                  pl.BlockSpec((NSA_REP, 4 * t, t), lambda b, gg, i: (gg, 0, 0)),
                  pl.BlockSpec((t, LANE), lambda b, gg, i: (b * nq + i, gg)),
                  tile],
        out_specs=tile,
        out_shape=jax.ShapeDtypeStruct((batch * seq_len, g * gw), BF16),
        scratch_shapes=_att_scratch(rows, t, slots=3) + [pltpu.VMEM((rows, HEAD_DIM), F32), pltpu.VMEM((t, t), F32)],
        compiler_params=_params("parallel", "parallel", "arbitrary"),
        name="nsa_main_attention",
    )(qkv, qkv, qkv, qkv, qkv, sel, tables, gates, oc)


def _rope_tables(seq_len, width):
    half = MLA_ROPE // 2
    inv = np.power(ROPE_THETA, -np.arange(half, dtype=np.float32) / half).astype(np.float32)
    ang = np.arange(seq_len, dtype=np.float32)[:, None] * inv[None, :]
    reps = width // half
    return jnp.asarray(np.tile(np.cos(ang), (1, reps))), jnp.asarray(np.tile(np.sin(ang), (1, reps)))


def _rope_weights(w):
    k, n, _ = w.shape
    half = MLA_ROPE // 2
    wb = jnp.concatenate([-w[..., half:], w[..., :half]], axis=-1)
    return w.reshape(k, n * MLA_ROPE).astype(BF16), wb.reshape(k, n * MLA_ROPE).astype(BF16)


def kernel(x, c, t5_bias, ada_w, ada_b, norm_g, final_g, ffn_w_up, ffn_conv_w, ffn_conv_b, ffn_w_down, sb_w_in, sb_w_out, nsa_w_in, nsa_cmp_pe, nsa_cmp_w1, nsa_cmp_w2, nsa_w_out, diff_w_in, diff_lambda, diff_head_g, diff_w_out, mla_w_in, mla_q_g, mla_w_qb, mla_kv_g, mla_w_kvb, mla_w_out):
    batch, seq_len, d = x.shape
    depth = ada_w.shape[0]
    h, dh, g = N_HEADS, HEAD_DIM, NSA_GROUPS
    sizes = dict(batch=batch, seq_len=seq_len)

    mod = _ada_mod(c, ada_w, ada_b)
    tables = _attention_tables(t5_bias, ATT_T)

    xf = x.reshape(batch * seq_len, d)
    for i in range(depth):
        mixer, j = i % 4, i // 4
        sh1, sc1, gt1, sh2, sc2, gt2 = (mod[i, :, n * d:(n + 1) * d].reshape(batch, 1, d) for n in range(6))
        nm = functools.partial(_norm_mm, xf, norm_g[i, 0], seq_len=seq_len, shift=sh1, scale=sc1)
        if mixer == 0:
            w_in = sb_w_in[j]
            w_in = jnp.concatenate([w_in[:, :h * dh] * SB_QSCALE, w_in[:, h * dh:]], axis=1)
            qkv = nm(w_in.astype(BF16), name="sb_in")
            o = _sb_attention(qkv, **sizes)
            w_out = sb_w_out[j]
        elif mixer == 1:
            w_in = nsa_w_in[j]
            n_q, n_kv = h * dh, g * dh
            w_att = jnp.concatenate([w_in[:, :n_q] * NSA_QSCALE, w_in[:, n_q + 2 * n_kv:n_q + 6 * n_kv]], axis=1)
            w_cmp = w_in[:, n_q:n_q + 2 * n_kv]
            w_g = w_in[:, n_q + 6 * n_kv:].reshape(d, 3, g, NSA_REP).transpose(0, 2, 1, 3).reshape(d, g, 3 * NSA_REP)
            w_g = jnp.pad(w_g, ((0, 0), (0, 0), (0, LANE - 3 * NSA_REP))).reshape(d, g * LANE)
            qkv, hn = nm(w_att.astype(BF16), emit_h=True, name="nsa_in")
            hmm = functools.partial(_norm_mm, hn, None, seq_len=seq_len, normed=True)
            raw = hmm(w_cmp.astype(BF16), out_dtype=F32, name="nsa_in_cmp")
            gates = hmm(w_g.astype(BF16), out_dtype=F32, act="sigmoid", name="nsa_in_gates")
            kvc = _nsa_compress(raw, nsa_cmp_pe[j], nsa_cmp_w1[j].reshape(2, NSA_CMP_BLOCK, dh, dh).astype(BF16),
                                nsa_cmp_w2[j].astype(BF16), **sizes)
            n_slots = seq_len // NSA_CMP_STRIDE
            dist_c = np.arange(seq_len)[:, None] - (NSA_CMP_STRIDE * np.arange(n_slots)[None, :] + NSA_CMP_BLOCK - 1)
            bias_c = _t5_gather(t5_bias, _t5_bucket_np(dist_c), tr=min(seq_len, 512), mult=LOG2E)
            oc, sel = _nsa_cmp_attention(qkv, kvc, bias_c, gates, **sizes)
            o = _nsa_main_attention(qkv, sel, tables, gates, oc, **sizes)
            w_out = nsa_w_out[j]
        elif mixer == 2:
            lambda_init = 0.8 - 0.6 * math.exp(-0.3 * i)
            w_in = diff_w_in[j]
            w_in = jnp.concatenate([w_in[:, :h * dh] * DIFF_QSCALE, w_in[:, h * dh:]], axis=1)
            qkv = nm(w_in.astype(BF16), name="diff_in")
            o = _diff_attention(qkv, _attention_tables(t5_bias, DIFF_T), diff_lambda[j], diff_head_g[j],
                                lambda_init=lambda_init, **sizes)
            w_out = diff_w_out[j]
        else:
            w_in = mla_w_in[j]
            nq_l, nkv_l = MLA_Q_LORA, MLA_KV_LORA
            w_lat = jnp.concatenate([w_in[:, :nq_l], jnp.zeros((d, 2 * nkv_l - nq_l), w_in.dtype),
                                     w_in[:, nq_l:nq_l + nkv_l]], axis=1)
            lat, hn = nm(w_lat.astype(BF16), out_dtype=F32, emit_h=True, name="mla_in")
            w_kr = w_in[:, nq_l + nkv_l:]
            zero = jnp.zeros_like(w_kr)
            cos2, sin2 = _rope_tables(seq_len, 2 * LANE)
            wa, wb = _rope_weights(jnp.stack([w_kr, zero, zero, w_kr], axis=1))
            kr = _norm_mm(hn, None, wa, wb=wb, cos=cos2, sin=sin2, seq_len=seq_len, normed=True, name="mla_in_rope")
            w_qb = (mla_w_qb[j] * MLA_QSCALE).reshape(nq_l, h, MLA_NOPE + MLA_ROPE)
            qn = _norm_mm(lat, mla_q_g[j], w_qb[:, :, :MLA_NOPE].reshape(nq_l, h * MLA_NOPE).astype(BF16),
                          seq_len=seq_len, x_cols=nq_l, x_col_block=0, name="mla_q_nope")
            cosh, sinh = _rope_tables(seq_len, h * MLA_ROPE)
            wa, wb = _rope_weights(w_qb[:, :, MLA_NOPE:])
            qr = _norm_mm(lat, mla_q_g[j], wa, wb=wb, cos=cosh, sin=sinh, seq_len=seq_len, x_cols=nq_l,
                          x_col_block=0, name="mla_q_rope")
            kv = _norm_mm(lat, mla_kv_g[j], mla_w_kvb[j].astype(BF16), seq_len=seq_len, x_cols=nkv_l,
                          x_col_block=2, name="mla_kv")
            o = _mla_attention(qn, qr, kv, kr, **sizes)
            w_out = mla_w_out[j]
        xf = _mm_residual(o, w_out.astype(BF16), xf, gt1, seq_len=seq_len)
        xf = _conv_ffn(xf, norm_g[i, 1], sh2, sc2, ffn_w_up[i].astype(BF16), ffn_conv_w[i], ffn_conv_b[i],
                       ffn_w_down[i].astype(BF16), gt2, final_g, seq_len=seq_len, final_norm=(i == depth - 1))
    return xf.reshape(batch, seq_len, d)
```

```python
import functools
import math

import numpy as np
import jax
import jax.numpy as jnp
from jax import lax
from jax.experimental import pallas as pl
from jax.experimental.pallas import tpu as pltpu

F32 = jnp.float32
BF16 = jnp.bfloat16
EPS = 1e-6
NEG = -1e30

LANE = 128
HALO = 16
VMEM_LIMIT = 56 * 2**20

T5_BUCKETS = 32
T5_MAX_DIST = 128
N_HEADS = 16
HEAD_DIM = 128
NSA_GROUPS = 4
NSA_REP = 4
NSA_CMP_BLOCK = 32
NSA_CMP_STRIDE = 16
NSA_SEL_BLOCK = 64
NSA_TOPN = 16
NSA_WINDOW = 512
FORCED_SCORE = 1e9
DIFF_DIM = 64
MLA_Q_LORA = 768
MLA_KV_LORA = 512
MLA_NOPE = 128
MLA_ROPE = 64
ROPE_THETA = 10000.0
CONV_WIDTH = 3
ATT_T = 256
NSA_CMP_T = 512
DIFF_T = 512
MLA_T = 512
ATT_TQ = 512
ATT_TK = 256
LOG2E = 1.4426950408889634
SB_QSCALE = -(HEAD_DIM ** -0.5)
NSA_QSCALE = HEAD_DIM ** -0.5 * LOG2E
DIFF_QSCALE = DIFF_DIM ** -0.5 * LOG2E
MLA_QSCALE = (MLA_NOPE + MLA_ROPE) ** -0.5 * LOG2E
M_INIT = -1e30
MASKED = -2e30
SB_DEAD = -120.0
T5_MASKED = T5_BUCKETS


def _params(*sem):
    return pltpu.CompilerParams(dimension_semantics=sem, vmem_limit_bytes=VMEM_LIMIT)


def _dot(a, b):
    return jnp.dot(a, b, preferred_element_type=F32)


def _dot_nt(a, b):
    return lax.dot_general(a, b, (((1,), (1,)), ((), ())), preferred_element_type=F32)


def _lane_tile(x, reps):
    return jnp.concatenate([x] * reps, axis=1)


def _split_bf16(x):
    hi = x.astype(BF16)
    lo = (x - hi.astype(F32)).astype(BF16)
    return hi, lo


def _rms(x, g):
    ms = jnp.mean(x * x, axis=-1, keepdims=True)
    return x * lax.rsqrt(ms + EPS) * g


def _ada_kernel(c_ref, w_ref, b_ref, o_ref):
    c = c_ref[...]
    a = c * jax.nn.sigmoid(c)
    a_hi, a_lo = _split_bf16(a)
    w_hi, w_lo = _split_bf16(w_ref[0])
    o_ref[0] = _dot(a_hi, w_hi) + _dot(a_lo, w_hi) + _dot(a_hi, w_lo) + b_ref[0]


def _ada_mod(c, ada_w, ada_b, tn=1024):
    depth, d, n = ada_w.shape
    b = c.shape[0]
    return pl.pallas_call(
        _ada_kernel,
        grid=(depth, n // tn),
        in_specs=[pl.BlockSpec((b, d), lambda l, j: (0, 0)),
                  pl.BlockSpec((1, d, tn), lambda l, j: (l, 0, j)),
                  pl.BlockSpec((1, 1, tn), lambda l, j: (l, 0, j))],
        out_specs=pl.BlockSpec((1, b, tn), lambda l, j: (l, 0, j)),
        out_shape=jax.ShapeDtypeStruct((depth, b, n), F32),
        compiler_params=_params("parallel", "parallel"),
        name="ada_mod",
    )(c, ada_w, ada_b.reshape(depth, 1, n))


def _norm_mm_kernel(*refs, modulated, rope, act, normed):
    it = iter(refs)
    x_ref = next(it)
    g_ref = sh_ref = sc_ref = wb_ref = cos_ref = sin_ref = None
    if not normed:
        g_ref = next(it)
        if modulated:
            sh_ref, sc_ref = next(it), next(it)
    w_ref = next(it)
    if rope:
        wb_ref, cos_ref, sin_ref = next(it), next(it), next(it)
    o_ref = next(it)
    if normed:
        h = x_ref[...]
    else:
        h_ref = next(it)

        @pl.when(pl.program_id(1) == 0)
        def _():
            y = _rms(x_ref[...], g_ref[...])
            if modulated:
                y = y * (1.0 + sc_ref[0]) + sh_ref[0]
            h_ref[...] = y.astype(BF16)

        h = h_ref[...]
    acc = _dot(h, w_ref[...])
    if rope:
        acc = acc * cos_ref[...] + _dot(h, wb_ref[...]) * sin_ref[...]
    if act == "sigmoid":
        acc = jax.nn.sigmoid(acc)
    o_ref[...] = acc.astype(o_ref.dtype)


def _norm_mm(x, g, w, *, seq_len, shift=None, scale=None, wb=None, cos=None, sin=None, act=None, normed=False,
             emit_h=False, out_dtype=BF16, x_cols=None, x_col_block=0, tm=1024, tn=1024, name="norm_mm"):
    m = x.shape[0]
    k = x.shape[1] if x_cols is None else x_cols
    n = w.shape[1]
    while n % tn:
        tn //= 2
    modulated, rope = shift is not None, wb is not None
    tps = seq_len // tm
    in_specs = [pl.BlockSpec((tm, k), lambda i, j: (i, x_col_block))]
    args = [x]
    if not normed:
        in_specs.append(pl.BlockSpec((1, k), lambda i, j: (0, 0)))
        args.append(g.reshape(1, k))
        if modulated:
            in_specs += [pl.BlockSpec((1, 1, k), lambda i, j: (i // tps, 0, 0))] * 2
            args += [shift, scale]
    in_specs.append(pl.BlockSpec((k, tn), lambda i, j: (0, j)))
    args.append(w)
    if rope:
        in_specs.append(pl.BlockSpec((k, tn), lambda i, j: (0, j)))
        in_specs += [pl.BlockSpec((tm, tn), lambda i, j: (i % tps, j))] * 2
        args += [wb, cos, sin]
    out_specs = [pl.BlockSpec((tm, tn), lambda i, j: (i, j))]
    out_shape = [jax.ShapeDtypeStruct((m, n), out_dtype)]
    scratch = []
    if emit_h:
        out_specs.append(pl.BlockSpec((tm, k), lambda i, j: (i, 0)))
        out_shape.append(jax.ShapeDtypeStruct((m, k), BF16))
    elif not normed:
        scratch.append(pltpu.VMEM((tm, k), BF16))
    out = pl.pallas_call(
        functools.partial(_norm_mm_kernel, modulated=modulated, rope=rope, act=act, normed=normed),
        grid=(m // tm, n // tn),
        in_specs=in_specs,
        out_specs=out_specs,
        out_shape=out_shape,
        scratch_shapes=scratch,
        compiler_params=_params("parallel", "arbitrary"),
        name=name,
    )(*args)
    return out if emit_h else out[0]


def _mm_res_kernel(a_ref, w_ref, x_ref, gate_ref, o_ref):
    o_ref[...] = x_ref[...] + gate_ref[0] * _dot(a_ref[...], w_ref[...])


def _mm_residual(a, w, x, gate, *, seq_len, tm=1024, tn=1024, name="mm_residual"):
    m, k = a.shape
    n = w.shape[1]
    tps = seq_len // tm
    return pl.pallas_call(
        _mm_res_kernel,
        grid=(m // tm, n // tn),
        in_specs=[pl.BlockSpec((tm, k), lambda i, j: (i, 0)),
                  pl.BlockSpec((k, tn), lambda i, j: (0, j)),
                  pl.BlockSpec((tm, tn), lambda i, j: (i, j)),
                  pl.BlockSpec((1, 1, tn), lambda i, j: (i // tps, 0, j))],
        out_specs=pl.BlockSpec((tm, tn), lambda i, j: (i, j)),
        out_shape=jax.ShapeDtypeStruct((m, n), F32),
        compiler_params=_params("parallel", "arbitrary"),
        name=name,
    )(a, w, x, gate)


def _ffn_kernel(x_ref, xp_ref, g_ref, sh_ref, sc_ref, wg_ref, wu_ref, cwg_ref, cwu_ref, cbg_ref, cbu_ref,
                wd_ref, gate_ref, fg_ref, o_ref, h_ref, acc_ref, ug_scr, uu_scr, *, tiles_per_seq, final_norm):
    i, j = pl.program_id(0), pl.program_id(1)

    @pl.when(j == 0)
    def _():
        def nm(x):
            return _rms(x, g_ref[...]) * (1.0 + sc_ref[0]) + sh_ref[0]
        h_ref[HALO:, :] = nm(x_ref[...]).astype(BF16)
        keep = jnp.where(i % tiles_per_seq == 0, 0.0, 1.0)
        h_ref[:HALO, :] = (nm(xp_ref[...]) * keep).astype(BF16)
        acc_ref[...] = jnp.zeros_like(acc_ref)

    h = h_ref[...]

    def branch(w_ref, cw_ref, cb_ref, u_scr):
        u_scr[...] = _dot(h, w_ref[...])
        tm = u_scr.shape[0] - HALO
        cw = cw_ref[...]
        return (cw[0:1] * u_scr[pl.ds(HALO - 2, tm), :] + cw[1:2] * u_scr[pl.ds(HALO - 1, tm), :]
                + cw[2:3] * u_scr[pl.ds(HALO, tm), :] + cb_ref[...])

    gt = branch(wg_ref, cwg_ref, cbg_ref, ug_scr)
    up = branch(wu_ref, cwu_ref, cbu_ref, uu_scr)
    a = gt * jax.nn.sigmoid(gt) * up
    acc_ref[...] += _dot(a.astype(BF16), wd_ref[...])

    @pl.when(j == pl.num_programs(1) - 1)
    def _():
        y = x_ref[...] + gate_ref[0] * acc_ref[...]
        if final_norm:
            y = _rms(y, fg_ref[...])
        o_ref[...] = y


def _conv_ffn(x, g, shift, scale, w_up, conv_w, conv_b, w_down, gate, final_g, *, seq_len, final_norm,
              tm=512, tf=512):
    m, d = x.shape
    f = w_down.shape[0]
    nf = f // tf
    tps = seq_len // tm
    hb = tm // HALO
    row = lambda i, j: (i, 0)
    per_batch = lambda i, j: (i // tps, 0, 0)
    return pl.pallas_call(
        functools.partial(_ffn_kernel, tiles_per_seq=tps, final_norm=final_norm),
        grid=(m // tm, nf),
        in_specs=[pl.BlockSpec((tm, d), row),
                  pl.BlockSpec((HALO, d), lambda i, j: (jnp.maximum(i * hb - 1, 0), 0)),
                  pl.BlockSpec((1, d), lambda i, j: (0, 0)),
                  pl.BlockSpec((1, 1, d), per_batch),
                  pl.BlockSpec((1, 1, d), per_batch),
                  pl.BlockSpec((d, tf), lambda i, j: (0, j)),
                  pl.BlockSpec((d, tf), lambda i, j: (0, j + nf)),
                  pl.BlockSpec((CONV_WIDTH, tf), lambda i, j: (0, j)),
                  pl.BlockSpec((CONV_WIDTH, tf), lambda i, j: (0, j + nf)),
                  pl.BlockSpec((1, tf), lambda i, j: (0, j)),
                  pl.BlockSpec((1, tf), lambda i, j: (0, j + nf)),
                  pl.BlockSpec((tf, d), lambda i, j: (j, 0)),
                  pl.BlockSpec((1, 1, d), per_batch),
                  pl.BlockSpec((1, d), lambda i, j: (0, 0))],
        out_specs=pl.BlockSpec((tm, d), row),
        out_shape=jax.ShapeDtypeStruct((m, d), F32),
        scratch_shapes=[pltpu.VMEM((HALO + tm, d), BF16), pltpu.VMEM((tm, d), F32),
                        pltpu.VMEM((HALO + tm, tf), F32), pltpu.VMEM((HALO + tm, tf), F32)],
        compiler_params=_params("parallel", "arbitrary"),
        name="conv_ffn",
    )(x, x, g.reshape(1, d), shift, scale, w_up, w_up, conv_w, conv_w, conv_b.reshape(1, -1),
      conv_b.reshape(1, -1), w_down, gate, final_g.reshape(1, d))


def _t5_bucket_np(dist):
    n = np.maximum(dist, 0)
    max_exact = T5_BUCKETS // 2
    nf = np.maximum(n, 1).astype(np.float64)
    val = np.log(nf / max_exact) / math.log(T5_MAX_DIST / max_exact) * (T5_BUCKETS - max_exact)
    large = max_exact + np.trunc(val + 1e-6).astype(np.int64)
    return np.where(n < max_exact, n, np.minimum(large, T5_BUCKETS - 1)).astype(np.int32)


def _t5_gather_kernel(t5_ref, bkt_ref, o_ref, *, mult):
    h = pl.program_id(0)
    bk = bkt_ref[...]
    acc = jnp.zeros(bk.shape, F32)
    for b in range(T5_BUCKETS):
        acc = jnp.where(bk == b, t5_ref[b, h], acc)
    o_ref[0] = jnp.where(bk == T5_MASKED, MASKED, acc * mult)


def _t5_gather(t5_bias, bkt, tr, mult=1.0):
    rows, cols = bkt.shape
    heads = t5_bias.shape[1]
    return pl.pallas_call(
        functools.partial(_t5_gather_kernel, mult=mult),
        grid=(heads, rows // tr),
        in_specs=[pl.BlockSpec(memory_space=pltpu.SMEM),
                  pl.BlockSpec((tr, cols), lambda h, r: (r, 0))],
        out_specs=pl.BlockSpec((1, tr, cols), lambda h, r: (h, r, 0)),
        out_shape=jax.ShapeDtypeStruct((heads, rows, cols), F32),
        compiler_params=_params("parallel", "parallel"),
        name="t5_gather",
    )(t5_bias, jnp.asarray(bkt))


TAB_DIAG, TAB_SUB, TAB_FAR, TAB_EDGE = 0, 1, 2, 3


def _attention_tables(t5_bias, t):
    i = np.arange(t)[:, None]
    j = np.arange(t)[None, :]
    assert int(_t5_bucket_np(np.array(t + 1))) == T5_BUCKETS - 1
    far = np.full((t, t), T5_BUCKETS - 1, np.int32)
    bkt = np.concatenate([np.where(j <= i, _t5_bucket_np(i - j), T5_MASKED), _t5_bucket_np(t + i - j), far,
                          np.where(j > i, far, T5_MASKED)], axis=0).astype(np.int32)
    return _t5_gather(t5_bias, bkt, tr=t, mult=LOG2E)


def _tile_iota(t):
    return lax.broadcasted_iota(jnp.int32, (t, t), 0), lax.broadcasted_iota(jnp.int32, (t, t), 1)


def _rect_iota(i, tq, start, tk):
    rowg = i * tq + lax.broadcasted_iota(jnp.int32, (tq, tk), 0)
    colg = start + lax.broadcasted_iota(jnp.int32, (tq, tk), 1)
    return rowg, colg


def _osm_reset(m_scr, l_scr, acc_scr):
    m_scr[...] = jnp.full(m_scr.shape, M_INIT, F32)
    l_scr[...] = jnp.zeros(l_scr.shape, F32)
    acc_scr[...] = jnp.zeros(acc_scr.shape, F32)


def _osm_update(t2, rows, slot, p_scr, m_scr, l_scr, acc_scr):
    m_prev = m_scr[rows, :]
    m_new = jnp.maximum(m_prev, jnp.max(t2, axis=-1, keepdims=True))
    p = jnp.exp2(t2 - _lane_tile(m_new, t2.shape[1] // LANE))
    alpha = jnp.exp2(m_prev - m_new)
    l_scr[rows, :] = alpha * l_scr[rows, :]
    m_scr[rows, :] = m_new
    p_scr[slot, rows, :] = p.astype(BF16)
    acc_scr[rows, :] = alpha * acc_scr[rows, :]


def _osm_accum(v, slot, p_scr, l_scr, acc_scr):
    vext = jnp.concatenate([v, jnp.ones(v.shape, v.dtype)], axis=-1)
    half = p_scr.shape[1] // 2
    for part in range(2):
        rows = pl.ds(part * half, half)
        pv = _dot(p_scr[slot, rows, :], vext)
        acc_scr[rows, :] += pv[:, :HEAD_DIM]
        l_scr[rows, :] += pv[:, HEAD_DIM:]


def _pipelined_tiles(n_tiles, scores, update, update_last=None):
    update_last = update if update_last is None else update_last
    last = n_tiles - 1
    n_pairs = last // 2
    scores(0, 0)

    def pair(n, cr):
        scores(2 * n + 1, 1)
        update(2 * n, 0)
        scores(2 * n + 2, 0)
        update(2 * n + 1, 1)
        return cr

    lax.fori_loop(0, n_pairs, pair, 0)

    @pl.when(last % 2 == 0)
    def _():
        update_last(last, 0)

    @pl.when(last % 2 == 1)
    def _():
        scores(last, 1)
        update(last - 1, 0)
        update_last(last, 1)


def _att_scratch(rows, tk, slots=2):
    return [pltpu.VMEM((slots, rows, tk), F32), pltpu.VMEM((slots, rows, tk), BF16), pltpu.VMEM((rows, LANE), F32),
            pltpu.VMEM((rows, LANE), F32), pltpu.VMEM((rows, HEAD_DIM), F32)]


def _sb_kernel(q_ref, k_ref, v_ref, o_ref, s_scr, c_scr, acc_scr, *, tq, tk):
    i = pl.program_id(2)
    q = q_ref[...]
    row, col = _tile_iota(tk)
    upper = jnp.where(row > col, 1.0, 0.0).astype(BF16)
    c_scr[...] = jnp.zeros(c_scr.shape, F32)
    acc_scr[...] = jnp.zeros(acc_scr.shape, F32)
    reps = tk // LANE

    def scores(kb, slot):
        start = pl.multiple_of(kb * tk, tk)
        s_scr[slot] = _dot_nt(q, k_ref[pl.ds(start, tk), :])

    def update(kb, slot, diag):
        start = pl.multiple_of(kb * tk, tk)
        nz = s_scr[slot]
        e = jnp.exp2(jnp.abs(nz) * (-LOG2E))
        lk = jnp.minimum(nz, 0.0) - jnp.log(1.0 + e)
        if diag:
            rowg, colg = _rect_iota(i, tq, start, tk)
            past = colg < rowg
            lk = jnp.where(past, lk, 0.0)
        hi, lo = _split_bf16(lk)
        c = c_scr[...]
        later = _dot(hi, upper) + _dot(lo, upper) + _lane_tile(c, reps)
        a = jnp.exp(lk - nz + later)
        if diag:
            a = jnp.where(past, a, 0.0)
        acc_scr[...] += _dot(a.astype(BF16), v_ref[pl.ds(start, tk), :])
        c_scr[...] = c + jnp.sum(lk, axis=-1, keepdims=True)

    assert tq == 2 * tk
    scores(2 * i + 1, 1)
    scores(2 * i, 0)
    update(2 * i + 1, 1, True)
    scores(jnp.maximum(2 * i - 1, 0), 1)
    update(2 * i, 0, True)

    def more(carry):
        n, c_max = carry
        return (n < i) & (c_max > SB_DEAD)

    def pair(carry):
        n, _ = carry
        kb = 2 * (i - n) - 1
        scores(kb - 1, 0)
        update(kb, 1, False)
        c_mid = jnp.max(c_scr[...])

        @pl.when(c_mid > SB_DEAD)
        def _():
            scores(jnp.maximum(kb - 2, 0), 1)
            update(kb - 1, 0, False)

        return n + 1, jnp.where(c_mid > SB_DEAD, jnp.max(c_scr[...]), c_mid)

    lax.while_loop(more, pair, (0, jnp.max(c_scr[...])))
    o_ref[...] = acc_scr[...].astype(o_ref.dtype)


def _sb_attention(qkv, *, batch, seq_len, tq=ATT_TQ, tk=ATT_TK):
    h = N_HEADS
    nq = seq_len // tq
    return pl.pallas_call(
        functools.partial(_sb_kernel, tq=tq, tk=tk),
        grid=(batch, h, nq),
        in_specs=[pl.BlockSpec((tq, HEAD_DIM), lambda b, hh, i: (b * nq + i, hh)),
                  pl.BlockSpec((seq_len, HEAD_DIM), lambda b, hh, i: (b, h + hh)),
                  pl.BlockSpec((seq_len, HEAD_DIM), lambda b, hh, i: (b, 2 * h + hh))],
        out_specs=pl.BlockSpec((tq, HEAD_DIM), lambda b, hh, i: (b * nq + i, hh)),
        out_shape=jax.ShapeDtypeStruct((batch * seq_len, h * HEAD_DIM), BF16),
        scratch_shapes=[pltpu.VMEM((2, tq, tk), F32), pltpu.VMEM((tq, LANE), F32), pltpu.VMEM((tq, HEAD_DIM), F32)],
        compiler_params=_params("parallel", "parallel", "arbitrary"),
        name="sb_attention",
    )(qkv, qkv, qkv)


def _table_offset(kb, i, t):
    return pl.multiple_of(jnp.where(kb == i, TAB_DIAG * t, jnp.where(kb == i - 1, TAB_SUB * t, TAB_FAR * t)), t)


def _diff_kernel(q_ref, k_ref, v_ref, tab_ref, lam_ref, hg_ref, o_ref, s_scr, p_scr, m_scr, l_scr, acc_scr, *, t,
                 lambda_init):
    i = pl.program_id(2)
    lam = lam_ref[...]
    lmbda = (jnp.exp(jnp.sum(lam[0:1] * lam[1:2], axis=-1, keepdims=True))
             - jnp.exp(jnp.sum(lam[2:3] * lam[3:4], axis=-1, keepdims=True)) + lambda_init)
    for a in range(2):
        lanes = slice(a * HEAD_DIM, (a + 1) * HEAD_DIM)
        q = q_ref[:, lanes]
        lane = lax.broadcasted_iota(jnp.int32, q.shape, 1)
        zero = jnp.zeros_like(q)
        q2 = jnp.concatenate([jnp.where(lane < DIFF_DIM, q, zero), jnp.where(lane >= DIFF_DIM, q, zero)], axis=0)
        _osm_reset(m_scr, l_scr, acc_scr)

        def scores(kb, slot):
            s_scr[slot] = _dot_nt(q2, k_ref[pl.ds(pl.multiple_of(kb * t, t), t), lanes])

        def update(kb, slot):
            off = _table_offset(kb, i, t)
            for half in range(2):
                rows = pl.ds(half * t, t)
                t2 = s_scr[slot, rows, :] + tab_ref[a, pl.ds(off, t), :]
                _osm_update(t2, rows, slot, p_scr, m_scr, l_scr, acc_scr)
            _osm_accum(v_ref[pl.ds(pl.multiple_of(kb * t, t), t), lanes], slot, p_scr, l_scr, acc_scr)

        _pipelined_tiles(i + 1, scores, update)

        o = acc_scr[...] / l_scr[...]
        o = o[:t] - lmbda * o[t:]
        o_ref[:, lanes] = (_rms(o, hg_ref[...]) * (1.0 - lambda_init)).astype(o_ref.dtype)


def _diff_attention(qkv, tables, lam, head_g, *, batch, seq_len, lambda_init, t=DIFF_T):
    h = N_HEADS
    nq = seq_len // t
    pw = 2 * HEAD_DIM
    pairs = h // 2
    return pl.pallas_call(
        functools.partial(_diff_kernel, t=t, lambda_init=lambda_init),
        grid=(batch, pairs, nq),
        in_specs=[pl.BlockSpec((t, pw), lambda b, pp, i: (b * nq + i, pp)),
                  pl.BlockSpec((seq_len, pw), lambda b, pp, i: (b, pairs + pp)),
                  pl.BlockSpec((seq_len, pw), lambda b, pp, i: (b, 2 * pairs + pp)),
                  pl.BlockSpec((2, 4 * t, t), lambda b, pp, i: (pp, 0, 0)),
                  pl.BlockSpec((4, DIFF_DIM), lambda b, pp, i: (0, 0)),
                  pl.BlockSpec((1, HEAD_DIM), lambda b, pp, i: (0, 0))],
        out_specs=pl.BlockSpec((t, pw), lambda b, pp, i: (b * nq + i, pp)),
        out_shape=jax.ShapeDtypeStruct((batch * seq_len, h * HEAD_DIM), BF16),
        scratch_shapes=_att_scratch(2 * t, t),
        compiler_params=_params("parallel", "parallel", "arbitrary"),
        name="diff_attention",
    )(qkv, qkv, qkv, tables, lam, head_g.reshape(1, HEAD_DIM))


def _mla_kernel(qn_ref, qr_ref, kv_ref, kr_ref, o_ref, s_scr, p_scr, m_scr, l_scr, acc_scr, *, tq, tk):
    i = pl.program_id(2)
    assert tq == tk
    for a in range(2):
        lanes = slice(a * HEAD_DIM, (a + 1) * HEAD_DIM)
        kn_lanes = slice(2 * a * HEAD_DIM, (2 * a + 1) * HEAD_DIM)
        v_lanes = slice((2 * a + 1) * HEAD_DIM, (2 * a + 2) * HEAD_DIM)
        q = jnp.concatenate([qn_ref[:, lanes], qr_ref[...]], axis=-1)
        _osm_reset(m_scr, l_scr, acc_scr)

        def scores(kb, slot):
            rows = pl.ds(pl.multiple_of(kb * tk, tk), tk)
            k = jnp.concatenate([kv_ref[rows, kn_lanes], kr_ref[rows, lanes]], axis=-1)
            s_scr[slot] = _dot_nt(q, k)

        def update(kb, slot, mask):
            start = pl.multiple_of(kb * tk, tk)
            t2 = s_scr[slot]
            if mask:
                rowg, colg = _rect_iota(i, tq, start, tk)
                t2 = jnp.where(colg <= rowg, t2, MASKED)
            _osm_update(t2, pl.ds(0, tq), slot, p_scr, m_scr, l_scr, acc_scr)
            _osm_accum(kv_ref[pl.ds(start, tk), v_lanes], slot, p_scr, l_scr, acc_scr)

        _pipelined_tiles(i + 1, scores, lambda kb, slot: update(kb, slot, False),
                         lambda kb, slot: update(kb, slot, True))
        o_ref[:, lanes] = (acc_scr[...] / l_scr[...]).astype(o_ref.dtype)


def _mla_attention(qn, qr, kv, kr, *, batch, seq_len, tq=MLA_T, tk=MLA_T):
    h = N_HEADS
    nq = seq_len // tq
    pair = pl.BlockSpec((tq, 2 * HEAD_DIM), lambda b, pp, i: (b * nq + i, pp))
    return pl.pallas_call(
        functools.partial(_mla_kernel, tq=tq, tk=tk),
        grid=(batch, h // 2, nq),
        in_specs=[pair,
                  pl.BlockSpec((tq, LANE), lambda b, pp, i: (b * nq + i, pp)),
                  pl.BlockSpec((seq_len, 4 * HEAD_DIM), lambda b, pp, i: (b, pp)),
                  pl.BlockSpec((seq_len, 2 * LANE), lambda b, pp, i: (b, 0))],
        out_specs=pair,
        out_shape=jax.ShapeDtypeStruct((batch * seq_len, h * HEAD_DIM), BF16),
        scratch_shapes=_att_scratch(tq, tk),
        compiler_params=_params("parallel", "parallel", "arbitrary"),
        name="mla_attention",
    )(qn, qr, kv, kr)


def _compress_kernel(raw_ref, pe_ref, w1_ref, w2_ref, o_ref, *, n_slots):
    half = NSA_CMP_BLOCK // 2
    p1 = jnp.zeros((n_slots, HEAD_DIM), F32)
    p2 = jnp.zeros((n_slots, HEAD_DIM), F32)
    for l in range(half):
        a = raw_ref[pl.ds(l, n_slots, stride=NSA_CMP_STRIDE), :]
        p1 = p1 + _dot((a + pe_ref[0, l:l + 1, :]).astype(BF16), w1_ref[0, l])
        p2 = p2 + _dot((a + pe_ref[0, half + l:half + l + 1, :]).astype(BF16), w1_ref[0, half + l])
    pre = p1 + pltpu.roll(p2, n_slots - 1, 0)
    hid = pre * jax.nn.sigmoid(pre)
    o_ref[0, 0, 0] = _dot(hid.astype(BF16), w2_ref[0]).astype(o_ref.dtype)


def _nsa_compress(raw, pe, w1, w2, *, batch, seq_len):
    g = NSA_GROUPS
    n_slots = seq_len // NSA_CMP_STRIDE
    return pl.pallas_call(
        functools.partial(_compress_kernel, n_slots=n_slots),
        grid=(batch, 2, g),
        in_specs=[pl.BlockSpec((seq_len, HEAD_DIM), lambda b, kv, gg: (b, kv * g + gg)),
                  pl.BlockSpec((1, NSA_CMP_BLOCK, HEAD_DIM), lambda b, kv, gg: (kv, 0, 0)),
                  pl.BlockSpec((1, NSA_CMP_BLOCK, HEAD_DIM, HEAD_DIM), lambda b, kv, gg: (kv, 0, 0, 0)),
                  pl.BlockSpec((1, HEAD_DIM, HEAD_DIM), lambda b, kv, gg: (kv, 0, 0))],
        out_specs=pl.BlockSpec((1, 1, 1, n_slots, HEAD_DIM), lambda b, kv, gg: (b, kv, gg, 0, 0)),
        out_shape=jax.ShapeDtypeStruct((batch, 2, g, n_slots, HEAD_DIM), BF16),
        compiler_params=_params("parallel", "parallel", "parallel"),
        name="nsa_compress",
    )(raw, pe, w1, w2)


def _nsa_cmp_kernel(q_ref, kc_ref, vc_ref, bias_ref, gates_ref, ovt_ref, oc_ref, sel_ref, *, t, n_slots, n_sel,
                    n_top):
    i = pl.program_id(2)
    kc = kc_ref[0, 0, 0]
    vc = vc_ref[0, 0, 0]
    qpos = i * t + lax.broadcasted_iota(jnp.int32, (t, n_slots), 0)
    cmp_end = NSA_CMP_STRIDE * lax.broadcasted_iota(jnp.int32, (t, n_slots), 1) + (NSA_CMP_BLOCK - 1)
    valid = cmp_end <= qpos
    gates = gates_ref[...]
    psum = jnp.zeros((t, n_slots), F32)
    for r in range(NSA_REP):
        q = q_ref[:, r * HEAD_DIM:(r + 1) * HEAD_DIM]
        s = jnp.where(valid, _dot_nt(q, kc) + bias_ref[r], NEG)
        m = jnp.max(s, axis=-1, keepdims=True)
        p = jnp.where(valid, jnp.exp2(s - m), 0.0)
        p = p / jnp.maximum(jnp.sum(p, axis=-1, keepdims=True), 1e-30)
        psum = psum + p
        oc_ref[:, r * HEAD_DIM:(r + 1) * HEAD_DIM] = gates[:, r:r + 1] * _dot(p.astype(BF16), vc)

    hi, lo = _split_bf16(psum)
    ovt = ovt_ref[...]
    imp = _dot_nt(ovt, hi) + _dot_nt(ovt, lo)
    blk = lax.broadcasted_iota(jnp.int32, (n_sel, t), 0)
    tpos = i * t + lax.broadcasted_iota(jnp.int32, (n_sel, t), 1)
    cur = tpos // NSA_SEL_BLOCK
    forced = (blk == 0) | (blk == cur) | (blk == cur - 1)
    score = jnp.where(blk * NSA_SEL_BLOCK <= tpos, jnp.where(forced, FORCED_SCORE, imp), -1.0)
    rank = jnp.zeros((n_sel, t), F32)
    for mm in range(n_sel):
        sm = score[mm:mm + 1, :]
        ahead = (sm > score) | ((sm == score) & (blk > mm))
        rank = rank + jnp.where(ahead, 1.0, 0.0)
    sel_t = jnp.where(rank < n_top, 1.0, 0.0).astype(BF16)
    row, col = _tile_iota(t)
    eye = jnp.where(row == col, 1.0, 0.0).astype(BF16)
    sel_ref[0, 0] = _dot_nt(eye, sel_t).astype(sel_ref.dtype)


def _nsa_cmp_attention(q_all, kvc, bias_c, gates, *, batch, seq_len, t=NSA_CMP_T):
    g = NSA_GROUPS
    nq = seq_len // t
    n_slots = seq_len // NSA_CMP_STRIDE
    n_sel = seq_len // NSA_SEL_BLOCK
    c0 = NSA_CMP_STRIDE * np.arange(n_slots)[:, None]
    s0 = NSA_SEL_BLOCK * np.arange(n_sel)[None, :]
    overlap = (c0 < s0 + NSA_SEL_BLOCK) & (c0 + NSA_CMP_BLOCK > s0)
    ovt = jnp.asarray(overlap.T.astype(np.float32), dtype=BF16)
    gw = NSA_REP * HEAD_DIM
    return pl.pallas_call(
        functools.partial(_nsa_cmp_kernel, t=t, n_slots=n_slots, n_sel=n_sel, n_top=min(NSA_TOPN, n_sel)),
        grid=(batch, g, nq),
        in_specs=[pl.BlockSpec((t, gw), lambda b, gg, i: (b * nq + i, gg)),
                  pl.BlockSpec((1, 1, 1, n_slots, HEAD_DIM), lambda b, gg, i: (b, 0, gg, 0, 0)),
                  pl.BlockSpec((1, 1, 1, n_slots, HEAD_DIM), lambda b, gg, i: (b, 1, gg, 0, 0)),
                  pl.BlockSpec((NSA_REP, t, n_slots), lambda b, gg, i: (gg, i, 0)),
                  pl.BlockSpec((t, LANE), lambda b, gg, i: (b * nq + i, gg)),
                  pl.BlockSpec((n_sel, n_slots), lambda b, gg, i: (0, 0))],
        out_specs=[pl.BlockSpec((t, gw), lambda b, gg, i: (b * nq + i, gg)),
                   pl.BlockSpec((1, 1, t, n_sel), lambda b, gg, i: (b, gg, i, 0))],
        out_shape=[jax.ShapeDtypeStruct((batch * seq_len, g * gw), F32),
                   jax.ShapeDtypeStruct((batch, g, seq_len, n_sel), BF16)],
        compiler_params=_params("parallel", "parallel", "arbitrary"),
        name="nsa_cmp_attention",
    )(q_all, kvc, kvc, bias_c, gates, ovt)


def _nsa_main_kernel(q_ref, ks_ref, vs_ref, kw_ref, vw_ref, sel_ref, tab_ref, gates_ref, oc_ref, o_ref,
                     s_scr, p_scr, m_scr, l_scr, acc_scr, os_scr, mk_scr, *, t):
    i = pl.program_id(2)
    rep = NSA_REP
    q = jnp.concatenate([q_ref[:, r * HEAD_DIM:(r + 1) * HEAD_DIM] for r in range(rep)], axis=0)

    sel = sel_ref[0, 0]
    n_sel = sel.shape[1]
    blk_row = lax.broadcasted_iota(jnp.int32, (n_sel, t), 0)
    key_col = lax.broadcasted_iota(jnp.int32, (n_sel, t), 1)
    _osm_reset(m_scr, l_scr, acc_scr)

    def sel_scores(kb, slot):
        s_scr[slot] = _dot_nt(q, ks_ref[pl.ds(pl.multiple_of(kb * t, t), t), :])

    def sel_update(kb, slot):
        start = pl.multiple_of(kb * t, t)
        expand = jnp.where((start + key_col) // NSA_SEL_BLOCK == blk_row, 1.0, 0.0).astype(BF16)
        mk_scr[...] = _dot(sel, expand)
        off = _table_offset(kb, i, t)
        for r in range(rep):
            rows = pl.ds(r * t, t)
            t2 = s_scr[slot, rows, :] + tab_ref[r, pl.ds(off, t), :]
            t2 = jnp.where(mk_scr[...] > 0.5, t2, MASKED)
            _osm_update(t2, rows, slot, p_scr, m_scr, l_scr, acc_scr)
        _osm_accum(vs_ref[pl.ds(start, t), :], slot, p_scr, l_scr, acc_scr)

    _pipelined_tiles(i + 1, sel_scores, sel_update)
    os_scr[...] = acc_scr[...] / l_scr[...]

    n_back = NSA_WINDOW // t
    assert n_back == 2
    _osm_reset(m_scr, l_scr, acc_scr)

    def win_scores(kb, slot):
        s_scr[slot] = _dot_nt(q, kw_ref[pl.ds(pl.multiple_of(kb * t, t), t), :])

    def win_update(kb, slot, region):
        for r in range(rep):
            rows = pl.ds(r * t, t)
            t2 = s_scr[slot, rows, :] + tab_ref[r, pl.ds(region * t, t), :]
            _osm_update(t2, rows, slot, p_scr, m_scr, l_scr, acc_scr)
        _osm_accum(vw_ref[pl.ds(pl.multiple_of(kb * t, t), t), :], slot, p_scr, l_scr, acc_scr)

    def window(n_tiles):
        regions = (TAB_DIAG, TAB_SUB, TAB_EDGE)
        for back in range(n_tiles):
            win_scores(i - back, back)
        for back in reversed(range(n_tiles)):
            win_update(i - back, back, regions[back])

    pl.when(i >= 2)(lambda: window(3))
    pl.when(i == 1)(lambda: window(2))
    pl.when(i == 0)(lambda: window(1))
    o_w = acc_scr[...] / l_scr[...]
    o_s = os_scr[...]

    gates = gates_ref[...]
    for r in range(rep):
        rows = slice(r * t, (r + 1) * t)
        cols = slice(r * HEAD_DIM, (r + 1) * HEAD_DIM)
        o = oc_ref[:, cols] + gates[:, rep + r:rep + r + 1] * o_s[rows] + gates[:, 2 * rep + r:2 * rep + r + 1] * o_w[rows]
        o_ref[:, cols] = o.astype(o_ref.dtype)


def _nsa_main_attention(qkv, sel, tables, gates, oc, *, batch, seq_len, t=ATT_T):
    g = NSA_GROUPS
    nq = seq_len // t
    n_sel = seq_len // NSA_SEL_BLOCK
    gw = NSA_REP * HEAD_DIM
    qb = N_HEADS
    kv = lambda which: pl.BlockSpec((seq_len, HEAD_DIM), lambda b, gg, i: (b, qb + which * g + gg))
    tile = pl.BlockSpec((t, gw), lambda b, gg, i: (b * nq + i, gg))
    rows = NSA_REP * t
    return pl.pallas_call(
        functools.partial(_nsa_main_kernel, t=t),
        grid=(batch, g, nq),
        in_specs=[tile, kv(0), kv(1), kv(2), kv(3),
                  pl.BlockSpec((1, 1, t, n_sel), lambda b, gg, i: (b, gg, i, 0)),
                  pl.BlockSpec((NSA_REP, 4 * t, t), lambda b, gg, i: (gg, 0, 0)),
                  pl.BlockSpec((t, LANE), lambda b, gg, i: (b * nq + i, gg)),
                  tile],
        out_specs=tile,
        out_shape=jax.ShapeDtypeStruct((batch * seq_len, g * gw), BF16),
        scratch_shapes=_att_scratch(rows, t, slots=3) + [pltpu.VMEM((rows, HEAD_DIM), F32), pltpu.VMEM((t, t), F32)],
        compiler_params=_params("parallel", "parallel", "arbitrary"),
        name="nsa_main_attention",
    )(qkv, qkv, qkv, qkv, qkv, sel, tables, gates, oc)


def _rope_tables(seq_len, width):
    half = MLA_ROPE // 2
    inv = np.power(ROPE_THETA, -np.arange(half, dtype=np.float32) / half).astype(np.float32)
    ang = np.arange(seq_len, dtype=np.float32)[:, None] * inv[None, :]
    reps = width // half
    return jnp.asarray(np.tile(np.cos(ang), (1, reps))), jnp.asarray(np.tile(np.sin(ang), (1, reps)))


def _rope_weights(w):
    k, n, _ = w.shape
    half = MLA_ROPE // 2
    wb = jnp.concatenate([-w[..., half:], w[..., :half]], axis=-1)
    return w.reshape(k, n * MLA_ROPE).astype(BF16), wb.reshape(k, n * MLA_ROPE).astype(BF16)


def kernel(x, c, t5_bias, ada_w, ada_b, norm_g, final_g, ffn_w_up, ffn_conv_w, ffn_conv_b, ffn_w_down, sb_w_in, sb_w_out, nsa_w_in, nsa_cmp_pe, nsa_cmp_w1, nsa_cmp_w2, nsa_w_out, diff_w_in, diff_lambda, diff_head_g, diff_w_out, mla_w_in, mla_q_g, mla_w_qb, mla_kv_g, mla_w_kvb, mla_w_out):
    batch, seq_len, d = x.shape
    depth = ada_w.shape[0]
    h, dh, g = N_HEADS, HEAD_DIM, NSA_GROUPS
    sizes = dict(batch=batch, seq_len=seq_len)

    mod = _ada_mod(c, ada_w, ada_b)
    tables = _attention_tables(t5_bias, ATT_T)

    xf = x.reshape(batch * seq_len, d)
    for i in range(depth):
        mixer, j = i % 4, i // 4
        sh1, sc1, gt1, sh2, sc2, gt2 = (mod[i, :, n * d:(n + 1) * d].reshape(batch, 1, d) for n in range(6))
        nm = functools.partial(_norm_mm, xf, norm_g[i, 0], seq_len=seq_len, shift=sh1, scale=sc1)
        if mixer == 0:
            w_in = sb_w_in[j]
            w_in = jnp.concatenate([w_in[:, :h * dh] * SB_QSCALE, w_in[:, h * dh:]], axis=1)
            qkv = nm(w_in.astype(BF16), name="sb_in")
            o = _sb_attention(qkv, **sizes)
            w_out = sb_w_out[j]
        elif mixer == 1:
            w_in = nsa_w_in[j]
            n_q, n_kv = h * dh, g * dh
            w_att = jnp.concatenate([w_in[:, :n_q] * NSA_QSCALE, w_in[:, n_q + 2 * n_kv:n_q + 6 * n_kv]], axis=1)
            w_cmp = w_in[:, n_q:n_q + 2 * n_kv]
            w_g = w_in[:, n_q + 6 * n_kv:].reshape(d, 3, g, NSA_REP).transpose(0, 2, 1, 3).reshape(d, g, 3 * NSA_REP)
            w_g = jnp.pad(w_g, ((0, 0), (0, 0), (0, LANE - 3 * NSA_REP))).reshape(d, g * LANE)
            qkv, hn = nm(w_att.astype(BF16), emit_h=True, name="nsa_in")
            hmm = functools.partial(_norm_mm, hn, None, seq_len=seq_len, normed=True)
            raw = hmm(w_cmp.astype(BF16), out_dtype=F32, name="nsa_in_cmp")
            gates = hmm(w_g.astype(BF16), out_dtype=F32, act="sigmoid", name="nsa_in_gates")
            kvc = _nsa_compress(raw, nsa_cmp_pe[j], nsa_cmp_w1[j].reshape(2, NSA_CMP_BLOCK, dh, dh).astype(BF16),
                                nsa_cmp_w2[j].astype(BF16), **sizes)
            n_slots = seq_len // NSA_CMP_STRIDE
            dist_c = np.arange(seq_len)[:, None] - (NSA_CMP_STRIDE * np.arange(n_slots)[None, :] + NSA_CMP_BLOCK - 1)
            bias_c = _t5_gather(t5_bias, _t5_bucket_np(dist_c), tr=min(seq_len, 512), mult=LOG2E)
            oc, sel = _nsa_cmp_attention(qkv, kvc, bias_c, gates, **sizes)
            o = _nsa_main_attention(qkv, sel, tables, gates, oc, **sizes)
            w_out = nsa_w_out[j]
        elif mixer == 2:
            lambda_init = 0.8 - 0.6 * math.exp(-0.3 * i)
            w_in = diff_w_in[j]
            w_in = jnp.concatenate([w_in[:, :h * dh] * DIFF_QSCALE, w_in[:, h * dh:]], axis=1)
            qkv = nm(w_in.astype(BF16), name="diff_in")
            o = _diff_attention(qkv, _attention_tables(t5_bias, DIFF_T), diff_lambda[j], diff_head_g[j],
                                lambda_init=lambda_init, **sizes)
            w_out = diff_w_out[j]
        else:
            w_in = mla_w_in[j]
            nq_l, nkv_l = MLA_Q_LORA, MLA_KV_LORA
            w_lat = jnp.concatenate([w_in[:, :nq_l], jnp.zeros((d, 2 * nkv_l - nq_l), w_in.dtype),
                                     w_in[:, nq_l:nq_l + nkv_l]], axis=1)
            lat, hn = nm(w_lat.astype(BF16), out_dtype=F32, emit_h=True, name="mla_in")
            w_kr = w_in[:, nq_l + nkv_l:]
            zero = jnp.zeros_like(w_kr)
            cos2, sin2 = _rope_tables(seq_len, 2 * LANE)
            wa, wb = _rope_weights(jnp.stack([w_kr, zero, zero, w_kr], axis=1))
            kr = _norm_mm(hn, None, wa, wb=wb, cos=cos2, sin=sin2, seq_len=seq_len, normed=True, name="mla_in_rope")
            w_qb = (mla_w_qb[j] * MLA_QSCALE).reshape(nq_l, h, MLA_NOPE + MLA_ROPE)
            qn = _norm_mm(lat, mla_q_g[j], w_qb[:, :, :MLA_NOPE].reshape(nq_l, h * MLA_NOPE).astype(BF16),
                          seq_len=seq_len, x_cols=nq_l, x_col_block=0, name="mla_q_nope")
            cosh, sinh = _rope_tables(seq_len, h * MLA_ROPE)
            wa, wb = _rope_weights(w_qb[:, :, MLA_NOPE:])
            qr = _norm_mm(lat, mla_q_g[j], wa, wb=wb, cos=cosh, sin=sinh, seq_len=seq_len, x_cols=nq_l,
                          x_col_block=0, name="mla_q_rope")
            kv = _norm_mm(lat, mla_kv_g[j], mla_w_kvb[j].astype(BF16), seq_len=seq_len, x_cols=nkv_l,
                          x_col_block=2, name="mla_kv")
            o = _mla_attention(qn, qr, kv, kr, **sizes)
            w_out = mla_w_out[j]
        xf = _mm_residual(o, w_out.astype(BF16), xf, gt1, seq_len=seq_len)
        xf = _conv_ffn(xf, norm_g[i, 1], sh2, sc2, ffn_w_up[i].astype(BF16), ffn_conv_w[i], ffn_conv_b[i],
                       ffn_w_down[i].astype(BF16), gt2, final_g, seq_len=seq_len, final_norm=(i == depth - 1))
    return xf.reshape(batch, seq_len, d)
```

```python
import functools
import math

import numpy as np
import jax
import jax.numpy as jnp
from jax import lax
from jax.experimental import pallas as pl
from jax.experimental.pallas import tpu as pltpu

F32 = jnp.float32
BF16 = jnp.bfloat16
EPS = 1e-6
NEG = -1e30

LANE = 128
HALO = 16
VMEM_LIMIT = 56 * 2**20

T5_BUCKETS = 32
T5_MAX_DIST = 128
N_HEADS = 16
HEAD_DIM = 128
NSA_GROUPS = 4
NSA_REP = 4
NSA_CMP_BLOCK = 32
NSA_CMP_STRIDE = 16
NSA_SEL_BLOCK = 64
NSA_TOPN = 16
NSA_WINDOW = 512
FORCED_SCORE = 1e9
DIFF_DIM = 64
MLA_Q_LORA = 768
MLA_KV_LORA = 512
MLA_NOPE = 128
MLA_ROPE = 64
ROPE_THETA = 10000.0
CONV_WIDTH = 3
ATT_T = 256
NSA_CMP_T = 512
DIFF_T = 512
MLA_T = 512
ATT_TQ = 512
ATT_TK = 256
LOG2E = 1.4426950408889634
SB_QSCALE = -(HEAD_DIM ** -0.5)
NSA_QSCALE = HEAD_DIM ** -0.5 * LOG2E
DIFF_QSCALE = DIFF_DIM ** -0.5 * LOG2E
MLA_QSCALE = (MLA_NOPE + MLA_ROPE) ** -0.5 * LOG2E
M_INIT = -1e30
MASKED = -2e30
SB_DEAD = -120.0
T5_MASKED = T5_BUCKETS


def _params(*sem):
    return pltpu.CompilerParams(dimension_semantics=sem, vmem_limit_bytes=VMEM_LIMIT)


def _dot(a, b):
    return jnp.dot(a, b, preferred_element_type=F32)


def _dot_nt(a, b):
    return lax.dot_general(a, b, (((1,), (1,)), ((), ())), preferred_element_type=F32)


def _lane_tile(x, reps):
    return jnp.concatenate([x] * reps, axis=1)


def _split_bf16(x):
    hi = x.astype(BF16)
    lo = (x - hi.astype(F32)).astype(BF16)
    return hi, lo


def _rms(x, g):
    ms = jnp.mean(x * x, axis=-1, keepdims=True)
    return x * lax.rsqrt(ms + EPS) * g


def _ada_kernel(c_ref, w_ref, b_ref, o_ref):
    c = c_ref[...]
    a = c * jax.nn.sigmoid(c)
    a_hi, a_lo = _split_bf16(a)
    w_hi, w_lo = _split_bf16(w_ref[0])
    o_ref[0] = _dot(a_hi, w_hi) + _dot(a_lo, w_hi) + _dot(a_hi, w_lo) + b_ref[0]


def _ada_mod(c, ada_w, ada_b, tn=1024):
    depth, d, n = ada_w.shape
    b = c.shape[0]
    return pl.pallas_call(
        _ada_kernel,
        grid=(depth, n // tn),
        in_specs=[pl.BlockSpec((b, d), lambda l, j: (0, 0)),
                  pl.BlockSpec((1, d, tn), lambda l, j: (l, 0, j)),
                  pl.BlockSpec((1, 1, tn), lambda l, j: (l, 0, j))],
        out_specs=pl.BlockSpec((1, b, tn), lambda l, j: (l, 0, j)),
        out_shape=jax.ShapeDtypeStruct((depth, b, n), F32),
        compiler_params=_params("parallel", "parallel"),
        name="ada_mod",
    )(c, ada_w, ada_b.reshape(depth, 1, n))


def _norm_mm_kernel(*refs, modulated, rope, act, normed):
    it = iter(refs)
    x_ref = next(it)
    g_ref = sh_ref = sc_ref = wb_ref = cos_ref = sin_ref = None
    if not normed:
        g_ref = next(it)
        if modulated:
            sh_ref, sc_ref = next(it), next(it)
    w_ref = next(it)
    if rope:
        wb_ref, cos_ref, sin_ref = next(it), next(it), next(it)
    o_ref = next(it)
    if normed:
        h = x_ref[...]
    else:
        h_ref = next(it)

        @pl.when(pl.program_id(1) == 0)
        def _():
            y = _rms(x_ref[...], g_ref[...])
            if modulated:
                y = y * (1.0 + sc_ref[0]) + sh_ref[0]
            h_ref[...] = y.astype(BF16)

        h = h_ref[...]
    acc = _dot(h, w_ref[...])
    if rope:
        acc = acc * cos_ref[...] + _dot(h, wb_ref[...]) * sin_ref[...]
    if act == "sigmoid":
        acc = jax.nn.sigmoid(acc)
    o_ref[...] = acc.astype(o_ref.dtype)


def _norm_mm(x, g, w, *, seq_len, shift=None, scale=None, wb=None, cos=None, sin=None, act=None, normed=False,
             emit_h=False, out_dtype=BF16, x_cols=None, x_col_block=0, tm=1024, tn=1024, name="norm_mm"):
    m = x.shape[0]
    k = x.shape[1] if x_cols is None else x_cols
    n = w.shape[1]
    while n % tn:
        tn //= 2
    modulated, rope = shift is not None, wb is not None
    tps = seq_len // tm
    in_specs = [pl.BlockSpec((tm, k), lambda i, j: (i, x_col_block))]
    args = [x]
    if not normed:
        in_specs.append(pl.BlockSpec((1, k), lambda i, j: (0, 0)))
        args.append(g.reshape(1, k))
        if modulated:
            in_specs += [pl.BlockSpec((1, 1, k), lambda i, j: (i // tps, 0, 0))] * 2
            args += [shift, scale]
    in_specs.append(pl.BlockSpec((k, tn), lambda i, j: (0, j)))
    args.append(w)
    if rope:
        in_specs.append(pl.BlockSpec((k, tn), lambda i, j: (0, j)))
        in_specs += [pl.BlockSpec((tm, tn), lambda i, j: (i % tps, j))] * 2
        args += [wb, cos, sin]
    out_specs = [pl.BlockSpec((tm, tn), lambda i, j: (i, j))]
    out_shape = [jax.ShapeDtypeStruct((m, n), out_dtype)]
    scratch = []
    if emit_h:
        out_specs.append(pl.BlockSpec((tm, k), lambda i, j: (i, 0)))
        out_shape.append(jax.ShapeDtypeStruct((m, k), BF16))
    elif not normed:
        scratch.append(pltpu.VMEM((tm, k), BF16))
    out = pl.pallas_call(
        functools.partial(_norm_mm_kernel, modulated=modulated, rope=rope, act=act, normed=normed),
        grid=(m // tm, n // tn),
        in_specs=in_specs,
        out_specs=out_specs,
        out_shape=out_shape,
        scratch_shapes=scratch,
        compiler_params=_params("parallel", "arbitrary"),
        name=name,
    )(*args)
    return out if emit_h else out[0]


def _mm_res_kernel(a_ref, w_ref, x_ref, gate_ref, o_ref):
    o_ref[...] = x_ref[...] + gate_ref[0] * _dot(a_ref[...], w_ref[...])


def _mm_residual(a, w, x, gate, *, seq_len, tm=1024, tn=1024, name="mm_residual"):
    m, k = a.shape
    n = w.shape[1]
    tps = seq_len // tm
    return pl.pallas_call(
        _mm_res_kernel,
        grid=(m // tm, n // tn),
        in_specs=[pl.BlockSpec((tm, k), lambda i, j: (i, 0)),
                  pl.BlockSpec((k, tn), lambda i, j: (0, j)),
                  pl.BlockSpec((tm, tn), lambda i, j: (i, j)),
                  pl.BlockSpec((1, 1, tn), lambda i, j: (i // tps, 0, j))],
        out_specs=pl.BlockSpec((tm, tn), lambda i, j: (i, j)),
        out_shape=jax.ShapeDtypeStruct((m, n), F32),
        compiler_params=_params("parallel", "arbitrary"),
        name=name,
    )(a, w, x, gate)


def _ffn_kernel(x_ref, xp_ref, g_ref, sh_ref, sc_ref, wg_ref, wu_ref, cwg_ref, cwu_ref, cbg_ref, cbu_ref,
                wd_ref, gate_ref, fg_ref, o_ref, h_ref, acc_ref, ug_scr, uu_scr, *, tiles_per_seq, final_norm):
    i, j = pl.program_id(0), pl.program_id(1)

    @pl.when(j == 0)
    def _():
        def nm(x):
            return _rms(x, g_ref[...]) * (1.0 + sc_ref[0]) + sh_ref[0]
        h_ref[HALO:, :] = nm(x_ref[...]).astype(BF16)
        keep = jnp.where(i % tiles_per_seq == 0, 0.0, 1.0)
        h_ref[:HALO, :] = (nm(xp_ref[...]) * keep).astype(BF16)
        acc_ref[...] = jnp.zeros_like(acc_ref)

    h = h_ref[...]

    def branch(w_ref, cw_ref, cb_ref, u_scr):
        u_scr[...] = _dot(h, w_ref[...])
        tm = u_scr.shape[0] - HALO
        cw = cw_ref[...]
        return (cw[0:1] * u_scr[pl.ds(HALO - 2, tm), :] + cw[1:2] * u_scr[pl.ds(HALO - 1, tm), :]
                + cw[2:3] * u_scr[pl.ds(HALO, tm), :] + cb_ref[...])

    gt = branch(wg_ref, cwg_ref, cbg_ref, ug_scr)
    up = branch(wu_ref, cwu_ref, cbu_ref, uu_scr)
    a = gt * jax.nn.sigmoid(gt) * up
    acc_ref[...] += _dot(a.astype(BF16), wd_ref[...])

    @pl.when(j == pl.num_programs(1) - 1)
    def _():
        y = x_ref[...] + gate_ref[0] * acc_ref[...]
        if final_norm:
            y = _rms(y, fg_ref[...])
        o_ref[...] = y


def _conv_ffn(x, g, shift, scale, w_up, conv_w, conv_b, w_down, gate, final_g, *, seq_len, final_norm,
              tm=512, tf=512):
    m, d = x.shape
    f = w_down.shape[0]
    nf = f // tf
    tps = seq_len // tm
    hb = tm // HALO
    row = lambda i, j: (i, 0)
    per_batch = lambda i, j: (i // tps, 0, 0)
    return pl.pallas_call(
        functools.partial(_ffn_kernel, tiles_per_seq=tps, final_norm=final_norm),
        grid=(m // tm, nf),
        in_specs=[pl.BlockSpec((tm, d), row),
                  pl.BlockSpec((HALO, d), lambda i, j: (jnp.maximum(i * hb - 1, 0), 0)),
                  pl.BlockSpec((1, d), lambda i, j: (0, 0)),
                  pl.BlockSpec((1, 1, d), per_batch),
                  pl.BlockSpec((1, 1, d), per_batch),
                  pl.BlockSpec((d, tf), lambda i, j: (0, j)),
                  pl.BlockSpec((d, tf), lambda i, j: (0, j + nf)),
                  pl.BlockSpec((CONV_WIDTH, tf), lambda i, j: (0, j)),
                  pl.BlockSpec((CONV_WIDTH, tf), lambda i, j: (0, j + nf)),
                  pl.BlockSpec((1, tf), lambda i, j: (0, j)),
                  pl.BlockSpec((1, tf), lambda i, j: (0, j + nf)),
                  pl.BlockSpec((tf, d), lambda i, j: (j, 0)),
                  pl.BlockSpec((1, 1, d), per_batch),
                  pl.BlockSpec((1, d), lambda i, j: (0, 0))],
        out_specs=pl.BlockSpec((tm, d), row),
        out_shape=jax.ShapeDtypeStruct((m, d), F32),
        scratch_shapes=[pltpu.VMEM((HALO + tm, d), BF16), pltpu.VMEM((tm, d), F32),
                        pltpu.VMEM((HALO + tm, tf), F32), pltpu.VMEM((HALO + tm, tf), F32)],
        compiler_params=_params("parallel", "arbitrary"),
        name="conv_ffn",
    )(x, x, g.reshape(1, d), shift, scale, w_up, w_up, conv_w, conv_w, conv_b.reshape(1, -1),
      conv_b.reshape(1, -1), w_down, gate, final_g.reshape(1, d))


def _t5_bucket_np(dist):
    n = np.maximum(dist, 0)
    max_exact = T5_BUCKETS // 2
    nf = np.maximum(n, 1).astype(np.float64)
    val = np.log(nf / max_exact) / math.log(T5_MAX_DIST / max_exact) * (T5_BUCKETS - max_exact)
    large = max_exact + np.trunc(val + 1e-6).astype(np.int64)
    return np.where(n < max_exact, n, np.minimum(large, T5_BUCKETS - 1)).astype(np.int32)


def _t5_gather_kernel(t5_ref, bkt_ref, o_ref, *, mult):
    h = pl.program_id(0)
    bk = bkt_ref[...]
    acc = jnp.zeros(bk.shape, F32)
    for b in range(T5_BUCKETS):
        acc = jnp.where(bk == b, t5_ref[b, h], acc)
    o_ref[0] = jnp.where(bk == T5_MASKED, MASKED, acc * mult)


def _t5_gather(t5_bias, bkt, tr, mult=1.0):
    rows, cols = bkt.shape
    heads = t5_bias.shape[1]
    return pl.pallas_call(
        functools.partial(_t5_gather_kernel, mult=mult),
        grid=(heads, rows // tr),
        in_specs=[pl.BlockSpec(memory_space=pltpu.SMEM),
                  pl.BlockSpec((tr, cols), lambda h, r: (r, 0))],
        out_specs=pl.BlockSpec((1, tr, cols), lambda h, r: (h, r, 0)),
        out_shape=jax.ShapeDtypeStruct((heads, rows, cols), F32),
        compiler_params=_params("parallel", "parallel"),
        name="t5_gather",
    )(t5_bias, jnp.asarray(bkt))


TAB_DIAG, TAB_SUB, TAB_FAR, TAB_EDGE = 0, 1, 2, 3


def _attention_tables(t5_bias, t):
    i = np.arange(t)[:, None]
    j = np.arange(t)[None, :]
    assert int(_t5_bucket_np(np.array(t + 1))) == T5_BUCKETS - 1
    far = np.full((t, t), T5_BUCKETS - 1, np.int32)
    bkt = np.concatenate([np.where(j <= i, _t5_bucket_np(i - j), T5_MASKED), _t5_bucket_np(t + i - j), far,
                          np.where(j > i, far, T5_MASKED)], axis=0).astype(np.int32)
    return _t5_gather(t5_bias, bkt, tr=t, mult=LOG2E)


def _tile_iota(t):
    return lax.broadcasted_iota(jnp.int32, (t, t), 0), lax.broadcasted_iota(jnp.int32, (t, t), 1)


def _rect_iota(i, tq, start, tk):
    rowg = i * tq + lax.broadcasted_iota(jnp.int32, (tq, tk), 0)
    colg = start + lax.broadcasted_iota(jnp.int32, (tq, tk), 1)
    return rowg, colg


def _osm_reset(m_scr, l_scr, acc_scr):
    m_scr[...] = jnp.full(m_scr.shape, M_INIT, F32)
    l_scr[...] = jnp.zeros(l_scr.shape, F32)
    acc_scr[...] = jnp.zeros(acc_scr.shape, F32)


def _osm_update(t2, rows, slot, p_scr, m_scr, l_scr, acc_scr):
    m_prev = m_scr[rows, :]
    m_new = jnp.maximum(m_prev, jnp.max(t2, axis=-1, keepdims=True))
    p = jnp.exp2(t2 - _lane_tile(m_new, t2.shape[1] // LANE))
    alpha = jnp.exp2(m_prev - m_new)
    l_scr[rows, :] = alpha * l_scr[rows, :]
    m_scr[rows, :] = m_new
    p_scr[slot, rows, :] = p.astype(BF16)
    acc_scr[rows, :] = alpha * acc_scr[rows, :]


def _osm_accum(v, slot, p_scr, l_scr, acc_scr):
    vext = jnp.concatenate([v, jnp.ones(v.shape, v.dtype)], axis=-1)
    half = p_scr.shape[1] // 2
    for part in range(2):
        rows = pl.ds(part * half, half)
        pv = _dot(p_scr[slot, rows, :], vext)
        acc_scr[rows, :] += pv[:, :HEAD_DIM]
        l_scr[rows, :] += pv[:, HEAD_DIM:]


def _pipelined_tiles(n_tiles, scores, update, update_last=None):
    update_last = update if update_last is None else update_last
    last = n_tiles - 1
    n_pairs = last // 2
    scores(0, 0)

    def pair(n, cr):
        scores(2 * n + 1, 1)
        update(2 * n, 0)
        scores(2 * n + 2, 0)
        update(2 * n + 1, 1)
        return cr

    lax.fori_loop(0, n_pairs, pair, 0)

    @pl.when(last % 2 == 0)
    def _():
        update_last(last, 0)

    @pl.when(last % 2 == 1)
    def _():
        scores(last, 1)
        update(last - 1, 0)
        update_last(last, 1)


def _att_scratch(rows, tk, slots=2):
    return [pltpu.VMEM((slots, rows, tk), F32), pltpu.VMEM((slots, rows, tk), BF16), pltpu.VMEM((rows, LANE), F32),
            pltpu.VMEM((rows, LANE), F32), pltpu.VMEM((rows, HEAD_DIM), F32)]


def _sb_kernel(q_ref, k_ref, v_ref, o_ref, s_scr, c_scr, acc_scr, *, tq, tk):
    i = pl.program_id(2)
    row, col = _tile_iota(tk)
    upper = jnp.where(row > col, 1.0, 0.0).astype(BF16)
    for a in range(2):
        lanes = slice(a * HEAD_DIM, (a + 1) * HEAD_DIM)
        q = q_ref[:, lanes]
        c_scr[...] = jnp.zeros(c_scr.shape, F32)
        acc_scr[...] = jnp.zeros(acc_scr.shape, F32)
        reps = tk // LANE

        def scores(kb, slot):
            start = pl.multiple_of(kb * tk, tk)
            s_scr[slot] = _dot_nt(q, k_ref[pl.ds(start, tk), lanes])

        def update(kb, slot, diag):
            start = pl.multiple_of(kb * tk, tk)
            nz = s_scr[slot]
            e = jnp.exp2(jnp.abs(nz) * (-LOG2E))
            lk = jnp.minimum(nz, 0.0) - jnp.log(1.0 + e)
            if diag:
                rowg, colg = _rect_iota(i, tq, start, tk)
                past = colg < rowg
                lk = jnp.where(past, lk, 0.0)
            hi, lo = _split_bf16(lk)
            c = c_scr[...]
            later = _dot(hi, upper) + _dot(lo, upper) + _lane_tile(c, reps)
            a = jnp.exp(lk - nz + later)
            if diag:
                a = jnp.where(past, a, 0.0)
            acc_scr[...] += _dot(a.astype(BF16), v_ref[pl.ds(start, tk), lanes])
            c_scr[...] = c + jnp.sum(lk, axis=-1, keepdims=True)

        assert tq == 2 * tk
        scores(2 * i + 1, 1)
        scores(2 * i, 0)
        update(2 * i + 1, 1, True)
        scores(jnp.maximum(2 * i - 1, 0), 1)
        update(2 * i, 0, True)

        def more(carry):
            n, c_max = carry
            return (n < i) & (c_max > SB_DEAD)

        def pair(carry):
            n, _ = carry
            kb = 2 * (i - n) - 1
            scores(kb - 1, 0)
            update(kb, 1, False)
            c_mid = jnp.max(c_scr[...])

            @pl.when(c_mid > SB_DEAD)
            def _():
                scores(jnp.maximum(kb - 2, 0), 1)
                update(kb - 1, 0, False)

            return n + 1, jnp.where(c_mid > SB_DEAD, jnp.max(c_scr[...]), c_mid)

        lax.while_loop(more, pair, (0, jnp.max(c_scr[...])))
        o_ref[:, lanes] = acc_scr[...].astype(o_ref.dtype)


def _sb_attention(qkv, *, batch, seq_len, tq=ATT_TQ, tk=ATT_TK):
    h = N_HEADS
    nq = seq_len // tq
    pw = 2 * HEAD_DIM
    pairs = h // 2
    return pl.pallas_call(
        functools.partial(_sb_kernel, tq=tq, tk=tk),
        grid=(batch, pairs, nq),
        in_specs=[pl.BlockSpec((tq, pw), lambda b, pp, i: (b * nq + i, pp)),
                  pl.BlockSpec((seq_len, pw), lambda b, pp, i: (b, pairs + pp)),
                  pl.BlockSpec((seq_len, pw), lambda b, pp, i: (b, 2 * pairs + pp))],
        out_specs=pl.BlockSpec((tq, pw), lambda b, pp, i: (b * nq + i, pp)),
        out_shape=jax.ShapeDtypeStruct((batch * seq_len, h * HEAD_DIM), BF16),
        scratch_shapes=[pltpu.VMEM((2, tq, tk), F32), pltpu.VMEM((tq, LANE), F32), pltpu.VMEM((tq, HEAD_DIM), F32)],
        compiler_params=_params("parallel", "parallel", "arbitrary"),
        name="sb_attention",
    )(qkv, qkv, qkv)


def _table_offset(kb, i, t):
    return pl.multiple_of(jnp.where(kb == i, TAB_DIAG * t, jnp.where(kb == i - 1, TAB_SUB * t, TAB_FAR * t)), t)


def _diff_kernel(q_ref, k_ref, v_ref, tab_ref, lam_ref, hg_ref, o_ref, s_scr, p_scr, m_scr, l_scr, acc_scr, *, t,
                 lambda_init):
    i = pl.program_id(2)
    lam = lam_ref[...]
    lmbda = (jnp.exp(jnp.sum(lam[0:1] * lam[1:2], axis=-1, keepdims=True))
             - jnp.exp(jnp.sum(lam[2:3] * lam[3:4], axis=-1, keepdims=True)) + lambda_init)
    for a in range(2):
        lanes = slice(a * HEAD_DIM, (a + 1) * HEAD_DIM)
        q = q_ref[:, lanes]
        lane = lax.broadcasted_iota(jnp.int32, q.shape, 1)
        zero = jnp.zeros_like(q)
        q2 = jnp.concatenate([jnp.where(lane < DIFF_DIM, q, zero), jnp.where(lane >= DIFF_DIM, q, zero)], axis=0)
        _osm_reset(m_scr, l_scr, acc_scr)

        def scores(kb, slot):
            s_scr[slot] = _dot_nt(q2, k_ref[pl.ds(pl.multiple_of(kb * t, t), t), lanes])

        def update(kb, slot):
            off = _table_offset(kb, i, t)
            for half in range(2):
                rows = pl.ds(half * t, t)
                t2 = s_scr[slot, rows, :] + tab_ref[a, pl.ds(off, t), :]
                _osm_update(t2, rows, slot, p_scr, m_scr, l_scr, acc_scr)
            _osm_accum(v_ref[pl.ds(pl.multiple_of(kb * t, t), t), lanes], slot, p_scr, l_scr, acc_scr)

        _pipelined_tiles(i + 1, scores, update)

        o = acc_scr[...] / l_scr[...]
        o = o[:t] - lmbda * o[t:]
        o_ref[:, lanes] = (_rms(o, hg_ref[...]) * (1.0 - lambda_init)).astype(o_ref.dtype)


def _diff_attention(qkv, tables, lam, head_g, *, batch, seq_len, lambda_init, t=DIFF_T):
    h = N_HEADS
    nq = seq_len // t
    pw = 2 * HEAD_DIM
    pairs = h // 2
    return pl.pallas_call(
        functools.partial(_diff_kernel, t=t, lambda_init=lambda_init),
        grid=(batch, pairs, nq),
        in_specs=[pl.BlockSpec((t, pw), lambda b, pp, i: (b * nq + i, pp)),
                  pl.BlockSpec((seq_len, pw), lambda b, pp, i: (b, pairs + pp)),
                  pl.BlockSpec((seq_len, pw), lambda b, pp, i: (b, 2 * pairs + pp)),
                  pl.BlockSpec((2, 4 * t, t), lambda b, pp, i: (pp, 0, 0)),
                  pl.BlockSpec((4, DIFF_DIM), lambda b, pp, i: (0, 0)),
                  pl.BlockSpec((1, HEAD_DIM), lambda b, pp, i: (0, 0))],
        out_specs=pl.BlockSpec((t, pw), lambda b, pp, i: (b * nq + i, pp)),
        out_shape=jax.ShapeDtypeStruct((batch * seq_len, h * HEAD_DIM), BF16),
        scratch_shapes=_att_scratch(2 * t, t),
        compiler_params=_params("parallel", "parallel", "arbitrary"),
        name="diff_attention",
    )(qkv, qkv, qkv, tables, lam, head_g.reshape(1, HEAD_DIM))


def _mla_kernel(qn_ref, qr_ref, kv_ref, kr_ref, o_ref, s_scr, p_scr, m_scr, l_scr, acc_scr, *, tq, tk):
    i = pl.program_id(2)
    assert tq == tk
    for a in range(2):
        lanes = slice(a * HEAD_DIM, (a + 1) * HEAD_DIM)
        kn_lanes = slice(2 * a * HEAD_DIM, (2 * a + 1) * HEAD_DIM)
        v_lanes = slice((2 * a + 1) * HEAD_DIM, (2 * a + 2) * HEAD_DIM)
        q = jnp.concatenate([qn_ref[:, lanes], qr_ref[...]], axis=-1)
        _osm_reset(m_scr, l_scr, acc_scr)

        def scores(kb, slot):
            rows = pl.ds(pl.multiple_of(kb * tk, tk), tk)
            k = jnp.concatenate([kv_ref[rows, kn_lanes], kr_ref[rows, lanes]], axis=-1)
            s_scr[slot] = _dot_nt(q, k)

        def update(kb, slot, mask):
            start = pl.multiple_of(kb * tk, tk)
            t2 = s_scr[slot]
            if mask:
                rowg, colg = _rect_iota(i, tq, start, tk)
                t2 = jnp.where(colg <= rowg, t2, MASKED)
            _osm_update(t2, pl.ds(0, tq), slot, p_scr, m_scr, l_scr, acc_scr)
            _osm_accum(kv_ref[pl.ds(start, tk), v_lanes], slot, p_scr, l_scr, acc_scr)

        _pipelined_tiles(i + 1, scores, lambda kb, slot: update(kb, slot, False),
                         lambda kb, slot: update(kb, slot, True))
        o_ref[:, lanes] = (acc_scr[...] / l_scr[...]).astype(o_ref.dtype)


def _mla_attention(qn, qr, kv, kr, *, batch, seq_len, tq=MLA_T, tk=MLA_T):
    h = N_HEADS
    nq = seq_len // tq
    pair = pl.BlockSpec((tq, 2 * HEAD_DIM), lambda b, pp, i: (b * nq + i, pp))
    return pl.pallas_call(
        functools.partial(_mla_kernel, tq=tq, tk=tk),
        grid=(batch, h // 2, nq),
        in_specs=[pair,
                  pl.BlockSpec((tq, LANE), lambda b, pp, i: (b * nq + i, pp)),
                  pl.BlockSpec((seq_len, 4 * HEAD_DIM), lambda b, pp, i: (b, pp)),
                  pl.BlockSpec((seq_len, 2 * LANE), lambda b, pp, i: (b, 0))],
        out_specs=pair,
        out_shape=jax.ShapeDtypeStruct((batch * seq_len, h * HEAD_DIM), BF16),
        scratch_shapes=_att_scratch(tq, tk),
        compiler_params=_params("parallel", "parallel", "arbitrary"),
        name="mla_attention",
    )(qn, qr, kv, kr)


def _compress_kernel(raw_ref, pe_ref, w1_ref, w2_ref, o_ref, *, n_slots):
    half = NSA_CMP_BLOCK // 2
    p1 = jnp.zeros((n_slots, HEAD_DIM), F32)
    p2 = jnp.zeros((n_slots, HEAD_DIM), F32)
    for l in range(half):
        a = raw_ref[pl.ds(l, n_slots, stride=NSA_CMP_STRIDE), :]
        p1 = p1 + _dot((a + pe_ref[0, l:l + 1, :]).astype(BF16), w1_ref[0, l])
        p2 = p2 + _dot((a + pe_ref[0, half + l:half + l + 1, :]).astype(BF16), w1_ref[0, half + l])
    pre = p1 + pltpu.roll(p2, n_slots - 1, 0)
    hid = pre * jax.nn.sigmoid(pre)
    o_ref[0, 0, 0] = _dot(hid.astype(BF16), w2_ref[0]).astype(o_ref.dtype)


def _nsa_compress(raw, pe, w1, w2, *, batch, seq_len):
    g = NSA_GROUPS
    n_slots = seq_len // NSA_CMP_STRIDE
    return pl.pallas_call(
        functools.partial(_compress_kernel, n_slots=n_slots),
        grid=(batch, 2, g),
        in_specs=[pl.BlockSpec((seq_len, HEAD_DIM), lambda b, kv, gg: (b, kv * g + gg)),
                  pl.BlockSpec((1, NSA_CMP_BLOCK, HEAD_DIM), lambda b, kv, gg: (kv, 0, 0)),
                  pl.BlockSpec((1, NSA_CMP_BLOCK, HEAD_DIM, HEAD_DIM), lambda b, kv, gg: (kv, 0, 0, 0)),
                  pl.BlockSpec((1, HEAD_DIM, HEAD_DIM), lambda b, kv, gg: (kv, 0, 0))],
        out_specs=pl.BlockSpec((1, 1, 1, n_slots, HEAD_DIM), lambda b, kv, gg: (b, kv, gg, 0, 0)),
        out_shape=jax.ShapeDtypeStruct((batch, 2, g, n_slots, HEAD_DIM), BF16),
        compiler_params=_params("parallel", "parallel", "parallel"),
        name="nsa_compress",
    )(raw, pe, w1, w2)


def _nsa_cmp_kernel(q_ref, kc_ref, vc_ref, bias_ref, gates_ref, ovt_ref, oc_ref, sel_ref, *, t, n_slots, n_sel,
                    n_top):
    i = pl.program_id(2)
    kc = kc_ref[0, 0, 0]
    vc = vc_ref[0, 0, 0]
    qpos = i * t + lax.broadcasted_iota(jnp.int32, (t, n_slots), 0)
    cmp_end = NSA_CMP_STRIDE * lax.broadcasted_iota(jnp.int32, (t, n_slots), 1) + (NSA_CMP_BLOCK - 1)
    valid = cmp_end <= qpos
    gates = gates_ref[...]
    psum = jnp.zeros((t, n_slots), F32)
    for r in range(NSA_REP):
        q = q_ref[:, r * HEAD_DIM:(r + 1) * HEAD_DIM]
        s = jnp.where(valid, _dot_nt(q, kc) + bias_ref[r], NEG)
        m = jnp.max(s, axis=-1, keepdims=True)
        p = jnp.where(valid, jnp.exp2(s - m), 0.0)
        p = p / jnp.maximum(jnp.sum(p, axis=-1, keepdims=True), 1e-30)
        psum = psum + p
        oc_ref[:, r * HEAD_DIM:(r + 1) * HEAD_DIM] = gates[:, r:r + 1] * _dot(p.astype(BF16), vc)

    hi, lo = _split_bf16(psum)
    ovt = ovt_ref[...]
    imp = _dot_nt(ovt, hi) + _dot_nt(ovt, lo)
    blk = lax.broadcasted_iota(jnp.int32, (n_sel, t), 0)
    tpos = i * t + lax.broadcasted_iota(jnp.int32, (n_sel, t), 1)
    cur = tpos // NSA_SEL_BLOCK
    forced = (blk == 0) | (blk == cur) | (blk == cur - 1)
    score = jnp.where(blk * NSA_SEL_BLOCK <= tpos, jnp.where(forced, FORCED_SCORE, imp), -1.0)
    rank = jnp.zeros((n_sel, t), F32)
    for mm in range(n_sel):
        sm = score[mm:mm + 1, :]
        ahead = (sm > score) | ((sm == score) & (blk > mm))
        rank = rank + jnp.where(ahead, 1.0, 0.0)
    sel_t = jnp.where(rank < n_top, 1.0, 0.0).astype(BF16)
    row, col = _tile_iota(t)
    eye = jnp.where(row == col, 1.0, 0.0).astype(BF16)
    sel_ref[0, 0] = _dot_nt(eye, sel_t).astype(sel_ref.dtype)


def _nsa_cmp_attention(q_all, kvc, bias_c, gates, *, batch, seq_len, t=NSA_CMP_T):
    g = NSA_GROUPS
    nq = seq_len // t
    n_slots = seq_len // NSA_CMP_STRIDE
    n_sel = seq_len // NSA_SEL_BLOCK
    c0 = NSA_CMP_STRIDE * np.arange(n_slots)[:, None]
    s0 = NSA_SEL_BLOCK * np.arange(n_sel)[None, :]
    overlap = (c0 < s0 + NSA_SEL_BLOCK) & (c0 + NSA_CMP_BLOCK > s0)
    ovt = jnp.asarray(overlap.T.astype(np.float32), dtype=BF16)
    gw = NSA_REP * HEAD_DIM
    return pl.pallas_call(
        functools.partial(_nsa_cmp_kernel, t=t, n_slots=n_slots, n_sel=n_sel, n_top=min(NSA_TOPN, n_sel)),
        grid=(batch, g, nq),
        in_specs=[pl.BlockSpec((t, gw), lambda b, gg, i: (b * nq + i, gg)),
                  pl.BlockSpec((1, 1, 1, n_slots, HEAD_DIM), lambda b, gg, i: (b, 0, gg, 0, 0)),
                  pl.BlockSpec((1, 1, 1, n_slots, HEAD_DIM), lambda b, gg, i: (b, 1, gg, 0, 0)),
                  pl.BlockSpec((NSA_REP, t, n_slots), lambda b, gg, i: (gg, i, 0)),
                  pl.BlockSpec((t, LANE), lambda b, gg, i: (b * nq + i, gg)),
                  pl.BlockSpec((n_sel, n_slots), lambda b, gg, i: (0, 0))],
        out_specs=[pl.BlockSpec((t, gw), lambda b, gg, i: (b * nq + i, gg)),
                   pl.BlockSpec((1, 1, t, n_sel), lambda b, gg, i: (b, gg, i, 0))],
        out_shape=[jax.ShapeDtypeStruct((batch * seq_len, g * gw), F32),
                   jax.ShapeDtypeStruct((batch, g, seq_len, n_sel), BF16)],
        compiler_params=_params("parallel", "parallel", "arbitrary"),
        name="nsa_cmp_attention",
    )(q_all, kvc, kvc, bias_c, gates, ovt)


def _nsa_main_kernel(q_ref, ks_ref, vs_ref, kw_ref, vw_ref, sel_ref, tab_ref, gates_ref, oc_ref, o_ref,
                     s_scr, p_scr, m_scr, l_scr, acc_scr, os_scr, mk_scr, *, t):
    i = pl.program_id(2)
    rep = NSA_REP
    q = jnp.concatenate([q_ref[:, r * HEAD_DIM:(r + 1) * HEAD_DIM] for r in range(rep)], axis=0)

    sel = sel_ref[0, 0]
    n_sel = sel.shape[1]
    blk_row = lax.broadcasted_iota(jnp.int32, (n_sel, t), 0)
    key_col = lax.broadcasted_iota(jnp.int32, (n_sel, t), 1)
    _osm_reset(m_scr, l_scr, acc_scr)

    def sel_scores(kb, slot):
        s_scr[slot] = _dot_nt(q, ks_ref[pl.ds(pl.multiple_of(kb * t, t), t), :])

    def sel_update(kb, slot):
        start = pl.multiple_of(kb * t, t)
        expand = jnp.where((start + key_col) // NSA_SEL_BLOCK == blk_row, 1.0, 0.0).astype(BF16)
        mk_scr[...] = _dot(sel, expand)
        off = _table_offset(kb, i, t)
        for r in range(rep):
            rows = pl.ds(r * t, t)
            t2 = s_scr[slot, rows, :] + tab_ref[r, pl.ds(off, t), :]
            t2 = jnp.where(mk_scr[...] > 0.5, t2, MASKED)
            _osm_update(t2, rows, slot, p_scr, m_scr, l_scr, acc_scr)
        _osm_accum(vs_ref[pl.ds(start, t), :], slot, p_scr, l_scr, acc_scr)

    _pipelined_tiles(i + 1, sel_scores, sel_update)
    os_scr[...] = acc_scr[...] / l_scr[...]

    n_back = NSA_WINDOW // t
    assert n_back == 2
    _osm_reset(m_scr, l_scr, acc_scr)

    def win_scores(kb, slot):
        s_scr[slot] = _dot_nt(q, kw_ref[pl.ds(pl.multiple_of(kb * t, t), t), :])

    def win_update(kb, slot, region):
        for r in range(rep):
            rows = pl.ds(r * t, t)
            t2 = s_scr[slot, rows, :] + tab_ref[r, pl.ds(region * t, t), :]
            _osm_update(t2, rows, slot, p_scr, m_scr, l_scr, acc_scr)
        _osm_accum(vw_ref[pl.ds(pl.multiple_of(kb * t, t), t), :], slot, p_scr, l_scr, acc_scr)

    def window(n_tiles):
        regions = (TAB_DIAG, TAB_SUB, TAB_EDGE)
        for back in range(n_tiles):
            win_scores(i - back, back)
        for back in reversed(range(n_tiles)):
            win_update(i - back, back, regions[back])

    pl.when(i >= 2)(lambda: window(3))
    pl.when(i == 1)(lambda: window(2))
    pl.when(i == 0)(lambda: window(1))
    o_w = acc_scr[...] / l_scr[...]
    o_s = os_scr[...]

    gates = gates_ref[...]
    for r in range(rep):
        rows = slice(r * t, (r + 1) * t)
        cols = slice(r * HEAD_DIM, (r + 1) * HEAD_DIM)
        o = oc_ref[:, cols] + gates[:, rep + r:rep + r + 1] * o_s[rows] + gates[:, 2 * rep + r:2 * rep + r + 1] * o_w[rows]
        o_ref[:, cols] = o.astype(o_ref.dtype)


def _nsa_main_attention(qkv, sel, tables, gates, oc, *, batch, seq_len, t=ATT_T):
    g = NSA_GROUPS
    nq = seq_len // t
    n_sel = seq_len // NSA_SEL_BLOCK
    gw = NSA_REP * HEAD_DIM
    qb = N_HEADS
    kv = lambda which: pl.BlockSpec((seq_len, HEAD_DIM), lambda b, gg, i: (b, qb + which * g + gg))
    tile = pl.BlockSpec((t, gw), lambda b, gg, i: (b * nq + i, gg))
    rows = NSA_REP * t
    return pl.pallas_call(
        functools.partial(_nsa_main_kernel, t=t),
        grid=(batch, g, nq),
        in_specs=[tile, kv(0), kv(1), kv(2), kv(3),
                  pl.BlockSpec((1, 1, t, n_sel), lambda b, gg, i: (b, gg, i, 0)),
                  pl.BlockSpec((NSA_REP, 4 * t, t), lambda b, gg, i: (gg, 0, 0)),
                  pl.BlockSpec((t, LANE), lambda b, gg, i: (b * nq + i, gg)),
                  tile],
        out_specs=tile,
        out_shape=jax.ShapeDtypeStruct((batch * seq_len, g * gw), BF16),
        scratch_shapes=_att_scratch(rows, t, slots=3) + [pltpu.VMEM((rows, HEAD_DIM), F32), pltpu.VMEM((t, t), F32)],
        compiler_params=_params("parallel", "parallel", "arbitrary"),
        name="nsa_main_attention",
    )(qkv, qkv, qkv, qkv, qkv, sel, tables, gates, oc)


def _rope_tables(seq_len, width):
    half = MLA_ROPE // 2
    inv = np.power(ROPE_THETA, -np.arange(half, dtype=np.float32) / half).astype(np.float32)
    ang = np.arange(seq_len, dtype=np.float32)[:, None] * inv[None, :]
    reps = width // half
    return jnp.asarray(np.tile(np.cos(ang), (1, reps))), jnp.asarray(np.tile(np.sin(ang), (1, reps)))


def _rope_weights(w):
    k, n, _ = w.shape
    half = MLA_ROPE // 2
    wb = jnp.concatenate([-w[..., half:], w[..., :half]], axis=-1)
    return w.reshape(k, n * MLA_ROPE).astype(BF16), wb.reshape(k, n * MLA_ROPE).astype(BF16)


def kernel(x, c, t5_bias, ada_w, ada_b, norm_g, final_g, ffn_w_up, ffn_conv_w, ffn_conv_b, ffn_w_down, sb_w_in, sb_w_out, nsa_w_in, nsa_cmp_pe, nsa_cmp_w1, nsa_cmp_w2, nsa_w_out, diff_w_in, diff_lambda, diff_head_g, diff_w_out, mla_w_in, mla_q_g, mla_w_qb, mla_kv_g, mla_w_kvb, mla_w_out):
    batch, seq_len, d = x.shape
    depth = ada_w.shape[0]
    h, dh, g = N_HEADS, HEAD_DIM, NSA_GROUPS
    sizes = dict(batch=batch, seq_len=seq_len)

    mod = _ada_mod(c, ada_w, ada_b)
    tables = _attention_tables(t5_bias, ATT_T)

    xf = x.reshape(batch * seq_len, d)
    for i in range(depth):
        mixer, j = i % 4, i // 4
        sh1, sc1, gt1, sh2, sc2, gt2 = (mod[i, :, n * d:(n + 1) * d].reshape(batch, 1, d) for n in range(6))
        nm = functools.partial(_norm_mm, xf, norm_g[i, 0], seq_len=seq_len, shift=sh1, scale=sc1)
        if mixer == 0:
            w_in = sb_w_in[j]
            w_in = jnp.concatenate([w_in[:, :h * dh] * SB_QSCALE, w_in[:, h * dh:]], axis=1)
            qkv = nm(w_in.astype(BF16), name="sb_in")
            o = _sb_attention(qkv, **sizes)
            w_out = sb_w_out[j]
        elif mixer == 1:
            w_in = nsa_w_in[j]
            n_q, n_kv = h * dh, g * dh
            w_att = jnp.concatenate([w_in[:, :n_q] * NSA_QSCALE, w_in[:, n_q + 2 * n_kv:n_q + 6 * n_kv]], axis=1)
            w_cmp = w_in[:, n_q:n_q + 2 * n_kv]
            w_g = w_in[:, n_q + 6 * n_kv:].reshape(d, 3, g, NSA_REP).transpose(0, 2, 1, 3).reshape(d, g, 3 * NSA_REP)
            w_g = jnp.pad(w_g, ((0, 0), (0, 0), (0, LANE - 3 * NSA_REP))).reshape(d, g * LANE)
            qkv, hn = nm(w_att.astype(BF16), emit_h=True, name="nsa_in")
            hmm = functools.partial(_norm_mm, hn, None, seq_len=seq_len, normed=True)
            raw = hmm(w_cmp.astype(BF16), out_dtype=F32, name="nsa_in_cmp")
            gates = hmm(w_g.astype(BF16), out_dtype=F32, act="sigmoid", name="nsa_in_gates")
            kvc = _nsa_compress(raw, nsa_cmp_pe[j], nsa_cmp_w1[j].reshape(2, NSA_CMP_BLOCK, dh, dh).astype(BF16),
                                nsa_cmp_w2[j].astype(BF16), **sizes)
            n_slots = seq_len // NSA_CMP_STRIDE
            dist_c = np.arange(seq_len)[:, None] - (NSA_CMP_STRIDE * np.arange(n_slots)[None, :] + NSA_CMP_BLOCK - 1)
            bias_c = _t5_gather(t5_bias, _t5_bucket_np(dist_c), tr=min(seq_len, 512), mult=LOG2E)
            oc, sel = _nsa_cmp_attention(qkv, kvc, bias_c, gates, **sizes)
            o = _nsa_main_attention(qkv, sel, tables, gates, oc, **sizes)
            w_out = nsa_w_out[j]
        elif mixer == 2:
            lambda_init = 0.8 - 0.6 * math.exp(-0.3 * i)
            w_in = diff_w_in[j]
            w_in = jnp.concatenate([w_in[:, :h * dh] * DIFF_QSCALE, w_in[:, h * dh:]], axis=1)
            qkv = nm(w_in.astype(BF16), name="diff_in")
            o = _diff_attention(qkv, _attention_tables(t5_bias, DIFF_T), diff_lambda[j], diff_head_g[j],
                                lambda_init=lambda_init, **sizes)
            w_out = diff_w_out[j]
        else:
            w_in = mla_w_in[j]
            nq_l, nkv_l = MLA_Q_LORA, MLA_KV_LORA
            w_lat = jnp.concatenate([w_in[:, :nq_l], jnp.zeros((d, 2 * nkv_l - nq_l), w_in.dtype),
                                     w_in[:, nq_l:nq_l + nkv_l]], axis=1)
            lat, hn = nm(w_lat.astype(BF16), out_dtype=F32, emit_h=True, name="mla_in")
            w_kr = w_in[:, nq_l + nkv_l:]
            zero = jnp.zeros_like(w_kr)
            cos2, sin2 = _rope_tables(seq_len, 2 * LANE)
            wa, wb = _rope_weights(jnp.stack([w_kr, zero, zero, w_kr], axis=1))
            kr = _norm_mm(hn, None, wa, wb=wb, cos=cos2, sin=sin2, seq_len=seq_len, normed=True, name="mla_in_rope")
            w_qb = (mla_w_qb[j] * MLA_QSCALE).reshape(nq_l, h, MLA_NOPE + MLA_ROPE)
            qn = _norm_mm(lat, mla_q_g[j], w_qb[:, :, :MLA_NOPE].reshape(nq_l, h * MLA_NOPE).astype(BF16),
                          seq_len=seq_len, x_cols=nq_l, x_col_block=0, name="mla_q_nope")
            cosh, sinh = _rope_tables(seq_len, h * MLA_ROPE)
            wa, wb = _rope_weights(w_qb[:, :, MLA_NOPE:])
            qr = _norm_mm(lat, mla_q_g[j], wa, wb=wb, cos=cosh, sin=sinh, seq_len=seq_len, x_cols=nq_l,
                          x_col_block=0, name="mla_q_rope")
            kv = _norm_mm(lat, mla_kv_g[j], mla_w_kvb[j].astype(BF16), seq_len=seq_len, x_cols=nkv_l,
                          x_col_block=2, name="mla_kv")
            o = _mla_attention(qn, qr, kv, kr, **sizes)
            w_out = mla_w_out[j]
        xf = _mm_residual(o, w_out.astype(BF16), xf, gt1, seq_len=seq_len)
        xf = _conv_ffn(xf, norm_g[i, 1], sh2, sc2, ffn_w_up[i].astype(BF16), ffn_conv_w[i], ffn_conv_b[i],
                       ffn_w_down[i].astype(BF16), gt2, final_g, seq_len=seq_len, final_norm=(i == depth - 1))
    return xf.reshape(batch, seq_len, d)
```

```python
import functools
import math

import numpy as np
import jax
import jax.numpy as jnp
from jax import lax
from jax.experimental import pallas as pl
from jax.experimental.pallas import tpu as pltpu

F32 = jnp.float32
BF16 = jnp.bfloat16
EPS = 1e-6
NEG = -1e30

LANE = 128
HALO = 16
VMEM_LIMIT = 56 * 2**20

T5_BUCKETS = 32
T5_MAX_DIST = 128
N_HEADS = 16
HEAD_DIM = 128
NSA_GROUPS = 4
NSA_REP = 4
NSA_CMP_BLOCK = 32
NSA_CMP_STRIDE = 16
NSA_SEL_BLOCK = 64
NSA_TOPN = 16
NSA_WINDOW = 512
FORCED_SCORE = 1e9
DIFF_DIM = 64
MLA_Q_LORA = 768
MLA_KV_LORA = 512
MLA_NOPE = 128
MLA_ROPE = 64
ROPE_THETA = 10000.0
CONV_WIDTH = 3
ATT_T = 256
NSA_CMP_T = 512
WIDE_TN = 2048
DIFF_T = 512
MLA_T = 512
ATT_TQ = 512
ATT_TK = 256
LOG2E = 1.4426950408889634
SB_QSCALE = -(HEAD_DIM ** -0.5)
NSA_QSCALE = HEAD_DIM ** -0.5 * LOG2E
DIFF_QSCALE = DIFF_DIM ** -0.5 * LOG2E
MLA_QSCALE = (MLA_NOPE + MLA_ROPE) ** -0.5 * LOG2E
M_INIT = -1e30
MASKED = -2e30
SB_DEAD = -120.0
T5_MASKED = T5_BUCKETS


def _params(*sem):
    return pltpu.CompilerParams(dimension_semantics=sem, vmem_limit_bytes=VMEM_LIMIT)


def _dot(a, b):
    return jnp.dot(a, b, preferred_element_type=F32)


def _dot_nt(a, b):
    return lax.dot_general(a, b, (((1,), (1,)), ((), ())), preferred_element_type=F32)


def _lane_tile(x, reps):
    return jnp.concatenate([x] * reps, axis=1)


def _split_bf16(x):
    hi = x.astype(BF16)
    lo = (x - hi.astype(F32)).astype(BF16)
    return hi, lo


def _rms(x, g):
    ms = jnp.mean(x * x, axis=-1, keepdims=True)
    return x * lax.rsqrt(ms + EPS) * g


def _ada_kernel(c_ref, w_ref, b_ref, o_ref):
    c = c_ref[...]
    a = c * jax.nn.sigmoid(c)
    a_hi, a_lo = _split_bf16(a)
    w_hi, w_lo = _split_bf16(w_ref[0])
    o_ref[0] = _dot(a_hi, w_hi) + _dot(a_lo, w_hi) + _dot(a_hi, w_lo) + b_ref[0]


def _ada_mod(c, ada_w, ada_b, tn=1024):
    depth, d, n = ada_w.shape
    b = c.shape[0]
    return pl.pallas_call(
        _ada_kernel,
        grid=(depth, n // tn),
        in_specs=[pl.BlockSpec((b, d), lambda l, j: (0, 0)),
                  pl.BlockSpec((1, d, tn), lambda l, j: (l, 0, j)),
                  pl.BlockSpec((1, 1, tn), lambda l, j: (l, 0, j))],
        out_specs=pl.BlockSpec((1, b, tn), lambda l, j: (l, 0, j)),
        out_shape=jax.ShapeDtypeStruct((depth, b, n), F32),
        compiler_params=_params("parallel", "parallel"),
        name="ada_mod",
    )(c, ada_w, ada_b.reshape(depth, 1, n))


def _norm_mm_kernel(*refs, modulated, rope, act, normed):
    it = iter(refs)
    x_ref = next(it)
    g_ref = sh_ref = sc_ref = wb_ref = cos_ref = sin_ref = None
    if not normed:
        g_ref = next(it)
        if modulated:
            sh_ref, sc_ref = next(it), next(it)
    w_ref = next(it)
    if rope:
        wb_ref, cos_ref, sin_ref = next(it), next(it), next(it)
    o_ref = next(it)
    if normed:
        h = x_ref[...]
    else:
        h_ref = next(it)

        @pl.when(pl.program_id(1) == 0)
        def _():
            y = _rms(x_ref[...], g_ref[...])
            if modulated:
                y = y * (1.0 + sc_ref[0]) + sh_ref[0]
            h_ref[...] = y.astype(BF16)

        h = h_ref[...]
    acc = _dot(h, w_ref[...])
    if rope:
        acc = acc * cos_ref[...] + _dot(h, wb_ref[...]) * sin_ref[...]
    if act == "sigmoid":
        acc = jax.nn.sigmoid(acc)
    o_ref[...] = acc.astype(o_ref.dtype)


def _norm_mm(x, g, w, *, seq_len, shift=None, scale=None, wb=None, cos=None, sin=None, act=None, normed=False,
             emit_h=False, out_dtype=BF16, x_cols=None, x_col_block=0, tm=1024, tn=1024, name="norm_mm"):
    m = x.shape[0]
    k = x.shape[1] if x_cols is None else x_cols
    n = w.shape[1]
    while n % tn:
        tn //= 2
    modulated, rope = shift is not None, wb is not None
    tps = seq_len // tm
    in_specs = [pl.BlockSpec((tm, k), lambda i, j: (i, x_col_block))]
    args = [x]
    if not normed:
        in_specs.append(pl.BlockSpec((1, k), lambda i, j: (0, 0)))
        args.append(g.reshape(1, k))
        if modulated:
            in_specs += [pl.BlockSpec((1, 1, k), lambda i, j: (i // tps, 0, 0))] * 2
            args += [shift, scale]
    in_specs.append(pl.BlockSpec((k, tn), lambda i, j: (0, j)))
    args.append(w)
    if rope:
        in_specs.append(pl.BlockSpec((k, tn), lambda i, j: (0, j)))
        in_specs += [pl.BlockSpec((tm, tn), lambda i, j: (i % tps, j))] * 2
        args += [wb, cos, sin]
    out_specs = [pl.BlockSpec((tm, tn), lambda i, j: (i, j))]
    out_shape = [jax.ShapeDtypeStruct((m, n), out_dtype)]
    scratch = []
    if emit_h:
        out_specs.append(pl.BlockSpec((tm, k), lambda i, j: (i, 0)))
        out_shape.append(jax.ShapeDtypeStruct((m, k), BF16))
    elif not normed:
        scratch.append(pltpu.VMEM((tm, k), BF16))
    out = pl.pallas_call(
        functools.partial(_norm_mm_kernel, modulated=modulated, rope=rope, act=act, normed=normed),
        grid=(m // tm, n // tn),
        in_specs=in_specs,
        out_specs=out_specs,
        out_shape=out_shape,
        scratch_shapes=scratch,
        compiler_params=_params("parallel", "arbitrary"),
        name=name,
    )(*args)
    return out if emit_h else out[0]


def _mm_res_kernel(a_ref, w_ref, x_ref, gate_ref, o_ref):
    o_ref[...] = x_ref[...] + gate_ref[0] * _dot(a_ref[...], w_ref[...])


def _mm_residual(a, w, x, gate, *, seq_len, tm=1024, tn=1024, name="mm_residual"):
    m, k = a.shape
    n = w.shape[1]
    tps = seq_len // tm
    return pl.pallas_call(
        _mm_res_kernel,
        grid=(m // tm, n // tn),
        in_specs=[pl.BlockSpec((tm, k), lambda i, j: (i, 0)),
                  pl.BlockSpec((k, tn), lambda i, j: (0, j)),
                  pl.BlockSpec((tm, tn), lambda i, j: (i, j)),
                  pl.BlockSpec((1, 1, tn), lambda i, j: (i // tps, 0, j))],
        out_specs=pl.BlockSpec((tm, tn), lambda i, j: (i, j)),
        out_shape=jax.ShapeDtypeStruct((m, n), F32),
        compiler_params=_params("parallel", "arbitrary"),
        name=name,
    )(a, w, x, gate)


def _ffn_kernel(x_ref, xp_ref, g_ref, sh_ref, sc_ref, wg_ref, wu_ref, cwg_ref, cwu_ref, cbg_ref, cbu_ref,
                wd_ref, gate_ref, fg_ref, o_ref, h_ref, acc_ref, ug_scr, uu_scr, *, tiles_per_seq, final_norm):
    i, j = pl.program_id(0), pl.program_id(1)

    @pl.when(j == 0)
    def _():
        def nm(x):
            return _rms(x, g_ref[...]) * (1.0 + sc_ref[0]) + sh_ref[0]
        h_ref[HALO:, :] = nm(x_ref[...]).astype(BF16)
        keep = jnp.where(i % tiles_per_seq == 0, 0.0, 1.0)
        h_ref[:HALO, :] = (nm(xp_ref[...]) * keep).astype(BF16)
        acc_ref[...] = jnp.zeros_like(acc_ref)

    h = h_ref[...]

    def branch(w_ref, cw_ref, cb_ref, u_scr):
        u_scr[...] = _dot(h, w_ref[...])
        tm = u_scr.shape[0] - HALO
        cw = cw_ref[...]
        return (cw[0:1] * u_scr[pl.ds(HALO - 2, tm), :] + cw[1:2] * u_scr[pl.ds(HALO - 1, tm), :]
                + cw[2:3] * u_scr[pl.ds(HALO, tm), :] + cb_ref[...])

    gt = branch(wg_ref, cwg_ref, cbg_ref, ug_scr)
    up = branch(wu_ref, cwu_ref, cbu_ref, uu_scr)
    a = gt * jax.nn.sigmoid(gt) * up
    acc_ref[...] += _dot(a.astype(BF16), wd_ref[...])

    @pl.when(j == pl.num_programs(1) - 1)
    def _():
        y = x_ref[...] + gate_ref[0] * acc_ref[...]
        if final_norm:
            y = _rms(y, fg_ref[...])
        o_ref[...] = y


def _conv_ffn(x, g, shift, scale, w_up, conv_w, conv_b, w_down, gate, final_g, *, seq_len, final_norm,
              tm=512, tf=512):
    m, d = x.shape
    f = w_down.shape[0]
    nf = f // tf
    tps = seq_len // tm
    hb = tm // HALO
    row = lambda i, j: (i, 0)
    per_batch = lambda i, j: (i // tps, 0, 0)
    return pl.pallas_call(
        functools.partial(_ffn_kernel, tiles_per_seq=tps, final_norm=final_norm),
        grid=(m // tm, nf),
        in_specs=[pl.BlockSpec((tm, d), row),
                  pl.BlockSpec((HALO, d), lambda i, j: (jnp.maximum(i * hb - 1, 0), 0)),
                  pl.BlockSpec((1, d), lambda i, j: (0, 0)),
                  pl.BlockSpec((1, 1, d), per_batch),
                  pl.BlockSpec((1, 1, d), per_batch),
                  pl.BlockSpec((d, tf), lambda i, j: (0, j)),
                  pl.BlockSpec((d, tf), lambda i, j: (0, j + nf)),
                  pl.BlockSpec((CONV_WIDTH, tf), lambda i, j: (0, j)),
                  pl.BlockSpec((CONV_WIDTH, tf), lambda i, j: (0, j + nf)),
                  pl.BlockSpec((1, tf), lambda i, j: (0, j)),
                  pl.BlockSpec((1, tf), lambda i, j: (0, j + nf)),
                  pl.BlockSpec((tf, d), lambda i, j: (j, 0)),
                  pl.BlockSpec((1, 1, d), per_batch),
                  pl.BlockSpec((1, d), lambda i, j: (0, 0))],
        out_specs=pl.BlockSpec((tm, d), row),
        out_shape=jax.ShapeDtypeStruct((m, d), F32),
        scratch_shapes=[pltpu.VMEM((HALO + tm, d), BF16), pltpu.VMEM((tm, d), F32),
                        pltpu.VMEM((HALO + tm, tf), F32), pltpu.VMEM((HALO + tm, tf), F32)],
        compiler_params=_params("parallel", "arbitrary"),
        name="conv_ffn",
    )(x, x, g.reshape(1, d), shift, scale, w_up, w_up, conv_w, conv_w, conv_b.reshape(1, -1),
      conv_b.reshape(1, -1), w_down, gate, final_g.reshape(1, d))


def _t5_bucket_np(dist):
    n = np.maximum(dist, 0)
    max_exact = T5_BUCKETS // 2
    nf = np.maximum(n, 1).astype(np.float64)
    val = np.log(nf / max_exact) / math.log(T5_MAX_DIST / max_exact) * (T5_BUCKETS - max_exact)
    large = max_exact + np.trunc(val + 1e-6).astype(np.int64)
    return np.where(n < max_exact, n, np.minimum(large, T5_BUCKETS - 1)).astype(np.int32)


def _t5_gather_kernel(t5_ref, bkt_ref, o_ref, *, mult):
    h = pl.program_id(0)
    bk = bkt_ref[...]
    acc = jnp.zeros(bk.shape, F32)
    for b in range(T5_BUCKETS):
        acc = jnp.where(bk == b, t5_ref[b, h], acc)
    o_ref[0] = jnp.where(bk == T5_MASKED, MASKED, acc * mult)


def _t5_gather(t5_bias, bkt, tr, mult=1.0):
    rows, cols = bkt.shape
    heads = t5_bias.shape[1]
    return pl.pallas_call(
        functools.partial(_t5_gather_kernel, mult=mult),
        grid=(heads, rows // tr),
        in_specs=[pl.BlockSpec(memory_space=pltpu.SMEM),
                  pl.BlockSpec((tr, cols), lambda h, r: (r, 0))],
        out_specs=pl.BlockSpec((1, tr, cols), lambda h, r: (h, r, 0)),
        out_shape=jax.ShapeDtypeStruct((heads, rows, cols), F32),
        compiler_params=_params("parallel", "parallel"),
        name="t5_gather",
    )(t5_bias, jnp.asarray(bkt))


TAB_DIAG, TAB_SUB, TAB_FAR, TAB_EDGE = 0, 1, 2, 3


def _attention_tables(t5_bias, t):
    i = np.arange(t)[:, None]
    j = np.arange(t)[None, :]
    assert int(_t5_bucket_np(np.array(t + 1))) == T5_BUCKETS - 1
    far = np.full((t, t), T5_BUCKETS - 1, np.int32)
    bkt = np.concatenate([np.where(j <= i, _t5_bucket_np(i - j), T5_MASKED), _t5_bucket_np(t + i - j), far,
                          np.where(j > i, far, T5_MASKED)], axis=0).astype(np.int32)
    return _t5_gather(t5_bias, bkt, tr=t, mult=LOG2E)


def _tile_iota(t):
    return lax.broadcasted_iota(jnp.int32, (t, t), 0), lax.broadcasted_iota(jnp.int32, (t, t), 1)


def _rect_iota(i, tq, start, tk):
    rowg = i * tq + lax.broadcasted_iota(jnp.int32, (tq, tk), 0)
    colg = start + lax.broadcasted_iota(jnp.int32, (tq, tk), 1)
    return rowg, colg


def _osm_reset(m_scr, l_scr, acc_scr):
    m_scr[...] = jnp.full(m_scr.shape, M_INIT, F32)
    l_scr[...] = jnp.zeros(l_scr.shape, F32)
    acc_scr[...] = jnp.zeros(acc_scr.shape, F32)


def _osm_update(t2, rows, slot, p_scr, m_scr, l_scr, acc_scr):
    m_prev = m_scr[rows, :]
    m_new = jnp.maximum(m_prev, jnp.max(t2, axis=-1, keepdims=True))
    p = jnp.exp2(t2 - _lane_tile(m_new, t2.shape[1] // LANE))
    alpha = jnp.exp2(m_prev - m_new)
    l_scr[rows, :] = alpha * l_scr[rows, :]
    m_scr[rows, :] = m_new
    p_scr[slot, rows, :] = p.astype(BF16)
    acc_scr[rows, :] = alpha * acc_scr[rows, :]


def _osm_accum(v, slot, p_scr, l_scr, acc_scr):
    vext = jnp.concatenate([v, jnp.ones(v.shape, v.dtype)], axis=-1)
    half = p_scr.shape[1] // 2
    for part in range(2):
        rows = pl.ds(part * half, half)
        pv = _dot(p_scr[slot, rows, :], vext)
        acc_scr[rows, :] += pv[:, :HEAD_DIM]
        l_scr[rows, :] += pv[:, HEAD_DIM:]


def _pipelined_tiles(n_tiles, scores, update, update_last=None):
    update_last = update if update_last is None else update_last
    last = n_tiles - 1
    n_pairs = last // 2
    scores(0, 0)

    def pair(n, cr):
        scores(2 * n + 1, 1)
        update(2 * n, 0)
        scores(2 * n + 2, 0)
        update(2 * n + 1, 1)
        return cr

    lax.fori_loop(0, n_pairs, pair, 0)

    @pl.when(last % 2 == 0)
    def _():
        update_last(last, 0)

    @pl.when(last % 2 == 1)
    def _():
        scores(last, 1)
        update(last - 1, 0)
        update_last(last, 1)


def _att_scratch(rows, tk, slots=2):
    return [pltpu.VMEM((slots, rows, tk), F32), pltpu.VMEM((slots, rows, tk), BF16), pltpu.VMEM((rows, LANE), F32),
            pltpu.VMEM((rows, LANE), F32), pltpu.VMEM((rows, HEAD_DIM), F32)]


def _sb_kernel(q_ref, k_ref, v_ref, o_ref, s_scr, c_scr, acc_scr, *, tq, tk):
    i = pl.program_id(2)
    q = q_ref[...]
    row, col = _tile_iota(tk)
    upper = jnp.where(row > col, 1.0, 0.0).astype(BF16)
    c_scr[...] = jnp.zeros(c_scr.shape, F32)
    acc_scr[...] = jnp.zeros(acc_scr.shape, F32)
    reps = tk // LANE

    def scores(kb, slot):
        start = pl.multiple_of(kb * tk, tk)
        s_scr[slot] = _dot_nt(q, k_ref[pl.ds(start, tk), :])

    def update(kb, slot, diag):
        start = pl.multiple_of(kb * tk, tk)
        nz = s_scr[slot]
        e = jnp.exp2(jnp.abs(nz) * (-LOG2E))
        lk = jnp.minimum(nz, 0.0) - jnp.log(1.0 + e)
        if diag:
            rowg, colg = _rect_iota(i, tq, start, tk)
            past = colg < rowg
            lk = jnp.where(past, lk, 0.0)
        hi, lo = _split_bf16(lk)
        c = c_scr[...]
        later = _dot(hi, upper) + _dot(lo, upper) + _lane_tile(c, reps)
        a = jnp.exp(lk - nz + later)
        if diag:
            a = jnp.where(past, a, 0.0)
        acc_scr[...] += _dot(a.astype(BF16), v_ref[pl.ds(start, tk), :])
        c_scr[...] = c + jnp.sum(lk, axis=-1, keepdims=True)

    assert tq == 2 * tk
    scores(2 * i + 1, 1)
    scores(2 * i, 0)
    update(2 * i + 1, 1, True)
    scores(jnp.maximum(2 * i - 1, 0), 1)
    update(2 * i, 0, True)

    def more(carry):
        n, c_max = carry
        return (n < i) & (c_max > SB_DEAD)

    def pair(carry):
        n, _ = carry
        kb = 2 * (i - n) - 1
        scores(kb - 1, 0)
        update(kb, 1, False)
        c_mid = jnp.max(c_scr[...])

        @pl.when(c_mid > SB_DEAD)
        def _():
            scores(jnp.maximum(kb - 2, 0), 1)
            update(kb - 1, 0, False)

        return n + 1, jnp.where(c_mid > SB_DEAD, jnp.max(c_scr[...]), c_mid)

    lax.while_loop(more, pair, (0, jnp.max(c_scr[...])))
    o_ref[...] = acc_scr[...].astype(o_ref.dtype)


def _sb_attention(qkv, *, batch, seq_len, tq=ATT_TQ, tk=ATT_TK):
    h = N_HEADS
    nq = seq_len // tq
    return pl.pallas_call(
        functools.partial(_sb_kernel, tq=tq, tk=tk),
        grid=(batch, h, nq),
        in_specs=[pl.BlockSpec((tq, HEAD_DIM), lambda b, hh, i: (b * nq + i, hh)),
                  pl.BlockSpec((seq_len, HEAD_DIM), lambda b, hh, i: (b, h + hh)),
                  pl.BlockSpec((seq_len, HEAD_DIM), lambda b, hh, i: (b, 2 * h + hh))],
        out_specs=pl.BlockSpec((tq, HEAD_DIM), lambda b, hh, i: (b * nq + i, hh)),
        out_shape=jax.ShapeDtypeStruct((batch * seq_len, h * HEAD_DIM), BF16),
        scratch_shapes=[pltpu.VMEM((2, tq, tk), F32), pltpu.VMEM((tq, LANE), F32), pltpu.VMEM((tq, HEAD_DIM), F32)],
        compiler_params=_params("parallel", "parallel", "arbitrary"),
        name="sb_attention",
    )(qkv, qkv, qkv)


def _table_offset(kb, i, t):
    return pl.multiple_of(jnp.where(kb == i, TAB_DIAG * t, jnp.where(kb == i - 1, TAB_SUB * t, TAB_FAR * t)), t)


def _diff_kernel(q_ref, k_ref, v_ref, tab_ref, lam_ref, hg_ref, o_ref, s_scr, p_scr, m_scr, l_scr, acc_scr, *, t,
                 lambda_init):
    i = pl.program_id(2)
    lam = lam_ref[...]
    lmbda = (jnp.exp(jnp.sum(lam[0:1] * lam[1:2], axis=-1, keepdims=True))
             - jnp.exp(jnp.sum(lam[2:3] * lam[3:4], axis=-1, keepdims=True)) + lambda_init)
    for a in range(2):
        lanes = slice(a * HEAD_DIM, (a + 1) * HEAD_DIM)
        q = q_ref[:, lanes]
        lane = lax.broadcasted_iota(jnp.int32, q.shape, 1)
        zero = jnp.zeros_like(q)
        q2 = jnp.concatenate([jnp.where(lane < DIFF_DIM, q, zero), jnp.where(lane >= DIFF_DIM, q, zero)], axis=0)
        _osm_reset(m_scr, l_scr, acc_scr)

        def scores(kb, slot):
            s_scr[slot] = _dot_nt(q2, k_ref[pl.ds(pl.multiple_of(kb * t, t), t), lanes])

        def update(kb, slot):
            off = _table_offset(kb, i, t)
            for half in range(2):
                rows = pl.ds(half * t, t)
                t2 = s_scr[slot, rows, :] + tab_ref[a, pl.ds(off, t), :]
                _osm_update(t2, rows, slot, p_scr, m_scr, l_scr, acc_scr)
            _osm_accum(v_ref[pl.ds(pl.multiple_of(kb * t, t), t), lanes], slot, p_scr, l_scr, acc_scr)

        _pipelined_tiles(i + 1, scores, update)

        o = acc_scr[...] / l_scr[...]
        o = o[:t] - lmbda * o[t:]
        o_ref[:, lanes] = (_rms(o, hg_ref[...]) * (1.0 - lambda_init)).astype(o_ref.dtype)


def _diff_attention(qkv, tables, lam, head_g, *, batch, seq_len, lambda_init, t=DIFF_T):
    h = N_HEADS
    nq = seq_len // t
    pw = 2 * HEAD_DIM
    pairs = h // 2
    return pl.pallas_call(
        functools.partial(_diff_kernel, t=t, lambda_init=lambda_init),
        grid=(batch, pairs, nq),
        in_specs=[pl.BlockSpec((t, pw), lambda b, pp, i: (b * nq + i, pp)),
                  pl.BlockSpec((seq_len, pw), lambda b, pp, i: (b, pairs + pp)),
                  pl.BlockSpec((seq_len, pw), lambda b, pp, i: (b, 2 * pairs + pp)),
                  pl.BlockSpec((2, 4 * t, t), lambda b, pp, i: (pp, 0, 0)),
                  pl.BlockSpec((4, DIFF_DIM), lambda b, pp, i: (0, 0)),
                  pl.BlockSpec((1, HEAD_DIM), lambda b, pp, i: (0, 0))],
        out_specs=pl.BlockSpec((t, pw), lambda b, pp, i: (b * nq + i, pp)),
        out_shape=jax.ShapeDtypeStruct((batch * seq_len, h * HEAD_DIM), BF16),
        scratch_shapes=_att_scratch(2 * t, t),
        compiler_params=_params("parallel", "parallel", "arbitrary"),
        name="diff_attention",
    )(qkv, qkv, qkv, tables, lam, head_g.reshape(1, HEAD_DIM))


def _mla_kernel(qn_ref, qr_ref, kv_ref, kr_ref, o_ref, s_scr, p_scr, m_scr, l_scr, acc_scr, *, tq, tk):
    i = pl.program_id(2)
    assert tq == tk
    for a in range(2):
        lanes = slice(a * HEAD_DIM, (a + 1) * HEAD_DIM)
        kn_lanes = slice(2 * a * HEAD_DIM, (2 * a + 1) * HEAD_DIM)
        v_lanes = slice((2 * a + 1) * HEAD_DIM, (2 * a + 2) * HEAD_DIM)
        q = jnp.concatenate([qn_ref[:, lanes], qr_ref[...]], axis=-1)
        _osm_reset(m_scr, l_scr, acc_scr)

        def scores(kb, slot):
            rows = pl.ds(pl.multiple_of(kb * tk, tk), tk)
            k = jnp.concatenate([kv_ref[rows, kn_lanes], kr_ref[rows, lanes]], axis=-1)
            s_scr[slot] = _dot_nt(q, k)

        def update(kb, slot, mask):
            start = pl.multiple_of(kb * tk, tk)
            t2 = s_scr[slot]
            if mask:
                rowg, colg = _rect_iota(i, tq, start, tk)
                t2 = jnp.where(colg <= rowg, t2, MASKED)
            _osm_update(t2, pl.ds(0, tq), slot, p_scr, m_scr, l_scr, acc_scr)
            _osm_accum(kv_ref[pl.ds(start, tk), v_lanes], slot, p_scr, l_scr, acc_scr)

        _pipelined_tiles(i + 1, scores, lambda kb, slot: update(kb, slot, False),
                         lambda kb, slot: update(kb, slot, True))
        o_ref[:, lanes] = (acc_scr[...] / l_scr[...]).astype(o_ref.dtype)


def _mla_attention(qn, qr, kv, kr, *, batch, seq_len, tq=MLA_T, tk=MLA_T):
    h = N_HEADS
    nq = seq_len // tq
    pair = pl.BlockSpec((tq, 2 * HEAD_DIM), lambda b, pp, i: (b * nq + i, pp))
    return pl.pallas_call(
        functools.partial(_mla_kernel, tq=tq, tk=tk),
        grid=(batch, h // 2, nq),
        in_specs=[pair,
                  pl.BlockSpec((tq, LANE), lambda b, pp, i: (b * nq + i, pp)),
                  pl.BlockSpec((seq_len, 4 * HEAD_DIM), lambda b, pp, i: (b, pp)),
                  pl.BlockSpec((seq_len, 2 * LANE), lambda b, pp, i: (b, 0))],
        out_specs=pair,
        out_shape=jax.ShapeDtypeStruct((batch * seq_len, h * HEAD_DIM), BF16),
        scratch_shapes=_att_scratch(tq, tk),
        compiler_params=_params("parallel", "parallel", "arbitrary"),
        name="mla_attention",
    )(qn, qr, kv, kr)


def _compress_kernel(raw_ref, pe_ref, w1_ref, w2_ref, o_ref, *, n_slots):
    half = NSA_CMP_BLOCK // 2
    p1 = jnp.zeros((n_slots, HEAD_DIM), F32)
    p2 = jnp.zeros((n_slots, HEAD_DIM), F32)
    for l in range(half):
        a = raw_ref[pl.ds(l, n_slots, stride=NSA_CMP_STRIDE), :]
        p1 = p1 + _dot((a + pe_ref[0, l:l + 1, :]).astype(BF16), w1_ref[0, l])
        p2 = p2 + _dot((a + pe_ref[0, half + l:half + l + 1, :]).astype(BF16), w1_ref[0, half + l])
    pre = p1 + pltpu.roll(p2, n_slots - 1, 0)
    hid = pre * jax.nn.sigmoid(pre)
    o_ref[0, 0, 0] = _dot(hid.astype(BF16), w2_ref[0]).astype(o_ref.dtype)


def _nsa_compress(raw, pe, w1, w2, *, batch, seq_len):
    g = NSA_GROUPS
    n_slots = seq_len // NSA_CMP_STRIDE
    return pl.pallas_call(
        functools.partial(_compress_kernel, n_slots=n_slots),
        grid=(batch, 2, g),
        in_specs=[pl.BlockSpec((seq_len, HEAD_DIM), lambda b, kv, gg: (b, kv * g + gg)),
                  pl.BlockSpec((1, NSA_CMP_BLOCK, HEAD_DIM), lambda b, kv, gg: (kv, 0, 0)),
                  pl.BlockSpec((1, NSA_CMP_BLOCK, HEAD_DIM, HEAD_DIM), lambda b, kv, gg: (kv, 0, 0, 0)),
                  pl.BlockSpec((1, HEAD_DIM, HEAD_DIM), lambda b, kv, gg: (kv, 0, 0))],
        out_specs=pl.BlockSpec((1, 1, 1, n_slots, HEAD_DIM), lambda b, kv, gg: (b, kv, gg, 0, 0)),
        out_shape=jax.ShapeDtypeStruct((batch, 2, g, n_slots, HEAD_DIM), BF16),
        compiler_params=_params("parallel", "parallel", "parallel"),
        name="nsa_compress",
    )(raw, pe, w1, w2)


def _nsa_cmp_kernel(q_ref, kc_ref, vc_ref, bias_ref, gates_ref, ovt_ref, oc_ref, sel_ref, *, t, n_slots, n_sel,
                    n_top):
    i = pl.program_id(2)
    kc = kc_ref[0, 0, 0]
    vc = vc_ref[0, 0, 0]
    qpos = i * t + lax.broadcasted_iota(jnp.int32, (t, n_slots), 0)
    cmp_end = NSA_CMP_STRIDE * lax.broadcasted_iota(jnp.int32, (t, n_slots), 1) + (NSA_CMP_BLOCK - 1)
    valid = cmp_end <= qpos
    gates = gates_ref[...]
    psum = jnp.zeros((t, n_slots), F32)
    for r in range(NSA_REP):
        q = q_ref[:, r * HEAD_DIM:(r + 1) * HEAD_DIM]
        s = jnp.where(valid, _dot_nt(q, kc) + bias_ref[r], NEG)
        m = jnp.max(s, axis=-1, keepdims=True)
        p = jnp.where(valid, jnp.exp2(s - m), 0.0)
        p = p / jnp.maximum(jnp.sum(p, axis=-1, keepdims=True), 1e-30)
        psum = psum + p
        oc_ref[:, r * HEAD_DIM:(r + 1) * HEAD_DIM] = gates[:, r:r + 1] * _dot(p.astype(BF16), vc)

    hi, lo = _split_bf16(psum)
    ovt = ovt_ref[...]
    imp = _dot_nt(ovt, hi) + _dot_nt(ovt, lo)
    blk = lax.broadcasted_iota(jnp.int32, (n_sel, t), 0)
    tpos = i * t + lax.broadcasted_iota(jnp.int32, (n_sel, t), 1)
    cur = tpos // NSA_SEL_BLOCK
    forced = (blk == 0) | (blk == cur) | (blk == cur - 1)
    score = jnp.where(blk * NSA_SEL_BLOCK <= tpos, jnp.where(forced, FORCED_SCORE, imp), -1.0)
    rank = jnp.zeros((n_sel, t), F32)
    for mm in range(n_sel):
        sm = score[mm:mm + 1, :]
        ahead = (sm > score) | ((sm == score) & (blk > mm))
        rank = rank + jnp.where(ahead, 1.0, 0.0)
    sel_t = jnp.where(rank < n_top, 1.0, 0.0).astype(BF16)
    row, col = _tile_iota(t)
    eye = jnp.where(row == col, 1.0, 0.0).astype(BF16)
    sel_ref[0, 0] = _dot_nt(eye, sel_t).astype(sel_ref.dtype)


def _nsa_cmp_attention(q_all, kvc, bias_c, gates, *, batch, seq_len, t=NSA_CMP_T):
    g = NSA_GROUPS
    nq = seq_len // t
    n_slots = seq_len // NSA_CMP_STRIDE
    n_sel = seq_len // NSA_SEL_BLOCK
    c0 = NSA_CMP_STRIDE * np.arange(n_slots)[:, None]
    s0 = NSA_SEL_BLOCK * np.arange(n_sel)[None, :]
    overlap = (c0 < s0 + NSA_SEL_BLOCK) & (c0 + NSA_CMP_BLOCK > s0)
    ovt = jnp.asarray(overlap.T.astype(np.float32), dtype=BF16)
    gw = NSA_REP * HEAD_DIM
    return pl.pallas_call(
        functools.partial(_nsa_cmp_kernel, t=t, n_slots=n_slots, n_sel=n_sel, n_top=min(NSA_TOPN, n_sel)),
        grid=(batch, g, nq),
        in_specs=[pl.BlockSpec((t, gw), lambda b, gg, i: (b * nq + i, gg)),
                  pl.BlockSpec((1, 1, 1, n_slots, HEAD_DIM), lambda b, gg, i: (b, 0, gg, 0, 0)),
                  pl.BlockSpec((1, 1, 1, n_slots, HEAD_DIM), lambda b, gg, i: (b, 1, gg, 0, 0)),
                  pl.BlockSpec((NSA_REP, t, n_slots), lambda b, gg, i: (gg, i, 0)),
                  pl.BlockSpec((t, LANE), lambda b, gg, i: (b * nq + i, gg)),
                  pl.BlockSpec((n_sel, n_slots), lambda b, gg, i: (0, 0))],
        out_specs=[pl.BlockSpec((t, gw), lambda b, gg, i: (b * nq + i, gg)),
                   pl.BlockSpec((1, 1, t, n_sel), lambda b, gg, i: (b, gg, i, 0))],
        out_shape=[jax.ShapeDtypeStruct((batch * seq_len, g * gw), F32),
                   jax.ShapeDtypeStruct((batch, g, seq_len, n_sel), BF16)],
        compiler_params=_params("parallel", "parallel", "arbitrary"),
        name="nsa_cmp_attention",
    )(q_all, kvc, kvc, bias_c, gates, ovt)


def _nsa_main_kernel(q_ref, ks_ref, vs_ref, kw_ref, vw_ref, sel_ref, tab_ref, gates_ref, oc_ref, o_ref,
                     s_scr, p_scr, m_scr, l_scr, acc_scr, os_scr, mk_scr, *, t):
    i = pl.program_id(2)
    rep = NSA_REP
    q = jnp.concatenate([q_ref[:, r * HEAD_DIM:(r + 1) * HEAD_DIM] for r in range(rep)], axis=0)

    sel = sel_ref[0, 0]
    n_sel = sel.shape[1]
    blk_row = lax.broadcasted_iota(jnp.int32, (n_sel, t), 0)
    key_col = lax.broadcasted_iota(jnp.int32, (n_sel, t), 1)
    _osm_reset(m_scr, l_scr, acc_scr)

    def sel_scores(kb, slot):
        s_scr[slot] = _dot_nt(q, ks_ref[pl.ds(pl.multiple_of(kb * t, t), t), :])

    def sel_update(kb, slot):
        start = pl.multiple_of(kb * t, t)
        expand = jnp.where((start + key_col) // NSA_SEL_BLOCK == blk_row, 1.0, 0.0).astype(BF16)
        mk_scr[...] = _dot(sel, expand)
        off = _table_offset(kb, i, t)
        for r in range(rep):
            rows = pl.ds(r * t, t)
            t2 = s_scr[slot, rows, :] + tab_ref[r, pl.ds(off, t), :]
            t2 = jnp.where(mk_scr[...] > 0.5, t2, MASKED)
            _osm_update(t2, rows, slot, p_scr, m_scr, l_scr, acc_scr)
        _osm_accum(vs_ref[pl.ds(start, t), :], slot, p_scr, l_scr, acc_scr)

    _pipelined_tiles(i + 1, sel_scores, sel_update)
    os_scr[...] = acc_scr[...] / l_scr[...]

    n_back = NSA_WINDOW // t
    assert n_back == 2
    _osm_reset(m_scr, l_scr, acc_scr)

    def win_scores(kb, slot):
        s_scr[slot] = _dot_nt(q, kw_ref[pl.ds(pl.multiple_of(kb * t, t), t), :])

    def win_update(kb, slot, region):
        for r in range(rep):
            rows = pl.ds(r * t, t)
            t2 = s_scr[slot, rows, :] + tab_ref[r, pl.ds(region * t, t), :]
            _osm_update(t2, rows, slot, p_scr, m_scr, l_scr, acc_scr)
        _osm_accum(vw_ref[pl.ds(pl.multiple_of(kb * t, t), t), :], slot, p_scr, l_scr, acc_scr)

    def window(n_tiles):
        regions = (TAB_DIAG, TAB_SUB, TAB_EDGE)
        for back in range(n_tiles):
            win_scores(i - back, back)
        for back in reversed(range(n_tiles)):
            win_update(i - back, back, regions[back])

    pl.when(i >= 2)(lambda: window(3))
    pl.when(i == 1)(lambda: window(2))
    pl.when(i == 0)(lambda: window(1))
    o_w = acc_scr[...] / l_scr[...]
    o_s = os_scr[...]

    gates = gates_ref[...]
    for r in range(rep):
        rows = slice(r * t, (r + 1) * t)
        cols = slice(r * HEAD_DIM, (r + 1) * HEAD_DIM)
        o = oc_ref[:, cols] + gates[:, rep + r:rep + r + 1] * o_s[rows] + gates[:, 2 * rep + r:2 * rep + r + 1] * o_w[rows]
        o_ref[:, cols] = o.astype(o_ref.dtype)


def _nsa_main_attention(qkv, sel, tables, gates, oc, *, batch, seq_len, t=ATT_T):
    g = NSA_GROUPS
    nq = seq_len // t
    n_sel = seq_len // NSA_SEL_BLOCK
    gw = NSA_REP * HEAD_DIM
    qb = N_HEADS
    kv = lambda which: pl.BlockSpec((seq_len, HEAD_DIM), lambda b, gg, i: (b, qb + which * g + gg))
    tile = pl.BlockSpec((t, gw), lambda b, gg, i: (b * nq + i, gg))
    rows = NSA_REP * t
    return pl.pallas_call(
        functools.partial(_nsa_main_kernel, t=t),
        grid=(batch, g, nq),
        in_specs=[tile, kv(0), kv(1), kv(2), kv(3),
                  pl.BlockSpec((1, 1, t, n_sel), lambda b, gg, i: (b, gg, i, 0)),
                  pl.BlockSpec((NSA_REP, 4 * t, t), lambda b, gg, i: (gg, 0, 0)),
                  pl.BlockSpec((t, LANE), lambda b, gg, i: (b * nq + i, gg)),
                  tile],
        out_specs=tile,
        out_shape=jax.ShapeDtypeStruct((batch * seq_len, g * gw), BF16),
        scratch_shapes=_att_scratch(rows, t, slots=3) + [pltpu.VMEM((rows, HEAD_DIM), F32), pltpu.VMEM((t, t), F32)],
        compiler_params=_params("parallel", "parallel", "arbitrary"),
        name="nsa_main_attention",
    )(qkv, qkv, qkv, qkv, qkv, sel, tables, gates, oc)


def _rope_tables(seq_len, width):
    half = MLA_ROPE // 2
    inv = np.power(ROPE_THETA, -np.arange(half, dtype=np.float32) / half).astype(np.float32)
    ang = np.arange(seq_len, dtype=np.float32)[:, None] * inv[None, :]
    reps = width // half
    return jnp.asarray(np.tile(np.cos(ang), (1, reps))), jnp.asarray(np.tile(np.sin(ang), (1, reps)))


def _rope_weights(w):
    k, n, _ = w.shape
    half = MLA_ROPE // 2
    wb = jnp.concatenate([-w[..., half:], w[..., :half]], axis=-1)
    return w.reshape(k, n * MLA_ROPE).astype(BF16), wb.reshape(k, n * MLA_ROPE).astype(BF16)


def kernel(x, c, t5_bias, ada_w, ada_b, norm_g, final_g, ffn_w_up, ffn_conv_w, ffn_conv_b, ffn_w_down, sb_w_in, sb_w_out, nsa_w_in, nsa_cmp_pe, nsa_cmp_w1, nsa_cmp_w2, nsa_w_out, diff_w_in, diff_lambda, diff_head_g, diff_w_out, mla_w_in, mla_q_g, mla_w_qb, mla_kv_g, mla_w_kvb, mla_w_out):
    batch, seq_len, d = x.shape
    depth = ada_w.shape[0]
    h, dh, g = N_HEADS, HEAD_DIM, NSA_GROUPS
    sizes = dict(batch=batch, seq_len=seq_len)

    mod = _ada_mod(c, ada_w, ada_b)
    tables = _attention_tables(t5_bias, ATT_T)

    xf = x.reshape(batch * seq_len, d)
    for i in range(depth):
        mixer, j = i % 4, i // 4
        sh1, sc1, gt1, sh2, sc2, gt2 = (mod[i, :, n * d:(n + 1) * d].reshape(batch, 1, d) for n in range(6))
        nm = functools.partial(_norm_mm, xf, norm_g[i, 0], seq_len=seq_len, shift=sh1, scale=sc1)
        if mixer == 0:
            w_in = sb_w_in[j]
            w_in = jnp.concatenate([w_in[:, :h * dh] * SB_QSCALE, w_in[:, h * dh:]], axis=1)
            qkv = nm(w_in.astype(BF16), tn=WIDE_TN, name="sb_in")
            o = _sb_attention(qkv, **sizes)
            w_out = sb_w_out[j]
        elif mixer == 1:
            w_in = nsa_w_in[j]
            n_q, n_kv = h * dh, g * dh
            w_att = jnp.concatenate([w_in[:, :n_q] * NSA_QSCALE, w_in[:, n_q + 2 * n_kv:n_q + 6 * n_kv]], axis=1)
            w_cmp = w_in[:, n_q:n_q + 2 * n_kv]
            w_g = w_in[:, n_q + 6 * n_kv:].reshape(d, 3, g, NSA_REP).transpose(0, 2, 1, 3).reshape(d, g, 3 * NSA_REP)
            w_g = jnp.pad(w_g, ((0, 0), (0, 0), (0, LANE - 3 * NSA_REP))).reshape(d, g * LANE)
            qkv, hn = nm(w_att.astype(BF16), emit_h=True, name="nsa_in")
            hmm = functools.partial(_norm_mm, hn, None, seq_len=seq_len, normed=True)
            raw = hmm(w_cmp.astype(BF16), out_dtype=F32, name="nsa_in_cmp")
            gates = hmm(w_g.astype(BF16), out_dtype=F32, act="sigmoid", name="nsa_in_gates")
            kvc = _nsa_compress(raw, nsa_cmp_pe[j], nsa_cmp_w1[j].reshape(2, NSA_CMP_BLOCK, dh, dh).astype(BF16),
                                nsa_cmp_w2[j].astype(BF16), **sizes)
            n_slots = seq_len // NSA_CMP_STRIDE
            dist_c = np.arange(seq_len)[:, None] - (NSA_CMP_STRIDE * np.arange(n_slots)[None, :] + NSA_CMP_BLOCK - 1)
            bias_c = _t5_gather(t5_bias, _t5_bucket_np(dist_c), tr=min(seq_len, 512), mult=LOG2E)
            oc, sel = _nsa_cmp_attention(qkv, kvc, bias_c, gates, **sizes)
            o = _nsa_main_attention(qkv, sel, tables, gates, oc, **sizes)
            w_out = nsa_w_out[j]
        elif mixer == 2:
            lambda_init = 0.8 - 0.6 * math.exp(-0.3 * i)
            w_in = diff_w_in[j]
            w_in = jnp.concatenate([w_in[:, :h * dh] * DIFF_QSCALE, w_in[:, h * dh:]], axis=1)
            qkv = nm(w_in.astype(BF16), tn=WIDE_TN, name="diff_in")
            o = _diff_attention(qkv, _attention_tables(t5_bias, DIFF_T), diff_lambda[j], diff_head_g[j],
                                lambda_init=lambda_init, **sizes)
            w_out = diff_w_out[j]
        else:
            w_in = mla_w_in[j]
            nq_l, nkv_l = MLA_Q_LORA, MLA_KV_LORA
            w_lat = jnp.concatenate([w_in[:, :nq_l], jnp.zeros((d, 2 * nkv_l - nq_l), w_in.dtype),
                                     w_in[:, nq_l:nq_l + nkv_l]], axis=1)
            lat, hn = nm(w_lat.astype(BF16), out_dtype=F32, emit_h=True, name="mla_in")
            w_kr = w_in[:, nq_l + nkv_l:]
            zero = jnp.zeros_like(w_kr)
            cos2, sin2 = _rope_tables(seq_len, 2 * LANE)
            wa, wb = _rope_weights(jnp.stack([w_kr, zero, zero, w_kr], axis=1))
            kr = _norm_mm(hn, None, wa, wb=wb, cos=cos2, sin=sin2, seq_len=seq_len, normed=True, name="mla_in_rope")
            w_qb = (mla_w_qb[j] * MLA_QSCALE).reshape(nq_l, h, MLA_NOPE + MLA_ROPE)
            qn = _norm_mm(lat, mla_q_g[j], w_qb[:, :, :MLA_NOPE].reshape(nq_l, h * MLA_NOPE).astype(BF16),
                          seq_len=seq_len, x_cols=nq_l, x_col_block=0, tn=WIDE_TN, name="mla_q_nope")
            cosh, sinh = _rope_tables(seq_len, h * MLA_ROPE)
            wa, wb = _rope_weights(w_qb[:, :, MLA_NOPE:])
            qr = _norm_mm(lat, mla_q_g[j], wa, wb=wb, cos=cosh, sin=sinh, seq_len=seq_len, x_cols=nq_l,
                          x_col_block=0, name="mla_q_rope")
            kv = _norm_mm(lat, mla_kv_g[j], mla_w_kvb[j].astype(BF16), seq_len=seq_len, x_cols=nkv_l,
                          x_col_block=2, tn=WIDE_TN, name="mla_kv")
            o = _mla_attention(qn, qr, kv, kr, **sizes)
            w_out = mla_w_out[j]
        xf = _mm_residual(o, w_out.astype(BF16), xf, gt1, seq_len=seq_len)
        xf = _conv_ffn(xf, norm_g[i, 1], sh2, sc2, ffn_w_up[i].astype(BF16), ffn_conv_w[i], ffn_conv_b[i],
                       ffn_w_down[i].astype(BF16), gt2, final_g, seq_len=seq_len, final_norm=(i == depth - 1))
    return xf.reshape(batch, seq_len, d)
```

```python
import functools
import math

import numpy as np
import jax
import jax.numpy as jnp
from jax import lax
from jax.experimental import pallas as pl
from jax.experimental.pallas import tpu as pltpu

F32 = jnp.float32
BF16 = jnp.bfloat16
EPS = 1e-6
NEG = -1e30

LANE = 128
HALO = 16
VMEM_LIMIT = 56 * 2**20

T5_BUCKETS = 32
T5_MAX_DIST = 128
N_HEADS = 16
HEAD_DIM = 128
NSA_GROUPS = 4
NSA_REP = 4
NSA_CMP_BLOCK = 32
NSA_CMP_STRIDE = 16
NSA_SEL_BLOCK = 64
NSA_TOPN = 16
NSA_WINDOW = 512
FORCED_SCORE = 1e9
DIFF_DIM = 64
MLA_Q_LORA = 768
MLA_KV_LORA = 512
MLA_NOPE = 128
MLA_ROPE = 64
ROPE_THETA = 10000.0
CONV_WIDTH = 3
ATT_T = 256
NSA_CMP_T = 512
WIDE_TN = 2048
DIFF_T = 512
MLA_T = 512
ATT_TQ = 512
ATT_TK = 256
LOG2E = 1.4426950408889634
SB_QSCALE = -(HEAD_DIM ** -0.5)
NSA_QSCALE = HEAD_DIM ** -0.5 * LOG2E
DIFF_QSCALE = DIFF_DIM ** -0.5 * LOG2E
MLA_QSCALE = (MLA_NOPE + MLA_ROPE) ** -0.5 * LOG2E
M_INIT = -1e30
MASKED = -2e30
SB_DEAD = -120.0
T5_MASKED = T5_BUCKETS


def _params(*sem):
    return pltpu.CompilerParams(dimension_semantics=sem, vmem_limit_bytes=VMEM_LIMIT)


def _dot(a, b):
    return jnp.dot(a, b, preferred_element_type=F32)


def _dot_nt(a, b):
    return lax.dot_general(a, b, (((1,), (1,)), ((), ())), preferred_element_type=F32)


def _lane_tile(x, reps):
    return jnp.concatenate([x] * reps, axis=1)


def _split_bf16(x):
    hi = x.astype(BF16)
    lo = (x - hi.astype(F32)).astype(BF16)
    return hi, lo


def _rms(x, g):
    ms = jnp.mean(x * x, axis=-1, keepdims=True)
    return x * lax.rsqrt(ms + EPS) * g


def _ada_kernel(c_ref, w_ref, b_ref, o_ref):
    c = c_ref[...]
    a = c * jax.nn.sigmoid(c)
    a_hi, a_lo = _split_bf16(a)
    w_hi, w_lo = _split_bf16(w_ref[0])
    o_ref[0] = _dot(a_hi, w_hi) + _dot(a_lo, w_hi) + _dot(a_hi, w_lo) + b_ref[0]


def _ada_mod(c, ada_w, ada_b, tn=1024):
    depth, d, n = ada_w.shape
    b = c.shape[0]
    return pl.pallas_call(
        _ada_kernel,
        grid=(depth, n // tn),
        in_specs=[pl.BlockSpec((b, d), lambda l, j: (0, 0)),
                  pl.BlockSpec((1, d, tn), lambda l, j: (l, 0, j)),
                  pl.BlockSpec((1, 1, tn), lambda l, j: (l, 0, j))],
        out_specs=pl.BlockSpec((1, b, tn), lambda l, j: (l, 0, j)),
        out_shape=jax.ShapeDtypeStruct((depth, b, n), F32),
        compiler_params=_params("parallel", "parallel"),
        name="ada_mod",
    )(c, ada_w, ada_b.reshape(depth, 1, n))


def _norm_mm_kernel(*refs, modulated, rope, act, normed):
    it = iter(refs)
    x_ref = next(it)
    g_ref = sh_ref = sc_ref = wb_ref = cos_ref = sin_ref = None
    if not normed:
        g_ref = next(it)
        if modulated:
            sh_ref, sc_ref = next(it), next(it)
    w_ref = next(it)
    if rope:
        wb_ref, cos_ref, sin_ref = next(it), next(it), next(it)
    o_ref = next(it)
    if normed:
        h = x_ref[...]
    else:
        h_ref = next(it)

        @pl.when(pl.program_id(1) == 0)
        def _():
            y = _rms(x_ref[...], g_ref[...])
            if modulated:
                y = y * (1.0 + sc_ref[0]) + sh_ref[0]
            h_ref[...] = y.astype(BF16)

        h = h_ref[...]
    acc = _dot(h, w_ref[...])
    if rope:
        acc = acc * cos_ref[...] + _dot(h, wb_ref[...]) * sin_ref[...]
    if act == "sigmoid":
        acc = jax.nn.sigmoid(acc)
    o_ref[...] = acc.astype(o_ref.dtype)


def _norm_mm(x, g, w, *, seq_len, shift=None, scale=None, wb=None, cos=None, sin=None, act=None, normed=False,
             emit_h=False, out_dtype=BF16, x_cols=None, x_col_block=0, tm=1024, tn=1024, name="norm_mm"):
    m = x.shape[0]
    k = x.shape[1] if x_cols is None else x_cols
    n = w.shape[1]
    while n % tn:
        tn //= 2
    modulated, rope = shift is not None, wb is not None
    tps = seq_len // tm
    in_specs = [pl.BlockSpec((tm, k), lambda i, j: (i, x_col_block))]
    args = [x]
    if not normed:
        in_specs.append(pl.BlockSpec((1, k), lambda i, j: (0, 0)))
        args.append(g.reshape(1, k))
        if modulated:
            in_specs += [pl.BlockSpec((1, 1, k), lambda i, j: (i // tps, 0, 0))] * 2
            args += [shift, scale]
    in_specs.append(pl.BlockSpec((k, tn), lambda i, j: (0, j)))
    args.append(w)
    if rope:
        in_specs.append(pl.BlockSpec((k, tn), lambda i, j: (0, j)))
        in_specs += [pl.BlockSpec((tm, tn), lambda i, j: (i % tps, j))] * 2
        args += [wb, cos, sin]
    out_specs = [pl.BlockSpec((tm, tn), lambda i, j: (i, j))]
    out_shape = [jax.ShapeDtypeStruct((m, n), out_dtype)]
    scratch = []
    if emit_h:
        out_specs.append(pl.BlockSpec((tm, k), lambda i, j: (i, 0)))
        out_shape.append(jax.ShapeDtypeStruct((m, k), BF16))
    elif not normed:
        scratch.append(pltpu.VMEM((tm, k), BF16))
    out = pl.pallas_call(
        functools.partial(_norm_mm_kernel, modulated=modulated, rope=rope, act=act, normed=normed),
        grid=(m // tm, n // tn),
        in_specs=in_specs,
        out_specs=out_specs,
        out_shape=out_shape,
        scratch_shapes=scratch,
        compiler_params=_params("parallel", "arbitrary"),
        name=name,
    )(*args)
    return out if emit_h else out[0]


def _mm_res_kernel(a_ref, w_ref, x_ref, gate_ref, o_ref):
    o_ref[...] = x_ref[...] + gate_ref[0] * _dot(a_ref[...], w_ref[...])


def _mm_residual(a, w, x, gate, *, seq_len, tm=1024, tn=1024, name="mm_residual"):
    m, k = a.shape
    n = w.shape[1]
    tps = seq_len // tm
    return pl.pallas_call(
        _mm_res_kernel,
        grid=(m // tm, n // tn),
        in_specs=[pl.BlockSpec((tm, k), lambda i, j: (i, 0)),
                  pl.BlockSpec((k, tn), lambda i, j: (0, j)),
                  pl.BlockSpec((tm, tn), lambda i, j: (i, j)),
                  pl.BlockSpec((1, 1, tn), lambda i, j: (i // tps, 0, j))],
        out_specs=pl.BlockSpec((tm, tn), lambda i, j: (i, j)),
        out_shape=jax.ShapeDtypeStruct((m, n), F32),
        compiler_params=_params("parallel", "arbitrary"),
        name=name,
    )(a, w, x, gate)


def _ffn_kernel(x_ref, xp_ref, g_ref, sh_ref, sc_ref, gate_ref, fg_ref, wup_hbm, cw_hbm, cb_hbm, wd_hbm, o_ref,
                h_ref, acc_ref, ug_scr, uu_scr, *, tiles_per_seq, final_norm, nf, tf):
    i = pl.program_id(0)
    d = x_ref.shape[1]

    def nm(x):
        return _rms(x, g_ref[...]) * (1.0 + sc_ref[0]) + sh_ref[0]
    h_ref[HALO:, :] = nm(x_ref[...]).astype(BF16)
    keep = jnp.where(i % tiles_per_seq == 0, 0.0, 1.0)
    h_ref[:HALO, :] = (nm(xp_ref[...]) * keep).astype(BF16)
    acc_ref[...] = jnp.zeros_like(acc_ref)

    def hidden_tile(wg_ref, wu_ref, cwg_ref, cwu_ref, cbg_ref, cbu_ref, wd_ref):
        h = h_ref[...]

        def branch(w_ref, cw_ref, cb_ref, u_scr):
            u_scr[...] = _dot(h, w_ref[...])
            tm = u_scr.shape[0] - HALO
            cw = cw_ref[...]
            return (cw[0:1] * u_scr[pl.ds(HALO - 2, tm), :] + cw[1:2] * u_scr[pl.ds(HALO - 1, tm), :]
                    + cw[2:3] * u_scr[pl.ds(HALO, tm), :] + cb_ref[...])

        gt = branch(wg_ref, cwg_ref, cbg_ref, ug_scr)
        up = branch(wu_ref, cwu_ref, cbu_ref, uu_scr)
        a = gt * jax.nn.sigmoid(gt) * up
        acc_ref[...] += _dot(a.astype(BF16), wd_ref[...])

    lo = lambda j: (0, j)
    hi = lambda j: (0, j + nf)
    pltpu.emit_pipeline(
        hidden_tile, grid=(nf,),
        in_specs=[pl.BlockSpec((d, tf), lo), pl.BlockSpec((d, tf), hi),
                  pl.BlockSpec((CONV_WIDTH, tf), lo), pl.BlockSpec((CONV_WIDTH, tf), hi),
                  pl.BlockSpec((1, tf), lo), pl.BlockSpec((1, tf), hi),
                  pl.BlockSpec((tf, d), lambda j: (j, 0))],
    )(wup_hbm, wup_hbm, cw_hbm, cw_hbm, cb_hbm, cb_hbm, wd_hbm)

    y = x_ref[...] + gate_ref[0] * acc_ref[...]
    if final_norm:
        y = _rms(y, fg_ref[...])
    o_ref[...] = y


def _conv_ffn(x, g, shift, scale, w_up, conv_w, conv_b, w_down, gate, final_g, *, seq_len, final_norm,
              tm=512, tf=512):
    m, d = x.shape
    f = w_down.shape[0]
    nf = f // tf
    tps = seq_len // tm
    hb = tm // HALO
    row = lambda i: (i, 0)
    per_batch = lambda i: (i // tps, 0, 0)
    hbm = pl.BlockSpec(memory_space=pl.ANY)
    return pl.pallas_call(
        functools.partial(_ffn_kernel, tiles_per_seq=tps, final_norm=final_norm, nf=nf, tf=tf),
        grid=(m // tm,),
        in_specs=[pl.BlockSpec((tm, d), row),
                  pl.BlockSpec((HALO, d), lambda i: (jnp.maximum(i * hb - 1, 0), 0)),
                  pl.BlockSpec((1, d), lambda i: (0, 0)),
                  pl.BlockSpec((1, 1, d), per_batch),
                  pl.BlockSpec((1, 1, d), per_batch),
                  pl.BlockSpec((1, 1, d), per_batch),
                  pl.BlockSpec((1, d), lambda i: (0, 0)),
                  hbm, hbm, hbm, hbm],
        out_specs=pl.BlockSpec((tm, d), row),
        out_shape=jax.ShapeDtypeStruct((m, d), F32),
        scratch_shapes=[pltpu.VMEM((HALO + tm, d), BF16), pltpu.VMEM((tm, d), F32),
                        pltpu.VMEM((HALO + tm, tf), F32), pltpu.VMEM((HALO + tm, tf), F32)],
        compiler_params=_params("arbitrary"),
        name="conv_ffn",
    )(x, x, g.reshape(1, d), shift, scale, gate, final_g.reshape(1, d), w_up, conv_w, conv_b.reshape(1, -1), w_down)


def _t5_bucket_np(dist):
    n = np.maximum(dist, 0)
    max_exact = T5_BUCKETS // 2
    nf = np.maximum(n, 1).astype(np.float64)
    val = np.log(nf / max_exact) / math.log(T5_MAX_DIST / max_exact) * (T5_BUCKETS - max_exact)
    large = max_exact + np.trunc(val + 1e-6).astype(np.int64)
    return np.where(n < max_exact, n, np.minimum(large, T5_BUCKETS - 1)).astype(np.int32)


def _t5_gather_kernel(t5_ref, bkt_ref, o_ref, *, mult):
    h = pl.program_id(0)
    bk = bkt_ref[...]
    acc = jnp.zeros(bk.shape, F32)
    for b in range(T5_BUCKETS):
        acc = jnp.where(bk == b, t5_ref[b, h], acc)
    o_ref[0] = jnp.where(bk == T5_MASKED, MASKED, acc * mult)


def _t5_gather(t5_bias, bkt, tr, mult=1.0):
    rows, cols = bkt.shape
    heads = t5_bias.shape[1]
    return pl.pallas_call(
        functools.partial(_t5_gather_kernel, mult=mult),
        grid=(heads, rows // tr),
        in_specs=[pl.BlockSpec(memory_space=pltpu.SMEM),
                  pl.BlockSpec((tr, cols), lambda h, r: (r, 0))],
        out_specs=pl.BlockSpec((1, tr, cols), lambda h, r: (h, r, 0)),
        out_shape=jax.ShapeDtypeStruct((heads, rows, cols), F32),
        compiler_params=_params("parallel", "parallel"),
        name="t5_gather",
    )(t5_bias, jnp.asarray(bkt))


TAB_DIAG, TAB_SUB, TAB_FAR, TAB_EDGE = 0, 1, 2, 3


def _attention_tables(t5_bias, t):
    i = np.arange(t)[:, None]
    j = np.arange(t)[None, :]
    assert int(_t5_bucket_np(np.array(t + 1))) == T5_BUCKETS - 1
    far = np.full((t, t), T5_BUCKETS - 1, np.int32)
    bkt = np.concatenate([np.where(j <= i, _t5_bucket_np(i - j), T5_MASKED), _t5_bucket_np(t + i - j), far,
                          np.where(j > i, far, T5_MASKED)], axis=0).astype(np.int32)
    return _t5_gather(t5_bias, bkt, tr=t, mult=LOG2E)


def _tile_iota(t):
    return lax.broadcasted_iota(jnp.int32, (t, t), 0), lax.broadcasted_iota(jnp.int32, (t, t), 1)


def _rect_iota(i, tq, start, tk):
    rowg = i * tq + lax.broadcasted_iota(jnp.int32, (tq, tk), 0)
    colg = start + lax.broadcasted_iota(jnp.int32, (tq, tk), 1)
    return rowg, colg


def _osm_reset(m_scr, l_scr, acc_scr):
    m_scr[...] = jnp.full(m_scr.shape, M_INIT, F32)
    l_scr[...] = jnp.zeros(l_scr.shape, F32)
    acc_scr[...] = jnp.zeros(acc_scr.shape, F32)


def _osm_update(t2, rows, slot, p_scr, m_scr, l_scr, acc_scr):
    m_prev = m_scr[rows, :]
    m_new = jnp.maximum(m_prev, jnp.max(t2, axis=-1, keepdims=True))
    p = jnp.exp2(t2 - _lane_tile(m_new, t2.shape[1] // LANE))
    alpha = jnp.exp2(m_prev - m_new)
    l_scr[rows, :] = alpha * l_scr[rows, :]
    m_scr[rows, :] = m_new
    p_scr[slot, rows, :] = p.astype(BF16)
    acc_scr[rows, :] = alpha * acc_scr[rows, :]


def _osm_accum(v, slot, p_scr, l_scr, acc_scr):
    vext = jnp.concatenate([v, jnp.ones(v.shape, v.dtype)], axis=-1)
    half = p_scr.shape[1] // 2
    for part in range(2):
        rows = pl.ds(part * half, half)
        pv = _dot(p_scr[slot, rows, :], vext)
        acc_scr[rows, :] += pv[:, :HEAD_DIM]
        l_scr[rows, :] += pv[:, HEAD_DIM:]


def _pipelined_tiles(n_tiles, scores, update, update_last=None):
    update_last = update if update_last is None else update_last
    last = n_tiles - 1
    n_pairs = last // 2
    scores(0, 0)

    def pair(n, cr):
        scores(2 * n + 1, 1)
        update(2 * n, 0)
        scores(2 * n + 2, 0)
        update(2 * n + 1, 1)
        return cr

    lax.fori_loop(0, n_pairs, pair, 0)

    @pl.when(last % 2 == 0)
    def _():
        update_last(last, 0)

    @pl.when(last % 2 == 1)
    def _():
        scores(last, 1)
        update(last - 1, 0)
        update_last(last, 1)


def _att_scratch(rows, tk, slots=2):
    return [pltpu.VMEM((slots, rows, tk), F32), pltpu.VMEM((slots, rows, tk), BF16), pltpu.VMEM((rows, LANE), F32),
            pltpu.VMEM((rows, LANE), F32), pltpu.VMEM((rows, HEAD_DIM), F32)]


def _sb_kernel(q_ref, k_ref, v_ref, o_ref, s_scr, c_scr, acc_scr, *, tq, tk):
    i = pl.program_id(2)
    q = q_ref[...]
    row, col = _tile_iota(tk)
    upper = jnp.where(row > col, 1.0, 0.0).astype(BF16)
    c_scr[...] = jnp.zeros(c_scr.shape, F32)
    acc_scr[...] = jnp.zeros(acc_scr.shape, F32)
    reps = tk // LANE

    def scores(kb, slot):
        start = pl.multiple_of(kb * tk, tk)
        s_scr[slot] = _dot_nt(q, k_ref[pl.ds(start, tk), :])

    def update(kb, slot, diag):
        start = pl.multiple_of(kb * tk, tk)
        nz = s_scr[slot]
        e = jnp.exp2(jnp.abs(nz) * (-LOG2E))
        lk = jnp.minimum(nz, 0.0) - jnp.log(1.0 + e)
        if diag:
            rowg, colg = _rect_iota(i, tq, start, tk)
            past = colg < rowg
            lk = jnp.where(past, lk, 0.0)
        hi, lo = _split_bf16(lk)
        c = c_scr[...]
        later = _dot(hi, upper) + _dot(lo, upper) + _lane_tile(c, reps)
        a = jnp.exp(lk - nz + later)
        if diag:
            a = jnp.where(past, a, 0.0)
        acc_scr[...] += _dot(a.astype(BF16), v_ref[pl.ds(start, tk), :])
        c_scr[...] = c + jnp.sum(lk, axis=-1, keepdims=True)

    assert tq == 2 * tk
    scores(2 * i + 1, 1)
    scores(2 * i, 0)
    update(2 * i + 1, 1, True)
    scores(jnp.maximum(2 * i - 1, 0), 1)
    update(2 * i, 0, True)

    def more(carry):
        n, c_max = carry
        return (n < i) & (c_max > SB_DEAD)

    def pair(carry):
        n, _ = carry
        kb = 2 * (i - n) - 1
        scores(kb - 1, 0)
        update(kb, 1, False)
        c_mid = jnp.max(c_scr[...])

        @pl.when(c_mid > SB_DEAD)
        def _():
            scores(jnp.maximum(kb - 2, 0), 1)
            update(kb - 1, 0, False)

        return n + 1, jnp.where(c_mid > SB_DEAD, jnp.max(c_scr[...]), c_mid)

    lax.while_loop(more, pair, (0, jnp.max(c_scr[...])))
    o_ref[...] = acc_scr[...].astype(o_ref.dtype)


def _sb_attention(qkv, *, batch, seq_len, tq=ATT_TQ, tk=ATT_TK):
    h = N_HEADS
    nq = seq_len // tq
    return pl.pallas_call(
        functools.partial(_sb_kernel, tq=tq, tk=tk),
        grid=(batch, h, nq),
        in_specs=[pl.BlockSpec((tq, HEAD_DIM), lambda b, hh, i: (b * nq + i, hh)),
                  pl.BlockSpec((seq_len, HEAD_DIM), lambda b, hh, i: (b, h + hh)),
                  pl.BlockSpec((seq_len, HEAD_DIM), lambda b, hh, i: (b, 2 * h + hh))],
        out_specs=pl.BlockSpec((tq, HEAD_DIM), lambda b, hh, i: (b * nq + i, hh)),
        out_shape=jax.ShapeDtypeStruct((batch * seq_len, h * HEAD_DIM), BF16),
        scratch_shapes=[pltpu.VMEM((2, tq, tk), F32), pltpu.VMEM((tq, LANE), F32), pltpu.VMEM((tq, HEAD_DIM), F32)],
        compiler_params=_params("parallel", "parallel", "arbitrary"),
        name="sb_attention",
    )(qkv, qkv, qkv)


def _table_offset(kb, i, t):
    return pl.multiple_of(jnp.where(kb == i, TAB_DIAG * t, jnp.where(kb == i - 1, TAB_SUB * t, TAB_FAR * t)), t)


def _diff_kernel(q_ref, k_ref, v_ref, tab_ref, lam_ref, hg_ref, o_ref, s_scr, p_scr, m_scr, l_scr, acc_scr, *, t,
                 lambda_init):
    i = pl.program_id(2)
    lam = lam_ref[...]
    lmbda = (jnp.exp(jnp.sum(lam[0:1] * lam[1:2], axis=-1, keepdims=True))
             - jnp.exp(jnp.sum(lam[2:3] * lam[3:4], axis=-1, keepdims=True)) + lambda_init)
    for a in range(2):
        lanes = slice(a * HEAD_DIM, (a + 1) * HEAD_DIM)
        q = q_ref[:, lanes]
        lane = lax.broadcasted_iota(jnp.int32, q.shape, 1)
        zero = jnp.zeros_like(q)
        q2 = jnp.concatenate([jnp.where(lane < DIFF_DIM, q, zero), jnp.where(lane >= DIFF_DIM, q, zero)], axis=0)
        _osm_reset(m_scr, l_scr, acc_scr)

        def scores(kb, slot):
            s_scr[slot] = _dot_nt(q2, k_ref[pl.ds(pl.multiple_of(kb * t, t), t), lanes])

        def update(kb, slot):
            off = _table_offset(kb, i, t)
            for half in range(2):
                rows = pl.ds(half * t, t)
                t2 = s_scr[slot, rows, :] + tab_ref[a, pl.ds(off, t), :]
                _osm_update(t2, rows, slot, p_scr, m_scr, l_scr, acc_scr)
            _osm_accum(v_ref[pl.ds(pl.multiple_of(kb * t, t), t), lanes], slot, p_scr, l_scr, acc_scr)

        _pipelined_tiles(i + 1, scores, update)

        o = acc_scr[...] / l_scr[...]
        o = o[:t] - lmbda * o[t:]
        o_ref[:, lanes] = (_rms(o, hg_ref[...]) * (1.0 - lambda_init)).astype(o_ref.dtype)


def _diff_attention(qkv, tables, lam, head_g, *, batch, seq_len, lambda_init, t=DIFF_T):
    h = N_HEADS
    nq = seq_len // t
    pw = 2 * HEAD_DIM
    pairs = h // 2
    return pl.pallas_call(
        functools.partial(_diff_kernel, t=t, lambda_init=lambda_init),
        grid=(batch, pairs, nq),
        in_specs=[pl.BlockSpec((t, pw), lambda b, pp, i: (b * nq + i, pp)),
                  pl.BlockSpec((seq_len, pw), lambda b, pp, i: (b, pairs + pp)),
                  pl.BlockSpec((seq_len, pw), lambda b, pp, i: (b, 2 * pairs + pp)),
                  pl.BlockSpec((2, 4 * t, t), lambda b, pp, i: (pp, 0, 0)),
                  pl.BlockSpec((4, DIFF_DIM), lambda b, pp, i: (0, 0)),
                  pl.BlockSpec((1, HEAD_DIM), lambda b, pp, i: (0, 0))],
        out_specs=pl.BlockSpec((t, pw), lambda b, pp, i: (b * nq + i, pp)),
        out_shape=jax.ShapeDtypeStruct((batch * seq_len, h * HEAD_DIM), BF16),
        scratch_shapes=_att_scratch(2 * t, t),
        compiler_params=_params("parallel", "parallel", "arbitrary"),
        name="diff_attention",
    )(qkv, qkv, qkv, tables, lam, head_g.reshape(1, HEAD_DIM))


def _mla_kernel(qn_ref, qr_ref, kv_ref, kr_ref, o_ref, s_scr, p_scr, m_scr, l_scr, acc_scr, *, tq, tk):
    i = pl.program_id(2)
    assert tq == tk
    for a in range(2):
        lanes = slice(a * HEAD_DIM, (a + 1) * HEAD_DIM)
        kn_lanes = slice(2 * a * HEAD_DIM, (2 * a + 1) * HEAD_DIM)
        v_lanes = slice((2 * a + 1) * HEAD_DIM, (2 * a + 2) * HEAD_DIM)
        q = jnp.concatenate([qn_ref[:, lanes], qr_ref[...]], axis=-1)
        _osm_reset(m_scr, l_scr, acc_scr)

        def scores(kb, slot):
            rows = pl.ds(pl.multiple_of(kb * tk, tk), tk)
            k = jnp.concatenate([kv_ref[rows, kn_lanes], kr_ref[rows, lanes]], axis=-1)
            s_scr[slot] = _dot_nt(q, k)

        def update(kb, slot, mask):
            start = pl.multiple_of(kb * tk, tk)
            t2 = s_scr[slot]
            if mask:
                rowg, colg = _rect_iota(i, tq, start, tk)
                t2 = jnp.where(colg <= rowg, t2, MASKED)
            _osm_update(t2, pl.ds(0, tq), slot, p_scr, m_scr, l_scr, acc_scr)
            _osm_accum(kv_ref[pl.ds(start, tk), v_lanes], slot, p_scr, l_scr, acc_scr)

        _pipelined_tiles(i + 1, scores, lambda kb, slot: update(kb, slot, False),
                         lambda kb, slot: update(kb, slot, True))
        o_ref[:, lanes] = (acc_scr[...] / l_scr[...]).astype(o_ref.dtype)


def _mla_attention(qn, qr, kv, kr, *, batch, seq_len, tq=MLA_T, tk=MLA_T):
    h = N_HEADS
    nq = seq_len // tq
    pair = pl.BlockSpec((tq, 2 * HEAD_DIM), lambda b, pp, i: (b * nq + i, pp))
    return pl.pallas_call(
        functools.partial(_mla_kernel, tq=tq, tk=tk),
        grid=(batch, h // 2, nq),
        in_specs=[pair,
                  pl.BlockSpec((tq, LANE), lambda b, pp, i: (b * nq + i, pp)),
                  pl.BlockSpec((seq_len, 4 * HEAD_DIM), lambda b, pp, i: (b, pp)),
                  pl.BlockSpec((seq_len, 2 * LANE), lambda b, pp, i: (b, 0))],
        out_specs=pair,
        out_shape=jax.ShapeDtypeStruct((batch * seq_len, h * HEAD_DIM), BF16),
        scratch_shapes=_att_scratch(tq, tk),
        compiler_params=_params("parallel", "parallel", "arbitrary"),
        name="mla_attention",
    )(qn, qr, kv, kr)


def _compress_kernel(raw_ref, pe_ref, w1_ref, w2_ref, o_ref, *, n_slots):
    half = NSA_CMP_BLOCK // 2
    p1 = jnp.zeros((n_slots, HEAD_DIM), F32)
    p2 = jnp.zeros((n_slots, HEAD_DIM), F32)
    for l in range(half):
        a = raw_ref[pl.ds(l, n_slots, stride=NSA_CMP_STRIDE), :]
        p1 = p1 + _dot((a + pe_ref[0, l:l + 1, :]).astype(BF16), w1_ref[0, l])
        p2 = p2 + _dot((a + pe_ref[0, half + l:half + l + 1, :]).astype(BF16), w1_ref[0, half + l])
    pre = p1 + pltpu.roll(p2, n_slots - 1, 0)
    hid = pre * jax.nn.sigmoid(pre)
    o_ref[0, 0, 0] = _dot(hid.astype(BF16), w2_ref[0]).astype(o_ref.dtype)


def _nsa_compress(raw, pe, w1, w2, *, batch, seq_len):
    g = NSA_GROUPS
    n_slots = seq_len // NSA_CMP_STRIDE
    return pl.pallas_call(
        functools.partial(_compress_kernel, n_slots=n_slots),
        grid=(batch, 2, g),
        in_specs=[pl.BlockSpec((seq_len, HEAD_DIM), lambda b, kv, gg: (b, kv * g + gg)),
                  pl.BlockSpec((1, NSA_CMP_BLOCK, HEAD_DIM), lambda b, kv, gg: (kv, 0, 0)),
                  pl.BlockSpec((1, NSA_CMP_BLOCK, HEAD_DIM, HEAD_DIM), lambda b, kv, gg: (kv, 0, 0, 0)),
                  pl.BlockSpec((1, HEAD_DIM, HEAD_DIM), lambda b, kv, gg: (kv, 0, 0))],
        out_specs=pl.BlockSpec((1, 1, 1, n_slots, HEAD_DIM), lambda b, kv, gg: (b, kv, gg, 0, 0)),
        out_shape=jax.ShapeDtypeStruct((batch, 2, g, n_slots, HEAD_DIM), BF16),
        compiler_params=_params("parallel", "parallel", "parallel"),
        name="nsa_compress",
    )(raw, pe, w1, w2)


def _nsa_cmp_kernel(q_ref, kc_ref, vc_ref, bias_ref, gates_ref, ovt_ref, oc_ref, sel_ref, *, t, n_slots, n_sel,
                    n_top):
    i = pl.program_id(2)
    kc = kc_ref[0, 0, 0]
    vc = vc_ref[0, 0, 0]
    qpos = i * t + lax.broadcasted_iota(jnp.int32, (t, n_slots), 0)
    cmp_end = NSA_CMP_STRIDE * lax.broadcasted_iota(jnp.int32, (t, n_slots), 1) + (NSA_CMP_BLOCK - 1)
    valid = cmp_end <= qpos
    gates = gates_ref[...]
    psum = jnp.zeros((t, n_slots), F32)
    for r in range(NSA_REP):
        q = q_ref[:, r * HEAD_DIM:(r + 1) * HEAD_DIM]
        s = jnp.where(valid, _dot_nt(q, kc) + bias_ref[r], NEG)
        m = jnp.max(s, axis=-1, keepdims=True)
        p = jnp.where(valid, jnp.exp2(s - m), 0.0)
        p = p / jnp.maximum(jnp.sum(p, axis=-1, keepdims=True), 1e-30)
        psum = psum + p
        oc_ref[:, r * HEAD_DIM:(r + 1) * HEAD_DIM] = gates[:, r:r + 1] * _dot(p.astype(BF16), vc)

    hi, lo = _split_bf16(psum)
    ovt = ovt_ref[...]
    imp = _dot_nt(ovt, hi) + _dot_nt(ovt, lo)
    blk = lax.broadcasted_iota(jnp.int32, (n_sel, t), 0)
    tpos = i * t + lax.broadcasted_iota(jnp.int32, (n_sel, t), 1)
    cur = tpos // NSA_SEL_BLOCK
    forced = (blk == 0) | (blk == cur) | (blk == cur - 1)
    score = jnp.where(blk * NSA_SEL_BLOCK <= tpos, jnp.where(forced, FORCED_SCORE, imp), -1.0)
    rank = jnp.zeros((n_sel, t), F32)
    for mm in range(n_sel):
        sm = score[mm:mm + 1, :]
        ahead = (sm > score) | ((sm == score) & (blk > mm))
        rank = rank + jnp.where(ahead, 1.0, 0.0)
    sel_t = jnp.where(rank < n_top, 1.0, 0.0).astype(BF16)
    row, col = _tile_iota(t)
    eye = jnp.where(row == col, 1.0, 0.0).astype(BF16)
    sel_ref[0, 0] = _dot_nt(eye, sel_t).astype(sel_ref.dtype)


def _nsa_cmp_attention(q_all, kvc, bias_c, gates, *, batch, seq_len, t=NSA_CMP_T):
    g = NSA_GROUPS
    nq = seq_len // t
    n_slots = seq_len // NSA_CMP_STRIDE
    n_sel = seq_len // NSA_SEL_BLOCK
    c0 = NSA_CMP_STRIDE * np.arange(n_slots)[:, None]
    s0 = NSA_SEL_BLOCK * np.arange(n_sel)[None, :]
    overlap = (c0 < s0 + NSA_SEL_BLOCK) & (c0 + NSA_CMP_BLOCK > s0)
    ovt = jnp.asarray(overlap.T.astype(np.float32), dtype=BF16)
    gw = NSA_REP * HEAD_DIM
    return pl.pallas_call(
        functools.partial(_nsa_cmp_kernel, t=t, n_slots=n_slots, n_sel=n_sel, n_top=min(NSA_TOPN, n_sel)),
        grid=(batch, g, nq),
        in_specs=[pl.BlockSpec((t, gw), lambda b, gg, i: (b * nq + i, gg)),
                  pl.BlockSpec((1, 1, 1, n_slots, HEAD_DIM), lambda b, gg, i: (b, 0, gg, 0, 0)),
                  pl.BlockSpec((1, 1, 1, n_slots, HEAD_DIM), lambda b, gg, i: (b, 1, gg, 0, 0)),
                  pl.BlockSpec((NSA_REP, t, n_slots), lambda b, gg, i: (gg, i, 0)),
                  pl.BlockSpec((t, LANE), lambda b, gg, i: (b * nq + i, gg)),
                  pl.BlockSpec((n_sel, n_slots), lambda b, gg, i: (0, 0))],
        out_specs=[pl.BlockSpec((t, gw), lambda b, gg, i: (b * nq + i, gg)),
                   pl.BlockSpec((1, 1, t, n_sel), lambda b, gg, i: (b, gg, i, 0))],
        out_shape=[jax.ShapeDtypeStruct((batch * seq_len, g * gw), F32),
                   jax.ShapeDtypeStruct((batch, g, seq_len, n_sel), BF16)],
        compiler_params=_params("parallel", "parallel", "arbitrary"),
        name="nsa_cmp_attention",
    )(q_all, kvc, kvc, bias_c, gates, ovt)


def _nsa_main_kernel(q_ref, ks_ref, vs_ref, kw_ref, vw_ref, sel_ref, tab_ref, gates_ref, oc_ref, o_ref,
                     s_scr, p_scr, m_scr, l_scr, acc_scr, os_scr, mk_scr, *, t):
    i = pl.program_id(2)
    rep = NSA_REP
    q = jnp.concatenate([q_ref[:, r * HEAD_DIM:(r + 1) * HEAD_DIM] for r in range(rep)], axis=0)

    sel = sel_ref[0, 0]
    n_sel = sel.shape[1]
    blk_row = lax.broadcasted_iota(jnp.int32, (n_sel, t), 0)
    key_col = lax.broadcasted_iota(jnp.int32, (n_sel, t), 1)
    _osm_reset(m_scr, l_scr, acc_scr)

    def sel_scores(kb, slot):
        s_scr[slot] = _dot_nt(q, ks_ref[pl.ds(pl.multiple_of(kb * t, t), t), :])

    def sel_update(kb, slot):
        start = pl.multiple_of(kb * t, t)
        expand = jnp.where((start + key_col) // NSA_SEL_BLOCK == blk_row, 1.0, 0.0).astype(BF16)
        mk_scr[...] = _dot(sel, expand)
        off = _table_offset(kb, i, t)
        for r in range(rep):
            rows = pl.ds(r * t, t)
            t2 = s_scr[slot, rows, :] + tab_ref[r, pl.ds(off, t), :]
            t2 = jnp.where(mk_scr[...] > 0.5, t2, MASKED)
            _osm_update(t2, rows, slot, p_scr, m_scr, l_scr, acc_scr)
        _osm_accum(vs_ref[pl.ds(start, t), :], slot, p_scr, l_scr, acc_scr)

    _pipelined_tiles(i + 1, sel_scores, sel_update)
    os_scr[...] = acc_scr[...] / l_scr[...]

    n_back = NSA_WINDOW // t
    assert n_back == 2
    _osm_reset(m_scr, l_scr, acc_scr)

    def win_scores(kb, slot):
        s_scr[slot] = _dot_nt(q, kw_ref[pl.ds(pl.multiple_of(kb * t, t), t), :])

    def win_update(kb, slot, region):
        for r in range(rep):
            rows = pl.ds(r * t, t)
            t2 = s_scr[slot, rows, :] + tab_ref[r, pl.ds(region * t, t), :]
            _osm_update(t2, rows, slot, p_scr, m_scr, l_scr, acc_scr)
        _osm_accum(vw_ref[pl.ds(pl.multiple_of(kb * t, t), t), :], slot, p_scr, l_scr, acc_scr)

    def window(n_tiles):
        regions = (TAB_DIAG, TAB_SUB, TAB_EDGE)
        for back in range(n_tiles):
            win_scores(i - back, back)
        for back in reversed(range(n_tiles)):
            win_update(i - back, back, regions[back])

    pl.when(i >= 2)(lambda: window(3))
    pl.when(i == 1)(lambda: window(2))
    pl.when(i == 0)(lambda: window(1))
    o_w = acc_scr[...] / l_scr[...]
    o_s = os_scr[...]

    gates = gates_ref[...]
    for r in range(rep):
        rows = slice(r * t, (r + 1) * t)
        cols = slice(r * HEAD_DIM, (r + 1) * HEAD_DIM)
        o = oc_ref[:, cols] + gates[:, rep + r:rep + r + 1] * o_s[rows] + gates[:, 2 * rep + r:2 * rep + r + 1] * o_w[rows]
        o_ref[:, cols] = o.astype(o_ref.dtype)


def _nsa_main_attention(qkv, sel, tables, gates, oc, *, batch, seq_len, t=ATT_T):
    g = NSA_GROUPS
    nq = seq_len // t
    n_sel = seq_len // NSA_SEL_BLOCK
    gw = NSA_REP * HEAD_DIM
    qb = N_HEADS
    kv = lambda which: pl.BlockSpec((seq_len, HEAD_DIM), lambda b, gg, i: (b, qb + which * g + gg))
    tile = pl.BlockSpec((t, gw), lambda b, gg, i: (b * nq + i, gg))
    rows = NSA_REP * t
    return pl.pallas_call(
        functools.partial(_nsa_main_kernel, t=t),
        grid=(batch, g, nq),
        in_specs=[tile, kv(0), kv(1), kv(2), kv(3),
                  pl.BlockSpec((1, 1, t, n_sel), lambda b, gg, i: (b, gg, i, 0)),
                  pl.BlockSpec((NSA_REP, 4 * t, t), lambda b, gg, i: (gg, 0, 0)),
                  pl.BlockSpec((t, LANE), lambda b, gg, i: (b * nq + i, gg)),
                  tile],
        out_specs=tile,
        out_shape=jax.ShapeDtypeStruct((batch * seq_len, g * gw), BF16),
        scratch_shapes=_att_scratch(rows, t, slots=3) + [pltpu.VMEM((rows, HEAD_DIM), F32), pltpu.VMEM((t, t), F32)],
        compiler_params=_params("parallel", "parallel", "arbitrary"),
        name="nsa_main_attention",
    )(qkv, qkv, qkv, qkv, qkv, sel, tables, gates, oc)


def _rope_tables(seq_len, width):
    half = MLA_ROPE // 2
    inv = np.power(ROPE_THETA, -np.arange(half, dtype=np.float32) / half).astype(np.float32)
    ang = np.arange(seq_len, dtype=np.float32)[:, None] * inv[None, :]
    reps = width // half
    return jnp.asarray(np.tile(np.cos(ang), (1, reps))), jnp.asarray(np.tile(np.sin(ang), (1, reps)))


def _rope_weights(w):
    k, n, _ = w.shape
    half = MLA_ROPE // 2
    wb = jnp.concatenate([-w[..., half:], w[..., :half]], axis=-1)
    return w.reshape(k, n * MLA_ROPE).astype(BF16), wb.reshape(k, n * MLA_ROPE).astype(BF16)


def kernel(x, c, t5_bias, ada_w, ada_b, norm_g, final_g, ffn_w_up, ffn_conv_w, ffn_conv_b, ffn_w_down, sb_w_in, sb_w_out, nsa_w_in, nsa_cmp_pe, nsa_cmp_w1, nsa_cmp_w2, nsa_w_out, diff_w_in, diff_lambda, diff_head_g, diff_w_out, mla_w_in, mla_q_g, mla_w_qb, mla_kv_g, mla_w_kvb, mla_w_out):
    batch, seq_len, d = x.shape
    depth = ada_w.shape[0]
    h, dh, g = N_HEADS, HEAD_DIM, NSA_GROUPS
    sizes = dict(batch=batch, seq_len=seq_len)

    mod = _ada_mod(c, ada_w, ada_b)
    tables = _attention_tables(t5_bias, ATT_T)

    xf = x.reshape(batch * seq_len, d)
    for i in range(depth):
        mixer, j = i % 4, i // 4
        sh1, sc1, gt1, sh2, sc2, gt2 = (mod[i, :, n * d:(n + 1) * d].reshape(batch, 1, d) for n in range(6))
        nm = functools.partial(_norm_mm, xf, norm_g[i, 0], seq_len=seq_len, shift=sh1, scale=sc1)
        if mixer == 0:
            w_in = sb_w_in[j]
            w_in = jnp.concatenate([w_in[:, :h * dh] * SB_QSCALE, w_in[:, h * dh:]], axis=1)
            qkv = nm(w_in.astype(BF16), tn=WIDE_TN, name="sb_in")
            o = _sb_attention(qkv, **sizes)
            w_out = sb_w_out[j]
        elif mixer == 1:
            w_in = nsa_w_in[j]
            n_q, n_kv = h * dh, g * dh
            w_att = jnp.concatenate([w_in[:, :n_q] * NSA_QSCALE, w_in[:, n_q + 2 * n_kv:n_q + 6 * n_kv]], axis=1)
            w_cmp = w_in[:, n_q:n_q + 2 * n_kv]
            w_g = w_in[:, n_q + 6 * n_kv:].reshape(d, 3, g, NSA_REP).transpose(0, 2, 1, 3).reshape(d, g, 3 * NSA_REP)
            w_g = jnp.pad(w_g, ((0, 0), (0, 0), (0, LANE - 3 * NSA_REP))).reshape(d, g * LANE)
            qkv, hn = nm(w_att.astype(BF16), emit_h=True, name="nsa_in")
            hmm = functools.partial(_norm_mm, hn, None, seq_len=seq_len, normed=True)
            raw = hmm(w_cmp.astype(BF16), out_dtype=F32, name="nsa_in_cmp")
            gates = hmm(w_g.astype(BF16), out_dtype=F32, act="sigmoid", name="nsa_in_gates")
            kvc = _nsa_compress(raw, nsa_cmp_pe[j], nsa_cmp_w1[j].reshape(2, NSA_CMP_BLOCK, dh, dh).astype(BF16),
                                nsa_cmp_w2[j].astype(BF16), **sizes)
            n_slots = seq_len // NSA_CMP_STRIDE
            dist_c = np.arange(seq_len)[:, None] - (NSA_CMP_STRIDE * np.arange(n_slots)[None, :] + NSA_CMP_BLOCK - 1)
            bias_c = _t5_gather(t5_bias, _t5_bucket_np(dist_c), tr=min(seq_len, 512), mult=LOG2E)
            oc, sel = _nsa_cmp_attention(qkv, kvc, bias_c, gates, **sizes)
            o = _nsa_main_attention(qkv, sel, tables, gates, oc, **sizes)
            w_out = nsa_w_out[j]
        elif mixer == 2:
            lambda_init = 0.8 - 0.6 * math.exp(-0.3 * i)
            w_in = diff_w_in[j]
            w_in = jnp.concatenate([w_in[:, :h * dh] * DIFF_QSCALE, w_in[:, h * dh:]], axis=1)
            qkv = nm(w_in.astype(BF16), tn=WIDE_TN, name="diff_in")
            o = _diff_attention(qkv, _attention_tables(t5_bias, DIFF_T), diff_lambda[j], diff_head_g[j],
                                lambda_init=lambda_init, **sizes)
            w_out = diff_w_out[j]
        else:
            w_in = mla_w_in[j]
            nq_l, nkv_l = MLA_Q_LORA, MLA_KV_LORA
            w_lat = jnp.concatenate([w_in[:, :nq_l], jnp.zeros((d, 2 * nkv_l - nq_l), w_in.dtype),
                                     w_in[:, nq_l:nq_l + nkv_l]], axis=1)
            lat, hn = nm(w_lat.astype(BF16), out_dtype=F32, emit_h=True, name="mla_in")
            w_kr = w_in[:, nq_l + nkv_l:]
            zero = jnp.zeros_like(w_kr)
            cos2, sin2 = _rope_tables(seq_len, 2 * LANE)
            wa, wb = _rope_weights(jnp.stack([w_kr, zero, zero, w_kr], axis=1))
            kr = _norm_mm(hn, None, wa, wb=wb, cos=cos2, sin=sin2, seq_len=seq_len, normed=True, name="mla_in_rope")
            w_qb = (mla_w_qb[j] * MLA_QSCALE).reshape(nq_l, h, MLA_NOPE + MLA_ROPE)
            qn = _norm_mm(lat, mla_q_g[j], w_qb[:, :, :MLA_NOPE].reshape(nq_l, h * MLA_NOPE).astype(BF16),
                          seq_len=seq_len, x_cols=nq_l, x_col_block=0, tn=WIDE_TN, name="mla_q_nope")
            cosh, sinh = _rope_tables(seq_len, h * MLA_ROPE)
            wa, wb = _rope_weights(w_qb[:, :, MLA_NOPE:])
            qr = _norm_mm(lat, mla_q_g[j], wa, wb=wb, cos=cosh, sin=sinh, seq_len=seq_len, x_cols=nq_l,
                          x_col_block=0, name="mla_q_rope")
            kv = _norm_mm(lat, mla_kv_g[j], mla_w_kvb[j].astype(BF16), seq_len=seq_len, x_cols=nkv_l,
                          x_col_block=2, tn=WIDE_TN, name="mla_kv")
            o = _mla_attention(qn, qr, kv, kr, **sizes)
            w_out = mla_w_out[j]
        xf = _mm_residual(o, w_out.astype(BF16), xf, gt1, seq_len=seq_len)
        xf = _conv_ffn(xf, norm_g[i, 1], sh2, sc2, ffn_w_up[i].astype(BF16), ffn_conv_w[i], ffn_conv_b[i],
                       ffn_w_down[i].astype(BF16), gt2, final_g, seq_len=seq_len, final_norm=(i == depth - 1))
    return xf.reshape(batch, seq_len, d)
```
